```python
import math
import jax
import jax.numpy as jnp
from jax import lax
import numpy as np

D_MODEL = 1024
BATCH = 8
SEQ = 2048
DEPTH = 4
DEC_BATCH = 32
DEC_SEQ = 1
PAST_LEN = 8192
PAGE_SIZE = 128

N_MIXERS = 3
N_A = (DEPTH + 2) // 3
N_B = (DEPTH + 1) // 3
N_C = DEPTH // 3
EPS = 1e-6
F32 = jnp.float32

M_DINNER = 2 * D_MODEL
M_HEADDIM = 64
M_HEADS = M_DINNER // M_HEADDIM
M_GROUPS = 4
M_DSTATE = 128
M_CONV = 4
M_GN = M_GROUPS * M_DSTATE
M_CONV_DIM = M_DINNER + 2 * M_GN
M_PROJ = M_DINNER + M_CONV_DIM + M_HEADS
M_CHUNK = 128

N_HEADS = 16
HEAD_DIM = D_MODEL // N_HEADS
KV_HEADS = 4
HPG = N_HEADS // KV_HEADS
KV_DIM = KV_HEADS * HEAD_DIM
CMP_BLOCK = 32
CMP_STRIDE = 16
SEL_BLOCK = 64
SEL_TOPK = 16
WINDOW = 512
SEL_QBLK = 32
WIN_QBLK = 128
NSA_PROJ = N_HEADS * HEAD_DIM + 6 * KV_DIM + 3 * N_HEADS
SCALE = HEAD_DIM ** -0.5
REL_BUCKETS = 32
REL_MAX_DIST = 128

H_DK = 128
H_HEADS = D_MODEL // H_DK
H_DV = D_MODEL // H_HEADS
H_CHUNK = 32

MOE_GROUPS = 4
MOE_EPG = 4
MOE_EXPERTS = MOE_GROUPS * MOE_EPG
MOE_TOPK = 2
MOE_FF = 512

kernel_name = 'hybrid_ssd_nsa_hgrn2_hmoe_step'


def rmsnorm(x, g):
    xf = x.astype(F32)
    y = xf * lax.rsqrt(jnp.mean(xf * xf, axis=-1, keepdims=True) + EPS)
    return (y * g.astype(F32)).astype(x.dtype)


def causal_dwconv(u, buf, w, b):
    L = u.shape[1]
    ext = jnp.concatenate([buf.astype(u.dtype), u], axis=1)
    out = b + sum(ext[:, k:k + L] * w[k] for k in range(M_CONV))
    return out, ext[:, L:]


def ssd_scan(x, dt, a, bm, cm, h0):
    bsz, L = x.shape[0], x.shape[1]
    q = M_CHUNK if L % M_CHUNK == 0 else L
    nc = L // q
    hpg = M_HEADS // M_GROUPS
    xdt = (x * dt[..., None]).reshape(bsz, nc, q, M_GROUPS, hpg, M_HEADDIM)
    acum = jnp.cumsum((dt * a).reshape(bsz, nc, q, M_GROUPS, hpg), axis=2)
    bc = bm.reshape(bsz, nc, q, M_GROUPS, M_DSTATE)
    cc = cm.reshape(bsz, nc, q, M_GROUPS, M_DSTATE)
    tri = jnp.tril(jnp.ones((q, q), bool))[:, :, None, None]
    seg = acum[:, :, :, None] - acum[:, :, None]
    decay = jnp.exp(jnp.where(tri, seg, -jnp.inf))
    cb = jnp.einsum('bclgn,bcsgn->bclsg', cc, bc)
    y_diag = jnp.einsum('bclsg,bclsgh,bcsghp->bclghp', cb, decay, xdt)
    decay_end = jnp.exp(acum[:, :, -1:] - acum)
    states = jnp.einsum('bcsgn,bcsgh,bcsghp->bcghpn', bc, decay_end, xdt)
    chunk_decay = jnp.exp(acum[:, :, -1])

    def step(h, inp):
        cd, st = inp
        return cd[..., None, None] * h + st, h

    h_last, h_prev = lax.scan(step, h0.reshape(bsz, M_GROUPS, hpg, M_HEADDIM, M_DSTATE),
                              (chunk_decay.swapaxes(0, 1), states.swapaxes(0, 1)))
    y_off = jnp.einsum('bclgn,bclgh,bcghpn->bclghp', cc, jnp.exp(acum), h_prev.swapaxes(0, 1))
    y = (y_diag + y_off).reshape(bsz, L, M_HEADS, M_HEADDIM)
    return y, h_last.reshape(bsz, M_HEADS, M_HEADDIM, M_DSTATE)


def mamba_mixer(u, conv_buf, h0, w_in, conv_w, conv_b, dt_bias, a_log, d_skip, norm_g, w_out):
    bsz, L, _ = u.shape
    proj = u @ w_in
    z = proj[..., :M_DINNER]
    xbc = proj[..., M_DINNER:M_DINNER + M_CONV_DIM]
    dt_raw = proj[..., M_DINNER + M_CONV_DIM:]
    xbc, new_buf = causal_dwconv(xbc, conv_buf, conv_w, conv_b)
    xbc = jax.nn.silu(xbc)
    xs = xbc[..., :M_DINNER].reshape(bsz, L, M_HEADS, M_HEADDIM).astype(F32)
    bm = xbc[..., M_DINNER:M_DINNER + M_GN].reshape(bsz, L, M_GROUPS, M_DSTATE).astype(F32)
    cm = xbc[..., M_DINNER + M_GN:].reshape(bsz, L, M_GROUPS, M_DSTATE).astype(F32)
    dt = jax.nn.softplus(dt_raw.astype(F32) + dt_bias.astype(F32))
    a = -jnp.exp(a_log.astype(F32))
    y, h = ssd_scan(xs, dt, a, bm, cm, h0.astype(F32))
    y = y + xs * d_skip.astype(F32)[:, None]
    y = y.reshape(bsz, L, M_DINNER).astype(u.dtype)
    y = rmsnorm(y * jax.nn.silu(z), norm_g)
    return y @ w_out, new_buf, h.astype(h0.dtype)


def rel_bucket(dist):
    exact = REL_BUCKETS // 2
    d = jnp.maximum(dist, 0)
    ratio = jnp.log(jnp.maximum(d, 1).astype(F32) / exact) / math.log(REL_MAX_DIST / exact)
    large = jnp.minimum(exact + (ratio * (REL_BUCKETS - exact)).astype(jnp.int32), REL_BUCKETS - 1)
    return jnp.where(d < exact, d, large)


def head_bias(rel_table, dist):
    b = rel_table[rel_bucket(dist)].astype(F32)
    return jnp.moveaxis(b, -1, 0).reshape(KV_HEADS, HPG, dist.shape[0], dist.shape[1])


def masked_softmax(s, valid):
    s = jnp.where(valid, s, -1e30)
    e = jnp.where(valid, jnp.exp(s - jnp.max(s, axis=-1, keepdims=True)), 0.0)
    return e / jnp.maximum(jnp.sum(e, axis=-1, keepdims=True), 1e-30)


def group_attend(qg, qpos, k, v, kpos, valid, rel_table):
    s = jnp.einsum('bqghd,bkgd->bghqk', qg, k).astype(F32) * SCALE
    s = s + head_bias(rel_table, qpos[:, None] - kpos[None, :])
    p = masked_softmax(s, valid)
    o = jnp.einsum('bghqk,bkgd->bqghd', p.astype(v.dtype), v)
    return o, p


def compress(rows, pe, w1, w2):
    bsz, T = rows.shape[0], rows.shape[1]
    nc = (T - CMP_BLOCK) // CMP_STRIDE + 1
    idx = jnp.arange(nc)[:, None] * CMP_STRIDE + jnp.arange(CMP_BLOCK)[None, :]
    blk = rows[:, idx] + pe[:, None, :]
    blk = jnp.moveaxis(blk, 2, 3).reshape(bsz, nc, KV_HEADS, CMP_BLOCK * HEAD_DIM)
    return jax.nn.silu(blk @ w1) @ w2


def select_blocks(p_cmp, qpos, n_sel):
    nc = p_cmp.shape[-1]
    cstart = jnp.arange(nc) * CMP_STRIDE
    sstart = jnp.arange(n_sel) * SEL_BLOCK
    overlap = ((cstart[:, None] < sstart[None, :] + SEL_BLOCK) &
               (cstart[:, None] + CMP_BLOCK > sstart[None, :])).astype(F32)
    imp = jnp.einsum('bghqc,cs->bgqs', p_cmp, overlap)
    qblk = qpos // SEL_BLOCK
    j = jnp.arange(n_sel)
    forced = (j[None, :] == qblk[:, None]) | (j[None, :] == 0)
    score = jnp.where(forced, 1e9, jnp.where(j[None, :] <= qblk[:, None], imp, -1e30))
    _, idx = lax.top_k(score, min(SEL_TOPK, n_sel))
    return idx


def sel_attend(qg, qpos, idx, kblk, vblk, rel_table):
    bsz, lq = qg.shape[0], qg.shape[1]
    bi = jnp.arange(bsz)[:, None, None, None]
    gi = jnp.arange(KV_HEADS)[None, :, None, None]
    kg = kblk[bi, gi, idx]
    vg = vblk[bi, gi, idx]
    s = jnp.einsum('bqghd,bgqkrd->bghqkr', qg, kg).astype(F32) * SCALE
    kpos = idx[..., None] * SEL_BLOCK + jnp.arange(SEL_BLOCK)
    dist = qpos[:, None, None] - kpos
    g5 = jnp.arange(KV_HEADS)[None, :, None, None, None]
    bias = rel_table.reshape(REL_BUCKETS, KV_HEADS, HPG)[rel_bucket(dist), g5].astype(F32)
    s = s + jnp.moveaxis(bias, -1, 2)
    nk = idx.shape[-1]
    valid = (dist >= 0)[:, :, None].reshape(bsz, KV_HEADS, 1, lq, nk * SEL_BLOCK)
    p = masked_softmax(s.reshape(bsz, KV_HEADS, HPG, lq, nk * SEL_BLOCK), valid)
    p = p.reshape(bsz, KV_HEADS, HPG, lq, nk, SEL_BLOCK)
    return jnp.einsum('bghqkr,bgqkrd->bqghd', p.astype(vg.dtype), vg)


def nsa_mixer(u, pos0, kv_cmp_past, kv_sel_past, kv_win_past, n_keep,
              w_in, q_g, k_g, cmp_pe, cmp_w1, cmp_w2, w_out, rel_table):
    bsz, L, _ = u.shape
    sizes = [N_HEADS * HEAD_DIM] + [KV_DIM] * 6 + [3 * N_HEADS]
    q, kc, vc, ks, vs, kw, vw, gl = jnp.split(u @ w_in, np.cumsum(sizes)[:-1].tolist(), axis=-1)
    q = rmsnorm(q.reshape(bsz, L, KV_HEADS, HPG, HEAD_DIM), q_g)

    def kv(t):
        return t.reshape(bsz, L, KV_HEADS, HEAD_DIM)

    new_cmp = jnp.stack([kv(kc), kv(vc)], axis=2)
    new_sel = jnp.stack([rmsnorm(kv(ks), k_g[1]), kv(vs)], axis=2)
    new_win = jnp.stack([rmsnorm(kv(kw), k_g[2]), kv(vw)], axis=2)
    qpos = pos0 + jnp.arange(L)

    crows = jnp.concatenate([kv_cmp_past.astype(u.dtype), new_cmp], axis=1)
    kcmp = rmsnorm(compress(crows[:, :, 0], cmp_pe[0], cmp_w1[0], cmp_w2[0]), k_g[0])
    vcmp = compress(crows[:, :, 1], cmp_pe[1], cmp_w1[1], cmp_w2[1])
    ends = jnp.arange(kcmp.shape[1]) * CMP_STRIDE + CMP_BLOCK - 1
    o_cmp, p_cmp = group_attend(q, qpos, kcmp, vcmp, ends, qpos[:, None] >= ends[None, :], rel_table)

    srows = jnp.concatenate([kv_sel_past.astype(u.dtype), new_sel], axis=1)
    T = srows.shape[1]
    n_sel = -(-T // SEL_BLOCK)
    srows = jnp.pad(srows, ((0, 0), (0, n_sel * SEL_BLOCK - T), (0, 0), (0, 0), (0, 0)))
    blocks = srows.reshape(bsz, n_sel, SEL_BLOCK, 2, KV_HEADS, HEAD_DIM).transpose(3, 0, 4, 1, 2, 5)
    idx = select_blocks(p_cmp, qpos, n_sel)
    qb = SEL_QBLK if L % SEL_QBLK == 0 else L
    nqb = L // qb

    def sel_block(args):
        qgi, qposi, idxi = args
        return sel_attend(qgi, qposi, idxi, blocks[0], blocks[1], rel_table)

    o_sel = lax.map(sel_block, (q.reshape(bsz, nqb, qb, KV_HEADS, HPG, HEAD_DIM).swapaxes(0, 1),
                                qpos.reshape(nqb, qb),
                                idx.reshape(bsz, KV_HEADS, nqb, qb, -1).transpose(2, 0, 1, 3, 4)))
    o_sel = o_sel.swapaxes(0, 1).reshape(bsz, L, KV_HEADS, HPG, HEAD_DIM)

    p_win = kv_win_past.shape[1]
    wrows = jnp.concatenate([kv_win_past.astype(u.dtype), new_win], axis=1)
    wpad = jnp.pad(wrows, ((0, 0), (WINDOW, 0), (0, 0), (0, 0), (0, 0)))
    n_all = WINDOW + p_win + L
    kpos_all = pos0 - p_win - WINDOW + jnp.arange(n_all)
    kvalid_all = jnp.arange(n_all) >= WINDOW
    wq = WIN_QBLK if L % WIN_QBLK == 0 else L

    def win_block(i):
        start = p_win + i * wq
        qgi = lax.dynamic_slice_in_dim(q, i * wq, wq, axis=1)
        kvi = lax.dynamic_slice_in_dim(wpad, start, WINDOW + wq, axis=1)
        kposi = lax.dynamic_slice_in_dim(kpos_all, start, WINDOW + wq)
        kvalidi = lax.dynamic_slice_in_dim(kvalid_all, start, WINDOW + wq)
        qposi = pos0 + i * wq + jnp.arange(wq)
        dist = qposi[:, None] - kposi[None, :]
        valid = kvalidi[None, :] & (dist >= 0) & (dist <= WINDOW)
        o, _ = group_attend(qgi, qposi, kvi[:, :, 0], kvi[:, :, 1], kposi, valid, rel_table)
        return o

    o_win = lax.map(win_block, jnp.arange(L // wq)).swapaxes(0, 1).reshape(bsz, L, KV_HEADS, HPG, HEAD_DIM)

    g = jax.nn.sigmoid(gl.astype(F32)).reshape(bsz, L, KV_HEADS, HPG, 3)
    o = g[..., 0:1] * o_cmp + g[..., 1:2] * o_sel + g[..., 2:3] * o_win
    y = o.astype(u.dtype).reshape(bsz, L, N_HEADS * HEAD_DIM) @ w_out
    return y, new_cmp, new_sel, wrows[:, -n_keep:]


def gla_chunked(q, k, v, logf, s0):
    bsz, L = q.shape[0], q.shape[1]
    nc = -(-L // H_CHUNK)
    pad = nc * H_CHUNK - L

    def prep(t):
        t = jnp.pad(t, ((0, 0), (0, pad), (0, 0), (0, 0)))
        return t.reshape(bsz, nc, H_CHUNK, H_HEADS, t.shape[-1])

    q, k, v, logf = prep(q), prep(k), prep(v), prep(logf)
    acum = jnp.cumsum(logf, axis=2)
    alast = acum[:, :, -1:]
    qe = q * jnp.exp(acum)
    ke = k * jnp.exp(-acum)
    kd = k * jnp.exp(alast - acum)
    tri = jnp.tril(jnp.ones((H_CHUNK, H_CHUNK), bool))
    att = jnp.where(tri, jnp.einsum('bcthk,bcshk->bchts', qe, ke), 0.0)
    o_intra = jnp.einsum('bchts,bcshv->bcthv', att, v)
    upd = jnp.einsum('bcshk,bcshv->bchkv', kd, v)

    def step(s, inp):
        dec, up = inp
        return dec[..., None] * s + up, s

    s_last, s_prev = lax.scan(step, s0, (jnp.exp(alast[:, :, 0]).swapaxes(0, 1), upd.swapaxes(0, 1)))
    o_inter = jnp.einsum('bcthk,bchkv->bcthv', qe, s_prev.swapaxes(0, 1))
    o = (o_intra + o_inter).reshape(bsz, nc * H_CHUNK, H_HEADS, H_DV)[:, :L]
    return o, s_last


def hgrn2_mixer(u, s0, lb, w_in, norm_g, w_out):
    bsz, L, _ = u.shape
    q, fr, iv, gate = jnp.split(u @ w_in, 4, axis=-1)
    q = jax.nn.silu(q).reshape(bsz, L, H_HEADS, H_DK).astype(F32)
    forget = lb + (1.0 - lb) * jax.nn.sigmoid(fr.astype(F32))
    logf = jnp.log(forget).reshape(bsz, L, H_HEADS, H_DK)
    k = (1.0 - forget).reshape(bsz, L, H_HEADS, H_DK)
    v = iv.reshape(bsz, L, H_HEADS, H_DV).astype(F32)
    o, s = gla_chunked(q, k, v, logf, s0.astype(F32))
    o = rmsnorm(o.astype(u.dtype), norm_g) * jax.nn.silu(gate.reshape(bsz, L, H_HEADS, H_DV))
    return o.reshape(bsz, L, D_MODEL) @ w_out, s.astype(s0.dtype)


def hier_moe(u, w_rg, b_rg, w_re, b_re, w1, w3, w2):
    shp = u.shape
    x = u.reshape(-1, D_MODEL)
    pg = jax.nn.softmax((x @ w_rg).astype(F32) + b_rg.astype(F32), axis=-1)
    pg_top, g_idx = lax.top_k(pg, 1)
    le = ((x @ w_re).astype(F32) + b_re.astype(F32)).reshape(-1, MOE_GROUPS, MOE_EPG)
    le_g = jnp.einsum('tge,tg->te', le, jax.nn.one_hot(g_idx[:, 0], MOE_GROUPS, dtype=F32))
    top_l, e_idx = lax.top_k(le_g, MOE_TOPK)
    wts = pg_top * jax.nn.softmax(top_l, axis=-1)
    eid = g_idx * MOE_EPG + e_idx
    combine = jnp.einsum('tk,tke->te', wts, jax.nn.one_hot(eid, MOE_EXPERTS, dtype=F32))
    h = jax.nn.silu(jnp.einsum('td,edf->tef', x, w1)) * jnp.einsum('td,edf->tef', x, w3)
    y = jnp.einsum('tef,efd->td', h * combine[:, :, None].astype(h.dtype), w2)
    return y.reshape(shp)


def gather_pages(cache, page_table):
    pages = cache[page_table]
    b, n, p = pages.shape[0], pages.shape[1], pages.shape[2]
    return pages.reshape(b, n * p, *pages.shape[3:])


def setup_inputs(seed: int = 0) -> dict:
    key = jax.random.key(seed)
    keys = iter(jax.random.split(key, 48))

    def nrm(shape, scale):
        return scale * jax.random.normal(next(keys), shape, F32)

    n_pages = PAST_LEN // PAGE_SIZE
    n_pool = (DEC_BATCH * n_pages * 5) // 4
    win_buf = min(WINDOW, PAST_LEN)
    page_table = jax.random.permutation(next(keys), n_pool)[:DEC_BATCH * n_pages]
    page_table = page_table.reshape(DEC_BATCH, n_pages).astype(jnp.int32)
    dt0 = jnp.exp(jax.random.uniform(next(keys), (N_A, M_HEADS), F32, math.log(1e-3), math.log(1e-1)))
    a_log = jnp.log(jax.random.uniform(next(keys), (N_A, M_HEADS), F32, 1.0, 16.0))
    return {
        'x_prompt': nrm((BATCH, SEQ, D_MODEL), 1.0),
        'x_sample': nrm((DEC_BATCH, DEC_SEQ, D_MODEL), 1.0),
        'state_ssm': nrm((N_A, DEC_BATCH, M_HEADS, M_HEADDIM, M_DSTATE), 0.3),
        'state_conv': nrm((N_A, DEC_BATCH, M_CONV - 1, M_CONV_DIM), 1.0),
        'cache_kv_cmp': nrm((N_B, n_pool, PAGE_SIZE, 2, KV_HEADS, HEAD_DIM), 1.0),
        'cache_kv_sel': nrm((N_B, n_pool, PAGE_SIZE, 2, KV_HEADS, HEAD_DIM), 1.0),
        'cache_kv_win': nrm((N_B, DEC_BATCH, win_buf, 2, KV_HEADS, HEAD_DIM), 1.0),
        'state_hgrn': nrm((N_C, DEC_BATCH, H_HEADS, H_DK, H_DV), 0.3),
        'page_table': page_table,
        'norm_g': 1.0 + nrm((DEPTH, 2, D_MODEL), 0.02),
        'rel_table': nrm((REL_BUCKETS, N_HEADS), 0.3),
        'm_w_in': nrm((N_A, D_MODEL, M_PROJ), D_MODEL ** -0.5),
        'm_conv_w': nrm((N_A, M_CONV, M_CONV_DIM), M_CONV ** -0.5),
        'm_conv_b': nrm((N_A, M_CONV_DIM), 0.02),
        'm_dt_bias': dt0 + jnp.log(-jnp.expm1(-dt0)),
        'm_a_log': a_log,
        'm_d': 1.0 + nrm((N_A, M_HEADS), 0.1),
        'm_norm_g': 1.0 + nrm((N_A, M_DINNER), 0.02),
        'm_w_out': nrm((N_A, M_DINNER, D_MODEL), M_DINNER ** -0.5),
        'n_w_in': nrm((N_B, D_MODEL, NSA_PROJ), D_MODEL ** -0.5),
        'n_q_g': 1.0 + nrm((N_B, HEAD_DIM), 0.02),
        'n_k_g': 1.0 + nrm((N_B, 3, HEAD_DIM), 0.02),
        'n_cmp_pe': nrm((N_B, 2, CMP_BLOCK, HEAD_DIM), 0.1),
        'n_cmp_w1': nrm((N_B, 2, CMP_BLOCK * HEAD_DIM, HEAD_DIM), (CMP_BLOCK * HEAD_DIM) ** -0.5),
        'n_cmp_w2': nrm((N_B, 2, HEAD_DIM, HEAD_DIM), HEAD_DIM ** -0.5),
        'n_w_out': nrm((N_B, N_HEADS * HEAD_DIM, D_MODEL), (N_HEADS * HEAD_DIM) ** -0.5),
        'h_w_in': nrm((N_C, D_MODEL, 4 * D_MODEL), D_MODEL ** -0.5),
        'h_lb': nrm((DEPTH, H_HEADS * H_DK), 0.1),
        'h_norm_g': 1.0 + nrm((N_C, H_DV), 0.02),
        'h_w_out': nrm((N_C, D_MODEL, D_MODEL), D_MODEL ** -0.5),
        'moe_w_rg': nrm((DEPTH, D_MODEL, MOE_GROUPS), D_MODEL ** -0.5),
        'moe_b_rg': nrm((DEPTH, MOE_GROUPS), 0.01),
        'moe_w_re': nrm((DEPTH, D_MODEL, MOE_EXPERTS), D_MODEL ** -0.5),
        'moe_b_re': nrm((DEPTH, MOE_EXPERTS), 0.01),
        'moe_w1': nrm((DEPTH, MOE_EXPERTS, D_MODEL, MOE_FF), D_MODEL ** -0.5),
        'moe_w3': nrm((DEPTH, MOE_EXPERTS, D_MODEL, MOE_FF), D_MODEL ** -0.5),
        'moe_w2': nrm((DEPTH, MOE_EXPERTS, MOE_FF, D_MODEL), MOE_FF ** -0.5),
    }


def reference(x_prompt, x_sample, state_ssm, state_conv, cache_kv_cmp, cache_kv_sel, cache_kv_win,
              state_hgrn, page_table, norm_g, rel_table,
              m_w_in, m_conv_w, m_conv_b, m_dt_bias, m_a_log, m_d, m_norm_g, m_w_out,
              n_w_in, n_q_g, n_k_g, n_cmp_pe, n_cmp_w1, n_cmp_w2, n_w_out,
              h_w_in, h_lb, h_norm_g, h_w_out,
              moe_w_rg, moe_b_rg, moe_w_re, moe_b_re, moe_w1, moe_w3, moe_w2):
    bp, lp = x_prompt.shape[0], x_prompt.shape[1]
    past_len = page_table.shape[1] * cache_kv_cmp.shape[2]
    dt = x_prompt.dtype
    lbs = jax.nn.softmax(h_lb.astype(F32), axis=0)
    lbs = jnp.cumsum(lbs, axis=0) - lbs[0]
    xp, xs = x_prompt, x_sample
    ssm_p, conv_p, cmp_p, sel_p, win_p, hg_p = [], [], [], [], [], []
    ssm_s, conv_s, cmp_s, sel_s, win_s, hg_s = [], [], [], [], [], []
    for i in range(DEPTH):
        kind, j = i % N_MIXERS, i // N_MIXERS
        hp = rmsnorm(xp, norm_g[i, 0])
        hs = rmsnorm(xs, norm_g[i, 0])
        if kind == 0:
            w = (m_w_in[j], m_conv_w[j], m_conv_b[j], m_dt_bias[j], m_a_log[j], m_d[j], m_norm_g[j], m_w_out[j])
            yp, cbuf, hh = mamba_mixer(hp, jnp.zeros((bp, M_CONV - 1, M_CONV_DIM), dt),
                                       jnp.zeros((bp, M_HEADS, M_HEADDIM, M_DSTATE), dt), *w)
            ssm_p.append(hh)
            conv_p.append(cbuf)
            ys, cbuf, hh = mamba_mixer(hs, state_conv[j], state_ssm[j], *w)
            ssm_s.append(hh)
            conv_s.append(cbuf)
        elif kind == 1:
            w = (n_w_in[j], n_q_g[j], n_k_g[j], n_cmp_pe[j], n_cmp_w1[j], n_cmp_w2[j], n_w_out[j], rel_table)
            empty = jnp.zeros((bp, 0, 2, KV_HEADS, HEAD_DIM), dt)
            yp, rc, rs, wb = nsa_mixer(hp, 0, empty, empty, empty, min(WINDOW, lp), *w)
            cmp_p.append(rc)
            sel_p.append(rs)
            win_p.append(wb)
            ys, rc, rs, wb = nsa_mixer(hs, past_len, gather_pages(cache_kv_cmp[j], page_table),
                                       gather_pages(cache_kv_sel[j], page_table), cache_kv_win[j],
                                       cache_kv_win.shape[2], *w)
            cmp_s.append(rc)
            sel_s.append(rs)
            win_s.append(wb)
        else:
            w = (lbs[i], h_w_in[j], h_norm_g[j], h_w_out[j])
            yp, st = hgrn2_mixer(hp, jnp.zeros((bp, H_HEADS, H_DK, H_DV), dt), *w)
            hg_p.append(st)
            ys, st = hgrn2_mixer(hs, state_hgrn[j], *w)
            hg_s.append(st)
        xp = xp + yp
        xs = xs + ys
        mw = (moe_w_rg[i], moe_b_rg[i], moe_w_re[i], moe_b_re[i], moe_w1[i], moe_w3[i], moe_w2[i])
        xp = xp + hier_moe(rmsnorm(xp, norm_g[i, 1]), *mw)
        xs = xs + hier_moe(rmsnorm(xs, norm_g[i, 1]), *mw)
    return (xp, xs,
            jnp.stack(ssm_p), jnp.stack(conv_p), jnp.stack(cmp_p), jnp.stack(sel_p), jnp.stack(win_p), jnp.stack(hg_p),
            jnp.stack(ssm_s), jnp.stack(conv_s), jnp.stack(cmp_s), jnp.stack(sel_s), jnp.stack(win_s), jnp.stack(hg_s))
```

```python
import functools
import math

import jax
import jax.numpy as jnp
import numpy as np
from jax import lax
from jax.experimental import pallas as pl
from jax.experimental.pallas import tpu as pltpu

F32 = jnp.float32
BF16 = jnp.bfloat16
EPS = 1e-6

D_MODEL = 1024
DEPTH = 4
N_MIXERS = 3

M_DINNER = 2 * D_MODEL
M_HEADDIM = 64
M_HEADS = M_DINNER // M_HEADDIM
M_GROUPS = 4
M_DSTATE = 128
M_CONV = 4
M_GN = M_GROUPS * M_DSTATE
M_CONV_DIM = M_DINNER + 2 * M_GN
M_CHUNK = 128

N_HEADS = 16
HEAD_DIM = D_MODEL // N_HEADS
KV_HEADS = 4
HPG = N_HEADS // KV_HEADS
KV_DIM = KV_HEADS * HEAD_DIM
CMP_BLOCK = 32
CMP_STRIDE = 16
SEL_BLOCK = 64
SEL_TOPK = 16
WINDOW = 512
SEL_QBLK = 32
WIN_QBLK = 128
SCALE = HEAD_DIM ** -0.5
REL_BUCKETS = 32
REL_MAX_DIST = 128

H_DK = 128
H_HEADS = D_MODEL // H_DK
H_DV = D_MODEL // H_HEADS
H_CHUNK = 32

MOE_GROUPS = 4
MOE_EPG = 4
MOE_EXPERTS = MOE_GROUPS * MOE_EPG
MOE_TOPK = 2
MOE_FF = 512

VMEM_LIMIT_BYTES = 48 * 1024 * 1024


def _rmsnorm(x, g):
    xf = x.astype(F32)
    y = xf * lax.rsqrt(jnp.mean(xf * xf, axis=-1, keepdims=True) + EPS)
    return (y * g.astype(F32)).astype(x.dtype)


def _norm_matmul_body(x_ref, g_ref, w_ref, o_ref, *, normalize):
    x = x_ref[...]
    if normalize:
        x = x * lax.rsqrt(jnp.mean(x * x, axis=-1, keepdims=True) + EPS) * g_ref[...]
    o_ref[...] = jnp.dot(x.astype(BF16), w_ref[...].astype(BF16), preferred_element_type=F32)


def _pick_tile(n, pref):
    t = min(n, pref)
    while n % t:
        t //= 2
    return t


def norm_matmul(x, g, w, *, normalize=True, tm=512, tn=512):
    t, k = x.shape
    n_true = w.shape[1]
    tm = _pick_tile(t, tm)
    if n_true % tn:
        w = jnp.pad(w, ((0, 0), (0, tn - n_true % tn)))
    n = w.shape[1]
    if g is None:
        g = jnp.ones((k,), F32)
    out = pl.pallas_call(
        functools.partial(_norm_matmul_body, normalize=normalize),
        grid=(t // tm, n // tn),
        in_specs=[
            pl.BlockSpec((tm, k), lambda i, j: (i, 0)),
            pl.BlockSpec((1, k), lambda i, j: (0, 0)),
            pl.BlockSpec((k, tn), lambda i, j: (0, j)),
        ],
        out_specs=pl.BlockSpec((tm, tn), lambda i, j: (i, j)),
        out_shape=jax.ShapeDtypeStruct((t, n), F32),
        compiler_params=pltpu.CompilerParams(
            dimension_semantics=("parallel", "arbitrary"),
            vmem_limit_bytes=VMEM_LIMIT_BYTES),
    )(x, g.reshape(1, k), w)
    return out[:, :n_true]


def _moe_body(x_ref, comb_ref, w1_ref, w3_ref, w2_ref, res_ref, o_ref, acc_ref):
    e = pl.program_id(1)

    @pl.when(e == 0)
    def _():
        acc_ref[...] = jnp.zeros_like(acc_ref)

    x = x_ref[...].astype(BF16)
    a = jnp.dot(x, w1_ref[0].astype(BF16), preferred_element_type=F32)
    b = jnp.dot(x, w3_ref[0].astype(BF16), preferred_element_type=F32)
    comb = comb_ref[...]
    lane = lax.broadcasted_iota(jnp.int32, comb.shape, 1)
    c = jnp.sum(jnp.where(lane == e, comb, 0.0), axis=1, keepdims=True)
    h = (a * jax.nn.sigmoid(a)) * b * c
    acc_ref[...] += jnp.dot(h.astype(BF16), w2_ref[0].astype(BF16), preferred_element_type=F32)

    @pl.when(e == pl.num_programs(1) - 1)
    def _():
        o_ref[...] = res_ref[...] + acc_ref[...]


def moe_dense(xn, comb, w1, w3, w2, res, *, tm=512):
    t = xn.shape[0]
    tm = _pick_tile(t, tm)
    return pl.pallas_call(
        _moe_body,
        grid=(t // tm, MOE_EXPERTS),
        in_specs=[
            pl.BlockSpec((tm, D_MODEL), lambda i, e: (i, 0)),
            pl.BlockSpec((tm, MOE_EXPERTS), lambda i, e: (i, 0)),
            pl.BlockSpec((1, D_MODEL, MOE_FF), lambda i, e: (e, 0, 0)),
            pl.BlockSpec((1, D_MODEL, MOE_FF), lambda i, e: (e, 0, 0)),
            pl.BlockSpec((1, MOE_FF, D_MODEL), lambda i, e: (e, 0, 0)),
            pl.BlockSpec((tm, D_MODEL), lambda i, e: (i, 0)),
        ],
        out_specs=pl.BlockSpec((tm, D_MODEL), lambda i, e: (i, 0)),
        out_shape=jax.ShapeDtypeStruct((t, D_MODEL), F32),
        scratch_shapes=[pltpu.VMEM((tm, D_MODEL), F32)],
        compiler_params=pltpu.CompilerParams(
            dimension_semantics=("parallel", "arbitrary"),
            vmem_limit_bytes=VMEM_LIMIT_BYTES),
    )(xn, comb, w1, w3, w2, res)


def hier_moe_residual(x, g, w_rg, b_rg, w_re, b_re, w1, w3, w2):
    xn = _rmsnorm(x, g)
    hi = lax.Precision.HIGHEST
    pg = jax.nn.softmax(jnp.dot(xn, w_rg, precision=hi) + b_rg, axis=-1)
    pg_top, g_idx = lax.top_k(pg, 1)
    le = (jnp.dot(xn, w_re, precision=hi) + b_re).reshape(-1, MOE_GROUPS, MOE_EPG)
    le_g = jnp.einsum('tge,tg->te', le, jax.nn.one_hot(g_idx[:, 0], MOE_GROUPS, dtype=F32), precision=hi)
    top_l, e_idx = lax.top_k(le_g, MOE_TOPK)
    wts = pg_top * jax.nn.softmax(top_l, axis=-1)
    eid = g_idx * MOE_EPG + e_idx
    combine = jnp.einsum('tk,tke->te', wts, jax.nn.one_hot(eid, MOE_EXPERTS, dtype=F32), precision=hi)
    return moe_dense(xn, combine, w1, w3, w2, x)


def _causal_dwconv(u, buf, w, b):
    L = u.shape[1]
    ext = jnp.concatenate([buf.astype(u.dtype), u], axis=1)
    out = b + sum(ext[:, k:k + L] * w[k] for k in range(M_CONV))
    return out, ext[:, L:]


def _ssd_scan(x, dt, a, bm, cm, h0):
    bsz, L = x.shape[0], x.shape[1]
    q = M_CHUNK if L % M_CHUNK == 0 else L
    nc = L // q
    hpg = M_HEADS // M_GROUPS
    xdt = (x * dt[..., None]).reshape(bsz, nc, q, M_GROUPS, hpg, M_HEADDIM)
    acum = jnp.cumsum((dt * a).reshape(bsz, nc, q, M_GROUPS, hpg), axis=2)
    bc = bm.reshape(bsz, nc, q, M_GROUPS, M_DSTATE)
    cc = cm.reshape(bsz, nc, q, M_GROUPS, M_DSTATE)
    tri = jnp.tril(jnp.ones((q, q), bool))[:, :, None, None]
    seg = acum[:, :, :, None] - acum[:, :, None]
    decay = jnp.exp(jnp.where(tri, seg, -jnp.inf))
    cb = jnp.einsum('bclgn,bcsgn->bclsg', cc, bc)
    y_diag = jnp.einsum('bclsg,bclsgh,bcsghp->bclghp', cb, decay, xdt)
    decay_end = jnp.exp(acum[:, :, -1:] - acum)
    states = jnp.einsum('bcsgn,bcsgh,bcsghp->bcghpn', bc, decay_end, xdt)
    chunk_decay = jnp.exp(acum[:, :, -1])

    def step(h, inp):
        cd, st = inp
        return cd[..., None, None] * h + st, h

    h_last, h_prev = lax.scan(step, h0.reshape(bsz, M_GROUPS, hpg, M_HEADDIM, M_DSTATE),
                              (chunk_decay.swapaxes(0, 1), states.swapaxes(0, 1)))
    y_off = jnp.einsum('bclgn,bclgh,bcghpn->bclghp', cc, jnp.exp(acum), h_prev.swapaxes(0, 1))
    y = (y_diag + y_off).reshape(bsz, L, M_HEADS, M_HEADDIM)
    return y, h_last.reshape(bsz, M_HEADS, M_HEADDIM, M_DSTATE)


def mamba_mixer(x, g, conv_buf, h0, w_in, conv_w, conv_b, dt_bias, a_log, d_skip, norm_g, w_out):
    bsz, L, _ = x.shape
    proj = norm_matmul(x.reshape(bsz * L, D_MODEL), g, w_in).reshape(bsz, L, -1)
    z = proj[..., :M_DINNER]
    xbc = proj[..., M_DINNER:M_DINNER + M_CONV_DIM]
    dt_raw = proj[..., M_DINNER + M_CONV_DIM:]
    xbc, new_buf = _causal_dwconv(xbc, conv_buf, conv_w, conv_b)
    xbc = jax.nn.silu(xbc)
    xs = xbc[..., :M_DINNER].reshape(bsz, L, M_HEADS, M_HEADDIM)
    bm = xbc[..., M_DINNER:M_DINNER + M_GN].reshape(bsz, L, M_GROUPS, M_DSTATE)
    cm = xbc[..., M_DINNER + M_GN:].reshape(bsz, L, M_GROUPS, M_DSTATE)
    dt = jax.nn.softplus(dt_raw + dt_bias)
    a = -jnp.exp(a_log)
    y, h = _ssd_scan(xs, dt, a, bm, cm, h0)
    y = y + xs * d_skip[:, None]
    y = y.reshape(bsz, L, M_DINNER)
    y = _rmsnorm(y * jax.nn.silu(z), norm_g)
    out = norm_matmul(y.reshape(bsz * L, M_DINNER), None, w_out, normalize=False)
    return out.reshape(bsz, L, D_MODEL), new_buf, h


def _rel_bucket(dist):
    exact = REL_BUCKETS // 2
    d = jnp.maximum(dist, 0)
    ratio = jnp.log(jnp.maximum(d, 1).astype(F32) / exact) / math.log(REL_MAX_DIST / exact)
    large = jnp.minimum(exact + (ratio * (REL_BUCKETS - exact)).astype(jnp.int32), REL_BUCKETS - 1)
    return jnp.where(d < exact, d, large)


def _head_bias(rel_table, dist):
    b = rel_table[_rel_bucket(dist)].astype(F32)
    return jnp.moveaxis(b, -1, 0).reshape(KV_HEADS, HPG, dist.shape[0], dist.shape[1])


def _masked_softmax(s, valid):
    s = jnp.where(valid, s, -1e30)
    e = jnp.where(valid, jnp.exp(s - jnp.max(s, axis=-1, keepdims=True)), 0.0)
    return e / jnp.maximum(jnp.sum(e, axis=-1, keepdims=True), 1e-30)


def _group_attend(qg, qpos, k, v, kpos, valid, rel_table):
    s = jnp.einsum('bqghd,bkgd->bghqk', qg, k).astype(F32) * SCALE
    s = s + _head_bias(rel_table, qpos[:, None] - kpos[None, :])
    p = _masked_softmax(s, valid)
    o = jnp.einsum('bghqk,bkgd->bqghd', p.astype(v.dtype), v)
    return o, p


def _compress(rows, pe, w1, w2):
    bsz, T = rows.shape[0], rows.shape[1]
    nc = (T - CMP_BLOCK) // CMP_STRIDE + 1
    idx = jnp.arange(nc)[:, None] * CMP_STRIDE + jnp.arange(CMP_BLOCK)[None, :]
    blk = rows[:, idx] + pe[:, None, :]
    blk = jnp.moveaxis(blk, 2, 3).reshape(bsz, nc, KV_HEADS, CMP_BLOCK * HEAD_DIM)
    return jax.nn.silu(blk @ w1) @ w2


def _select_blocks(p_cmp, qpos, n_sel):
    nc = p_cmp.shape[-1]
    cstart = jnp.arange(nc) * CMP_STRIDE
    sstart = jnp.arange(n_sel) * SEL_BLOCK
    overlap = ((cstart[:, None] < sstart[None, :] + SEL_BLOCK) &
               (cstart[:, None] + CMP_BLOCK > sstart[None, :])).astype(F32)
    imp = jnp.einsum('bghqc,cs->bgqs', p_cmp, overlap)
    qblk = qpos // SEL_BLOCK
    j = jnp.arange(n_sel)
    forced = (j[None, :] == qblk[:, None]) | (j[None, :] == 0)
    score = jnp.where(forced, 1e9, jnp.where(j[None, :] <= qblk[:, None], imp, -1e30))
    _, idx = lax.top_k(score, min(SEL_TOPK, n_sel))
    return idx


def _sel_attend(qg, qpos, idx, kblk, vblk, rel_table):
    bsz, lq = qg.shape[0], qg.shape[1]
    bi = jnp.arange(bsz)[:, None, None, None]
    gi = jnp.arange(KV_HEADS)[None, :, None, None]
    kg = kblk[bi, gi, idx]
    vg = vblk[bi, gi, idx]
    s = jnp.einsum('bqghd,bgqkrd->bghqkr', qg, kg).astype(F32) * SCALE
    kpos = idx[..., None] * SEL_BLOCK + jnp.arange(SEL_BLOCK)
    dist = qpos[:, None, None] - kpos
    g5 = jnp.arange(KV_HEADS)[None, :, None, None, None]
    bias = rel_table.reshape(REL_BUCKETS, KV_HEADS, HPG)[_rel_bucket(dist), g5].astype(F32)
    s = s + jnp.moveaxis(bias, -1, 2)
    nk = idx.shape[-1]
    valid = (dist >= 0)[:, :, None].reshape(bsz, KV_HEADS, 1, lq, nk * SEL_BLOCK)
    p = _masked_softmax(s.reshape(bsz, KV_HEADS, HPG, lq, nk * SEL_BLOCK), valid)
    p = p.reshape(bsz, KV_HEADS, HPG, lq, nk, SEL_BLOCK)
    return jnp.einsum('bghqkr,bgqkrd->bqghd', p.astype(vg.dtype), vg)


def nsa_mixer(x, g, pos0, kv_cmp_past, kv_sel_past, kv_win_past, n_keep,
              w_in, q_g, k_g, cmp_pe, cmp_w1, cmp_w2, w_out, rel_table):
    bsz, L, _ = x.shape
    sizes = [N_HEADS * HEAD_DIM] + [KV_DIM] * 6 + [3 * N_HEADS]
    proj = norm_matmul(x.reshape(bsz * L, D_MODEL), g, w_in).reshape(bsz, L, -1)
    q, kc, vc, ks, vs, kw, vw, gl = jnp.split(proj, np.cumsum(sizes)[:-1].tolist(), axis=-1)
    q = _rmsnorm(q.reshape(bsz, L, KV_HEADS, HPG, HEAD_DIM), q_g)

    def kv(t):
        return t.reshape(bsz, L, KV_HEADS, HEAD_DIM)

    new_cmp = jnp.stack([kv(kc), kv(vc)], axis=2)
    new_sel = jnp.stack([_rmsnorm(kv(ks), k_g[1]), kv(vs)], axis=2)
    new_win = jnp.stack([_rmsnorm(kv(kw), k_g[2]), kv(vw)], axis=2)
    qpos = pos0 + jnp.arange(L)

    crows = jnp.concatenate([kv_cmp_past.astype(x.dtype), new_cmp], axis=1)
    kcmp = _rmsnorm(_compress(crows[:, :, 0], cmp_pe[0], cmp_w1[0], cmp_w2[0]), k_g[0])
    vcmp = _compress(crows[:, :, 1], cmp_pe[1], cmp_w1[1], cmp_w2[1])
    ends = jnp.arange(kcmp.shape[1]) * CMP_STRIDE + CMP_BLOCK - 1
    o_cmp, p_cmp = _group_attend(q, qpos, kcmp, vcmp, ends, qpos[:, None] >= ends[None, :], rel_table)

    srows = jnp.concatenate([kv_sel_past.astype(x.dtype), new_sel], axis=1)
    T = srows.shape[1]
    n_sel = -(-T // SEL_BLOCK)
    srows = jnp.pad(srows, ((0, 0), (0, n_sel * SEL_BLOCK - T), (0, 0), (0, 0), (0, 0)))
    blocks = srows.reshape(bsz, n_sel, SEL_BLOCK, 2, KV_HEADS, HEAD_DIM).transpose(3, 0, 4, 1, 2, 5)
    idx = _select_blocks(p_cmp, qpos, n_sel)
    qb = SEL_QBLK if L % SEL_QBLK == 0 else L
    nqb = L // qb

    def sel_block(args):
        qgi, qposi, idxi = args
        return _sel_attend(qgi, qposi, idxi, blocks[0], blocks[1], rel_table)

    o_sel = lax.map(sel_block, (q.reshape(bsz, nqb, qb, KV_HEADS, HPG, HEAD_DIM).swapaxes(0, 1),
                                qpos.reshape(nqb, qb),
                                idx.reshape(bsz, KV_HEADS, nqb, qb, -1).transpose(2, 0, 1, 3, 4)))
    o_sel = o_sel.swapaxes(0, 1).reshape(bsz, L, KV_HEADS, HPG, HEAD_DIM)

    p_win = kv_win_past.shape[1]
    wrows = jnp.concatenate([kv_win_past.astype(x.dtype), new_win], axis=1)
    wpad = jnp.pad(wrows, ((0, 0), (WINDOW, 0), (0, 0), (0, 0), (0, 0)))
    n_all = WINDOW + p_win + L
    kpos_all = pos0 - p_win - WINDOW + jnp.arange(n_all)
    kvalid_all = jnp.arange(n_all) >= WINDOW
    wq = WIN_QBLK if L % WIN_QBLK == 0 else L

    def win_block(i):
        start = p_win + i * wq
        qgi = lax.dynamic_slice_in_dim(q, i * wq, wq, axis=1)
        kvi = lax.dynamic_slice_in_dim(wpad, start, WINDOW + wq, axis=1)
        kposi = lax.dynamic_slice_in_dim(kpos_all, start, WINDOW + wq)
        kvalidi = lax.dynamic_slice_in_dim(kvalid_all, start, WINDOW + wq)
        qposi = pos0 + i * wq + jnp.arange(wq)
        dist = qposi[:, None] - kposi[None, :]
        valid = kvalidi[None, :] & (dist >= 0) & (dist <= WINDOW)
        o, _ = _group_attend(qgi, qposi, kvi[:, :, 0], kvi[:, :, 1], kposi, valid, rel_table)
        return o

    o_win = lax.map(win_block, jnp.arange(L // wq)).swapaxes(0, 1).reshape(bsz, L, KV_HEADS, HPG, HEAD_DIM)

    gate = jax.nn.sigmoid(gl).reshape(bsz, L, KV_HEADS, HPG, 3)
    o = gate[..., 0:1] * o_cmp + gate[..., 1:2] * o_sel + gate[..., 2:3] * o_win
    y = norm_matmul(o.reshape(bsz * L, N_HEADS * HEAD_DIM), None, w_out, normalize=False)
    return y.reshape(bsz, L, D_MODEL), new_cmp, new_sel, wrows[:, -n_keep:]


def _gla_chunked(q, k, v, logf, s0):
    bsz, L = q.shape[0], q.shape[1]
    nc = -(-L // H_CHUNK)
    pad = nc * H_CHUNK - L

    def prep(t):
        t = jnp.pad(t, ((0, 0), (0, pad), (0, 0), (0, 0)))
        return t.reshape(bsz, nc, H_CHUNK, H_HEADS, t.shape[-1])

    q, k, v, logf = prep(q), prep(k), prep(v), prep(logf)
    acum = jnp.cumsum(logf, axis=2)
    alast = acum[:, :, -1:]
    qe = q * jnp.exp(acum)
    ke = k * jnp.exp(-acum)
    kd = k * jnp.exp(alast - acum)
    tri = jnp.tril(jnp.ones((H_CHUNK, H_CHUNK), bool))
    att = jnp.where(tri, jnp.einsum('bcthk,bcshk->bchts', qe, ke), 0.0)
    o_intra = jnp.einsum('bchts,bcshv->bcthv', att, v)
    upd = jnp.einsum('bcshk,bcshv->bchkv', kd, v)

    def step(s, inp):
        dec, up = inp
        return dec[..., None] * s + up, s

    s_last, s_prev = lax.scan(step, s0, (jnp.exp(alast[:, :, 0]).swapaxes(0, 1), upd.swapaxes(0, 1)))
    o_inter = jnp.einsum('bcthk,bchkv->bcthv', qe, s_prev.swapaxes(0, 1))
    o = (o_intra + o_inter).reshape(bsz, nc * H_CHUNK, H_HEADS, H_DV)[:, :L]
    return o, s_last


def hgrn2_mixer(x, g, s0, lb, w_in, norm_g, w_out):
    bsz, L, _ = x.shape
    proj = norm_matmul(x.reshape(bsz * L, D_MODEL), g, w_in).reshape(bsz, L, -1)
    q, fr, iv, gate = jnp.split(proj, 4, axis=-1)
    q = jax.nn.silu(q).reshape(bsz, L, H_HEADS, H_DK)
    forget = lb + (1.0 - lb) * jax.nn.sigmoid(fr)
    logf = jnp.log(forget).reshape(bsz, L, H_HEADS, H_DK)
    k = (1.0 - forget).reshape(bsz, L, H_HEADS, H_DK)
    v = iv.reshape(bsz, L, H_HEADS, H_DV)
    o, s = _gla_chunked(q, k, v, logf, s0)
    o = _rmsnorm(o, norm_g) * jax.nn.silu(gate.reshape(bsz, L, H_HEADS, H_DV))
    y = norm_matmul(o.reshape(bsz * L, D_MODEL), None, w_out, normalize=False)
    return y.reshape(bsz, L, D_MODEL), s


def _gather_pages(cache, page_table):
    pages = cache[page_table]
    b, n, p = pages.shape[0], pages.shape[1], pages.shape[2]
    return pages.reshape(b, n * p, *pages.shape[3:])


def kernel(x_prompt, x_sample, state_ssm, state_conv, cache_kv_cmp, cache_kv_sel, cache_kv_win, state_hgrn, page_table, norm_g, rel_table, m_w_in, m_conv_w, m_conv_b, m_dt_bias, m_a_log, m_d, m_norm_g, m_w_out, n_w_in, n_q_g, n_k_g, n_cmp_pe, n_cmp_w1, n_cmp_w2, n_w_out, h_w_in, h_lb, h_norm_g, h_w_out, moe_w_rg, moe_b_rg, moe_w_re, moe_b_re, moe_w1, moe_w3, moe_w2):
    bp, lp = x_prompt.shape[0], x_prompt.shape[1]
    past_len = page_table.shape[1] * cache_kv_cmp.shape[2]
    dt = x_prompt.dtype
    lbs = jax.nn.softmax(h_lb.astype(F32), axis=0)
    lbs = jnp.cumsum(lbs, axis=0) - lbs[0]
    xp, xs = x_prompt, x_sample
    ssm_p, conv_p, cmp_p, sel_p, win_p, hg_p = [], [], [], [], [], []
    ssm_s, conv_s, cmp_s, sel_s, win_s, hg_s = [], [], [], [], [], []
    for i in range(DEPTH):
        kind, j = i % N_MIXERS, i // N_MIXERS
        g0 = norm_g[i, 0]
        if kind == 0:
            w = (m_w_in[j], m_conv_w[j], m_conv_b[j], m_dt_bias[j], m_a_log[j], m_d[j], m_norm_g[j], m_w_out[j])
            yp, cbuf, hh = mamba_mixer(xp, g0, jnp.zeros((bp, M_CONV - 1, M_CONV_DIM), dt),
                                       jnp.zeros((bp, M_HEADS, M_HEADDIM, M_DSTATE), dt), *w)
            ssm_p.append(hh)
            conv_p.append(cbuf)
            ys, cbuf, hh = mamba_mixer(xs, g0, state_conv[j], state_ssm[j], *w)
            ssm_s.append(hh)
            conv_s.append(cbuf)
        elif kind == 1:
            w = (n_w_in[j], n_q_g[j], n_k_g[j], n_cmp_pe[j], n_cmp_w1[j], n_cmp_w2[j], n_w_out[j], rel_table)
            empty = jnp.zeros((bp, 0, 2, KV_HEADS, HEAD_DIM), dt)
            yp, rc, rs, wb = nsa_mixer(xp, g0, 0, empty, empty, empty, min(WINDOW, lp), *w)
            cmp_p.append(rc)
            sel_p.append(rs)
            win_p.append(wb)
            ys, rc, rs, wb = nsa_mixer(xs, g0, past_len, _gather_pages(cache_kv_cmp[j], page_table),
                                       _gather_pages(cache_kv_sel[j], page_table), cache_kv_win[j],
                                       cache_kv_win.shape[2], *w)
            cmp_s.append(rc)
            sel_s.append(rs)
            win_s.append(wb)
        else:
            w = (lbs[i], h_w_in[j], h_norm_g[j], h_w_out[j])
            yp, st = hgrn2_mixer(xp, g0, jnp.zeros((bp, H_HEADS, H_DK, H_DV), dt), *w)
            hg_p.append(st)
            ys, st = hgrn2_mixer(xs, g0, state_hgrn[j], *w)
            hg_s.append(st)
        xp = xp + yp
        xs = xs + ys
        mw = (moe_w_rg[i], moe_b_rg[i], moe_w_re[i], moe_b_re[i], moe_w1[i], moe_w3[i], moe_w2[i])
        xp = hier_moe_residual(xp.reshape(-1, D_MODEL), norm_g[i, 1], *mw).reshape(xp.shape)
        xs = hier_moe_residual(xs.reshape(-1, D_MODEL), norm_g[i, 1], *mw).reshape(xs.shape)
    return (xp, xs,
            jnp.stack(ssm_p), jnp.stack(conv_p), jnp.stack(cmp_p), jnp.stack(sel_p), jnp.stack(win_p), jnp.stack(hg_p),
            jnp.stack(ssm_s), jnp.stack(conv_s), jnp.stack(cmp_s), jnp.stack(sel_s), jnp.stack(win_s), jnp.stack(hg_s))
```

```python
import functools
import math

import jax
import jax.numpy as jnp
import numpy as np
from jax import lax
from jax.experimental import pallas as pl
from jax.experimental.pallas import tpu as pltpu

F32 = jnp.float32
BF16 = jnp.bfloat16
EPS = 1e-6

D_MODEL = 1024
DEPTH = 4
N_MIXERS = 3

M_DINNER = 2 * D_MODEL
M_HEADDIM = 64
M_HEADS = M_DINNER // M_HEADDIM
M_GROUPS = 4
M_DSTATE = 128
M_CONV = 4
M_GN = M_GROUPS * M_DSTATE
M_CONV_DIM = M_DINNER + 2 * M_GN
M_CHUNK = 128

N_HEADS = 16
HEAD_DIM = D_MODEL // N_HEADS
KV_HEADS = 4
HPG = N_HEADS // KV_HEADS
KV_DIM = KV_HEADS * HEAD_DIM
CMP_BLOCK = 32
CMP_STRIDE = 16
SEL_BLOCK = 64
SEL_TOPK = 16
WINDOW = 512
SEL_QBLK = 32
WIN_QBLK = 128
SCALE = HEAD_DIM ** -0.5
REL_BUCKETS = 32
REL_MAX_DIST = 128

H_DK = 128
H_HEADS = D_MODEL // H_DK
H_DV = D_MODEL // H_HEADS
H_CHUNK = 32

MOE_GROUPS = 4
MOE_EPG = 4
MOE_EXPERTS = MOE_GROUPS * MOE_EPG
MOE_TOPK = 2
MOE_FF = 512

VMEM_LIMIT_BYTES = 48 * 1024 * 1024


def _rmsnorm(x, g):
    xf = x.astype(F32)
    y = xf * lax.rsqrt(jnp.mean(xf * xf, axis=-1, keepdims=True) + EPS)
    return (y * g.astype(F32)).astype(x.dtype)


def _norm_matmul_body(x_ref, g_ref, w_ref, o_ref, *, normalize):
    x = x_ref[...]
    if normalize:
        x = x * lax.rsqrt(jnp.mean(x * x, axis=-1, keepdims=True) + EPS) * g_ref[...]
    o_ref[...] = jnp.dot(x.astype(BF16), w_ref[...].astype(BF16), preferred_element_type=F32)


def _pick_tile(n, pref):
    t = min(n, pref)
    while n % t:
        t //= 2
    return t


def norm_matmul(x, g, w, *, normalize=True, tm=512, tn=512):
    t, k = x.shape
    n_true = w.shape[1]
    tm = _pick_tile(t, tm)
    if n_true % tn:
        w = jnp.pad(w, ((0, 0), (0, tn - n_true % tn)))
    n = w.shape[1]
    if g is None:
        g = jnp.ones((k,), F32)
    out = pl.pallas_call(
        functools.partial(_norm_matmul_body, normalize=normalize),
        name="norm_matmul",
        grid=(t // tm, n // tn),
        in_specs=[
            pl.BlockSpec((tm, k), lambda i, j: (i, 0)),
            pl.BlockSpec((1, k), lambda i, j: (0, 0)),
            pl.BlockSpec((k, tn), lambda i, j: (0, j)),
        ],
        out_specs=pl.BlockSpec((tm, tn), lambda i, j: (i, j)),
        out_shape=jax.ShapeDtypeStruct((t, n), F32),
        compiler_params=pltpu.CompilerParams(
            dimension_semantics=("parallel", "arbitrary"),
            vmem_limit_bytes=VMEM_LIMIT_BYTES),
    )(x, g.reshape(1, k), w)
    return out[:, :n_true]


def _moe_body(x_ref, comb_ref, w1_ref, w3_ref, w2_ref, res_ref, o_ref, acc_ref):
    e = pl.program_id(1)

    @pl.when(e == 0)
    def _():
        acc_ref[...] = jnp.zeros_like(acc_ref)

    x = x_ref[...].astype(BF16)
    a = jnp.dot(x, w1_ref[0].astype(BF16), preferred_element_type=F32)
    b = jnp.dot(x, w3_ref[0].astype(BF16), preferred_element_type=F32)
    comb = comb_ref[...]
    lane = lax.broadcasted_iota(jnp.int32, comb.shape, 1)
    c = jnp.sum(jnp.where(lane == e, comb, 0.0), axis=1, keepdims=True)
    h = (a * jax.nn.sigmoid(a)) * b * c
    acc_ref[...] += jnp.dot(h.astype(BF16), w2_ref[0].astype(BF16), preferred_element_type=F32)

    @pl.when(e == pl.num_programs(1) - 1)
    def _():
        o_ref[...] = res_ref[...] + acc_ref[...]


def moe_dense(xn, comb, w1, w3, w2, res, *, tm=512):
    t = xn.shape[0]
    tm = _pick_tile(t, tm)
    return pl.pallas_call(
        _moe_body,
        name="moe_dense",
        grid=(t // tm, MOE_EXPERTS),
        in_specs=[
            pl.BlockSpec((tm, D_MODEL), lambda i, e: (i, 0)),
            pl.BlockSpec((tm, MOE_EXPERTS), lambda i, e: (i, 0)),
            pl.BlockSpec((1, D_MODEL, MOE_FF), lambda i, e: (e, 0, 0)),
            pl.BlockSpec((1, D_MODEL, MOE_FF), lambda i, e: (e, 0, 0)),
            pl.BlockSpec((1, MOE_FF, D_MODEL), lambda i, e: (e, 0, 0)),
            pl.BlockSpec((tm, D_MODEL), lambda i, e: (i, 0)),
        ],
        out_specs=pl.BlockSpec((tm, D_MODEL), lambda i, e: (i, 0)),
        out_shape=jax.ShapeDtypeStruct((t, D_MODEL), F32),
        scratch_shapes=[pltpu.VMEM((tm, D_MODEL), F32)],
        compiler_params=pltpu.CompilerParams(
            dimension_semantics=("parallel", "arbitrary"),
            vmem_limit_bytes=VMEM_LIMIT_BYTES),
    )(xn, comb, w1, w3, w2, res)


def hier_moe_residual(x, g, w_rg, b_rg, w_re, b_re, w1, w3, w2):
    xn = _rmsnorm(x, g)
    hi = lax.Precision.HIGHEST
    pg = jax.nn.softmax(jnp.dot(xn, w_rg, precision=hi) + b_rg, axis=-1)
    pg_top, g_idx = lax.top_k(pg, 1)
    le = (jnp.dot(xn, w_re, precision=hi) + b_re).reshape(-1, MOE_GROUPS, MOE_EPG)
    le_g = jnp.einsum('tge,tg->te', le, jax.nn.one_hot(g_idx[:, 0], MOE_GROUPS, dtype=F32), precision=hi)
    top_l, e_idx = lax.top_k(le_g, MOE_TOPK)
    wts = pg_top * jax.nn.softmax(top_l, axis=-1)
    eid = g_idx * MOE_EPG + e_idx
    combine = jnp.einsum('tk,tke->te', wts, jax.nn.one_hot(eid, MOE_EXPERTS, dtype=F32), precision=hi)
    return moe_dense(xn, combine, w1, w3, w2, x)


def _causal_dwconv(u, buf, w, b):
    L = u.shape[1]
    ext = jnp.concatenate([buf.astype(u.dtype), u], axis=1)
    out = b + sum(ext[:, k:k + L] * w[k] for k in range(M_CONV))
    return out, ext[:, L:]


def _ssd_scan(x, dt, a, bm, cm, h0):
    bsz, L = x.shape[0], x.shape[1]
    q = M_CHUNK if L % M_CHUNK == 0 else L
    nc = L // q
    hpg = M_HEADS // M_GROUPS
    xdt = (x * dt[..., None]).reshape(bsz, nc, q, M_GROUPS, hpg, M_HEADDIM)
    acum = jnp.cumsum((dt * a).reshape(bsz, nc, q, M_GROUPS, hpg), axis=2)
    bc = bm.reshape(bsz, nc, q, M_GROUPS, M_DSTATE)
    cc = cm.reshape(bsz, nc, q, M_GROUPS, M_DSTATE)
    tri = jnp.tril(jnp.ones((q, q), bool))[:, :, None, None]
    seg = acum[:, :, :, None] - acum[:, :, None]
    decay = jnp.exp(jnp.where(tri, seg, -jnp.inf))
    cb = jnp.einsum('bclgn,bcsgn->bclsg', cc, bc)
    y_diag = jnp.einsum('bclsg,bclsgh,bcsghp->bclghp', cb, decay, xdt)
    decay_end = jnp.exp(acum[:, :, -1:] - acum)
    states = jnp.einsum('bcsgn,bcsgh,bcsghp->bcghpn', bc, decay_end, xdt)
    chunk_decay = jnp.exp(acum[:, :, -1])

    def step(h, inp):
        cd, st = inp
        return cd[..., None, None] * h + st, h

    h_last, h_prev = lax.scan(step, h0.reshape(bsz, M_GROUPS, hpg, M_HEADDIM, M_DSTATE),
                              (chunk_decay.swapaxes(0, 1), states.swapaxes(0, 1)))
    y_off = jnp.einsum('bclgn,bclgh,bcghpn->bclghp', cc, jnp.exp(acum), h_prev.swapaxes(0, 1))
    y = (y_diag + y_off).reshape(bsz, L, M_HEADS, M_HEADDIM)
    return y, h_last.reshape(bsz, M_HEADS, M_HEADDIM, M_DSTATE)


def mamba_mixer(x, g, conv_buf, h0, w_in, conv_w, conv_b, dt_bias, a_log, d_skip, norm_g, w_out):
    bsz, L, _ = x.shape
    proj = norm_matmul(x.reshape(bsz * L, D_MODEL), g, w_in).reshape(bsz, L, -1)
    z = proj[..., :M_DINNER]
    xbc = proj[..., M_DINNER:M_DINNER + M_CONV_DIM]
    dt_raw = proj[..., M_DINNER + M_CONV_DIM:]
    xbc, new_buf = _causal_dwconv(xbc, conv_buf, conv_w, conv_b)
    xbc = jax.nn.silu(xbc)
    xs = xbc[..., :M_DINNER].reshape(bsz, L, M_HEADS, M_HEADDIM)
    bm = xbc[..., M_DINNER:M_DINNER + M_GN].reshape(bsz, L, M_GROUPS, M_DSTATE)
    cm = xbc[..., M_DINNER + M_GN:].reshape(bsz, L, M_GROUPS, M_DSTATE)
    dt = jax.nn.softplus(dt_raw + dt_bias)
    a = -jnp.exp(a_log)
    y, h = _ssd_scan(xs, dt, a, bm, cm, h0)
    y = y + xs * d_skip[:, None]
    y = y.reshape(bsz, L, M_DINNER)
    y = _rmsnorm(y * jax.nn.silu(z), norm_g)
    out = norm_matmul(y.reshape(bsz * L, M_DINNER), None, w_out, normalize=False)
    return out.reshape(bsz, L, D_MODEL), new_buf, h


def _rel_bucket(dist):
    exact = REL_BUCKETS // 2
    d = jnp.maximum(dist, 0)
    ratio = jnp.log(jnp.maximum(d, 1).astype(F32) / exact) / math.log(REL_MAX_DIST / exact)
    large = jnp.minimum(exact + (ratio * (REL_BUCKETS - exact)).astype(jnp.int32), REL_BUCKETS - 1)
    return jnp.where(d < exact, d, large)


def _head_bias(rel_table, dist):
    b = rel_table[_rel_bucket(dist)].astype(F32)
    return jnp.moveaxis(b, -1, 0).reshape(KV_HEADS, HPG, dist.shape[0], dist.shape[1])


def _masked_softmax(s, valid):
    s = jnp.where(valid, s, -1e30)
    e = jnp.where(valid, jnp.exp(s - jnp.max(s, axis=-1, keepdims=True)), 0.0)
    return e / jnp.maximum(jnp.sum(e, axis=-1, keepdims=True), 1e-30)


def _group_attend(qg, qpos, k, v, kpos, valid, rel_table):
    s = jnp.einsum('bqghd,bkgd->bghqk', qg, k).astype(F32) * SCALE
    s = s + _head_bias(rel_table, qpos[:, None] - kpos[None, :])
    p = _masked_softmax(s, valid)
    o = jnp.einsum('bghqk,bkgd->bqghd', p.astype(v.dtype), v)
    return o, p


def _compress(rows, pe, w1, w2):
    bsz, T = rows.shape[0], rows.shape[1]
    nc = (T - CMP_BLOCK) // CMP_STRIDE + 1
    idx = jnp.arange(nc)[:, None] * CMP_STRIDE + jnp.arange(CMP_BLOCK)[None, :]
    blk = rows[:, idx] + pe[:, None, :]
    blk = jnp.moveaxis(blk, 2, 3).reshape(bsz, nc, KV_HEADS, CMP_BLOCK * HEAD_DIM)
    return jax.nn.silu(blk @ w1) @ w2


def _select_blocks(p_cmp, qpos, n_sel):
    nc = p_cmp.shape[-1]
    cstart = jnp.arange(nc) * CMP_STRIDE
    sstart = jnp.arange(n_sel) * SEL_BLOCK
    overlap = ((cstart[:, None] < sstart[None, :] + SEL_BLOCK) &
               (cstart[:, None] + CMP_BLOCK > sstart[None, :])).astype(F32)
    imp = jnp.einsum('bghqc,cs->bgqs', p_cmp, overlap)
    qblk = qpos // SEL_BLOCK
    j = jnp.arange(n_sel)
    forced = (j[None, :] == qblk[:, None]) | (j[None, :] == 0)
    score = jnp.where(forced, 1e9, jnp.where(j[None, :] <= qblk[:, None], imp, -1e30))
    _, idx = lax.top_k(score, min(SEL_TOPK, n_sel))
    return idx


def _sel_attend(qg, qpos, idx, kblk, vblk, rel_table):
    bsz, lq = qg.shape[0], qg.shape[1]
    bi = jnp.arange(bsz)[:, None, None, None]
    gi = jnp.arange(KV_HEADS)[None, :, None, None]
    kg = kblk[bi, gi, idx]
    vg = vblk[bi, gi, idx]
    s = jnp.einsum('bqghd,bgqkrd->bghqkr', qg, kg).astype(F32) * SCALE
    kpos = idx[..., None] * SEL_BLOCK + jnp.arange(SEL_BLOCK)
    dist = qpos[:, None, None] - kpos
    g5 = jnp.arange(KV_HEADS)[None, :, None, None, None]
    bias = rel_table.reshape(REL_BUCKETS, KV_HEADS, HPG)[_rel_bucket(dist), g5].astype(F32)
    s = s + jnp.moveaxis(bias, -1, 2)
    nk = idx.shape[-1]
    valid = (dist >= 0)[:, :, None].reshape(bsz, KV_HEADS, 1, lq, nk * SEL_BLOCK)
    p = _masked_softmax(s.reshape(bsz, KV_HEADS, HPG, lq, nk * SEL_BLOCK), valid)
    p = p.reshape(bsz, KV_HEADS, HPG, lq, nk, SEL_BLOCK)
    return jnp.einsum('bghqkr,bgqkrd->bqghd', p.astype(vg.dtype), vg)


def nsa_mixer(x, g, pos0, kv_cmp_past, kv_sel_past, kv_win_past, n_keep,
              w_in, q_g, k_g, cmp_pe, cmp_w1, cmp_w2, w_out, rel_table):
    bsz, L, _ = x.shape
    sizes = [N_HEADS * HEAD_DIM] + [KV_DIM] * 6 + [3 * N_HEADS]
    proj = norm_matmul(x.reshape(bsz * L, D_MODEL), g, w_in).reshape(bsz, L, -1)
    q, kc, vc, ks, vs, kw, vw, gl = jnp.split(proj, np.cumsum(sizes)[:-1].tolist(), axis=-1)
    q = _rmsnorm(q.reshape(bsz, L, KV_HEADS, HPG, HEAD_DIM), q_g)

    def kv(t):
        return t.reshape(bsz, L, KV_HEADS, HEAD_DIM)

    new_cmp = jnp.stack([kv(kc), kv(vc)], axis=2)
    new_sel = jnp.stack([_rmsnorm(kv(ks), k_g[1]), kv(vs)], axis=2)
    new_win = jnp.stack([_rmsnorm(kv(kw), k_g[2]), kv(vw)], axis=2)
    qpos = pos0 + jnp.arange(L)

    crows = jnp.concatenate([kv_cmp_past.astype(x.dtype), new_cmp], axis=1)
    kcmp = _rmsnorm(_compress(crows[:, :, 0], cmp_pe[0], cmp_w1[0], cmp_w2[0]), k_g[0])
    vcmp = _compress(crows[:, :, 1], cmp_pe[1], cmp_w1[1], cmp_w2[1])
    ends = jnp.arange(kcmp.shape[1]) * CMP_STRIDE + CMP_BLOCK - 1
    o_cmp, p_cmp = _group_attend(q, qpos, kcmp, vcmp, ends, qpos[:, None] >= ends[None, :], rel_table)

    srows = jnp.concatenate([kv_sel_past.astype(x.dtype), new_sel], axis=1)
    T = srows.shape[1]
    n_sel = -(-T // SEL_BLOCK)
    srows = jnp.pad(srows, ((0, 0), (0, n_sel * SEL_BLOCK - T), (0, 0), (0, 0), (0, 0)))
    blocks = srows.reshape(bsz, n_sel, SEL_BLOCK, 2, KV_HEADS, HEAD_DIM).transpose(3, 0, 4, 1, 2, 5)
    idx = _select_blocks(p_cmp, qpos, n_sel)
    qb = SEL_QBLK if L % SEL_QBLK == 0 else L
    nqb = L // qb

    def sel_block(args):
        qgi, qposi, idxi = args
        return _sel_attend(qgi, qposi, idxi, blocks[0], blocks[1], rel_table)

    o_sel = lax.map(sel_block, (q.reshape(bsz, nqb, qb, KV_HEADS, HPG, HEAD_DIM).swapaxes(0, 1),
                                qpos.reshape(nqb, qb),
                                idx.reshape(bsz, KV_HEADS, nqb, qb, -1).transpose(2, 0, 1, 3, 4)))
    o_sel = o_sel.swapaxes(0, 1).reshape(bsz, L, KV_HEADS, HPG, HEAD_DIM)

    p_win = kv_win_past.shape[1]
    wrows = jnp.concatenate([kv_win_past.astype(x.dtype), new_win], axis=1)
    wpad = jnp.pad(wrows, ((0, 0), (WINDOW, 0), (0, 0), (0, 0), (0, 0)))
    n_all = WINDOW + p_win + L
    kpos_all = pos0 - p_win - WINDOW + jnp.arange(n_all)
    kvalid_all = jnp.arange(n_all) >= WINDOW
    wq = WIN_QBLK if L % WIN_QBLK == 0 else L

    def win_block(i):
        start = p_win + i * wq
        qgi = lax.dynamic_slice_in_dim(q, i * wq, wq, axis=1)
        kvi = lax.dynamic_slice_in_dim(wpad, start, WINDOW + wq, axis=1)
        kposi = lax.dynamic_slice_in_dim(kpos_all, start, WINDOW + wq)
        kvalidi = lax.dynamic_slice_in_dim(kvalid_all, start, WINDOW + wq)
        qposi = pos0 + i * wq + jnp.arange(wq)
        dist = qposi[:, None] - kposi[None, :]
        valid = kvalidi[None, :] & (dist >= 0) & (dist <= WINDOW)
        o, _ = _group_attend(qgi, qposi, kvi[:, :, 0], kvi[:, :, 1], kposi, valid, rel_table)
        return o

    o_win = lax.map(win_block, jnp.arange(L // wq)).swapaxes(0, 1).reshape(bsz, L, KV_HEADS, HPG, HEAD_DIM)

    gate = jax.nn.sigmoid(gl).reshape(bsz, L, KV_HEADS, HPG, 3)
    o = gate[..., 0:1] * o_cmp + gate[..., 1:2] * o_sel + gate[..., 2:3] * o_win
    y = norm_matmul(o.reshape(bsz * L, N_HEADS * HEAD_DIM), None, w_out, normalize=False)
    return y.reshape(bsz, L, D_MODEL), new_cmp, new_sel, wrows[:, -n_keep:]


ATT_TQ = 128
ATT_TK = 128
NEG = -1e30


def _nt_dot(a, b):
    return lax.dot_general(a, b, (((1,), (1,)), ((), ())), preferred_element_type=F32)


def _compress_body(rk_ref, rv_ref, pe_ref, w1_ref, w2_ref, kg_ref, kc_ref, vc_ref, *, nb):
    half = (CMP_BLOCK // 2) * HEAD_DIM
    for kv, (r_ref, o_ref) in enumerate(((rk_ref, kc_ref), (rv_ref, vc_ref))):
        lo = (r_ref[0, 0, 0:nb, :] + pe_ref[kv, 0:1, :]).astype(BF16)
        hi = (r_ref[0, 0, 1:nb + 1, :] + pe_ref[kv, 1:2, :]).astype(BF16)
        h = (jnp.dot(lo, w1_ref[kv, :half, :].astype(BF16), preferred_element_type=F32)
             + jnp.dot(hi, w1_ref[kv, half:, :].astype(BF16), preferred_element_type=F32))
        h = h * jax.nn.sigmoid(h)
        o = jnp.dot(h.astype(BF16), w2_ref[kv].astype(BF16), preferred_element_type=F32)
        if kv == 0:
            o = o * lax.rsqrt(jnp.mean(o * o, axis=-1, keepdims=True) + EPS) * kg_ref[...]
        o_ref[0, 0] = o


def compress_rows(rk, rv, pe, w1, w2, kg):
    bsz, g, nbp, width = rk.shape
    nb = nbp - 8
    strip = pl.BlockSpec((1, 1, nbp, width), lambda b, j: (b, j, 0, 0))
    out = pl.BlockSpec((1, 1, nb, HEAD_DIM), lambda b, j: (b, j, 0, 0))
    full = lambda a: pl.BlockSpec(a.shape, lambda b, j: (0,) * a.ndim)
    pe2 = pe.reshape(2, 2, width)
    kg2 = kg.reshape(1, HEAD_DIM)
    return pl.pallas_call(
        functools.partial(_compress_body, nb=nb),
        name="nsa_compress",
        grid=(bsz, g),
        in_specs=[strip, strip, full(pe2), full(w1), full(w2), full(kg2)],
        out_specs=[out, out],
        out_shape=[jax.ShapeDtypeStruct((bsz, g, nb, HEAD_DIM), F32)] * 2,
        compiler_params=pltpu.CompilerParams(
            dimension_semantics=("parallel", "parallel"),
            vmem_limit_bytes=VMEM_LIMIT_BYTES),
    )(rk, rv, pe2, w1, w2, kg2)


def _split3(x):
    a = x.astype(BF16)
    r = x - a.astype(F32)
    b = r.astype(BF16)
    c = (r - b.astype(F32)).astype(BF16)
    return a, b, c


def _nsa_attn_body(q_ref, kc_ref, vc_ref, ks_ref, vs_ref, kw_ref, vw_ref, gate_ref, bcmp_ref, btile_ref,
                   ovl_ref, o_ref, *, n_cmp, n_sel, topk):
    i = pl.program_id(2)
    q0 = i * ATT_TQ
    rows = HPG * ATT_TQ
    qt = q_ref[0]
    qb = jnp.concatenate([qt[:, h * HEAD_DIM:(h + 1) * HEAD_DIM] for h in range(HPG)], axis=0).astype(BF16)
    row = lax.broadcasted_iota(jnp.int32, (rows, ATT_TK), 0)
    qpos = q0 + (row & (ATT_TQ - 1))
    col = lax.broadcasted_iota(jnp.int32, (rows, ATT_TK), 1)

    s = _nt_dot(qb, kc_ref[0, 0].astype(BF16)) * SCALE + bcmp_ref[0, 0]
    valid = (qpos >= col * CMP_STRIDE + (CMP_BLOCK - 1)) & (col < n_cmp)
    s = jnp.where(valid, s, NEG)
    e = jnp.where(valid, jnp.exp(s - jnp.max(s, axis=-1, keepdims=True)), 0.0)
    p = e / jnp.maximum(jnp.sum(e, axis=-1, keepdims=True), 1e-30)
    pb = p.astype(BF16)
    o_cmp = jnp.dot(pb, vc_ref[0, 0].astype(BF16), preferred_element_type=F32)

    pf = pb.astype(F32)
    psum = pf[0:ATT_TQ]
    for h in range(1, HPG):
        psum = psum + pf[h * ATT_TQ:(h + 1) * ATT_TQ]
    ovl = ovl_ref[...]
    imp_t = sum(_nt_dot(ovl, part) for part in _split3(psum))
    nblk = imp_t.shape[0]
    j = lax.broadcasted_iota(jnp.int32, (nblk, ATT_TQ), 0)
    qblk = (q0 + lax.broadcasted_iota(jnp.int32, (nblk, ATT_TQ), 1)) // SEL_BLOCK
    forced = (j == qblk) | (j == 0)
    score = jnp.where(forced, 1e9, jnp.where(j <= qblk, imp_t, NEG))
    score = jnp.where(j < n_sel, score, -3e38)
    rank = jnp.zeros((nblk, ATT_TQ), F32)
    for jp in range(n_sel):
        other = score[jp:jp + 1, :]
        beats = (other > score) | ((other == score) & (jp < j))
        rank = rank + jnp.where(beats, 1.0, 0.0)
    sel = jnp.where((rank < topk) & (j < n_sel), 1.0, 0.0).T.astype(BF16)

    jj = lax.broadcasted_iota(jnp.int32, (nblk, ATT_TK), 0)
    kk = lax.broadcasted_iota(jnp.int32, (nblk, ATT_TK), 1) // SEL_BLOCK
    blocks_per_step = ATT_TK // SEL_BLOCK

    def flash(k_ref, v_ref, lo, hi, valid_fn):
        def step(kc, carry):
            m, l, acc = carry
            start = pl.multiple_of(kc * ATT_TK, ATT_TK)
            kblk = k_ref[0, 0, pl.ds(start, ATT_TK), :].astype(BF16)
            vblk = v_ref[0, 0, pl.ds(start, ATT_TK), :].astype(BF16)
            sc = _nt_dot(qb, kblk) * SCALE + btile_ref[0, jnp.minimum(i - kc, 2)]
            ok = valid_fn(kc, qpos - (kc * ATT_TK + col))
            sc = jnp.where(ok, sc, NEG)
            m_new = jnp.maximum(m, jnp.max(sc, axis=-1, keepdims=True))
            alpha = jnp.exp(m - m_new)
            pr = jnp.where(ok, jnp.exp(sc - m_new), 0.0)
            l = alpha * l + jnp.sum(pr, axis=-1, keepdims=True)
            acc = alpha * acc + jnp.dot(pr.astype(BF16), vblk, preferred_element_type=F32)
            return m_new, l, acc

        init = (jnp.full((rows, 1), NEG, F32), jnp.zeros((rows, 1), F32), jnp.zeros((rows, HEAD_DIM), F32))
        _, l, acc = lax.fori_loop(lo, hi, step, init)
        return acc / jnp.maximum(l, 1e-30)

    def sel_valid(kc, dist):
        expand = jnp.where(jj == kc * blocks_per_step + kk, 1.0, 0.0).astype(BF16)
        chosen = jnp.dot(sel, expand, preferred_element_type=F32)
        chosen = jnp.concatenate([chosen] * HPG, axis=0)
        return (chosen > 0.5) & (dist >= 0)

    def win_valid(kc, dist):
        return (dist >= 0) & (dist <= WINDOW)

    o_sel = flash(ks_ref, vs_ref, 0, i + 1, sel_valid)
    o_win = flash(kw_ref, vw_ref, jnp.maximum(i - WINDOW // ATT_TK, 0), i + 1, win_valid)

    gate = jax.nn.sigmoid(gate_ref[0, 0])

    def gcol(br):
        return jnp.concatenate([gate[:, h * 3 + br:h * 3 + br + 1] for h in range(HPG)], axis=0)

    o = gcol(0) * o_cmp + gcol(1) * o_sel + gcol(2) * o_win
    o_ref[0] = jnp.concatenate([o[h * ATT_TQ:(h + 1) * ATT_TQ] for h in range(HPG)], axis=1)


def _bias_tables(rel_table, n_qtiles):
    tab = rel_table[_rel_bucket(jnp.arange(REL_MAX_DIST + 1))].astype(F32)
    tab = tab.T.reshape(KV_HEADS, HPG, REL_MAX_DIST + 1)
    t = np.arange(ATT_TQ)[:, None]
    s = np.arange(ATT_TK)[None, :]
    d_tile = np.stack([np.clip(delta * ATT_TK + t - s, 0, REL_MAX_DIST) for delta in range(3)])
    btile = tab[:, :, d_tile]
    btile = btile.transpose(0, 2, 1, 3, 4).reshape(KV_HEADS, 3, HPG * ATT_TQ, ATT_TK)
    qi = np.arange(n_qtiles)[:, None, None] * ATT_TQ
    d_cmp = np.clip(qi + t[None] - (s[None] * CMP_STRIDE + CMP_BLOCK - 1), 0, REL_MAX_DIST)
    bcmp = tab[:, :, d_cmp]
    bcmp = bcmp.transpose(2, 0, 1, 3, 4).reshape(n_qtiles, KV_HEADS, HPG * ATT_TQ, ATT_TK)
    return btile, bcmp


def nsa_prompt_attention(q, kcmp, vcmp, ks, vs, kw, vw, gl, rel_table):
    bsz, L, _ = q.shape
    assert L % ATT_TQ == 0 and kcmp.shape[2] == ATT_TK
    nq = L // ATT_TQ
    n_cmp = (L - CMP_BLOCK) // CMP_STRIDE + 1
    n_sel = L // SEL_BLOCK
    assert n_sel <= ATT_TK
    btile, bcmp = _bias_tables(rel_table, nq)
    c = np.arange(ATT_TK)[None, :] * CMP_STRIDE
    sb = np.arange(ATT_TK)[:, None] * SEL_BLOCK
    ovl = ((c < sb + SEL_BLOCK) & (c + CMP_BLOCK > sb) & (np.arange(ATT_TK)[None, :] < n_cmp)
           & (np.arange(ATT_TK)[:, None] < n_sel))
    ovl = jnp.asarray(ovl, BF16)
    width = HPG * HEAD_DIM
    kvspec = lambda n: pl.BlockSpec((1, 1, n, HEAD_DIM), lambda b, g, i: (b, g, 0, 0))
    return pl.pallas_call(
        functools.partial(_nsa_attn_body, n_cmp=n_cmp, n_sel=n_sel, topk=min(SEL_TOPK, n_sel)),
        name="nsa_attention",
        grid=(bsz, KV_HEADS, nq),
        in_specs=[
            pl.BlockSpec((1, ATT_TQ, width), lambda b, g, i: (b, i, g)),
            kvspec(ATT_TK), kvspec(ATT_TK), kvspec(L), kvspec(L), kvspec(L), kvspec(L),
            pl.BlockSpec((1, 1, ATT_TQ, HPG * 3), lambda b, g, i: (b, g, i, 0)),
            pl.BlockSpec((1, 1, HPG * ATT_TQ, ATT_TK), lambda b, g, i: (i, g, 0, 0)),
            pl.BlockSpec((1, 3, HPG * ATT_TQ, ATT_TK), lambda b, g, i: (g, 0, 0, 0)),
            pl.BlockSpec((ATT_TK, ATT_TK), lambda b, g, i: (0, 0)),
        ],
        out_specs=pl.BlockSpec((1, ATT_TQ, width), lambda b, g, i: (b, i, g)),
        out_shape=jax.ShapeDtypeStruct((bsz, L, N_HEADS * HEAD_DIM), F32),
        compiler_params=pltpu.CompilerParams(
            dimension_semantics=("parallel", "parallel", "arbitrary"),
            vmem_limit_bytes=VMEM_LIMIT_BYTES),
    )(q, kcmp, vcmp, ks, vs, kw, vw, gl, bcmp, btile, ovl)


def nsa_prompt_mixer(x, g, w_in, q_g, k_g, cmp_pe, cmp_w1, cmp_w2, w_out, rel_table):
    bsz, L, _ = x.shape
    t = bsz * L
    proj = norm_matmul(x.reshape(t, D_MODEL), g, w_in)
    c0 = N_HEADS * HEAD_DIM
    q = _rmsnorm(proj[:, :c0].reshape(t, N_HEADS, HEAD_DIM), q_g).reshape(bsz, L, c0)

    def kvpair(k, normed_g):
        kcols = proj[:, c0 + 2 * k * KV_DIM:c0 + (2 * k + 1) * KV_DIM]
        vcols = proj[:, c0 + (2 * k + 1) * KV_DIM:c0 + (2 * k + 2) * KV_DIM]
        if normed_g is not None:
            kcols = _rmsnorm(kcols.reshape(t, KV_HEADS, HEAD_DIM), normed_g).reshape(t, KV_DIM)
        rows = jnp.stack([kcols, vcols], axis=1).reshape(bsz, L, 2, KV_HEADS, HEAD_DIM)
        per_group = rows.transpose(2, 0, 3, 1, 4)
        return rows, per_group[0], per_group[1]

    new_cmp, kc_rows, vc_rows = kvpair(0, None)
    new_sel, ks, vs = kvpair(1, k_g[1])
    new_win, kw, vw = kvpair(2, k_g[2])
    gl = proj[:, c0 + 6 * KV_DIM:].reshape(bsz, L, KV_HEADS, HPG * 3).transpose(0, 2, 1, 3)

    nb = L // CMP_STRIDE

    def strips(r):
        r = r.reshape(bsz, KV_HEADS, nb, CMP_STRIDE * HEAD_DIM)
        return jnp.pad(r, ((0, 0), (0, 0), (0, ATT_TK + 8 - nb), (0, 0)))

    kcmp, vcmp = compress_rows(strips(kc_rows), strips(vc_rows), cmp_pe, cmp_w1, cmp_w2, k_g[0])
    o = nsa_prompt_attention(q, kcmp, vcmp, ks, vs, kw, vw, gl, rel_table)
    y = norm_matmul(o.reshape(t, c0), None, w_out, normalize=False)
    return y.reshape(bsz, L, D_MODEL), new_cmp, new_sel, new_win[:, -min(WINDOW, L):]


def _gla_chunked(q, k, v, logf, s0):
    bsz, L = q.shape[0], q.shape[1]
    nc = -(-L // H_CHUNK)
    pad = nc * H_CHUNK - L

    def prep(t):
        t = jnp.pad(t, ((0, 0), (0, pad), (0, 0), (0, 0)))
        return t.reshape(bsz, nc, H_CHUNK, H_HEADS, t.shape[-1])

    q, k, v, logf = prep(q), prep(k), prep(v), prep(logf)
    acum = jnp.cumsum(logf, axis=2)
    alast = acum[:, :, -1:]
    qe = q * jnp.exp(acum)
    ke = k * jnp.exp(-acum)
    kd = k * jnp.exp(alast - acum)
    tri = jnp.tril(jnp.ones((H_CHUNK, H_CHUNK), bool))
    att = jnp.where(tri, jnp.einsum('bcthk,bcshk->bchts', qe, ke), 0.0)
    o_intra = jnp.einsum('bchts,bcshv->bcthv', att, v)
    upd = jnp.einsum('bcshk,bcshv->bchkv', kd, v)

    def step(s, inp):
        dec, up = inp
        return dec[..., None] * s + up, s

    s_last, s_prev = lax.scan(step, s0, (jnp.exp(alast[:, :, 0]).swapaxes(0, 1), upd.swapaxes(0, 1)))
    o_inter = jnp.einsum('bcthk,bchkv->bcthv', qe, s_prev.swapaxes(0, 1))
    o = (o_intra + o_inter).reshape(bsz, nc * H_CHUNK, H_HEADS, H_DV)[:, :L]
    return o, s_last


def hgrn2_mixer(x, g, s0, lb, w_in, norm_g, w_out):
    bsz, L, _ = x.shape
    proj = norm_matmul(x.reshape(bsz * L, D_MODEL), g, w_in).reshape(bsz, L, -1)
    q, fr, iv, gate = jnp.split(proj, 4, axis=-1)
    q = jax.nn.silu(q).reshape(bsz, L, H_HEADS, H_DK)
    forget = lb + (1.0 - lb) * jax.nn.sigmoid(fr)
    logf = jnp.log(forget).reshape(bsz, L, H_HEADS, H_DK)
    k = (1.0 - forget).reshape(bsz, L, H_HEADS, H_DK)
    v = iv.reshape(bsz, L, H_HEADS, H_DV)
    o, s = _gla_chunked(q, k, v, logf, s0)
    o = _rmsnorm(o, norm_g) * jax.nn.silu(gate.reshape(bsz, L, H_HEADS, H_DV))
    y = norm_matmul(o.reshape(bsz * L, D_MODEL), None, w_out, normalize=False)
    return y.reshape(bsz, L, D_MODEL), s


def _gather_pages(cache, page_table):
    pages = cache[page_table]
    b, n, p = pages.shape[0], pages.shape[1], pages.shape[2]
    return pages.reshape(b, n * p, *pages.shape[3:])


def kernel(x_prompt, x_sample, state_ssm, state_conv, cache_kv_cmp, cache_kv_sel, cache_kv_win, state_hgrn, page_table, norm_g, rel_table, m_w_in, m_conv_w, m_conv_b, m_dt_bias, m_a_log, m_d, m_norm_g, m_w_out, n_w_in, n_q_g, n_k_g, n_cmp_pe, n_cmp_w1, n_cmp_w2, n_w_out, h_w_in, h_lb, h_norm_g, h_w_out, moe_w_rg, moe_b_rg, moe_w_re, moe_b_re, moe_w1, moe_w3, moe_w2):
    bp, lp = x_prompt.shape[0], x_prompt.shape[1]
    past_len = page_table.shape[1] * cache_kv_cmp.shape[2]
    dt = x_prompt.dtype
    lbs = jax.nn.softmax(h_lb.astype(F32), axis=0)
    lbs = jnp.cumsum(lbs, axis=0) - lbs[0]
    xp, xs = x_prompt, x_sample
    ssm_p, conv_p, cmp_p, sel_p, win_p, hg_p = [], [], [], [], [], []
    ssm_s, conv_s, cmp_s, sel_s, win_s, hg_s = [], [], [], [], [], []
    for i in range(DEPTH):
        kind, j = i % N_MIXERS, i // N_MIXERS
        g0 = norm_g[i, 0]
        if kind == 0:
            w = (m_w_in[j], m_conv_w[j], m_conv_b[j], m_dt_bias[j], m_a_log[j], m_d[j], m_norm_g[j], m_w_out[j])
            yp, cbuf, hh = mamba_mixer(xp, g0, jnp.zeros((bp, M_CONV - 1, M_CONV_DIM), dt),
                                       jnp.zeros((bp, M_HEADS, M_HEADDIM, M_DSTATE), dt), *w)
            ssm_p.append(hh)
            conv_p.append(cbuf)
            ys, cbuf, hh = mamba_mixer(xs, g0, state_conv[j], state_ssm[j], *w)
            ssm_s.append(hh)
            conv_s.append(cbuf)
        elif kind == 1:
            w = (n_w_in[j], n_q_g[j], n_k_g[j], n_cmp_pe[j], n_cmp_w1[j], n_cmp_w2[j], n_w_out[j], rel_table)
            yp, rc, rs, wb = nsa_prompt_mixer(xp, g0, *w)
            cmp_p.append(rc)
            sel_p.append(rs)
            win_p.append(wb)
            ys, rc, rs, wb = nsa_mixer(xs, g0, past_len, _gather_pages(cache_kv_cmp[j], page_table),
                                       _gather_pages(cache_kv_sel[j], page_table), cache_kv_win[j],
                                       cache_kv_win.shape[2], *w)
            cmp_s.append(rc)
            sel_s.append(rs)
            win_s.append(wb)
        else:
            w = (lbs[i], h_w_in[j], h_norm_g[j], h_w_out[j])
            yp, st = hgrn2_mixer(xp, g0, jnp.zeros((bp, H_HEADS, H_DK, H_DV), dt), *w)
            hg_p.append(st)
            ys, st = hgrn2_mixer(xs, g0, state_hgrn[j], *w)
            hg_s.append(st)
        xp = xp + yp
        xs = xs + ys
        mw = (moe_w_rg[i], moe_b_rg[i], moe_w_re[i], moe_b_re[i], moe_w1[i], moe_w3[i], moe_w2[i])
        xp = hier_moe_residual(xp.reshape(-1, D_MODEL), norm_g[i, 1], *mw).reshape(xp.shape)
        xs = hier_moe_residual(xs.reshape(-1, D_MODEL), norm_g[i, 1], *mw).reshape(xs.shape)
    return (xp, xs,
            jnp.stack(ssm_p), jnp.stack(conv_p), jnp.stack(cmp_p), jnp.stack(sel_p), jnp.stack(win_p), jnp.stack(hg_p),
            jnp.stack(ssm_s), jnp.stack(conv_s), jnp.stack(cmp_s), jnp.stack(sel_s), jnp.stack(win_s), jnp.stack(hg_s))
```

```python
import functools
import math

import jax
import jax.numpy as jnp
import numpy as np
from jax import lax
from jax.experimental import pallas as pl
from jax.experimental.pallas import tpu as pltpu

F32 = jnp.float32
BF16 = jnp.bfloat16
EPS = 1e-6

D_MODEL = 1024
DEPTH = 4
N_MIXERS = 3

M_DINNER = 2 * D_MODEL
M_HEADDIM = 64
M_HEADS = M_DINNER // M_HEADDIM
M_GROUPS = 4
M_DSTATE = 128
M_CONV = 4
M_GN = M_GROUPS * M_DSTATE
M_CONV_DIM = M_DINNER + 2 * M_GN
M_CHUNK = 128

N_HEADS = 16
HEAD_DIM = D_MODEL // N_HEADS
KV_HEADS = 4
HPG = N_HEADS // KV_HEADS
KV_DIM = KV_HEADS * HEAD_DIM
CMP_BLOCK = 32
CMP_STRIDE = 16
SEL_BLOCK = 64
SEL_TOPK = 16
WINDOW = 512
SEL_QBLK = 32
WIN_QBLK = 128
SCALE = HEAD_DIM ** -0.5
REL_BUCKETS = 32
REL_MAX_DIST = 128

H_DK = 128
H_HEADS = D_MODEL // H_DK
H_DV = D_MODEL // H_HEADS
H_CHUNK = 32

MOE_GROUPS = 4
MOE_EPG = 4
MOE_EXPERTS = MOE_GROUPS * MOE_EPG
MOE_TOPK = 2
MOE_FF = 512

VMEM_LIMIT_BYTES = 48 * 1024 * 1024


def _rmsnorm(x, g):
    xf = x.astype(F32)
    y = xf * lax.rsqrt(jnp.mean(xf * xf, axis=-1, keepdims=True) + EPS)
    return (y * g.astype(F32)).astype(x.dtype)


LANES = 128


def _norm_matmul_body(*refs, normalize, has_res):
    if has_res:
        x_ref, g_ref, w_ref, res_ref, o_ref, xb_ref = refs
    else:
        x_ref, g_ref, w_ref, o_ref, xb_ref = refs

    @pl.when(pl.program_id(1) == 0)
    def _():
        x = x_ref[...].astype(F32)
        if normalize:
            x = x * lax.rsqrt(jnp.mean(x * x, axis=-1, keepdims=True) + EPS) * g_ref[...]
        xb_ref[...] = x.astype(BF16)

    acc = jnp.dot(xb_ref[...], w_ref[...].astype(BF16), preferred_element_type=F32)
    if has_res:
        acc = acc + res_ref[...]
    o_ref[...] = acc


def _pick_tile(n, pref):
    t = min(n, pref)
    while n % t:
        t //= 2
    return t


def norm_matmul(x, g, w, *, normalize=True, res=None, tm=None, tn=512):
    t, k = x.shape
    n_true = w.shape[1]
    tm = _pick_tile(t, tm or (1024 if k <= 1024 else 512))
    if n_true % LANES:
        assert res is None
        w = jnp.pad(w, ((0, 0), (0, LANES - n_true % LANES)))
    n = w.shape[1]
    tn = _pick_tile(n, tn)
    if g is None:
        g = jnp.ones((k,), F32)
    in_specs = [
        pl.BlockSpec((tm, k), lambda i, j: (i, 0)),
        pl.BlockSpec((1, k), lambda i, j: (0, 0)),
        pl.BlockSpec((k, tn), lambda i, j: (0, j)),
    ]
    args = [x, g.reshape(1, k), w]
    if res is not None:
        in_specs.append(pl.BlockSpec((tm, tn), lambda i, j: (i, j)))
        args.append(res)
    out = pl.pallas_call(
        functools.partial(_norm_matmul_body, normalize=normalize, has_res=res is not None),
        name="norm_matmul",
        grid=(t // tm, n // tn),
        in_specs=in_specs,
        out_specs=pl.BlockSpec((tm, tn), lambda i, j: (i, j)),
        out_shape=jax.ShapeDtypeStruct((t, n), F32),
        scratch_shapes=[pltpu.VMEM((tm, k), BF16)],
        compiler_params=pltpu.CompilerParams(
            dimension_semantics=("parallel", "arbitrary"),
            vmem_limit_bytes=VMEM_LIMIT_BYTES),
    )(*args)
    return out if n == n_true else out[:, :n_true]


ROUTE_LANES = 128
MOE_TILE = 1024
MOE_ROWS = 160
NEG = -1e30


def _router_body(x_ref, g_ref, w_ref, b_ref, u_ref, xn_ref, rank_ref, wt_ref):
    x = x_ref[...]
    tm = x.shape[0]
    xn = x * lax.rsqrt(jnp.mean(x * x, axis=-1, keepdims=True) + EPS) * g_ref[...]
    xn_ref[...] = xn.astype(BF16)
    logits = jnp.dot(xn, w_ref[...], precision=lax.Precision.HIGHEST, preferred_element_type=F32) + b_ref[...]
    lane = lax.broadcasted_iota(jnp.int32, (tm, ROUTE_LANES), 1).astype(F32)

    def first_max(mask):
        v = jnp.max(jnp.where(mask, logits, NEG), axis=-1, keepdims=True)
        i = jnp.min(jnp.where(mask & (logits == v), lane, float(ROUTE_LANES)), axis=-1, keepdims=True)
        return v, i

    is_group = lane < MOE_GROUPS
    mg, g_idx = first_max(is_group)
    pg_top = 1.0 / jnp.sum(jnp.where(is_group, jnp.exp(logits - mg), 0.0), axis=-1, keepdims=True)
    lo = MOE_GROUPS + MOE_EPG * g_idx
    in_group = (lane >= lo) & (lane < lo + MOE_EPG)
    v1, i1 = first_max(in_group)
    v2, i2 = first_max(in_group & (lane != i1))
    e2 = jnp.exp(v2 - v1)
    w_a = pg_top / (1.0 + e2)
    w_b = pg_top * e2 / (1.0 + e2)
    info = jnp.where(lane == 0, i1 - MOE_GROUPS, jnp.where(lane == 1, i2 - MOE_GROUPS,
                     jnp.where(lane == 2, w_a, jnp.where(lane == 3, w_b, 0.0))))
    info_t = info.T
    e_a, e_b, w_at, w_bt = info_t[0:1], info_t[1:2], info_t[2:3], info_t[3:4]
    expert = lax.broadcasted_iota(jnp.int32, (MOE_EXPERTS, tm), 0).astype(F32)
    m_a = e_a == expert
    m_b = e_b == expert
    onehot = jnp.concatenate([jnp.where(m_a, 1.0, 0.0), jnp.where(m_b, 1.0, 0.0)], axis=0).astype(BF16)
    cum = jnp.dot(onehot, u_ref[...], preferred_element_type=F32)
    cum_a, cum_b = cum[:MOE_EXPERTS], cum[MOE_EXPERTS:]
    n_a = cum_a[:, tm - 1:tm]
    rank_ref[...] = jnp.where(m_a, cum_a - 1.0, jnp.where(m_b, n_a + cum_b - 1.0, -1.0))
    wt_ref[...] = jnp.where(m_a, w_at, jnp.where(m_b, w_bt, 0.0))


def moe_route(x, g, w_rg, b_rg, w_re, b_re, tm):
    t = x.shape[0]
    pad = ROUTE_LANES - MOE_GROUPS - MOE_EXPERTS
    w = jnp.pad(jnp.concatenate([w_rg, w_re], axis=1), ((0, 0), (0, pad)))
    b = jnp.pad(jnp.concatenate([b_rg, b_re]), (0, pad)).reshape(1, ROUTE_LANES)
    upper = jnp.asarray(np.triu(np.ones((tm, tm), np.float32)), BF16)
    const = lambda a: pl.BlockSpec(a.shape, lambda i: (0,) * a.ndim)
    g2 = g.reshape(1, D_MODEL)
    return pl.pallas_call(
        _router_body,
        name="moe_router",
        grid=(t // tm,),
        in_specs=[pl.BlockSpec((tm, D_MODEL), lambda i: (i, 0)), const(g2), const(w), const(b), const(upper)],
        out_specs=[pl.BlockSpec((tm, D_MODEL), lambda i: (i, 0)),
                   pl.BlockSpec((MOE_EXPERTS, tm), lambda i: (0, i)),
                   pl.BlockSpec((MOE_EXPERTS, tm), lambda i: (0, i))],
        out_shape=[jax.ShapeDtypeStruct((t, D_MODEL), BF16),
                   jax.ShapeDtypeStruct((MOE_EXPERTS, t), F32),
                   jax.ShapeDtypeStruct((MOE_EXPERTS, t), F32)],
        compiler_params=pltpu.CompilerParams(
            dimension_semantics=("parallel",), vmem_limit_bytes=VMEM_LIMIT_BYTES),
    )(x, g2, w, b, upper)


def _moe_expert_body(cnt_ref, xn_ref, rank_ref, wt_ref, w1_ref, w3_ref, w2_ref, res_ref, o_ref, *, rows):
    ti, e = pl.program_id(0), pl.program_id(1)

    @pl.when(e == 0)
    def _():
        o_ref[...] = res_ref[...]

    tm = xn_ref.shape[0]
    rank = rank_ref[pl.ds(e, 1), :]
    wt = wt_ref[pl.ds(e, 1), :]
    n_chunks = (cnt_ref[e, ti] + rows - 1) // rows

    def chunk(c, carry):
        r = (lax.broadcasted_iota(jnp.int32, (rows, tm), 0) + c * rows).astype(F32)
        sel = jnp.where(rank == r, 1.0, 0.0)
        selb = sel.astype(BF16)
        xs = jnp.dot(selb, xn_ref[...], preferred_element_type=F32).astype(BF16)
        a = jnp.dot(xs, w1_ref[0].astype(BF16), preferred_element_type=F32)
        b = jnp.dot(xs, w3_ref[0].astype(BF16), preferred_element_type=F32)
        h = (a * jax.nn.sigmoid(a) * b).astype(BF16)
        y = jnp.dot(h, w2_ref[0].astype(BF16), preferred_element_type=F32)
        y = (y * jnp.sum(sel * wt, axis=1, keepdims=True)).astype(BF16)
        o_ref[...] += lax.dot_general(selb, y, (((0,), (0,)), ((), ())), preferred_element_type=F32)
        return carry

    lax.fori_loop(0, n_chunks, chunk, 0)


def moe_experts(xn, rank, wt, counts, w1, w3, w2, res, tm):
    t = xn.shape[0]
    rows = min(MOE_ROWS, tm)
    grid_spec = pltpu.PrefetchScalarGridSpec(
        num_scalar_prefetch=1,
        grid=(t // tm, MOE_EXPERTS),
        in_specs=[
            pl.BlockSpec((tm, D_MODEL), lambda i, e, c: (i, 0)),
            pl.BlockSpec((MOE_EXPERTS, tm), lambda i, e, c: (0, i)),
            pl.BlockSpec((MOE_EXPERTS, tm), lambda i, e, c: (0, i)),
            pl.BlockSpec((1, D_MODEL, MOE_FF), lambda i, e, c: (e, 0, 0)),
            pl.BlockSpec((1, D_MODEL, MOE_FF), lambda i, e, c: (e, 0, 0)),
            pl.BlockSpec((1, MOE_FF, D_MODEL), lambda i, e, c: (e, 0, 0)),
            pl.BlockSpec((tm, D_MODEL), lambda i, e, c: (i, 0)),
        ],
        out_specs=pl.BlockSpec((tm, D_MODEL), lambda i, e, c: (i, 0)),
    )
    return pl.pallas_call(
        functools.partial(_moe_expert_body, rows=rows),
        name="moe_experts",
        grid_spec=grid_spec,
        out_shape=jax.ShapeDtypeStruct((t, D_MODEL), F32),
        compiler_params=pltpu.CompilerParams(
            dimension_semantics=("parallel", "arbitrary"),
            vmem_limit_bytes=VMEM_LIMIT_BYTES),
    )(counts, xn, rank, wt, w1, w3, w2, res)


def hier_moe_residual(x, g, w_rg, b_rg, w_re, b_re, w1, w3, w2):
    t_true = x.shape[0]
    tm = MOE_TILE if t_true % MOE_TILE == 0 else ROUTE_LANES
    if t_true % tm:
        x = jnp.pad(x, ((0, tm - t_true % tm), (0, 0)))
    t = x.shape[0]
    xn, rank, wt = moe_route(x, g, w_rg, b_rg, w_re, b_re, tm)
    counts = jnp.sum((rank >= 0).reshape(MOE_EXPERTS, t // tm, tm), axis=-1, dtype=jnp.int32)
    return moe_experts(xn, rank, wt, counts, w1, w3, w2, x, tm)[:t_true]


def _causal_dwconv(u, buf, w, b):
    L = u.shape[1]
    ext = jnp.concatenate([buf.astype(u.dtype), u], axis=1)
    out = b + sum(ext[:, k:k + L] * w[k] for k in range(M_CONV))
    return out, ext[:, L:]


def _ssd_scan(x, dt, a, bm, cm, h0):
    bsz, L = x.shape[0], x.shape[1]
    q = M_CHUNK if L % M_CHUNK == 0 else L
    nc = L // q
    hpg = M_HEADS // M_GROUPS
    xdt = (x * dt[..., None]).reshape(bsz, nc, q, M_GROUPS, hpg, M_HEADDIM)
    acum = jnp.cumsum((dt * a).reshape(bsz, nc, q, M_GROUPS, hpg), axis=2)
    bc = bm.reshape(bsz, nc, q, M_GROUPS, M_DSTATE)
    cc = cm.reshape(bsz, nc, q, M_GROUPS, M_DSTATE)
    tri = jnp.tril(jnp.ones((q, q), bool))[:, :, None, None]
    seg = acum[:, :, :, None] - acum[:, :, None]
    decay = jnp.exp(jnp.where(tri, seg, -jnp.inf))
    cb = jnp.einsum('bclgn,bcsgn->bclsg', cc, bc)
    y_diag = jnp.einsum('bclsg,bclsgh,bcsghp->bclghp', cb, decay, xdt)
    decay_end = jnp.exp(acum[:, :, -1:] - acum)
    states = jnp.einsum('bcsgn,bcsgh,bcsghp->bcghpn', bc, decay_end, xdt)
    chunk_decay = jnp.exp(acum[:, :, -1])

    def step(h, inp):
        cd, st = inp
        return cd[..., None, None] * h + st, h

    h_last, h_prev = lax.scan(step, h0.reshape(bsz, M_GROUPS, hpg, M_HEADDIM, M_DSTATE),
                              (chunk_decay.swapaxes(0, 1), states.swapaxes(0, 1)))
    y_off = jnp.einsum('bclgn,bclgh,bcghpn->bclghp', cc, jnp.exp(acum), h_prev.swapaxes(0, 1))
    y = (y_diag + y_off).reshape(bsz, L, M_HEADS, M_HEADDIM)
    return y, h_last.reshape(bsz, M_HEADS, M_HEADDIM, M_DSTATE)


def mamba_mixer(x, g, conv_buf, h0, w_in, conv_w, conv_b, dt_bias, a_log, d_skip, norm_g, w_out):
    bsz, L, _ = x.shape
    proj = norm_matmul(x.reshape(bsz * L, D_MODEL), g, w_in).reshape(bsz, L, -1)
    z = proj[..., :M_DINNER]
    xbc = proj[..., M_DINNER:M_DINNER + M_CONV_DIM]
    dt_raw = proj[..., M_DINNER + M_CONV_DIM:]
    xbc, new_buf = _causal_dwconv(xbc, conv_buf, conv_w, conv_b)
    xbc = jax.nn.silu(xbc)
    xs = xbc[..., :M_DINNER].reshape(bsz, L, M_HEADS, M_HEADDIM)
    bm = xbc[..., M_DINNER:M_DINNER + M_GN].reshape(bsz, L, M_GROUPS, M_DSTATE)
    cm = xbc[..., M_DINNER + M_GN:].reshape(bsz, L, M_GROUPS, M_DSTATE)
    dt = jax.nn.softplus(dt_raw + dt_bias)
    a = -jnp.exp(a_log)
    y, h = _ssd_scan(xs, dt, a, bm, cm, h0)
    y = y + xs * d_skip[:, None]
    y = y.reshape(bsz, L, M_DINNER)
    y = _rmsnorm(y * jax.nn.silu(z), norm_g)
    out = norm_matmul(y.reshape(bsz * L, M_DINNER), None, w_out, normalize=False)
    return out.reshape(bsz, L, D_MODEL), new_buf, h


SSD_HPG = M_HEADS // M_GROUPS
SSD_GROUP_ROWS = SSD_HPG * M_HEADDIM
CONV_PAD = 8


def _transpose_cols(x):
    return jnp.concatenate([x[:, j * LANES:(j + 1) * LANES].T for j in range(x.shape[1] // LANES)], axis=0)


def _transpose_rows(x):
    return jnp.concatenate([x[j * LANES:(j + 1) * LANES, :].T for j in range(x.shape[0] // LANES)], axis=1)


def _ssd_body(z_ref, xbc_ref, dtr_ref, cw_ref, cb_ref, dtb_ref, a_ref, dcol_ref, ng_ref, ltri_ref,
              y_ref, conv_ref, h_ref, xbuf, *, q):
    c = pl.program_id(1)

    @pl.when(c == 0)
    def _():
        h_ref[...] = jnp.zeros_like(h_ref)
        xbuf[0:CONV_PAD, :] = jnp.zeros((CONV_PAD, M_CONV_DIM), F32)

    xbuf[CONV_PAD:CONV_PAD + q, :] = xbc_ref[...]
    conv = cb_ref[...]
    for k in range(M_CONV):
        start = CONV_PAD - (M_CONV - 1) + k
        conv = conv + xbuf[start:start + q, :] * cw_ref[k:k + 1, :]
    tail = xbuf[CONV_PAD + q - (M_CONV - 1):CONV_PAD + q, :]
    xbuf[CONV_PAD - (M_CONV - 1):CONV_PAD, :] = tail
    conv_ref[0] = tail
    xc = conv * jax.nn.sigmoid(conv)
    xs = xc[:, :M_DINNER]
    xs_t = _transpose_cols(xs)

    pre = dtr_ref[...] + dtb_ref[...]
    dt = jnp.maximum(pre, 0.0) + jnp.log1p(jnp.exp(-jnp.abs(pre)))
    da = dt * a_ref[...]
    ltri = ltri_ref[...]
    acum = sum(jnp.dot(ltri, part, preferred_element_type=F32) for part in _split3(da))
    dt_t = dt.T
    acum_t = acum.T
    li = lax.broadcasted_iota(jnp.int32, (q, q), 0)
    si = lax.broadcasted_iota(jnp.int32, (q, q), 1)
    causal = li >= si

    y_t = []
    for g in range(M_GROUPS):
        bm = xc[:, M_DINNER + g * M_DSTATE:M_DINNER + (g + 1) * M_DSTATE].astype(BF16)
        cm = xc[:, M_DINNER + M_GN + g * M_DSTATE:M_DINNER + M_GN + (g + 1) * M_DSTATE].astype(BF16)
        cb = _nt_dot(cm, bm)
        r0 = g * SSD_GROUP_ROWS
        h_prev = h_ref[0, r0:r0 + SSD_GROUP_ROWS, :]
        y_off = _nt_dot(h_prev.astype(BF16), cm)
        x_dec, scale = [], []
        for hh in range(g * SSD_HPG, (g + 1) * SSD_HPG):
            a_row = acum_t[hh:hh + 1, :]
            a_col = acum[:, hh:hh + 1]
            decay = jnp.where(causal, jnp.exp(a_col - a_row), 0.0)
            m = (cb * decay).astype(BF16)
            rows = slice(hh * M_HEADDIM, (hh + 1) * M_HEADDIM)
            xs_h = xs_t[rows]
            xdt = xs_h * dt_t[hh:hh + 1, :]
            y_h = _nt_dot(xdt.astype(BF16), m)
            y_h = y_h + y_off[rows.start - r0:rows.stop - r0] * jnp.exp(a_row) + xs_h * dcol_ref[hh:hh + 1, :]
            y_t.append(y_h)
            a_last = a_row[:, q - 1:q]
            x_dec.append(xdt * jnp.exp(a_last - a_row))
            scale.append(jnp.broadcast_to(jnp.exp(a_last), (M_HEADDIM, 1)))
        upd = jnp.dot(jnp.concatenate(x_dec, axis=0).astype(BF16), bm, preferred_element_type=F32)
        h_ref[0, r0:r0 + SSD_GROUP_ROWS, :] = jnp.concatenate(scale, axis=0) * h_prev + upd

    y = _transpose_rows(jnp.concatenate(y_t, axis=0))
    zz = z_ref[...]
    yg = y * (zz * jax.nn.sigmoid(zz))
    yg = yg * lax.rsqrt(jnp.mean(yg * yg, axis=-1, keepdims=True) + EPS) * ng_ref[...]
    y_ref[...] = yg.astype(BF16)


def ssd_prompt(z, xbc, dtr, conv_w, conv_b, dt_bias, a_log, d_skip, norm_g, bsz, L):
    q = M_CHUNK
    assert L % q == 0 and q == LANES
    nc = L // q
    padl = lambda v: jnp.pad(v, (0, LANES - v.shape[0]))
    dtb = padl(dt_bias).reshape(1, LANES)
    a_row = padl(-jnp.exp(a_log)).reshape(1, LANES)
    dcol = jnp.broadcast_to(padl(d_skip).reshape(LANES, 1), (LANES, LANES))
    ltri = jnp.asarray(np.tril(np.ones((q, q), np.float32)), BF16)
    const = lambda a: pl.BlockSpec(a.shape, lambda b, c: (0,) * a.ndim)
    tok = lambda w: pl.BlockSpec((q, w), lambda b, c: (b * nc + c, 0))
    cb2 = conv_b.reshape(1, M_CONV_DIM)
    ng2 = norm_g.reshape(1, M_DINNER)
    y, conv_tail, h = pl.pallas_call(
        functools.partial(_ssd_body, q=q),
        name="ssd_chunks",
        grid=(bsz, nc),
        in_specs=[tok(M_DINNER), tok(M_CONV_DIM), tok(LANES), const(conv_w), const(cb2), const(dtb),
                  const(a_row), const(dcol), const(ng2), const(ltri)],
        out_specs=[tok(M_DINNER),
                   pl.BlockSpec((1, M_CONV - 1, M_CONV_DIM), lambda b, c: (b, 0, 0)),
                   pl.BlockSpec((1, M_DINNER, M_DSTATE), lambda b, c: (b, 0, 0))],
        out_shape=[jax.ShapeDtypeStruct((bsz * L, M_DINNER), BF16),
                   jax.ShapeDtypeStruct((bsz, M_CONV - 1, M_CONV_DIM), F32),
                   jax.ShapeDtypeStruct((bsz, M_DINNER, M_DSTATE), F32)],
        scratch_shapes=[pltpu.VMEM((CONV_PAD + q, M_CONV_DIM), F32)],
        compiler_params=pltpu.CompilerParams(
            dimension_semantics=("parallel", "arbitrary"),
            vmem_limit_bytes=VMEM_LIMIT_BYTES),
    )(z, xbc, dtr, conv_w, cb2, dtb, a_row, dcol, ng2, ltri)
    return y, conv_tail, h.reshape(bsz, M_HEADS, M_HEADDIM, M_DSTATE)


def mamba_prompt_mixer(x, g, w_in, conv_w, conv_b, dt_bias, a_log, d_skip, norm_g, w_out):
    bsz, L, _ = x.shape
    x2 = x.reshape(bsz * L, D_MODEL)
    z = norm_matmul(x2, g, w_in[:, :M_DINNER])
    xbc = norm_matmul(x2, g, w_in[:, M_DINNER:M_DINNER + M_CONV_DIM])
    dtr = norm_matmul(x2, g, jnp.pad(w_in[:, M_DINNER + M_CONV_DIM:], ((0, 0), (0, LANES - M_HEADS))))
    y, conv_tail, h = ssd_prompt(z, xbc, dtr, conv_w, conv_b, dt_bias, a_log, d_skip, norm_g, bsz, L)
    out = norm_matmul(y, None, w_out, normalize=False, res=x2)
    return out.reshape(bsz, L, D_MODEL), conv_tail, h


def _rel_bucket(dist):
    exact = REL_BUCKETS // 2
    d = jnp.maximum(dist, 0)
    ratio = jnp.log(jnp.maximum(d, 1).astype(F32) / exact) / math.log(REL_MAX_DIST / exact)
    large = jnp.minimum(exact + (ratio * (REL_BUCKETS - exact)).astype(jnp.int32), REL_BUCKETS - 1)
    return jnp.where(d < exact, d, large)


def _head_bias(rel_table, dist):
    b = rel_table[_rel_bucket(dist)].astype(F32)
    return jnp.moveaxis(b, -1, 0).reshape(KV_HEADS, HPG, dist.shape[0], dist.shape[1])


def _masked_softmax(s, valid):
    s = jnp.where(valid, s, -1e30)
    e = jnp.where(valid, jnp.exp(s - jnp.max(s, axis=-1, keepdims=True)), 0.0)
    return e / jnp.maximum(jnp.sum(e, axis=-1, keepdims=True), 1e-30)


def _group_attend(qg, qpos, k, v, kpos, valid, rel_table):
    s = jnp.einsum('bqghd,bkgd->bghqk', qg, k).astype(F32) * SCALE
    s = s + _head_bias(rel_table, qpos[:, None] - kpos[None, :])
    p = _masked_softmax(s, valid)
    o = jnp.einsum('bghqk,bkgd->bqghd', p.astype(v.dtype), v)
    return o, p


def _compress(rows, pe, w1, w2):
    bsz, T = rows.shape[0], rows.shape[1]
    nc = (T - CMP_BLOCK) // CMP_STRIDE + 1
    idx = jnp.arange(nc)[:, None] * CMP_STRIDE + jnp.arange(CMP_BLOCK)[None, :]
    blk = rows[:, idx] + pe[:, None, :]
    blk = jnp.moveaxis(blk, 2, 3).reshape(bsz, nc, KV_HEADS, CMP_BLOCK * HEAD_DIM)
    return jax.nn.silu(blk @ w1) @ w2


def _select_blocks(p_cmp, qpos, n_sel):
    nc = p_cmp.shape[-1]
    cstart = jnp.arange(nc) * CMP_STRIDE
    sstart = jnp.arange(n_sel) * SEL_BLOCK
    overlap = ((cstart[:, None] < sstart[None, :] + SEL_BLOCK) &
               (cstart[:, None] + CMP_BLOCK > sstart[None, :])).astype(F32)
    imp = jnp.einsum('bghqc,cs->bgqs', p_cmp, overlap)
    qblk = qpos // SEL_BLOCK
    j = jnp.arange(n_sel)
    forced = (j[None, :] == qblk[:, None]) | (j[None, :] == 0)
    score = jnp.where(forced, 1e9, jnp.where(j[None, :] <= qblk[:, None], imp, -1e30))
    _, idx = lax.top_k(score, min(SEL_TOPK, n_sel))
    return idx


def _sel_attend(qg, qpos, idx, kblk, vblk, rel_table):
    bsz, lq = qg.shape[0], qg.shape[1]
    bi = jnp.arange(bsz)[:, None, None, None]
    gi = jnp.arange(KV_HEADS)[None, :, None, None]
    kg = kblk[bi, gi, idx]
    vg = vblk[bi, gi, idx]
    s = jnp.einsum('bqghd,bgqkrd->bghqkr', qg, kg).astype(F32) * SCALE
    kpos = idx[..., None] * SEL_BLOCK + jnp.arange(SEL_BLOCK)
    dist = qpos[:, None, None] - kpos
    g5 = jnp.arange(KV_HEADS)[None, :, None, None, None]
    bias = rel_table.reshape(REL_BUCKETS, KV_HEADS, HPG)[_rel_bucket(dist), g5].astype(F32)
    s = s + jnp.moveaxis(bias, -1, 2)
    nk = idx.shape[-1]
    valid = (dist >= 0)[:, :, None].reshape(bsz, KV_HEADS, 1, lq, nk * SEL_BLOCK)
    p = _masked_softmax(s.reshape(bsz, KV_HEADS, HPG, lq, nk * SEL_BLOCK), valid)
    p = p.reshape(bsz, KV_HEADS, HPG, lq, nk, SEL_BLOCK)
    return jnp.einsum('bghqkr,bgqkrd->bqghd', p.astype(vg.dtype), vg)


def nsa_mixer(x, g, pos0, kv_cmp_past, kv_sel_past, kv_win_past, n_keep,
              w_in, q_g, k_g, cmp_pe, cmp_w1, cmp_w2, w_out, rel_table):
    bsz, L, _ = x.shape
    sizes = [N_HEADS * HEAD_DIM] + [KV_DIM] * 6 + [3 * N_HEADS]
    proj = norm_matmul(x.reshape(bsz * L, D_MODEL), g, w_in).reshape(bsz, L, -1)
    q, kc, vc, ks, vs, kw, vw, gl = jnp.split(proj, np.cumsum(sizes)[:-1].tolist(), axis=-1)
    q = _rmsnorm(q.reshape(bsz, L, KV_HEADS, HPG, HEAD_DIM), q_g)

    def kv(t):
        return t.reshape(bsz, L, KV_HEADS, HEAD_DIM)

    new_cmp = jnp.stack([kv(kc), kv(vc)], axis=2)
    new_sel = jnp.stack([_rmsnorm(kv(ks), k_g[1]), kv(vs)], axis=2)
    new_win = jnp.stack([_rmsnorm(kv(kw), k_g[2]), kv(vw)], axis=2)
    qpos = pos0 + jnp.arange(L)

    crows = jnp.concatenate([kv_cmp_past.astype(x.dtype), new_cmp], axis=1)
    kcmp = _rmsnorm(_compress(crows[:, :, 0], cmp_pe[0], cmp_w1[0], cmp_w2[0]), k_g[0])
    vcmp = _compress(crows[:, :, 1], cmp_pe[1], cmp_w1[1], cmp_w2[1])
    ends = jnp.arange(kcmp.shape[1]) * CMP_STRIDE + CMP_BLOCK - 1
    o_cmp, p_cmp = _group_attend(q, qpos, kcmp, vcmp, ends, qpos[:, None] >= ends[None, :], rel_table)

    srows = jnp.concatenate([kv_sel_past.astype(x.dtype), new_sel], axis=1)
    T = srows.shape[1]
    n_sel = -(-T // SEL_BLOCK)
    srows = jnp.pad(srows, ((0, 0), (0, n_sel * SEL_BLOCK - T), (0, 0), (0, 0), (0, 0)))
    blocks = srows.reshape(bsz, n_sel, SEL_BLOCK, 2, KV_HEADS, HEAD_DIM).transpose(3, 0, 4, 1, 2, 5)
    idx = _select_blocks(p_cmp, qpos, n_sel)
    qb = SEL_QBLK if L % SEL_QBLK == 0 else L
    nqb = L // qb

    def sel_block(args):
        qgi, qposi, idxi = args
        return _sel_attend(qgi, qposi, idxi, blocks[0], blocks[1], rel_table)

    o_sel = lax.map(sel_block, (q.reshape(bsz, nqb, qb, KV_HEADS, HPG, HEAD_DIM).swapaxes(0, 1),
                                qpos.reshape(nqb, qb),
                                idx.reshape(bsz, KV_HEADS, nqb, qb, -1).transpose(2, 0, 1, 3, 4)))
    o_sel = o_sel.swapaxes(0, 1).reshape(bsz, L, KV_HEADS, HPG, HEAD_DIM)

    p_win = kv_win_past.shape[1]
    wrows = jnp.concatenate([kv_win_past.astype(x.dtype), new_win], axis=1)
    wpad = jnp.pad(wrows, ((0, 0), (WINDOW, 0), (0, 0), (0, 0), (0, 0)))
    n_all = WINDOW + p_win + L
    kpos_all = pos0 - p_win - WINDOW + jnp.arange(n_all)
    kvalid_all = jnp.arange(n_all) >= WINDOW
    wq = WIN_QBLK if L % WIN_QBLK == 0 else L

    def win_block(i):
        start = p_win + i * wq
        qgi = lax.dynamic_slice_in_dim(q, i * wq, wq, axis=1)
        kvi = lax.dynamic_slice_in_dim(wpad, start, WINDOW + wq, axis=1)
        kposi = lax.dynamic_slice_in_dim(kpos_all, start, WINDOW + wq)
        kvalidi = lax.dynamic_slice_in_dim(kvalid_all, start, WINDOW + wq)
        qposi = pos0 + i * wq + jnp.arange(wq)
        dist = qposi[:, None] - kposi[None, :]
        valid = kvalidi[None, :] & (dist >= 0) & (dist <= WINDOW)
        o, _ = _group_attend(qgi, qposi, kvi[:, :, 0], kvi[:, :, 1], kposi, valid, rel_table)
        return o

    o_win = lax.map(win_block, jnp.arange(L // wq)).swapaxes(0, 1).reshape(bsz, L, KV_HEADS, HPG, HEAD_DIM)

    gate = jax.nn.sigmoid(gl).reshape(bsz, L, KV_HEADS, HPG, 3)
    o = gate[..., 0:1] * o_cmp + gate[..., 1:2] * o_sel + gate[..., 2:3] * o_win
    y = norm_matmul(o.reshape(bsz * L, N_HEADS * HEAD_DIM), None, w_out, normalize=False)
    return y.reshape(bsz, L, D_MODEL), new_cmp, new_sel, wrows[:, -n_keep:]


ATT_TQ = 128
ATT_TK = 128
NEG = -1e30


def _nt_dot(a, b):
    return lax.dot_general(a, b, (((1,), (1,)), ((), ())), preferred_element_type=F32)


def _compress_body(rk_ref, rv_ref, pe_ref, w1_ref, w2_ref, kg_ref, kc_ref, vc_ref, *, nb):
    half = (CMP_BLOCK // 2) * HEAD_DIM
    for kv, (r_ref, o_ref) in enumerate(((rk_ref, kc_ref), (rv_ref, vc_ref))):
        lo = (r_ref[0, 0, 0:nb, :] + pe_ref[kv, 0:1, :]).astype(BF16)
        hi = (r_ref[0, 0, 1:nb + 1, :] + pe_ref[kv, 1:2, :]).astype(BF16)
        h = (jnp.dot(lo, w1_ref[kv, :half, :].astype(BF16), preferred_element_type=F32)
             + jnp.dot(hi, w1_ref[kv, half:, :].astype(BF16), preferred_element_type=F32))
        h = h * jax.nn.sigmoid(h)
        o = jnp.dot(h.astype(BF16), w2_ref[kv].astype(BF16), preferred_element_type=F32)
        if kv == 0:
            o = o * lax.rsqrt(jnp.mean(o * o, axis=-1, keepdims=True) + EPS) * kg_ref[...]
        o_ref[0, 0] = o


def compress_rows(rk, rv, pe, w1, w2, kg):
    bsz, g, nbp, width = rk.shape
    nb = nbp - 8
    strip = pl.BlockSpec((1, 1, nbp, width), lambda b, j: (b, j, 0, 0))
    out = pl.BlockSpec((1, 1, nb, HEAD_DIM), lambda b, j: (b, j, 0, 0))
    full = lambda a: pl.BlockSpec(a.shape, lambda b, j: (0,) * a.ndim)
    pe2 = pe.reshape(2, 2, width)
    kg2 = kg.reshape(1, HEAD_DIM)
    return pl.pallas_call(
        functools.partial(_compress_body, nb=nb),
        name="nsa_compress",
        grid=(bsz, g),
        in_specs=[strip, strip, full(pe2), full(w1), full(w2), full(kg2)],
        out_specs=[out, out],
        out_shape=[jax.ShapeDtypeStruct((bsz, g, nb, HEAD_DIM), F32)] * 2,
        compiler_params=pltpu.CompilerParams(
            dimension_semantics=("parallel", "parallel"),
            vmem_limit_bytes=VMEM_LIMIT_BYTES),
    )(rk, rv, pe2, w1, w2, kg2)


def _split3(x):
    a = x.astype(BF16)
    r = x - a.astype(F32)
    b = r.astype(BF16)
    c = (r - b.astype(F32)).astype(BF16)
    return a, b, c


def _nsa_attn_body(q_ref, kc_ref, vc_ref, ks_ref, vs_ref, kw_ref, vw_ref, gate_ref, bcmp_ref, btile_ref,
                   ovl_ref, o_ref, *, n_cmp, n_sel, topk):
    i = pl.program_id(2)
    q0 = i * ATT_TQ
    rows = HPG * ATT_TQ
    qt = q_ref[0]
    qb = jnp.concatenate([qt[:, h * HEAD_DIM:(h + 1) * HEAD_DIM] for h in range(HPG)], axis=0).astype(BF16)
    row = lax.broadcasted_iota(jnp.int32, (rows, ATT_TK), 0)
    qpos = q0 + (row & (ATT_TQ - 1))
    col = lax.broadcasted_iota(jnp.int32, (rows, ATT_TK), 1)

    s = _nt_dot(qb, kc_ref[0, 0].astype(BF16)) * SCALE + bcmp_ref[0, 0]
    valid = (qpos >= col * CMP_STRIDE + (CMP_BLOCK - 1)) & (col < n_cmp)
    s = jnp.where(valid, s, NEG)
    e = jnp.where(valid, jnp.exp(s - jnp.max(s, axis=-1, keepdims=True)), 0.0)
    p = e / jnp.maximum(jnp.sum(e, axis=-1, keepdims=True), 1e-30)
    pb = p.astype(BF16)
    o_cmp = jnp.dot(pb, vc_ref[0, 0].astype(BF16), preferred_element_type=F32)

    pf = pb.astype(F32)
    psum = pf[0:ATT_TQ]
    for h in range(1, HPG):
        psum = psum + pf[h * ATT_TQ:(h + 1) * ATT_TQ]
    ovl = ovl_ref[...]
    imp_t = sum(_nt_dot(ovl, part) for part in _split3(psum))
    nblk = imp_t.shape[0]
    j = lax.broadcasted_iota(jnp.int32, (nblk, ATT_TQ), 0)
    qblk = (q0 + lax.broadcasted_iota(jnp.int32, (nblk, ATT_TQ), 1)) // SEL_BLOCK
    forced = (j == qblk) | (j == 0)
    score = jnp.where(forced, 1e9, jnp.where(j <= qblk, imp_t, NEG))
    score = jnp.where(j < n_sel, score, -3e38)
    rank = jnp.zeros((nblk, ATT_TQ), F32)
    for jp in range(n_sel):
        other = score[jp:jp + 1, :]
        beats = (other > score) | ((other == score) & (jp < j))
        rank = rank + jnp.where(beats, 1.0, 0.0)
    sel = jnp.where((rank < topk) & (j < n_sel), 1.0, 0.0).T.astype(BF16)

    jj = lax.broadcasted_iota(jnp.int32, (nblk, ATT_TK), 0)
    kk = lax.broadcasted_iota(jnp.int32, (nblk, ATT_TK), 1) // SEL_BLOCK
    blocks_per_step = ATT_TK // SEL_BLOCK

    def flash(k_ref, v_ref, lo, hi, valid_fn):
        def step(kc, carry):
            m, l, acc = carry
            start = pl.multiple_of(kc * ATT_TK, ATT_TK)
            kblk = k_ref[0, 0, pl.ds(start, ATT_TK), :].astype(BF16)
            vblk = v_ref[0, 0, pl.ds(start, ATT_TK), :].astype(BF16)
            sc = _nt_dot(qb, kblk) * SCALE + btile_ref[0, jnp.minimum(i - kc, 2)]
            ok = valid_fn(kc, qpos - (kc * ATT_TK + col))
            sc = jnp.where(ok, sc, NEG)
            m_new = jnp.maximum(m, jnp.max(sc, axis=-1, keepdims=True))
            alpha = jnp.exp(m - m_new)
            pr = jnp.where(ok, jnp.exp(sc - m_new), 0.0)
            l = alpha * l + jnp.sum(pr, axis=-1, keepdims=True)
            acc = alpha * acc + jnp.dot(pr.astype(BF16), vblk, preferred_element_type=F32)
            return m_new, l, acc

        init = (jnp.full((rows, 1), NEG, F32), jnp.zeros((rows, 1), F32), jnp.zeros((rows, HEAD_DIM), F32))
        _, l, acc = lax.fori_loop(lo, hi, step, init)
        return acc / jnp.maximum(l, 1e-30)

    def sel_valid(kc, dist):
        expand = jnp.where(jj == kc * blocks_per_step + kk, 1.0, 0.0).astype(BF16)
        chosen = jnp.dot(sel, expand, preferred_element_type=F32)
        chosen = jnp.concatenate([chosen] * HPG, axis=0)
        return (chosen > 0.5) & (dist >= 0)

    def win_valid(kc, dist):
        return (dist >= 0) & (dist <= WINDOW)

    o_sel = flash(ks_ref, vs_ref, 0, i + 1, sel_valid)
    o_win = flash(kw_ref, vw_ref, jnp.maximum(i - WINDOW // ATT_TK, 0), i + 1, win_valid)

    gate = jax.nn.sigmoid(gate_ref[0, 0])

    def gcol(br):
        return jnp.concatenate([gate[:, h * 3 + br:h * 3 + br + 1] for h in range(HPG)], axis=0)

    o = gcol(0) * o_cmp + gcol(1) * o_sel + gcol(2) * o_win
    o_ref[0] = jnp.concatenate([o[h * ATT_TQ:(h + 1) * ATT_TQ] for h in range(HPG)], axis=1)


def _bias_tables(rel_table, n_qtiles):
    tab = rel_table[_rel_bucket(jnp.arange(REL_MAX_DIST + 1))].astype(F32)
    tab = tab.T.reshape(KV_HEADS, HPG, REL_MAX_DIST + 1)
    t = np.arange(ATT_TQ)[:, None]
    s = np.arange(ATT_TK)[None, :]
    d_tile = np.stack([np.clip(delta * ATT_TK + t - s, 0, REL_MAX_DIST) for delta in range(3)])
    btile = tab[:, :, d_tile]
    btile = btile.transpose(0, 2, 1, 3, 4).reshape(KV_HEADS, 3, HPG * ATT_TQ, ATT_TK)
    qi = np.arange(n_qtiles)[:, None, None] * ATT_TQ
    d_cmp = np.clip(qi + t[None] - (s[None] * CMP_STRIDE + CMP_BLOCK - 1), 0, REL_MAX_DIST)
    bcmp = tab[:, :, d_cmp]
    bcmp = bcmp.transpose(2, 0, 1, 3, 4).reshape(n_qtiles, KV_HEADS, HPG * ATT_TQ, ATT_TK)
    return btile, bcmp


def nsa_prompt_attention(q, kcmp, vcmp, ks, vs, kw, vw, gl, rel_table):
    bsz, L, _ = q.shape
    assert L % ATT_TQ == 0 and kcmp.shape[2] == ATT_TK
    nq = L // ATT_TQ
    n_cmp = (L - CMP_BLOCK) // CMP_STRIDE + 1
    n_sel = L // SEL_BLOCK
    assert n_sel <= ATT_TK
    btile, bcmp = _bias_tables(rel_table, nq)
    c = np.arange(ATT_TK)[None, :] * CMP_STRIDE
    sb = np.arange(ATT_TK)[:, None] * SEL_BLOCK
    ovl = ((c < sb + SEL_BLOCK) & (c + CMP_BLOCK > sb) & (np.arange(ATT_TK)[None, :] < n_cmp)
           & (np.arange(ATT_TK)[:, None] < n_sel))
    ovl = jnp.asarray(ovl, BF16)
    width = HPG * HEAD_DIM
    kvspec = lambda n: pl.BlockSpec((1, 1, n, HEAD_DIM), lambda b, g, i: (b, g, 0, 0))
    return pl.pallas_call(
        functools.partial(_nsa_attn_body, n_cmp=n_cmp, n_sel=n_sel, topk=min(SEL_TOPK, n_sel)),
        name="nsa_attention",
        grid=(bsz, KV_HEADS, nq),
        in_specs=[
            pl.BlockSpec((1, ATT_TQ, width), lambda b, g, i: (b, i, g)),
            kvspec(ATT_TK), kvspec(ATT_TK), kvspec(L), kvspec(L), kvspec(L), kvspec(L),
            pl.BlockSpec((1, 1, ATT_TQ, HPG * 3), lambda b, g, i: (b, g, i, 0)),
            pl.BlockSpec((1, 1, HPG * ATT_TQ, ATT_TK), lambda b, g, i: (i, g, 0, 0)),
            pl.BlockSpec((1, 3, HPG * ATT_TQ, ATT_TK), lambda b, g, i: (g, 0, 0, 0)),
            pl.BlockSpec((ATT_TK, ATT_TK), lambda b, g, i: (0, 0)),
        ],
        out_specs=pl.BlockSpec((1, ATT_TQ, width), lambda b, g, i: (b, i, g)),
        out_shape=jax.ShapeDtypeStruct((bsz, L, N_HEADS * HEAD_DIM), F32),
        compiler_params=pltpu.CompilerParams(
            dimension_semantics=("parallel", "parallel", "arbitrary"),
            vmem_limit_bytes=VMEM_LIMIT_BYTES),
    )(q, kcmp, vcmp, ks, vs, kw, vw, gl, bcmp, btile, ovl)


def nsa_prompt_mixer(x, g, w_in, q_g, k_g, cmp_pe, cmp_w1, cmp_w2, w_out, rel_table):
    bsz, L, _ = x.shape
    t = bsz * L
    proj = norm_matmul(x.reshape(t, D_MODEL), g, w_in)
    c0 = N_HEADS * HEAD_DIM
    q = _rmsnorm(proj[:, :c0].reshape(t, N_HEADS, HEAD_DIM), q_g).reshape(bsz, L, c0)

    def kvpair(k, normed_g):
        kcols = proj[:, c0 + 2 * k * KV_DIM:c0 + (2 * k + 1) * KV_DIM]
        vcols = proj[:, c0 + (2 * k + 1) * KV_DIM:c0 + (2 * k + 2) * KV_DIM]
        if normed_g is not None:
            kcols = _rmsnorm(kcols.reshape(t, KV_HEADS, HEAD_DIM), normed_g).reshape(t, KV_DIM)
        rows = jnp.stack([kcols, vcols], axis=1).reshape(bsz, L, 2, KV_HEADS, HEAD_DIM)
        per_group = rows.transpose(2, 0, 3, 1, 4)
        return rows, per_group[0], per_group[1]

    new_cmp, kc_rows, vc_rows = kvpair(0, None)
    new_sel, ks, vs = kvpair(1, k_g[1])
    new_win, kw, vw = kvpair(2, k_g[2])
    gl = proj[:, c0 + 6 * KV_DIM:].reshape(bsz, L, KV_HEADS, HPG * 3).transpose(0, 2, 1, 3)

    nb = L // CMP_STRIDE

    def strips(r):
        r = r.reshape(bsz, KV_HEADS, nb, CMP_STRIDE * HEAD_DIM)
        return jnp.pad(r, ((0, 0), (0, 0), (0, ATT_TK + 8 - nb), (0, 0)))

    kcmp, vcmp = compress_rows(strips(kc_rows), strips(vc_rows), cmp_pe, cmp_w1, cmp_w2, k_g[0])
    o = nsa_prompt_attention(q, kcmp, vcmp, ks, vs, kw, vw, gl, rel_table)
    y = norm_matmul(o.reshape(t, c0), None, w_out, normalize=False, res=x.reshape(t, D_MODEL))
    return y.reshape(bsz, L, D_MODEL), new_cmp, new_sel, new_win[:, -min(WINDOW, L):]


def _gla_chunked(q, k, v, logf, s0):
    bsz, L = q.shape[0], q.shape[1]
    nc = -(-L // H_CHUNK)
    pad = nc * H_CHUNK - L

    def prep(t):
        t = jnp.pad(t, ((0, 0), (0, pad), (0, 0), (0, 0)))
        return t.reshape(bsz, nc, H_CHUNK, H_HEADS, t.shape[-1])

    q, k, v, logf = prep(q), prep(k), prep(v), prep(logf)
    acum = jnp.cumsum(logf, axis=2)
    alast = acum[:, :, -1:]
    qe = q * jnp.exp(acum)
    ke = k * jnp.exp(-acum)
    kd = k * jnp.exp(alast - acum)
    tri = jnp.tril(jnp.ones((H_CHUNK, H_CHUNK), bool))
    att = jnp.where(tri, jnp.einsum('bcthk,bcshk->bchts', qe, ke), 0.0)
    o_intra = jnp.einsum('bchts,bcshv->bcthv', att, v)
    upd = jnp.einsum('bcshk,bcshv->bchkv', kd, v)

    def step(s, inp):
        dec, up = inp
        return dec[..., None] * s + up, s

    s_last, s_prev = lax.scan(step, s0, (jnp.exp(alast[:, :, 0]).swapaxes(0, 1), upd.swapaxes(0, 1)))
    o_inter = jnp.einsum('bcthk,bchkv->bcthv', qe, s_prev.swapaxes(0, 1))
    o = (o_intra + o_inter).reshape(bsz, nc * H_CHUNK, H_HEADS, H_DV)[:, :L]
    return o, s_last


def hgrn2_mixer(x, g, s0, lb, w_in, norm_g, w_out):
    bsz, L, _ = x.shape
    proj = norm_matmul(x.reshape(bsz * L, D_MODEL), g, w_in).reshape(bsz, L, -1)
    q, fr, iv, gate = jnp.split(proj, 4, axis=-1)
    q = jax.nn.silu(q).reshape(bsz, L, H_HEADS, H_DK)
    forget = lb + (1.0 - lb) * jax.nn.sigmoid(fr)
    logf = jnp.log(forget).reshape(bsz, L, H_HEADS, H_DK)
    k = (1.0 - forget).reshape(bsz, L, H_HEADS, H_DK)
    v = iv.reshape(bsz, L, H_HEADS, H_DV)
    o, s = _gla_chunked(q, k, v, logf, s0)
    o = _rmsnorm(o, norm_g) * jax.nn.silu(gate.reshape(bsz, L, H_HEADS, H_DV))
    y = norm_matmul(o.reshape(bsz * L, D_MODEL), None, w_out, normalize=False)
    return y.reshape(bsz, L, D_MODEL), s


def _gather_pages(cache, page_table):
    pages = cache[page_table]
    b, n, p = pages.shape[0], pages.shape[1], pages.shape[2]
    return pages.reshape(b, n * p, *pages.shape[3:])


def kernel(x_prompt, x_sample, state_ssm, state_conv, cache_kv_cmp, cache_kv_sel, cache_kv_win, state_hgrn, page_table, norm_g, rel_table, m_w_in, m_conv_w, m_conv_b, m_dt_bias, m_a_log, m_d, m_norm_g, m_w_out, n_w_in, n_q_g, n_k_g, n_cmp_pe, n_cmp_w1, n_cmp_w2, n_w_out, h_w_in, h_lb, h_norm_g, h_w_out, moe_w_rg, moe_b_rg, moe_w_re, moe_b_re, moe_w1, moe_w3, moe_w2):
    bp, lp = x_prompt.shape[0], x_prompt.shape[1]
    past_len = page_table.shape[1] * cache_kv_cmp.shape[2]
    dt = x_prompt.dtype
    lbs = jax.nn.softmax(h_lb.astype(F32), axis=0)
    lbs = jnp.cumsum(lbs, axis=0) - lbs[0]
    xp, xs = x_prompt, x_sample
    ssm_p, conv_p, cmp_p, sel_p, win_p, hg_p = [], [], [], [], [], []
    ssm_s, conv_s, cmp_s, sel_s, win_s, hg_s = [], [], [], [], [], []
    for i in range(DEPTH):
        kind, j = i % N_MIXERS, i // N_MIXERS
        g0 = norm_g[i, 0]
        if kind == 0:
            w = (m_w_in[j], m_conv_w[j], m_conv_b[j], m_dt_bias[j], m_a_log[j], m_d[j], m_norm_g[j], m_w_out[j])
            xp, cbuf, hh = mamba_prompt_mixer(xp, g0, *w)
            ssm_p.append(hh)
            conv_p.append(cbuf)
            ys, cbuf, hh = mamba_mixer(xs, g0, state_conv[j], state_ssm[j], *w)
            ssm_s.append(hh)
            conv_s.append(cbuf)
        elif kind == 1:
            w = (n_w_in[j], n_q_g[j], n_k_g[j], n_cmp_pe[j], n_cmp_w1[j], n_cmp_w2[j], n_w_out[j], rel_table)
            xp, rc, rs, wb = nsa_prompt_mixer(xp, g0, *w)
            cmp_p.append(rc)
            sel_p.append(rs)
            win_p.append(wb)
            ys, rc, rs, wb = nsa_mixer(xs, g0, past_len, _gather_pages(cache_kv_cmp[j], page_table),
                                       _gather_pages(cache_kv_sel[j], page_table), cache_kv_win[j],
                                       cache_kv_win.shape[2], *w)
            cmp_s.append(rc)
            sel_s.append(rs)
            win_s.append(wb)
        else:
            w = (lbs[i], h_w_in[j], h_norm_g[j], h_w_out[j])
            yp, st = hgrn2_mixer(xp, g0, jnp.zeros((bp, H_HEADS, H_DK, H_DV), dt), *w)
            xp = xp + yp
            hg_p.append(st)
            ys, st = hgrn2_mixer(xs, g0, state_hgrn[j], *w)
            hg_s.append(st)
        xs = xs + ys
        mw = (moe_w_rg[i], moe_b_rg[i], moe_w_re[i], moe_b_re[i], moe_w1[i], moe_w3[i], moe_w2[i])
        xp = hier_moe_residual(xp.reshape(-1, D_MODEL), norm_g[i, 1], *mw).reshape(xp.shape)
        xs = hier_moe_residual(xs.reshape(-1, D_MODEL), norm_g[i, 1], *mw).reshape(xs.shape)
    return (xp, xs,
            jnp.stack(ssm_p), jnp.stack(conv_p), jnp.stack(cmp_p), jnp.stack(sel_p), jnp.stack(win_p), jnp.stack(hg_p),
            jnp.stack(ssm_s), jnp.stack(conv_s), jnp.stack(cmp_s), jnp.stack(sel_s), jnp.stack(win_s), jnp.stack(hg_s))
```

```python
import functools
import math

import jax
import jax.numpy as jnp
import numpy as np
from jax import lax
from jax.experimental import pallas as pl
from jax.experimental.pallas import tpu as pltpu

F32 = jnp.float32
BF16 = jnp.bfloat16
EPS = 1e-6

D_MODEL = 1024
DEPTH = 4
N_MIXERS = 3

M_DINNER = 2 * D_MODEL
M_HEADDIM = 64
M_HEADS = M_DINNER // M_HEADDIM
M_GROUPS = 4
M_DSTATE = 128
M_CONV = 4
M_GN = M_GROUPS * M_DSTATE
M_CONV_DIM = M_DINNER + 2 * M_GN
M_CHUNK = 128

N_HEADS = 16
HEAD_DIM = D_MODEL // N_HEADS
KV_HEADS = 4
HPG = N_HEADS // KV_HEADS
KV_DIM = KV_HEADS * HEAD_DIM
CMP_BLOCK = 32
CMP_STRIDE = 16
SEL_BLOCK = 64
SEL_TOPK = 16
WINDOW = 512
SEL_QBLK = 32
WIN_QBLK = 128
SCALE = HEAD_DIM ** -0.5
REL_BUCKETS = 32
REL_MAX_DIST = 128

H_DK = 128
H_HEADS = D_MODEL // H_DK
H_DV = D_MODEL // H_HEADS
H_CHUNK = 32

MOE_GROUPS = 4
MOE_EPG = 4
MOE_EXPERTS = MOE_GROUPS * MOE_EPG
MOE_TOPK = 2
MOE_FF = 512

VMEM_LIMIT_BYTES = 48 * 1024 * 1024
DECODE_VMEM_LIMIT_BYTES = 56 * 1024 * 1024


def _rmsnorm(x, g):
    xf = x.astype(F32)
    y = xf * lax.rsqrt(jnp.mean(xf * xf, axis=-1, keepdims=True) + EPS)
    return (y * g.astype(F32)).astype(x.dtype)


LANES = 128


def _norm_matmul_body(*refs, normalize, has_res):
    if has_res:
        x_ref, g_ref, w_ref, res_ref, o_ref, xb_ref = refs
    else:
        x_ref, g_ref, w_ref, o_ref, xb_ref = refs

    @pl.when(pl.program_id(1) == 0)
    def _():
        x = x_ref[...].astype(F32)
        if normalize:
            x = x * lax.rsqrt(jnp.mean(x * x, axis=-1, keepdims=True) + EPS) * g_ref[...]
        xb_ref[...] = x.astype(BF16)

    acc = jnp.dot(xb_ref[...], w_ref[...].astype(BF16), preferred_element_type=F32)
    if has_res:
        acc = acc + res_ref[...]
    o_ref[...] = acc


def _pick_tile(n, pref):
    t = min(n, pref)
    while n % t:
        t //= 2
    return t


def norm_matmul(x, g, w, *, normalize=True, res=None, tm=None, tn=512):
    t, k = x.shape
    n_true = w.shape[1]
    tm = _pick_tile(t, tm or (1024 if k <= 1024 else 512))
    if n_true % LANES:
        assert res is None
        w = jnp.pad(w, ((0, 0), (0, LANES - n_true % LANES)))
    n = w.shape[1]
    tn = _pick_tile(n, tn)
    if g is None:
        g = jnp.ones((k,), F32)
    in_specs = [
        pl.BlockSpec((tm, k), lambda i, j: (i, 0)),
        pl.BlockSpec((1, k), lambda i, j: (0, 0)),
        pl.BlockSpec((k, tn), lambda i, j: (0, j)),
    ]
    args = [x, g.reshape(1, k), w]
    if res is not None:
        in_specs.append(pl.BlockSpec((tm, tn), lambda i, j: (i, j)))
        args.append(res)
    out = pl.pallas_call(
        functools.partial(_norm_matmul_body, normalize=normalize, has_res=res is not None),
        name="norm_matmul",
        grid=(t // tm, n // tn),
        in_specs=in_specs,
        out_specs=pl.BlockSpec((tm, tn), lambda i, j: (i, j)),
        out_shape=jax.ShapeDtypeStruct((t, n), F32),
        scratch_shapes=[pltpu.VMEM((tm, k), BF16)],
        compiler_params=pltpu.CompilerParams(
            dimension_semantics=("parallel", "arbitrary"),
            vmem_limit_bytes=VMEM_LIMIT_BYTES),
    )(*args)
    return out if n == n_true else out[:, :n_true]


ROUTE_LANES = 128
MOE_TILE = 1024
MOE_ROWS = 160
NEG = -1e30


def _router_body(x_ref, g_ref, w_ref, b_ref, u_ref, xn_ref, rank_ref, wt_ref):
    x = x_ref[...]
    tm = x.shape[0]
    xn = x * lax.rsqrt(jnp.mean(x * x, axis=-1, keepdims=True) + EPS) * g_ref[...]
    xn_ref[...] = xn.astype(BF16)
    logits = jnp.dot(xn, w_ref[...], precision=lax.Precision.HIGHEST, preferred_element_type=F32) + b_ref[...]
    lane = lax.broadcasted_iota(jnp.int32, (tm, ROUTE_LANES), 1).astype(F32)

    def first_max(mask):
        v = jnp.max(jnp.where(mask, logits, NEG), axis=-1, keepdims=True)
        i = jnp.min(jnp.where(mask & (logits == v), lane, float(ROUTE_LANES)), axis=-1, keepdims=True)
        return v, i

    is_group = lane < MOE_GROUPS
    mg, g_idx = first_max(is_group)
    pg_top = 1.0 / jnp.sum(jnp.where(is_group, jnp.exp(logits - mg), 0.0), axis=-1, keepdims=True)
    lo = MOE_GROUPS + MOE_EPG * g_idx
    in_group = (lane >= lo) & (lane < lo + MOE_EPG)
    v1, i1 = first_max(in_group)
    v2, i2 = first_max(in_group & (lane != i1))
    e2 = jnp.exp(v2 - v1)
    w_a = pg_top / (1.0 + e2)
    w_b = pg_top * e2 / (1.0 + e2)
    info = jnp.where(lane == 0, i1 - MOE_GROUPS, jnp.where(lane == 1, i2 - MOE_GROUPS,
                     jnp.where(lane == 2, w_a, jnp.where(lane == 3, w_b, 0.0))))
    info_t = info.T
    e_a, e_b, w_at, w_bt = info_t[0:1], info_t[1:2], info_t[2:3], info_t[3:4]
    expert = lax.broadcasted_iota(jnp.int32, (MOE_EXPERTS, tm), 0).astype(F32)
    m_a = e_a == expert
    m_b = e_b == expert
    onehot = jnp.concatenate([jnp.where(m_a, 1.0, 0.0), jnp.where(m_b, 1.0, 0.0)], axis=0).astype(BF16)
    cum = jnp.dot(onehot, u_ref[...], preferred_element_type=F32)
    cum_a, cum_b = cum[:MOE_EXPERTS], cum[MOE_EXPERTS:]
    n_a = cum_a[:, tm - 1:tm]
    rank_ref[...] = jnp.where(m_a, cum_a - 1.0, jnp.where(m_b, n_a + cum_b - 1.0, -1.0))
    wt_ref[...] = jnp.where(m_a, w_at, jnp.where(m_b, w_bt, 0.0))


def moe_route(x, g, w_rg, b_rg, w_re, b_re, tm):
    t = x.shape[0]
    pad = ROUTE_LANES - MOE_GROUPS - MOE_EXPERTS
    w = jnp.pad(jnp.concatenate([w_rg, w_re], axis=1), ((0, 0), (0, pad)))
    b = jnp.pad(jnp.concatenate([b_rg, b_re]), (0, pad)).reshape(1, ROUTE_LANES)
    upper = jnp.asarray(np.triu(np.ones((tm, tm), np.float32)), BF16)
    const = lambda a: pl.BlockSpec(a.shape, lambda i: (0,) * a.ndim)
    g2 = g.reshape(1, D_MODEL)
    return pl.pallas_call(
        _router_body,
        name="moe_router",
        grid=(t // tm,),
        in_specs=[pl.BlockSpec((tm, D_MODEL), lambda i: (i, 0)), const(g2), const(w), const(b), const(upper)],
        out_specs=[pl.BlockSpec((tm, D_MODEL), lambda i: (i, 0)),
                   pl.BlockSpec((MOE_EXPERTS, tm), lambda i: (0, i)),
                   pl.BlockSpec((MOE_EXPERTS, tm), lambda i: (0, i))],
        out_shape=[jax.ShapeDtypeStruct((t, D_MODEL), BF16),
                   jax.ShapeDtypeStruct((MOE_EXPERTS, t), F32),
                   jax.ShapeDtypeStruct((MOE_EXPERTS, t), F32)],
        compiler_params=pltpu.CompilerParams(
            dimension_semantics=("parallel",), vmem_limit_bytes=VMEM_LIMIT_BYTES),
    )(x, g2, w, b, upper)


def _moe_expert_body(cnt_ref, xn_ref, rank_ref, wt_ref, w1_ref, w3_ref, w2_ref, res_ref, o_ref, *, rows):
    ti, e = pl.program_id(0), pl.program_id(1)

    @pl.when(e == 0)
    def _():
        o_ref[...] = res_ref[...]

    tm = xn_ref.shape[0]
    rank = rank_ref[pl.ds(e, 1), :]
    wt = wt_ref[pl.ds(e, 1), :]
    n_chunks = (cnt_ref[e, ti] + rows - 1) // rows

    def chunk(c, carry):
        r = (lax.broadcasted_iota(jnp.int32, (rows, tm), 0) + c * rows).astype(F32)
        sel = jnp.where(rank == r, 1.0, 0.0)
        selb = sel.astype(BF16)
        xs = jnp.dot(selb, xn_ref[...], preferred_element_type=F32).astype(BF16)
        a = jnp.dot(xs, w1_ref[0].astype(BF16), preferred_element_type=F32)
        b = jnp.dot(xs, w3_ref[0].astype(BF16), preferred_element_type=F32)
        h = (a * jax.nn.sigmoid(a) * b).astype(BF16)
        y = jnp.dot(h, w2_ref[0].astype(BF16), preferred_element_type=F32)
        y = (y * jnp.sum(sel * wt, axis=1, keepdims=True)).astype(BF16)
        o_ref[...] += lax.dot_general(selb, y, (((0,), (0,)), ((), ())), preferred_element_type=F32)
        return carry

    lax.fori_loop(0, n_chunks, chunk, 0)


def moe_experts(xn, rank, wt, counts, w1, w3, w2, res, tm):
    t = xn.shape[0]
    rows = min(MOE_ROWS, tm)
    grid_spec = pltpu.PrefetchScalarGridSpec(
        num_scalar_prefetch=1,
        grid=(t // tm, MOE_EXPERTS),
        in_specs=[
            pl.BlockSpec((tm, D_MODEL), lambda i, e, c: (i, 0)),
            pl.BlockSpec((MOE_EXPERTS, tm), lambda i, e, c: (0, i)),
            pl.BlockSpec((MOE_EXPERTS, tm), lambda i, e, c: (0, i)),
            pl.BlockSpec((1, D_MODEL, MOE_FF), lambda i, e, c: (e, 0, 0)),
            pl.BlockSpec((1, D_MODEL, MOE_FF), lambda i, e, c: (e, 0, 0)),
            pl.BlockSpec((1, MOE_FF, D_MODEL), lambda i, e, c: (e, 0, 0)),
            pl.BlockSpec((tm, D_MODEL), lambda i, e, c: (i, 0)),
        ],
        out_specs=pl.BlockSpec((tm, D_MODEL), lambda i, e, c: (i, 0)),
    )
    return pl.pallas_call(
        functools.partial(_moe_expert_body, rows=rows),
        name="moe_experts",
        grid_spec=grid_spec,
        out_shape=jax.ShapeDtypeStruct((t, D_MODEL), F32),
        compiler_params=pltpu.CompilerParams(
            dimension_semantics=("parallel", "arbitrary"),
            vmem_limit_bytes=VMEM_LIMIT_BYTES),
    )(counts, xn, rank, wt, w1, w3, w2, res)


def hier_moe_residual(x, g, w_rg, b_rg, w_re, b_re, w1, w3, w2):
    t_true = x.shape[0]
    tm = MOE_TILE if t_true % MOE_TILE == 0 else ROUTE_LANES
    if t_true % tm:
        x = jnp.pad(x, ((0, tm - t_true % tm), (0, 0)))
    t = x.shape[0]
    xn, rank, wt = moe_route(x, g, w_rg, b_rg, w_re, b_re, tm)
    counts = jnp.sum((rank >= 0).reshape(MOE_EXPERTS, t // tm, tm), axis=-1, dtype=jnp.int32)
    return moe_experts(xn, rank, wt, counts, w1, w3, w2, x, tm)[:t_true]


def _causal_dwconv(u, buf, w, b):
    L = u.shape[1]
    ext = jnp.concatenate([buf.astype(u.dtype), u], axis=1)
    out = b + sum(ext[:, k:k + L] * w[k] for k in range(M_CONV))
    return out, ext[:, L:]


def _ssd_scan(x, dt, a, bm, cm, h0):
    bsz, L = x.shape[0], x.shape[1]
    q = M_CHUNK if L % M_CHUNK == 0 else L
    nc = L // q
    hpg = M_HEADS // M_GROUPS
    xdt = (x * dt[..., None]).reshape(bsz, nc, q, M_GROUPS, hpg, M_HEADDIM)
    acum = jnp.cumsum((dt * a).reshape(bsz, nc, q, M_GROUPS, hpg), axis=2)
    bc = bm.reshape(bsz, nc, q, M_GROUPS, M_DSTATE)
    cc = cm.reshape(bsz, nc, q, M_GROUPS, M_DSTATE)
    tri = jnp.tril(jnp.ones((q, q), bool))[:, :, None, None]
    seg = acum[:, :, :, None] - acum[:, :, None]
    decay = jnp.exp(jnp.where(tri, seg, -jnp.inf))
    cb = jnp.einsum('bclgn,bcsgn->bclsg', cc, bc)
    y_diag = jnp.einsum('bclsg,bclsgh,bcsghp->bclghp', cb, decay, xdt)
    decay_end = jnp.exp(acum[:, :, -1:] - acum)
    states = jnp.einsum('bcsgn,bcsgh,bcsghp->bcghpn', bc, decay_end, xdt)
    chunk_decay = jnp.exp(acum[:, :, -1])

    def step(h, inp):
        cd, st = inp
        return cd[..., None, None] * h + st, h

    h_last, h_prev = lax.scan(step, h0.reshape(bsz, M_GROUPS, hpg, M_HEADDIM, M_DSTATE),
                              (chunk_decay.swapaxes(0, 1), states.swapaxes(0, 1)))
    y_off = jnp.einsum('bclgn,bclgh,bcghpn->bclghp', cc, jnp.exp(acum), h_prev.swapaxes(0, 1))
    y = (y_diag + y_off).reshape(bsz, L, M_HEADS, M_HEADDIM)
    return y, h_last.reshape(bsz, M_HEADS, M_HEADDIM, M_DSTATE)


def mamba_mixer(x, g, conv_buf, h0, w_in, conv_w, conv_b, dt_bias, a_log, d_skip, norm_g, w_out):
    bsz, L, _ = x.shape
    proj = norm_matmul(x.reshape(bsz * L, D_MODEL), g, w_in).reshape(bsz, L, -1)
    z = proj[..., :M_DINNER]
    xbc = proj[..., M_DINNER:M_DINNER + M_CONV_DIM]
    dt_raw = proj[..., M_DINNER + M_CONV_DIM:]
    xbc, new_buf = _causal_dwconv(xbc, conv_buf, conv_w, conv_b)
    xbc = jax.nn.silu(xbc)
    xs = xbc[..., :M_DINNER].reshape(bsz, L, M_HEADS, M_HEADDIM)
    bm = xbc[..., M_DINNER:M_DINNER + M_GN].reshape(bsz, L, M_GROUPS, M_DSTATE)
    cm = xbc[..., M_DINNER + M_GN:].reshape(bsz, L, M_GROUPS, M_DSTATE)
    dt = jax.nn.softplus(dt_raw + dt_bias)
    a = -jnp.exp(a_log)
    y, h = _ssd_scan(xs, dt, a, bm, cm, h0)
    y = y + xs * d_skip[:, None]
    y = y.reshape(bsz, L, M_DINNER)
    y = _rmsnorm(y * jax.nn.silu(z), norm_g)
    out = norm_matmul(y.reshape(bsz * L, M_DINNER), None, w_out, normalize=False)
    return out.reshape(bsz, L, D_MODEL), new_buf, h


SSD_HPG = M_HEADS // M_GROUPS
SSD_GROUP_ROWS = SSD_HPG * M_HEADDIM
CONV_PAD = 8


def _transpose_cols(x):
    return jnp.concatenate([x[:, j * LANES:(j + 1) * LANES].T for j in range(x.shape[1] // LANES)], axis=0)


def _transpose_rows(x):
    return jnp.concatenate([x[j * LANES:(j + 1) * LANES, :].T for j in range(x.shape[0] // LANES)], axis=1)


def _ssd_body(z_ref, xbc_ref, dtr_ref, cw_ref, cb_ref, dtb_ref, a_ref, dcol_ref, ng_ref, ltri_ref,
              y_ref, conv_ref, h_ref, xbuf, *, q):
    c = pl.program_id(1)

    @pl.when(c == 0)
    def _():
        h_ref[...] = jnp.zeros_like(h_ref)
        xbuf[0:CONV_PAD, :] = jnp.zeros((CONV_PAD, M_CONV_DIM), F32)

    xbuf[CONV_PAD:CONV_PAD + q, :] = xbc_ref[...]
    conv = cb_ref[...]
    for k in range(M_CONV):
        start = CONV_PAD - (M_CONV - 1) + k
        conv = conv + xbuf[start:start + q, :] * cw_ref[k:k + 1, :]
    tail = xbuf[CONV_PAD + q - (M_CONV - 1):CONV_PAD + q, :]
    xbuf[CONV_PAD - (M_CONV - 1):CONV_PAD, :] = tail
    conv_ref[0] = tail
    xc = conv * jax.nn.sigmoid(conv)
    xs = xc[:, :M_DINNER]
    xs_t = _transpose_cols(xs)

    pre = dtr_ref[...] + dtb_ref[...]
    dt = jnp.maximum(pre, 0.0) + jnp.log1p(jnp.exp(-jnp.abs(pre)))
    da = dt * a_ref[...]
    ltri = ltri_ref[...]
    acum = sum(jnp.dot(ltri, part, preferred_element_type=F32) for part in _split3(da))
    dt_t = dt.T
    acum_t = acum.T
    li = lax.broadcasted_iota(jnp.int32, (q, q), 0)
    si = lax.broadcasted_iota(jnp.int32, (q, q), 1)
    causal = li >= si

    y_t = []
    for g in range(M_GROUPS):
        bm = xc[:, M_DINNER + g * M_DSTATE:M_DINNER + (g + 1) * M_DSTATE].astype(BF16)
        cm = xc[:, M_DINNER + M_GN + g * M_DSTATE:M_DINNER + M_GN + (g + 1) * M_DSTATE].astype(BF16)
        cb = _nt_dot(cm, bm)
        r0 = g * SSD_GROUP_ROWS
        h_prev = h_ref[0, r0:r0 + SSD_GROUP_ROWS, :]
        y_off = _nt_dot(h_prev.astype(BF16), cm)
        x_dec, scale = [], []
        for hh in range(g * SSD_HPG, (g + 1) * SSD_HPG):
            a_row = acum_t[hh:hh + 1, :]
            a_col = acum[:, hh:hh + 1]
            decay = jnp.where(causal, jnp.exp(a_col - a_row), 0.0)
            m = (cb * decay).astype(BF16)
            rows = slice(hh * M_HEADDIM, (hh + 1) * M_HEADDIM)
            xs_h = xs_t[rows]
            xdt = xs_h * dt_t[hh:hh + 1, :]
            y_h = _nt_dot(xdt.astype(BF16), m)
            y_h = y_h + y_off[rows.start - r0:rows.stop - r0] * jnp.exp(a_row) + xs_h * dcol_ref[hh:hh + 1, :]
            y_t.append(y_h)
            a_last = a_row[:, q - 1:q]
            x_dec.append(xdt * jnp.exp(a_last - a_row))
            scale.append(jnp.broadcast_to(jnp.exp(a_last), (M_HEADDIM, 1)))
        upd = jnp.dot(jnp.concatenate(x_dec, axis=0).astype(BF16), bm, preferred_element_type=F32)
        h_ref[0, r0:r0 + SSD_GROUP_ROWS, :] = jnp.concatenate(scale, axis=0) * h_prev + upd

    y = _transpose_rows(jnp.concatenate(y_t, axis=0))
    zz = z_ref[...]
    yg = y * (zz * jax.nn.sigmoid(zz))
    yg = yg * lax.rsqrt(jnp.mean(yg * yg, axis=-1, keepdims=True) + EPS) * ng_ref[...]
    y_ref[...] = yg.astype(BF16)


def ssd_prompt(z, xbc, dtr, conv_w, conv_b, dt_bias, a_log, d_skip, norm_g, bsz, L):
    q = M_CHUNK
    assert L % q == 0 and q == LANES
    nc = L // q
    padl = lambda v: jnp.pad(v, (0, LANES - v.shape[0]))
    dtb = padl(dt_bias).reshape(1, LANES)
    a_row = padl(-jnp.exp(a_log)).reshape(1, LANES)
    dcol = jnp.broadcast_to(padl(d_skip).reshape(LANES, 1), (LANES, LANES))
    ltri = jnp.asarray(np.tril(np.ones((q, q), np.float32)), BF16)
    const = lambda a: pl.BlockSpec(a.shape, lambda b, c: (0,) * a.ndim)
    tok = lambda w: pl.BlockSpec((q, w), lambda b, c: (b * nc + c, 0))
    cb2 = conv_b.reshape(1, M_CONV_DIM)
    ng2 = norm_g.reshape(1, M_DINNER)
    y, conv_tail, h = pl.pallas_call(
        functools.partial(_ssd_body, q=q),
        name="ssd_chunks",
        grid=(bsz, nc),
        in_specs=[tok(M_DINNER), tok(M_CONV_DIM), tok(LANES), const(conv_w), const(cb2), const(dtb),
                  const(a_row), const(dcol), const(ng2), const(ltri)],
        out_specs=[tok(M_DINNER),
                   pl.BlockSpec((1, M_CONV - 1, M_CONV_DIM), lambda b, c: (b, 0, 0)),
                   pl.BlockSpec((1, M_DINNER, M_DSTATE), lambda b, c: (b, 0, 0))],
        out_shape=[jax.ShapeDtypeStruct((bsz * L, M_DINNER), BF16),
                   jax.ShapeDtypeStruct((bsz, M_CONV - 1, M_CONV_DIM), F32),
                   jax.ShapeDtypeStruct((bsz, M_DINNER, M_DSTATE), F32)],
        scratch_shapes=[pltpu.VMEM((CONV_PAD + q, M_CONV_DIM), F32)],
        compiler_params=pltpu.CompilerParams(
            dimension_semantics=("parallel", "arbitrary"),
            vmem_limit_bytes=VMEM_LIMIT_BYTES),
    )(z, xbc, dtr, conv_w, cb2, dtb, a_row, dcol, ng2, ltri)
    return y, conv_tail, h.reshape(bsz, M_HEADS, M_HEADDIM, M_DSTATE)


def mamba_prompt_mixer(x, g, w_in, conv_w, conv_b, dt_bias, a_log, d_skip, norm_g, w_out):
    bsz, L, _ = x.shape
    x2 = x.reshape(bsz * L, D_MODEL)
    z = norm_matmul(x2, g, w_in[:, :M_DINNER])
    xbc = norm_matmul(x2, g, w_in[:, M_DINNER:M_DINNER + M_CONV_DIM])
    dtr = norm_matmul(x2, g, jnp.pad(w_in[:, M_DINNER + M_CONV_DIM:], ((0, 0), (0, LANES - M_HEADS))))
    y, conv_tail, h = ssd_prompt(z, xbc, dtr, conv_w, conv_b, dt_bias, a_log, d_skip, norm_g, bsz, L)
    out = norm_matmul(y, None, w_out, normalize=False, res=x2)
    return out.reshape(bsz, L, D_MODEL), conv_tail, h


def _rel_bucket(dist):
    exact = REL_BUCKETS // 2
    d = jnp.maximum(dist, 0)
    ratio = jnp.log(jnp.maximum(d, 1).astype(F32) / exact) / math.log(REL_MAX_DIST / exact)
    large = jnp.minimum(exact + (ratio * (REL_BUCKETS - exact)).astype(jnp.int32), REL_BUCKETS - 1)
    return jnp.where(d < exact, d, large)


def _head_bias(rel_table, dist):
    b = rel_table[_rel_bucket(dist)].astype(F32)
    return jnp.moveaxis(b, -1, 0).reshape(KV_HEADS, HPG, dist.shape[0], dist.shape[1])


def _masked_softmax(s, valid):
    s = jnp.where(valid, s, -1e30)
    e = jnp.where(valid, jnp.exp(s - jnp.max(s, axis=-1, keepdims=True)), 0.0)
    return e / jnp.maximum(jnp.sum(e, axis=-1, keepdims=True), 1e-30)


def _group_attend(qg, qpos, k, v, kpos, valid, rel_table):
    s = jnp.einsum('bqghd,bkgd->bghqk', qg, k).astype(F32) * SCALE
    s = s + _head_bias(rel_table, qpos[:, None] - kpos[None, :])
    p = _masked_softmax(s, valid)
    o = jnp.einsum('bghqk,bkgd->bqghd', p.astype(v.dtype), v)
    return o, p


def _compress(rows, pe, w1, w2):
    bsz, T = rows.shape[0], rows.shape[1]
    nc = (T - CMP_BLOCK) // CMP_STRIDE + 1
    idx = jnp.arange(nc)[:, None] * CMP_STRIDE + jnp.arange(CMP_BLOCK)[None, :]
    blk = rows[:, idx] + pe[:, None, :]
    blk = jnp.moveaxis(blk, 2, 3).reshape(bsz, nc, KV_HEADS, CMP_BLOCK * HEAD_DIM)
    return jax.nn.silu(blk @ w1) @ w2


def _select_blocks(p_cmp, qpos, n_sel):
    nc = p_cmp.shape[-1]
    cstart = jnp.arange(nc) * CMP_STRIDE
    sstart = jnp.arange(n_sel) * SEL_BLOCK
    overlap = ((cstart[:, None] < sstart[None, :] + SEL_BLOCK) &
               (cstart[:, None] + CMP_BLOCK > sstart[None, :])).astype(F32)
    imp = jnp.einsum('bghqc,cs->bgqs', p_cmp, overlap)
    qblk = qpos // SEL_BLOCK
    j = jnp.arange(n_sel)
    forced = (j[None, :] == qblk[:, None]) | (j[None, :] == 0)
    score = jnp.where(forced, 1e9, jnp.where(j[None, :] <= qblk[:, None], imp, -1e30))
    _, idx = lax.top_k(score, min(SEL_TOPK, n_sel))
    return idx


def _sel_attend(qg, qpos, idx, kblk, vblk, rel_table):
    bsz, lq = qg.shape[0], qg.shape[1]
    bi = jnp.arange(bsz)[:, None, None, None]
    gi = jnp.arange(KV_HEADS)[None, :, None, None]
    kg = kblk[bi, gi, idx]
    vg = vblk[bi, gi, idx]
    s = jnp.einsum('bqghd,bgqkrd->bghqkr', qg, kg).astype(F32) * SCALE
    kpos = idx[..., None] * SEL_BLOCK + jnp.arange(SEL_BLOCK)
    dist = qpos[:, None, None] - kpos
    g5 = jnp.arange(KV_HEADS)[None, :, None, None, None]
    bias = rel_table.reshape(REL_BUCKETS, KV_HEADS, HPG)[_rel_bucket(dist), g5].astype(F32)
    s = s + jnp.moveaxis(bias, -1, 2)
    nk = idx.shape[-1]
    valid = (dist >= 0)[:, :, None].reshape(bsz, KV_HEADS, 1, lq, nk * SEL_BLOCK)
    p = _masked_softmax(s.reshape(bsz, KV_HEADS, HPG, lq, nk * SEL_BLOCK), valid)
    p = p.reshape(bsz, KV_HEADS, HPG, lq, nk, SEL_BLOCK)
    return jnp.einsum('bghqkr,bgqkrd->bqghd', p.astype(vg.dtype), vg)


def nsa_mixer(x, g, pos0, kv_cmp_past, kv_sel_past, kv_win_past, n_keep,
              w_in, q_g, k_g, cmp_pe, cmp_w1, cmp_w2, w_out, rel_table):
    bsz, L, _ = x.shape
    sizes = [N_HEADS * HEAD_DIM] + [KV_DIM] * 6 + [3 * N_HEADS]
    proj = norm_matmul(x.reshape(bsz * L, D_MODEL), g, w_in).reshape(bsz, L, -1)
    q, kc, vc, ks, vs, kw, vw, gl = jnp.split(proj, np.cumsum(sizes)[:-1].tolist(), axis=-1)
    q = _rmsnorm(q.reshape(bsz, L, KV_HEADS, HPG, HEAD_DIM), q_g)

    def kv(t):
        return t.reshape(bsz, L, KV_HEADS, HEAD_DIM)

    new_cmp = jnp.stack([kv(kc), kv(vc)], axis=2)
    new_sel = jnp.stack([_rmsnorm(kv(ks), k_g[1]), kv(vs)], axis=2)
    new_win = jnp.stack([_rmsnorm(kv(kw), k_g[2]), kv(vw)], axis=2)
    qpos = pos0 + jnp.arange(L)

    crows = jnp.concatenate([kv_cmp_past.astype(x.dtype), new_cmp], axis=1)
    kcmp = _rmsnorm(_compress(crows[:, :, 0], cmp_pe[0], cmp_w1[0], cmp_w2[0]), k_g[0])
    vcmp = _compress(crows[:, :, 1], cmp_pe[1], cmp_w1[1], cmp_w2[1])
    ends = jnp.arange(kcmp.shape[1]) * CMP_STRIDE + CMP_BLOCK - 1
    o_cmp, p_cmp = _group_attend(q, qpos, kcmp, vcmp, ends, qpos[:, None] >= ends[None, :], rel_table)

    srows = jnp.concatenate([kv_sel_past.astype(x.dtype), new_sel], axis=1)
    T = srows.shape[1]
    n_sel = -(-T // SEL_BLOCK)
    srows = jnp.pad(srows, ((0, 0), (0, n_sel * SEL_BLOCK - T), (0, 0), (0, 0), (0, 0)))
    blocks = srows.reshape(bsz, n_sel, SEL_BLOCK, 2, KV_HEADS, HEAD_DIM).transpose(3, 0, 4, 1, 2, 5)
    idx = _select_blocks(p_cmp, qpos, n_sel)
    qb = SEL_QBLK if L % SEL_QBLK == 0 else L
    nqb = L // qb

    def sel_block(args):
        qgi, qposi, idxi = args
        return _sel_attend(qgi, qposi, idxi, blocks[0], blocks[1], rel_table)

    o_sel = lax.map(sel_block, (q.reshape(bsz, nqb, qb, KV_HEADS, HPG, HEAD_DIM).swapaxes(0, 1),
                                qpos.reshape(nqb, qb),
                                idx.reshape(bsz, KV_HEADS, nqb, qb, -1).transpose(2, 0, 1, 3, 4)))
    o_sel = o_sel.swapaxes(0, 1).reshape(bsz, L, KV_HEADS, HPG, HEAD_DIM)

    p_win = kv_win_past.shape[1]
    wrows = jnp.concatenate([kv_win_past.astype(x.dtype), new_win], axis=1)
    wpad = jnp.pad(wrows, ((0, 0), (WINDOW, 0), (0, 0), (0, 0), (0, 0)))
    n_all = WINDOW + p_win + L
    kpos_all = pos0 - p_win - WINDOW + jnp.arange(n_all)
    kvalid_all = jnp.arange(n_all) >= WINDOW
    wq = WIN_QBLK if L % WIN_QBLK == 0 else L

    def win_block(i):
        start = p_win + i * wq
        qgi = lax.dynamic_slice_in_dim(q, i * wq, wq, axis=1)
        kvi = lax.dynamic_slice_in_dim(wpad, start, WINDOW + wq, axis=1)
        kposi = lax.dynamic_slice_in_dim(kpos_all, start, WINDOW + wq)
        kvalidi = lax.dynamic_slice_in_dim(kvalid_all, start, WINDOW + wq)
        qposi = pos0 + i * wq + jnp.arange(wq)
        dist = qposi[:, None] - kposi[None, :]
        valid = kvalidi[None, :] & (dist >= 0) & (dist <= WINDOW)
        o, _ = _group_attend(qgi, qposi, kvi[:, :, 0], kvi[:, :, 1], kposi, valid, rel_table)
        return o

    o_win = lax.map(win_block, jnp.arange(L // wq)).swapaxes(0, 1).reshape(bsz, L, KV_HEADS, HPG, HEAD_DIM)

    gate = jax.nn.sigmoid(gl).reshape(bsz, L, KV_HEADS, HPG, 3)
    o = gate[..., 0:1] * o_cmp + gate[..., 1:2] * o_sel + gate[..., 2:3] * o_win
    y = norm_matmul(o.reshape(bsz * L, N_HEADS * HEAD_DIM), None, w_out, normalize=False)
    return y.reshape(bsz, L, D_MODEL), new_cmp, new_sel, wrows[:, -n_keep:]


ATT_TQ = 128
ATT_TK = 128
NEG = -1e30


def _nt_dot(a, b):
    return lax.dot_general(a, b, (((1,), (1,)), ((), ())), preferred_element_type=F32)


def _compress_body(rk_ref, rv_ref, pe_ref, w1_ref, w2_ref, kg_ref, kc_ref, vc_ref, *, nb):
    half = (CMP_BLOCK // 2) * HEAD_DIM
    for kv, (r_ref, o_ref) in enumerate(((rk_ref, kc_ref), (rv_ref, vc_ref))):
        lo = (r_ref[0, 0, 0:nb, :] + pe_ref[kv, 0:1, :]).astype(BF16)
        hi = (r_ref[0, 0, 1:nb + 1, :] + pe_ref[kv, 1:2, :]).astype(BF16)
        h = (jnp.dot(lo, w1_ref[kv, :half, :].astype(BF16), preferred_element_type=F32)
             + jnp.dot(hi, w1_ref[kv, half:, :].astype(BF16), preferred_element_type=F32))
        h = h * jax.nn.sigmoid(h)
        o = jnp.dot(h.astype(BF16), w2_ref[kv].astype(BF16), preferred_element_type=F32)
        if kv == 0:
            o = o * lax.rsqrt(jnp.mean(o * o, axis=-1, keepdims=True) + EPS) * kg_ref[...]
        o_ref[0, 0] = o


def compress_rows(rk, rv, pe, w1, w2, kg):
    bsz, g, nbp, width = rk.shape
    nb = nbp - 8
    strip = pl.BlockSpec((1, 1, nbp, width), lambda b, j: (b, j, 0, 0))
    out = pl.BlockSpec((1, 1, nb, HEAD_DIM), lambda b, j: (b, j, 0, 0))
    full = lambda a: pl.BlockSpec(a.shape, lambda b, j: (0,) * a.ndim)
    pe2 = pe.reshape(2, 2, width)
    kg2 = kg.reshape(1, HEAD_DIM)
    return pl.pallas_call(
        functools.partial(_compress_body, nb=nb),
        name="nsa_compress",
        grid=(bsz, g),
        in_specs=[strip, strip, full(pe2), full(w1), full(w2), full(kg2)],
        out_specs=[out, out],
        out_shape=[jax.ShapeDtypeStruct((bsz, g, nb, HEAD_DIM), F32)] * 2,
        compiler_params=pltpu.CompilerParams(
            dimension_semantics=("parallel", "parallel"),
            vmem_limit_bytes=VMEM_LIMIT_BYTES),
    )(rk, rv, pe2, w1, w2, kg2)


def _split3(x):
    a = x.astype(BF16)
    r = x - a.astype(F32)
    b = r.astype(BF16)
    c = (r - b.astype(F32)).astype(BF16)
    return a, b, c


def _nsa_attn_body(q_ref, kc_ref, vc_ref, ks_ref, vs_ref, kw_ref, vw_ref, gate_ref, bcmp_ref, btile_ref,
                   ovl_ref, o_ref, *, n_cmp, n_sel, topk):
    i = pl.program_id(2)
    q0 = i * ATT_TQ
    rows = HPG * ATT_TQ
    qt = q_ref[0]
    qb = jnp.concatenate([qt[:, h * HEAD_DIM:(h + 1) * HEAD_DIM] for h in range(HPG)], axis=0).astype(BF16)
    row = lax.broadcasted_iota(jnp.int32, (rows, ATT_TK), 0)
    qpos = q0 + (row & (ATT_TQ - 1))
    col = lax.broadcasted_iota(jnp.int32, (rows, ATT_TK), 1)

    s = _nt_dot(qb, kc_ref[0, 0].astype(BF16)) * SCALE + bcmp_ref[0, 0]
    valid = (qpos >= col * CMP_STRIDE + (CMP_BLOCK - 1)) & (col < n_cmp)
    s = jnp.where(valid, s, NEG)
    e = jnp.where(valid, jnp.exp(s - jnp.max(s, axis=-1, keepdims=True)), 0.0)
    p = e / jnp.maximum(jnp.sum(e, axis=-1, keepdims=True), 1e-30)
    pb = p.astype(BF16)
    o_cmp = jnp.dot(pb, vc_ref[0, 0].astype(BF16), preferred_element_type=F32)

    pf = pb.astype(F32)
    psum = pf[0:ATT_TQ]
    for h in range(1, HPG):
        psum = psum + pf[h * ATT_TQ:(h + 1) * ATT_TQ]
    ovl = ovl_ref[...]
    imp_t = sum(_nt_dot(ovl, part) for part in _split3(psum))
    nblk = imp_t.shape[0]
    j = lax.broadcasted_iota(jnp.int32, (nblk, ATT_TQ), 0)
    qblk = (q0 + lax.broadcasted_iota(jnp.int32, (nblk, ATT_TQ), 1)) // SEL_BLOCK
    forced = (j == qblk) | (j == 0)
    score = jnp.where(forced, 1e9, jnp.where(j <= qblk, imp_t, NEG))
    score = jnp.where(j < n_sel, score, -3e38)
    rank = jnp.zeros((nblk, ATT_TQ), F32)
    for jp in range(n_sel):
        other = score[jp:jp + 1, :]
        beats = (other > score) | ((other == score) & (jp < j))
        rank = rank + jnp.where(beats, 1.0, 0.0)
    sel = jnp.where((rank < topk) & (j < n_sel), 1.0, 0.0).T.astype(BF16)

    jj = lax.broadcasted_iota(jnp.int32, (nblk, ATT_TK), 0)
    kk = lax.broadcasted_iota(jnp.int32, (nblk, ATT_TK), 1) // SEL_BLOCK
    blocks_per_step = ATT_TK // SEL_BLOCK

    def flash(k_ref, v_ref, lo, hi, valid_fn):
        def step(kc, carry):
            m, l, acc = carry
            start = pl.multiple_of(kc * ATT_TK, ATT_TK)
            kblk = k_ref[0, 0, pl.ds(start, ATT_TK), :].astype(BF16)
            vblk = v_ref[0, 0, pl.ds(start, ATT_TK), :].astype(BF16)
            sc = _nt_dot(qb, kblk) * SCALE + btile_ref[0, jnp.minimum(i - kc, 2)]
            ok = valid_fn(kc, qpos - (kc * ATT_TK + col))
            sc = jnp.where(ok, sc, NEG)
            m_new = jnp.maximum(m, jnp.max(sc, axis=-1, keepdims=True))
            alpha = jnp.exp(m - m_new)
            pr = jnp.where(ok, jnp.exp(sc - m_new), 0.0)
            l = alpha * l + jnp.sum(pr, axis=-1, keepdims=True)
            acc = alpha * acc + jnp.dot(pr.astype(BF16), vblk, preferred_element_type=F32)
            return m_new, l, acc

        init = (jnp.full((rows, 1), NEG, F32), jnp.zeros((rows, 1), F32), jnp.zeros((rows, HEAD_DIM), F32))
        _, l, acc = lax.fori_loop(lo, hi, step, init)
        return acc / jnp.maximum(l, 1e-30)

    def sel_valid(kc, dist):
        expand = jnp.where(jj == kc * blocks_per_step + kk, 1.0, 0.0).astype(BF16)
        chosen = jnp.dot(sel, expand, preferred_element_type=F32)
        chosen = jnp.concatenate([chosen] * HPG, axis=0)
        return (chosen > 0.5) & (dist >= 0)

    def win_valid(kc, dist):
        return (dist >= 0) & (dist <= WINDOW)

    o_sel = flash(ks_ref, vs_ref, 0, i + 1, sel_valid)
    o_win = flash(kw_ref, vw_ref, jnp.maximum(i - WINDOW // ATT_TK, 0), i + 1, win_valid)

    gate = jax.nn.sigmoid(gate_ref[0, 0])

    def gcol(br):
        return jnp.concatenate([gate[:, h * 3 + br:h * 3 + br + 1] for h in range(HPG)], axis=0)

    o = gcol(0) * o_cmp + gcol(1) * o_sel + gcol(2) * o_win
    o_ref[0] = jnp.concatenate([o[h * ATT_TQ:(h + 1) * ATT_TQ] for h in range(HPG)], axis=1)


def _bias_tables(rel_table, n_qtiles):
    tab = rel_table[_rel_bucket(jnp.arange(REL_MAX_DIST + 1))].astype(F32)
    tab = tab.T.reshape(KV_HEADS, HPG, REL_MAX_DIST + 1)
    t = np.arange(ATT_TQ)[:, None]
    s = np.arange(ATT_TK)[None, :]
    d_tile = np.stack([np.clip(delta * ATT_TK + t - s, 0, REL_MAX_DIST) for delta in range(3)])
    btile = tab[:, :, d_tile]
    btile = btile.transpose(0, 2, 1, 3, 4).reshape(KV_HEADS, 3, HPG * ATT_TQ, ATT_TK)
    qi = np.arange(n_qtiles)[:, None, None] * ATT_TQ
    d_cmp = np.clip(qi + t[None] - (s[None] * CMP_STRIDE + CMP_BLOCK - 1), 0, REL_MAX_DIST)
    bcmp = tab[:, :, d_cmp]
    bcmp = bcmp.transpose(2, 0, 1, 3, 4).reshape(n_qtiles, KV_HEADS, HPG * ATT_TQ, ATT_TK)
    return btile, bcmp


def nsa_prompt_attention(q, kcmp, vcmp, ks, vs, kw, vw, gl, rel_table):
    bsz, L, _ = q.shape
    assert L % ATT_TQ == 0 and kcmp.shape[2] == ATT_TK
    nq = L // ATT_TQ
    n_cmp = (L - CMP_BLOCK) // CMP_STRIDE + 1
    n_sel = L // SEL_BLOCK
    assert n_sel <= ATT_TK
    btile, bcmp = _bias_tables(rel_table, nq)
    c = np.arange(ATT_TK)[None, :] * CMP_STRIDE
    sb = np.arange(ATT_TK)[:, None] * SEL_BLOCK
    ovl = ((c < sb + SEL_BLOCK) & (c + CMP_BLOCK > sb) & (np.arange(ATT_TK)[None, :] < n_cmp)
           & (np.arange(ATT_TK)[:, None] < n_sel))
    ovl = jnp.asarray(ovl, BF16)
    width = HPG * HEAD_DIM
    kvspec = lambda n: pl.BlockSpec((1, 1, n, HEAD_DIM), lambda b, g, i: (b, g, 0, 0))
    return pl.pallas_call(
        functools.partial(_nsa_attn_body, n_cmp=n_cmp, n_sel=n_sel, topk=min(SEL_TOPK, n_sel)),
        name="nsa_attention",
        grid=(bsz, KV_HEADS, nq),
        in_specs=[
            pl.BlockSpec((1, ATT_TQ, width), lambda b, g, i: (b, i, g)),
            kvspec(ATT_TK), kvspec(ATT_TK), kvspec(L), kvspec(L), kvspec(L), kvspec(L),
            pl.BlockSpec((1, 1, ATT_TQ, HPG * 3), lambda b, g, i: (b, g, i, 0)),
            pl.BlockSpec((1, 1, HPG * ATT_TQ, ATT_TK), lambda b, g, i: (i, g, 0, 0)),
            pl.BlockSpec((1, 3, HPG * ATT_TQ, ATT_TK), lambda b, g, i: (g, 0, 0, 0)),
            pl.BlockSpec((ATT_TK, ATT_TK), lambda b, g, i: (0, 0)),
        ],
        out_specs=pl.BlockSpec((1, ATT_TQ, width), lambda b, g, i: (b, i, g)),
        out_shape=jax.ShapeDtypeStruct((bsz, L, N_HEADS * HEAD_DIM), F32),
        compiler_params=pltpu.CompilerParams(
            dimension_semantics=("parallel", "parallel", "arbitrary"),
            vmem_limit_bytes=VMEM_LIMIT_BYTES),
    )(q, kcmp, vcmp, ks, vs, kw, vw, gl, bcmp, btile, ovl)


def nsa_prompt_mixer(x, g, w_in, q_g, k_g, cmp_pe, cmp_w1, cmp_w2, w_out, rel_table):
    bsz, L, _ = x.shape
    t = bsz * L
    proj = norm_matmul(x.reshape(t, D_MODEL), g, w_in)
    c0 = N_HEADS * HEAD_DIM
    q = _rmsnorm(proj[:, :c0].reshape(t, N_HEADS, HEAD_DIM), q_g).reshape(bsz, L, c0)

    def kvpair(k, normed_g):
        kcols = proj[:, c0 + 2 * k * KV_DIM:c0 + (2 * k + 1) * KV_DIM]
        vcols = proj[:, c0 + (2 * k + 1) * KV_DIM:c0 + (2 * k + 2) * KV_DIM]
        if normed_g is not None:
            kcols = _rmsnorm(kcols.reshape(t, KV_HEADS, HEAD_DIM), normed_g).reshape(t, KV_DIM)
        rows = jnp.stack([kcols, vcols], axis=1).reshape(bsz, L, 2, KV_HEADS, HEAD_DIM)
        per_group = rows.transpose(2, 0, 3, 1, 4)
        return rows, per_group[0], per_group[1]

    new_cmp, kc_rows, vc_rows = kvpair(0, None)
    new_sel, ks, vs = kvpair(1, k_g[1])
    new_win, kw, vw = kvpair(2, k_g[2])
    gl = proj[:, c0 + 6 * KV_DIM:].reshape(bsz, L, KV_HEADS, HPG * 3).transpose(0, 2, 1, 3)

    nb = L // CMP_STRIDE

    def strips(r):
        r = r.reshape(bsz, KV_HEADS, nb, CMP_STRIDE * HEAD_DIM)
        return jnp.pad(r, ((0, 0), (0, 0), (0, ATT_TK + 8 - nb), (0, 0)))

    kcmp, vcmp = compress_rows(strips(kc_rows), strips(vc_rows), cmp_pe, cmp_w1, cmp_w2, k_g[0])
    o = nsa_prompt_attention(q, kcmp, vcmp, ks, vs, kw, vw, gl, rel_table)
    y = norm_matmul(o.reshape(t, c0), None, w_out, normalize=False, res=x.reshape(t, D_MODEL))
    return y.reshape(bsz, L, D_MODEL), new_cmp, new_sel, new_win[:, -min(WINDOW, L):]


DEC_PAGES_PER_STEP = 8
STRIP = CMP_STRIDE
ROW_LANES = 2 * KV_DIM
SEL_LANES = 256


def _group_rmsnorm(x, gain_row):
    lane = lax.broadcasted_iota(jnp.int32, x.shape, 1) // HEAD_DIM
    sq = x * x
    ms = jnp.zeros_like(x)
    for grp in range(KV_HEADS):
        tot = jnp.sum(jnp.where(lane == grp, sq, 0.0), axis=-1, keepdims=True)
        ms = jnp.where(lane == grp, tot, ms)
    return x * lax.rsqrt(ms / HEAD_DIM + EPS) * gain_row


def _decode_cmp_body(pt_ref, *refs, n_strips, n_sel, qblk, topk):
    pages = refs[:DEC_PAGES_PER_STEP]
    (bd_ref, peterm_ref, w2bd_ref, kg_ref, qbd_ref, bias_ref, ovl_ref, upper_ref,
     ocmp_ref, idx_ref, seq, hibuf) = refs[DEC_PAGES_PER_STEP:]
    s = pl.program_id(1)
    strips_per_page = pages[0].shape[1]
    for r in range(DEC_PAGES_PER_STEP):
        row0 = pl.multiple_of((s * DEC_PAGES_PER_STEP + r) * strips_per_page, strips_per_page)
        seq[pl.ds(row0, strips_per_page), :] = pages[r][0]

    @pl.when(s == pl.num_programs(1) - 1)
    def _():
        n_cmp = n_strips - 1
        summaries = []
        for kv in range(2):
            halves = []
            for half in range(2):
                acc = jnp.zeros((n_strips, KV_DIM), F32)
                for l in range(STRIP):
                    c0 = l * ROW_LANES + kv * KV_DIM
                    xl = seq[:, c0:c0 + KV_DIM].astype(BF16)
                    acc = acc + jnp.dot(xl, bd_ref[(kv * 2 + half) * STRIP + l], preferred_element_type=F32)
                halves.append(acc + peterm_ref[kv * 2 + half:kv * 2 + half + 1, :])
            hibuf[0:n_strips, :] = halves[1]
            hibuf[n_strips:n_strips + 8, :] = jnp.zeros((8, KV_DIM), F32)
            h = halves[0] + hibuf[1:n_strips + 1, :]
            h = (h * jax.nn.sigmoid(h)).astype(BF16)
            o = jnp.dot(h, w2bd_ref[kv], preferred_element_type=F32)
            if kv == 0:
                o = _group_rmsnorm(o, kg_ref[...])
            summaries.append(o.astype(BF16))
        kcmp, vcmp = summaries

        qbd = qbd_ref[0].astype(BF16)
        sc = _nt_dot(qbd, kcmp) * SCALE + bias_ref[...]
        col = lax.broadcasted_iota(jnp.int32, sc.shape, 1)
        valid = col < n_cmp
        sc = jnp.where(valid, sc, NEG)
        e = jnp.where(valid, jnp.exp(sc - jnp.max(sc, axis=-1, keepdims=True)), 0.0)
        p = e / jnp.maximum(jnp.sum(e, axis=-1, keepdims=True), 1e-30)
        pb = p.astype(BF16)
        o_cmp = jnp.dot(pb, vcmp, preferred_element_type=F32)
        head_grp = lax.broadcasted_iota(jnp.int32, o_cmp.shape, 0) // HPG
        lane_grp = lax.broadcasted_iota(jnp.int32, o_cmp.shape, 1) // HEAD_DIM
        ocmp_ref[0] = jnp.where(head_grp == lane_grp, o_cmp, 0.0)

        pf = pb.astype(F32)
        psum = jnp.concatenate(
            [jnp.sum(pf[grp * HPG:(grp + 1) * HPG], axis=0, keepdims=True) for grp in range(KV_HEADS)]
            + [jnp.zeros((8 - KV_HEADS, n_strips), F32)], axis=0)
        imp = sum(jnp.dot(part, ovl_ref[...], preferred_element_type=F32) for part in _split3(psum))
        j = lax.broadcasted_iota(jnp.int32, imp.shape, 1)
        forced = (j == qblk) | (j == 0)
        score = jnp.where(forced, 1e9, jnp.where(j <= qblk, imp, NEG))
        score = jnp.where(j < n_sel, score, -3e38)
        rank = jnp.zeros(imp.shape, F32)
        for jp in range(n_sel):
            other = score[:, jp:jp + 1]
            beats = (other > score) | ((other == score) & (jp < j))
            rank = rank + jnp.where(beats, 1.0, 0.0)
        chosen = (rank < topk) & (j < n_sel)
        cum = jnp.dot(jnp.where(chosen, 1.0, 0.0).astype(BF16), upper_ref[...], preferred_element_type=F32)
        jf = j.astype(F32)
        out_lane = lax.broadcasted_iota(jnp.int32, (8, LANES), 1)
        out = jnp.zeros((8, LANES), F32)
        for k in range(topk):
            pick = jnp.sum(jnp.where(chosen & (cum == k + 1.0), jf, 0.0), axis=-1, keepdims=True)
            out = jnp.where(out_lane == k, pick, out)
        idx_ref[0] = out


def _decode_attn_body(tbl_ref, idx_ref, *refs, n_blk, qblk):
    blks = refs[:n_blk]
    (win_ref, qbd_ref, knew_ref, vnew_ref, wknew_ref, wvnew_ref, bsel_ref, bwin_ref, ocmp_ref, gate_ref,
     o_ref) = refs[n_blk:]
    b, g = pl.program_id(0), pl.program_id(1)
    qbd = qbd_ref[0, 0].astype(BF16)
    qf = qbd.astype(F32)
    own = lax.broadcasted_iota(jnp.int32, (HPG, KV_DIM), 1) // HEAD_DIM == g

    def attend(parts, k_new, v_new, bias_new):
        s_new = jnp.sum(qf * k_new.astype(BF16).astype(F32), axis=-1, keepdims=True) * SCALE + bias_new
        m = s_new
        for sc, ok, _ in parts:
            m = jnp.maximum(m, jnp.max(jnp.where(ok, sc, NEG), axis=-1, keepdims=True))
        p_new = jnp.exp(s_new - m)
        den = p_new
        acc = p_new.astype(BF16).astype(F32) * v_new.astype(BF16).astype(F32)
        for sc, ok, v in parts:
            pr = jnp.where(ok, jnp.exp(sc - m), 0.0)
            den = den + jnp.sum(pr, axis=-1, keepdims=True)
            acc = acc + jnp.dot(pr.astype(BF16), v, preferred_element_type=F32)
        return acc / jnp.maximum(den, 1e-30)

    sel_parts = []
    for k in range(n_blk):
        blk = blks[k][0]
        bidx = idx_ref[b, g, k]
        sc = _nt_dot(qbd, blk[:, :KV_DIM].astype(BF16)) * SCALE + bsel_ref[jnp.minimum(bidx, qblk), 0]
        ok = jnp.broadcast_to(bidx < qblk, sc.shape)
        sel_parts.append((sc, ok, blk[:, KV_DIM:].astype(BF16)))
    bias0 = bsel_ref[qblk, 0][:, 0:1]
    o_sel = attend(sel_parts, knew_ref[0], vnew_ref[0], bias0)

    win = win_ref[0]
    sc = _nt_dot(qbd, win[:, :KV_DIM].astype(BF16)) * SCALE + bwin_ref[0]
    o_win = attend([(sc, jnp.full(sc.shape, True), win[:, KV_DIM:].astype(BF16))], wknew_ref[0], wvnew_ref[0], bias0)

    gate = jax.nn.sigmoid(gate_ref[0, 0])
    o = gate[:, 0:1] * ocmp_ref[0, 0] + gate[:, 1:2] * o_sel + gate[:, 2:3] * o_win
    o_ref[0, 0] = jnp.where(own, o, 0.0)


def nsa_decode_mixer(x, g, past_len, cache_cmp, cache_sel, win_past, page_table,
                     w_in, q_g, k_g, cmp_pe, cmp_w1, cmp_w2, w_out, rel_table):
    bsz = x.shape[0]
    n_pool, page = cache_cmp.shape[0], cache_cmp.shape[1]
    n_pages = page_table.shape[1]
    assert past_len == n_pages * page and page % STRIP == 0 and n_pages % DEC_PAGES_PER_STEP == 0
    assert win_past.shape[1] == WINDOW and past_len >= WINDOW and page % SEL_BLOCK == 0
    x2 = x.reshape(bsz, D_MODEL)
    proj = norm_matmul(x2, g, w_in)
    c0 = N_HEADS * HEAD_DIM
    q = _rmsnorm(proj[:, :c0].reshape(bsz, N_HEADS, HEAD_DIM), q_g)

    def rows(k, gain):
        kcols = proj[:, c0 + 2 * k * KV_DIM:c0 + (2 * k + 1) * KV_DIM]
        vcols = proj[:, c0 + (2 * k + 1) * KV_DIM:c0 + (2 * k + 2) * KV_DIM]
        if gain is not None:
            kcols = _rmsnorm(kcols.reshape(bsz, KV_HEADS, HEAD_DIM), gain).reshape(bsz, KV_DIM)
        return kcols, vcols

    kc_new, vc_new = rows(0, None)
    ks_new, vs_new = rows(1, k_g[1])
    kw_new, vw_new = rows(2, k_g[2])
    gl = proj[:, c0 + 6 * KV_DIM:].reshape(bsz, KV_HEADS, HPG, 3)
    as_row = lambda kk, vv: jnp.stack([kk, vv], axis=1).reshape(bsz, 1, 2, KV_HEADS, HEAD_DIM)
    new_cmp, new_sel, new_win = as_row(kc_new, vc_new), as_row(ks_new, vs_new), as_row(kw_new, vw_new)

    eye = jnp.eye(KV_HEADS, dtype=F32)
    qbd = jnp.einsum('bghd,gk->bghkd', q.reshape(bsz, KV_HEADS, HPG, HEAD_DIM), eye).reshape(bsz, N_HEADS, KV_DIM)
    w1r = cmp_w1.reshape(2, 2, STRIP, HEAD_DIM, HEAD_DIM)
    bd = jnp.einsum('khlio,gj->khlgijo', w1r, eye).reshape(2 * 2 * STRIP, KV_DIM, KV_DIM).astype(BF16)
    w2bd = jnp.einsum('kio,gj->kgijo', cmp_w2, eye).reshape(2, KV_DIM, KV_DIM).astype(BF16)
    per = cmp_pe.reshape(2, 2, STRIP * HEAD_DIM)
    w1h = cmp_w1.reshape(2, 2, STRIP * HEAD_DIM, HEAD_DIM)
    peterm = jnp.einsum('khi,khio->kho', per, w1h, precision=lax.Precision.HIGHEST)
    peterm = jnp.pad(jnp.tile(peterm.reshape(4, HEAD_DIM), (1, KV_HEADS)), ((0, 4), (0, 0)))
    kg_row = jnp.tile(k_g[0], KV_HEADS).reshape(1, KV_DIM)

    n_strips = past_len // STRIP
    n_cmp = n_strips - 1
    qblk = past_len // SEL_BLOCK
    n_sel = qblk + 1
    topk = min(SEL_TOPK, n_sel)
    assert n_sel <= SEL_LANES and n_strips % 8 == 0
    tab = rel_table[_rel_bucket(jnp.arange(REL_MAX_DIST + 1))].astype(F32).T
    ends = np.arange(n_strips) * CMP_STRIDE + CMP_BLOCK - 1
    bias_cmp = tab[:, np.clip(past_len - ends, 0, REL_MAX_DIST)]
    cs = np.arange(n_strips)[:, None] * CMP_STRIDE
    ss = np.arange(SEL_LANES)[None, :] * SEL_BLOCK
    ovl = (cs < ss + SEL_BLOCK) & (cs + CMP_BLOCK > ss) & (np.arange(n_strips)[:, None] < n_cmp) \
        & (np.arange(SEL_LANES)[None, :] < n_sel)
    ovl = jnp.asarray(ovl, BF16)
    upper = jnp.asarray(np.triu(np.ones((SEL_LANES, SEL_LANES), np.float32)), BF16)

    strips_per_page = page // STRIP
    cmp_view = cache_cmp.reshape(n_pool, strips_per_page, STRIP * ROW_LANES)
    steps = n_pages // DEC_PAGES_PER_STEP
    const = lambda a: pl.BlockSpec(a.shape, lambda b, s, pt: (0,) * a.ndim)
    page_spec = lambda r: pl.BlockSpec((1, strips_per_page, STRIP * ROW_LANES),
                                       lambda b, s, pt: (pt[b, s * DEC_PAGES_PER_STEP + r], 0, 0))
    o_cmp, idx = pl.pallas_call(
        functools.partial(_decode_cmp_body, n_strips=n_strips, n_sel=n_sel, qblk=qblk, topk=topk),
        name="nsa_decode_cmp",
        grid_spec=pltpu.PrefetchScalarGridSpec(
            num_scalar_prefetch=1,
            grid=(bsz, steps),
            in_specs=[page_spec(r) for r in range(DEC_PAGES_PER_STEP)]
            + [const(bd), const(peterm), const(w2bd), const(kg_row),
               pl.BlockSpec((1, N_HEADS, KV_DIM), lambda b, s, pt: (b, 0, 0)),
               const(bias_cmp), const(ovl), const(upper)],
            out_specs=[pl.BlockSpec((1, N_HEADS, KV_DIM), lambda b, s, pt: (b, 0, 0)),
                       pl.BlockSpec((1, 8, LANES), lambda b, s, pt: (b, 0, 0))],
            scratch_shapes=[pltpu.VMEM((n_strips, STRIP * ROW_LANES), F32),
                            pltpu.VMEM((n_strips + 8, KV_DIM), F32)],
        ),
        out_shape=[jax.ShapeDtypeStruct((bsz, N_HEADS, KV_DIM), F32),
                   jax.ShapeDtypeStruct((bsz, 8, LANES), F32)],
        compiler_params=pltpu.CompilerParams(
            dimension_semantics=("parallel", "arbitrary"),
            vmem_limit_bytes=DECODE_VMEM_LIMIT_BYTES),
    )(page_table, *([cmp_view] * DEC_PAGES_PER_STEP), bd, peterm, w2bd, kg_row, qbd, bias_cmp, ovl, upper)

    blk_idx = idx[:, :KV_HEADS, :topk].astype(jnp.int32)
    per_page = page // SEL_BLOCK
    safe = jnp.minimum(blk_idx, qblk - 1)
    page_of = jnp.take_along_axis(page_table, (safe // per_page).reshape(bsz, -1), axis=1).reshape(safe.shape)
    tbl = page_of * per_page + safe % per_page
    sel_view = cache_sel.reshape(n_pool * per_page, SEL_BLOCK, ROW_LANES)
    win_view = win_past.reshape(bsz, WINDOW, ROW_LANES)

    dist_sel = past_len - (np.arange(n_sel)[:, None] * SEL_BLOCK + np.arange(SEL_BLOCK)[None, :])
    bsel = tab[:, np.clip(dist_sel, 0, REL_MAX_DIST)]
    bsel = bsel.reshape(KV_HEADS, HPG, n_sel, SEL_BLOCK).transpose(2, 0, 1, 3)
    bwin = tab[:, np.clip(past_len - (past_len - WINDOW + np.arange(WINDOW)), 0, REL_MAX_DIST)]
    bwin = bwin.reshape(KV_HEADS, HPG, WINDOW)

    def spread(a):
        return a.reshape(bsz, 1, KV_DIM)

    qbd4 = qbd.reshape(bsz, KV_HEADS, HPG, KV_DIM)
    ocmp4 = o_cmp.reshape(bsz, KV_HEADS, HPG, KV_DIM)
    blk_spec = lambda k: pl.BlockSpec((1, SEL_BLOCK, ROW_LANES), lambda b, g, t, i: (t[b, g, k], 0, 0))
    row_spec = pl.BlockSpec((1, 1, KV_DIM), lambda b, g, t, i: (b, 0, 0))
    per_bg = lambda *shape: pl.BlockSpec((1, 1) + shape, lambda b, g, t, i: (b, g) + (0,) * len(shape))
    o_full = pl.pallas_call(
        functools.partial(_decode_attn_body, n_blk=topk, qblk=qblk),
        name="nsa_decode_attn",
        grid_spec=pltpu.PrefetchScalarGridSpec(
            num_scalar_prefetch=2,
            grid=(bsz, KV_HEADS),
            in_specs=[blk_spec(k) for k in range(topk)]
            + [pl.BlockSpec((1, WINDOW, ROW_LANES), lambda b, g, t, i: (b, 0, 0)),
               per_bg(HPG, KV_DIM), row_spec, row_spec, row_spec, row_spec,
               pl.BlockSpec((n_sel, 1, HPG, SEL_BLOCK), lambda b, g, t, i: (0, g, 0, 0)),
               pl.BlockSpec((1, HPG, WINDOW), lambda b, g, t, i: (g, 0, 0)),
               per_bg(HPG, KV_DIM), per_bg(HPG, 3)],
            out_specs=per_bg(HPG, KV_DIM),
        ),
        out_shape=jax.ShapeDtypeStruct((bsz, KV_HEADS, HPG, KV_DIM), F32),
        compiler_params=pltpu.CompilerParams(
            dimension_semantics=("parallel", "arbitrary"),
            vmem_limit_bytes=VMEM_LIMIT_BYTES),
    )(tbl, blk_idx, *([sel_view] * topk), win_view, qbd4, spread(ks_new), spread(vs_new), spread(kw_new),
      spread(vw_new), bsel, bwin, ocmp4, gl)
    o = o_full.reshape(bsz, KV_HEADS, HPG, KV_HEADS, HEAD_DIM).sum(axis=3).reshape(bsz, c0)
    y = norm_matmul(o, None, w_out, normalize=False, res=x2)
    new_win_buf = jnp.concatenate([win_past[:, 1:], new_win], axis=1)
    return y.reshape(x.shape), new_cmp, new_sel, new_win_buf


def _gla_chunked(q, k, v, logf, s0):
    bsz, L = q.shape[0], q.shape[1]
    nc = -(-L // H_CHUNK)
    pad = nc * H_CHUNK - L

    def prep(t):
        t = jnp.pad(t, ((0, 0), (0, pad), (0, 0), (0, 0)))
        return t.reshape(bsz, nc, H_CHUNK, H_HEADS, t.shape[-1])

    q, k, v, logf = prep(q), prep(k), prep(v), prep(logf)
    acum = jnp.cumsum(logf, axis=2)
    alast = acum[:, :, -1:]
    qe = q * jnp.exp(acum)
    ke = k * jnp.exp(-acum)
    kd = k * jnp.exp(alast - acum)
    tri = jnp.tril(jnp.ones((H_CHUNK, H_CHUNK), bool))
    att = jnp.where(tri, jnp.einsum('bcthk,bcshk->bchts', qe, ke), 0.0)
    o_intra = jnp.einsum('bchts,bcshv->bcthv', att, v)
    upd = jnp.einsum('bcshk,bcshv->bchkv', kd, v)

    def step(s, inp):
        dec, up = inp
        return dec[..., None] * s + up, s

    s_last, s_prev = lax.scan(step, s0, (jnp.exp(alast[:, :, 0]).swapaxes(0, 1), upd.swapaxes(0, 1)))
    o_inter = jnp.einsum('bcthk,bchkv->bcthv', qe, s_prev.swapaxes(0, 1))
    o = (o_intra + o_inter).reshape(bsz, nc * H_CHUNK, H_HEADS, H_DV)[:, :L]
    return o, s_last


def hgrn2_mixer(x, g, s0, lb, w_in, norm_g, w_out):
    bsz, L, _ = x.shape
    proj = norm_matmul(x.reshape(bsz * L, D_MODEL), g, w_in).reshape(bsz, L, -1)
    q, fr, iv, gate = jnp.split(proj, 4, axis=-1)
    q = jax.nn.silu(q).reshape(bsz, L, H_HEADS, H_DK)
    forget = lb + (1.0 - lb) * jax.nn.sigmoid(fr)
    logf = jnp.log(forget).reshape(bsz, L, H_HEADS, H_DK)
    k = (1.0 - forget).reshape(bsz, L, H_HEADS, H_DK)
    v = iv.reshape(bsz, L, H_HEADS, H_DV)
    o, s = _gla_chunked(q, k, v, logf, s0)
    o = _rmsnorm(o, norm_g) * jax.nn.silu(gate.reshape(bsz, L, H_HEADS, H_DV))
    y = norm_matmul(o.reshape(bsz * L, D_MODEL), None, w_out, normalize=False)
    return y.reshape(bsz, L, D_MODEL), s


def _gather_pages(cache, page_table):
    pages = cache[page_table]
    b, n, p = pages.shape[0], pages.shape[1], pages.shape[2]
    return pages.reshape(b, n * p, *pages.shape[3:])


def kernel(x_prompt, x_sample, state_ssm, state_conv, cache_kv_cmp, cache_kv_sel, cache_kv_win, state_hgrn, page_table, norm_g, rel_table, m_w_in, m_conv_w, m_conv_b, m_dt_bias, m_a_log, m_d, m_norm_g, m_w_out, n_w_in, n_q_g, n_k_g, n_cmp_pe, n_cmp_w1, n_cmp_w2, n_w_out, h_w_in, h_lb, h_norm_g, h_w_out, moe_w_rg, moe_b_rg, moe_w_re, moe_b_re, moe_w1, moe_w3, moe_w2):
    bp, lp = x_prompt.shape[0], x_prompt.shape[1]
    past_len = page_table.shape[1] * cache_kv_cmp.shape[2]
    dt = x_prompt.dtype
    lbs = jax.nn.softmax(h_lb.astype(F32), axis=0)
    lbs = jnp.cumsum(lbs, axis=0) - lbs[0]
    xp, xs = x_prompt, x_sample
    ssm_p, conv_p, cmp_p, sel_p, win_p, hg_p = [], [], [], [], [], []
    ssm_s, conv_s, cmp_s, sel_s, win_s, hg_s = [], [], [], [], [], []
    for i in range(DEPTH):
        kind, j = i % N_MIXERS, i // N_MIXERS
        g0 = norm_g[i, 0]
        if kind == 0:
            w = (m_w_in[j], m_conv_w[j], m_conv_b[j], m_dt_bias[j], m_a_log[j], m_d[j], m_norm_g[j], m_w_out[j])
            xp, cbuf, hh = mamba_prompt_mixer(xp, g0, *w)
            ssm_p.append(hh)
            conv_p.append(cbuf)
            ys, cbuf, hh = mamba_mixer(xs, g0, state_conv[j], state_ssm[j], *w)
            xs = xs + ys
            ssm_s.append(hh)
            conv_s.append(cbuf)
        elif kind == 1:
            w = (n_w_in[j], n_q_g[j], n_k_g[j], n_cmp_pe[j], n_cmp_w1[j], n_cmp_w2[j], n_w_out[j], rel_table)
            xp, rc, rs, wb = nsa_prompt_mixer(xp, g0, *w)
            cmp_p.append(rc)
            sel_p.append(rs)
            win_p.append(wb)
            xs, rc, rs, wb = nsa_decode_mixer(xs, g0, past_len, cache_kv_cmp[j], cache_kv_sel[j],
                                              cache_kv_win[j], page_table, *w)
            cmp_s.append(rc)
            sel_s.append(rs)
            win_s.append(wb)
        else:
            w = (lbs[i], h_w_in[j], h_norm_g[j], h_w_out[j])
            yp, st = hgrn2_mixer(xp, g0, jnp.zeros((bp, H_HEADS, H_DK, H_DV), dt), *w)
            xp = xp + yp
            hg_p.append(st)
            ys, st = hgrn2_mixer(xs, g0, state_hgrn[j], *w)
            xs = xs + ys
            hg_s.append(st)
        mw = (moe_w_rg[i], moe_b_rg[i], moe_w_re[i], moe_b_re[i], moe_w1[i], moe_w3[i], moe_w2[i])
        xp = hier_moe_residual(xp.reshape(-1, D_MODEL), norm_g[i, 1], *mw).reshape(xp.shape)
        xs = hier_moe_residual(xs.reshape(-1, D_MODEL), norm_g[i, 1], *mw).reshape(xs.shape)
    return (xp, xs,
            jnp.stack(ssm_p), jnp.stack(conv_p), jnp.stack(cmp_p), jnp.stack(sel_p), jnp.stack(win_p), jnp.stack(hg_p),
            jnp.stack(ssm_s), jnp.stack(conv_s), jnp.stack(cmp_s), jnp.stack(sel_s), jnp.stack(win_s), jnp.stack(hg_s))
```

```python
import functools
import math

import jax
import jax.numpy as jnp
import numpy as np
from jax import lax
from jax.experimental import pallas as pl
from jax.experimental.pallas import tpu as pltpu

F32 = jnp.float32
BF16 = jnp.bfloat16
EPS = 1e-6

D_MODEL = 1024
DEPTH = 4
N_MIXERS = 3

M_DINNER = 2 * D_MODEL
M_HEADDIM = 64
M_HEADS = M_DINNER // M_HEADDIM
M_GROUPS = 4
M_DSTATE = 128
M_CONV = 4
M_GN = M_GROUPS * M_DSTATE
M_CONV_DIM = M_DINNER + 2 * M_GN
M_CHUNK = 128

N_HEADS = 16
HEAD_DIM = D_MODEL // N_HEADS
KV_HEADS = 4
HPG = N_HEADS // KV_HEADS
KV_DIM = KV_HEADS * HEAD_DIM
CMP_BLOCK = 32
CMP_STRIDE = 16
SEL_BLOCK = 64
SEL_TOPK = 16
WINDOW = 512
SEL_QBLK = 32
WIN_QBLK = 128
SCALE = HEAD_DIM ** -0.5
REL_BUCKETS = 32
REL_MAX_DIST = 128

H_DK = 128
H_HEADS = D_MODEL // H_DK
H_DV = D_MODEL // H_HEADS
H_CHUNK = 32

MOE_GROUPS = 4
MOE_EPG = 4
MOE_EXPERTS = MOE_GROUPS * MOE_EPG
MOE_TOPK = 2
MOE_FF = 512

VMEM_LIMIT_BYTES = 48 * 1024 * 1024
DECODE_VMEM_LIMIT_BYTES = 56 * 1024 * 1024


def _rmsnorm(x, g):
    xf = x.astype(F32)
    y = xf * lax.rsqrt(jnp.mean(xf * xf, axis=-1, keepdims=True) + EPS)
    return (y * g.astype(F32)).astype(x.dtype)


LANES = 128


def _norm_matmul_body(*refs, normalize, has_res):
    if has_res:
        x_ref, g_ref, w_ref, res_ref, o_ref, xb_ref = refs
    else:
        x_ref, g_ref, w_ref, o_ref, xb_ref = refs

    @pl.when(pl.program_id(1) == 0)
    def _():
        x = x_ref[...].astype(F32)
        if normalize:
            x = x * lax.rsqrt(jnp.mean(x * x, axis=-1, keepdims=True) + EPS) * g_ref[...]
        xb_ref[...] = x.astype(BF16)

    acc = jnp.dot(xb_ref[...], w_ref[...].astype(BF16), preferred_element_type=F32)
    if has_res:
        acc = acc + res_ref[...]
    o_ref[...] = acc


def _pick_tile(n, pref):
    t = min(n, pref)
    while n % t:
        t //= 2
    return t


def norm_matmul(x, g, w, *, normalize=True, res=None, tm=None, tn=512):
    t, k = x.shape
    n_true = w.shape[1]
    tm = _pick_tile(t, tm or (1024 if k <= 1024 else 512))
    if n_true % LANES:
        assert res is None
        w = jnp.pad(w, ((0, 0), (0, LANES - n_true % LANES)))
    n = w.shape[1]
    tn = _pick_tile(n, tn)
    if g is None:
        g = jnp.ones((k,), F32)
    in_specs = [
        pl.BlockSpec((tm, k), lambda i, j: (i, 0)),
        pl.BlockSpec((1, k), lambda i, j: (0, 0)),
        pl.BlockSpec((k, tn), lambda i, j: (0, j)),
    ]
    args = [x, g.reshape(1, k), w]
    if res is not None:
        in_specs.append(pl.BlockSpec((tm, tn), lambda i, j: (i, j)))
        args.append(res)
    out = pl.pallas_call(
        functools.partial(_norm_matmul_body, normalize=normalize, has_res=res is not None),
        name="norm_matmul",
        grid=(t // tm, n // tn),
        in_specs=in_specs,
        out_specs=pl.BlockSpec((tm, tn), lambda i, j: (i, j)),
        out_shape=jax.ShapeDtypeStruct((t, n), F32),
        scratch_shapes=[pltpu.VMEM((tm, k), BF16)],
        compiler_params=pltpu.CompilerParams(
            dimension_semantics=("parallel", "arbitrary"),
            vmem_limit_bytes=VMEM_LIMIT_BYTES),
    )(*args)
    return out if n == n_true else out[:, :n_true]


ROUTE_LANES = 128
MOE_TILE = 1024
MOE_ROWS = 160
NEG = -1e30


def _router_body(x_ref, g_ref, w_ref, b_ref, u_ref, xn_ref, rank_ref, wt_ref):
    x = x_ref[...]
    tm = x.shape[0]
    xn = x * lax.rsqrt(jnp.mean(x * x, axis=-1, keepdims=True) + EPS) * g_ref[...]
    xb = xn.astype(BF16)
    xn_ref[...] = xb
    logits = jnp.dot(xb, w_ref[...].astype(BF16), preferred_element_type=F32) + b_ref[...]
    lane = lax.broadcasted_iota(jnp.int32, (tm, ROUTE_LANES), 1).astype(F32)

    def first_max(mask):
        v = jnp.max(jnp.where(mask, logits, NEG), axis=-1, keepdims=True)
        i = jnp.min(jnp.where(mask & (logits == v), lane, float(ROUTE_LANES)), axis=-1, keepdims=True)
        return v, i

    is_group = lane < MOE_GROUPS
    mg, g_idx = first_max(is_group)
    pg_top = 1.0 / jnp.sum(jnp.where(is_group, jnp.exp(logits - mg), 0.0), axis=-1, keepdims=True)
    lo = MOE_GROUPS + MOE_EPG * g_idx
    in_group = (lane >= lo) & (lane < lo + MOE_EPG)
    v1, i1 = first_max(in_group)
    v2, i2 = first_max(in_group & (lane != i1))
    e2 = jnp.exp(v2 - v1)
    w_a = pg_top / (1.0 + e2)
    w_b = pg_top * e2 / (1.0 + e2)
    info = jnp.where(lane == 0, i1 - MOE_GROUPS, jnp.where(lane == 1, i2 - MOE_GROUPS,
                     jnp.where(lane == 2, w_a, jnp.where(lane == 3, w_b, 0.0))))
    info_t = info.T
    e_a, e_b, w_at, w_bt = info_t[0:1], info_t[1:2], info_t[2:3], info_t[3:4]
    expert = lax.broadcasted_iota(jnp.int32, (MOE_EXPERTS, tm), 0).astype(F32)
    m_a = e_a == expert
    m_b = e_b == expert
    onehot = jnp.concatenate([jnp.where(m_a, 1.0, 0.0), jnp.where(m_b, 1.0, 0.0)], axis=0).astype(BF16)
    cum = jnp.dot(onehot, u_ref[...], preferred_element_type=F32)
    cum_a, cum_b = cum[:MOE_EXPERTS], cum[MOE_EXPERTS:]
    n_a = cum_a[:, tm - 1:tm]
    rank_ref[...] = jnp.where(m_a, cum_a - 1.0, jnp.where(m_b, n_a + cum_b - 1.0, -1.0))
    wt_ref[...] = jnp.where(m_a, w_at, jnp.where(m_b, w_bt, 0.0))


def moe_route(x, g, w_rg, b_rg, w_re, b_re, tm):
    t = x.shape[0]
    pad = ROUTE_LANES - MOE_GROUPS - MOE_EXPERTS
    w = jnp.pad(jnp.concatenate([w_rg, w_re], axis=1), ((0, 0), (0, pad)))
    b = jnp.pad(jnp.concatenate([b_rg, b_re]), (0, pad)).reshape(1, ROUTE_LANES)
    upper = jnp.asarray(np.triu(np.ones((tm, tm), np.float32)), BF16)
    const = lambda a: pl.BlockSpec(a.shape, lambda i: (0,) * a.ndim)
    g2 = g.reshape(1, D_MODEL)
    return pl.pallas_call(
        _router_body,
        name="moe_router",
        grid=(t // tm,),
        in_specs=[pl.BlockSpec((tm, D_MODEL), lambda i: (i, 0)), const(g2), const(w), const(b), const(upper)],
        out_specs=[pl.BlockSpec((tm, D_MODEL), lambda i: (i, 0)),
                   pl.BlockSpec((MOE_EXPERTS, tm), lambda i: (0, i)),
                   pl.BlockSpec((MOE_EXPERTS, tm), lambda i: (0, i))],
        out_shape=[jax.ShapeDtypeStruct((t, D_MODEL), BF16),
                   jax.ShapeDtypeStruct((MOE_EXPERTS, t), F32),
                   jax.ShapeDtypeStruct((MOE_EXPERTS, t), F32)],
        compiler_params=pltpu.CompilerParams(
            dimension_semantics=("parallel",), vmem_limit_bytes=VMEM_LIMIT_BYTES),
    )(x, g2, w, b, upper)


def _moe_expert_body(cnt_ref, xn_ref, rank_ref, wt_ref, w1_ref, w3_ref, w2_ref, res_ref, o_ref, *, rows):
    ti, e = pl.program_id(0), pl.program_id(1)

    @pl.when(e == 0)
    def _():
        o_ref[...] = res_ref[...]

    tm = xn_ref.shape[0]
    rank = rank_ref[pl.ds(e, 1), :]
    wt = wt_ref[pl.ds(e, 1), :]
    n_chunks = (cnt_ref[e, ti] + rows - 1) // rows

    def chunk(c, carry):
        r = (lax.broadcasted_iota(jnp.int32, (rows, tm), 0) + c * rows).astype(F32)
        sel = jnp.where(rank == r, 1.0, 0.0)
        selb = sel.astype(BF16)
        xs = jnp.dot(selb, xn_ref[...], preferred_element_type=F32).astype(BF16)
        a = jnp.dot(xs, w1_ref[0].astype(BF16), preferred_element_type=F32)
        b = jnp.dot(xs, w3_ref[0].astype(BF16), preferred_element_type=F32)
        h = (a * jax.nn.sigmoid(a) * b * jnp.sum(sel * wt, axis=1, keepdims=True)).astype(BF16)
        y = jnp.dot(h, w2_ref[0].astype(BF16), preferred_element_type=F32)
        y_hi = y.astype(BF16)
        y_lo = (y - y_hi.astype(F32)).astype(BF16)
        tn = (((0,), (0,)), ((), ()))
        o_ref[...] += (lax.dot_general(selb, y_hi, tn, preferred_element_type=F32)
                       + lax.dot_general(selb, y_lo, tn, preferred_element_type=F32))
        return carry

    lax.fori_loop(0, n_chunks, chunk, 0)


def moe_experts(xn, rank, wt, counts, w1, w3, w2, res, tm):
    t = xn.shape[0]
    rows = min(MOE_ROWS, tm)
    grid_spec = pltpu.PrefetchScalarGridSpec(
        num_scalar_prefetch=1,
        grid=(t // tm, MOE_EXPERTS),
        in_specs=[
            pl.BlockSpec((tm, D_MODEL), lambda i, e, c: (i, 0)),
            pl.BlockSpec((MOE_EXPERTS, tm), lambda i, e, c: (0, i)),
            pl.BlockSpec((MOE_EXPERTS, tm), lambda i, e, c: (0, i)),
            pl.BlockSpec((1, D_MODEL, MOE_FF), lambda i, e, c: (e, 0, 0)),
            pl.BlockSpec((1, D_MODEL, MOE_FF), lambda i, e, c: (e, 0, 0)),
            pl.BlockSpec((1, MOE_FF, D_MODEL), lambda i, e, c: (e, 0, 0)),
            pl.BlockSpec((tm, D_MODEL), lambda i, e, c: (i, 0)),
        ],
        out_specs=pl.BlockSpec((tm, D_MODEL), lambda i, e, c: (i, 0)),
    )
    return pl.pallas_call(
        functools.partial(_moe_expert_body, rows=rows),
        name="moe_experts",
        grid_spec=grid_spec,
        out_shape=jax.ShapeDtypeStruct((t, D_MODEL), F32),
        compiler_params=pltpu.CompilerParams(
            dimension_semantics=("parallel", "arbitrary"),
            vmem_limit_bytes=VMEM_LIMIT_BYTES),
    )(counts, xn, rank, wt, w1, w3, w2, res)


def hier_moe_residual(x, g, w_rg, b_rg, w_re, b_re, w1, w3, w2):
    t_true = x.shape[0]
    tm = MOE_TILE if t_true % MOE_TILE == 0 else ROUTE_LANES
    if t_true % tm:
        x = jnp.pad(x, ((0, tm - t_true % tm), (0, 0)))
    t = x.shape[0]
    xn, rank, wt = moe_route(x, g, w_rg, b_rg, w_re, b_re, tm)
    counts = jnp.sum((rank >= 0).reshape(MOE_EXPERTS, t // tm, tm), axis=-1, dtype=jnp.int32)
    return moe_experts(xn, rank, wt, counts, w1, w3, w2, x, tm)[:t_true]


def _causal_dwconv(u, buf, w, b):
    L = u.shape[1]
    ext = jnp.concatenate([buf.astype(u.dtype), u], axis=1)
    out = b + sum(ext[:, k:k + L] * w[k] for k in range(M_CONV))
    return out, ext[:, L:]


def _ssd_scan(x, dt, a, bm, cm, h0):
    bsz, L = x.shape[0], x.shape[1]
    q = M_CHUNK if L % M_CHUNK == 0 else L
    nc = L // q
    hpg = M_HEADS // M_GROUPS
    xdt = (x * dt[..., None]).reshape(bsz, nc, q, M_GROUPS, hpg, M_HEADDIM)
    acum = jnp.cumsum((dt * a).reshape(bsz, nc, q, M_GROUPS, hpg), axis=2)
    bc = bm.reshape(bsz, nc, q, M_GROUPS, M_DSTATE)
    cc = cm.reshape(bsz, nc, q, M_GROUPS, M_DSTATE)
    tri = jnp.tril(jnp.ones((q, q), bool))[:, :, None, None]
    seg = acum[:, :, :, None] - acum[:, :, None]
    decay = jnp.exp(jnp.where(tri, seg, -jnp.inf))
    cb = jnp.einsum('bclgn,bcsgn->bclsg', cc, bc)
    y_diag = jnp.einsum('bclsg,bclsgh,bcsghp->bclghp', cb, decay, xdt)
    decay_end = jnp.exp(acum[:, :, -1:] - acum)
    states = jnp.einsum('bcsgn,bcsgh,bcsghp->bcghpn', bc, decay_end, xdt)
    chunk_decay = jnp.exp(acum[:, :, -1])

    def step(h, inp):
        cd, st = inp
        return cd[..., None, None] * h + st, h

    h_last, h_prev = lax.scan(step, h0.reshape(bsz, M_GROUPS, hpg, M_HEADDIM, M_DSTATE),
                              (chunk_decay.swapaxes(0, 1), states.swapaxes(0, 1)))
    y_off = jnp.einsum('bclgn,bclgh,bcghpn->bclghp', cc, jnp.exp(acum), h_prev.swapaxes(0, 1))
    y = (y_diag + y_off).reshape(bsz, L, M_HEADS, M_HEADDIM)
    return y, h_last.reshape(bsz, M_HEADS, M_HEADDIM, M_DSTATE)


def mamba_mixer(x, g, conv_buf, h0, w_in, conv_w, conv_b, dt_bias, a_log, d_skip, norm_g, w_out):
    bsz, L, _ = x.shape
    proj = norm_matmul(x.reshape(bsz * L, D_MODEL), g, w_in).reshape(bsz, L, -1)
    z = proj[..., :M_DINNER]
    xbc = proj[..., M_DINNER:M_DINNER + M_CONV_DIM]
    dt_raw = proj[..., M_DINNER + M_CONV_DIM:]
    xbc, new_buf = _causal_dwconv(xbc, conv_buf, conv_w, conv_b)
    xbc = jax.nn.silu(xbc)
    xs = xbc[..., :M_DINNER].reshape(bsz, L, M_HEADS, M_HEADDIM)
    bm = xbc[..., M_DINNER:M_DINNER + M_GN].reshape(bsz, L, M_GROUPS, M_DSTATE)
    cm = xbc[..., M_DINNER + M_GN:].reshape(bsz, L, M_GROUPS, M_DSTATE)
    dt = jax.nn.softplus(dt_raw + dt_bias)
    a = -jnp.exp(a_log)
    y, h = _ssd_scan(xs, dt, a, bm, cm, h0)
    y = y + xs * d_skip[:, None]
    y = y.reshape(bsz, L, M_DINNER)
    y = _rmsnorm(y * jax.nn.silu(z), norm_g)
    out = norm_matmul(y.reshape(bsz * L, M_DINNER), None, w_out, normalize=False)
    return out.reshape(bsz, L, D_MODEL), new_buf, h


SSD_HPG = M_HEADS // M_GROUPS
SSD_GROUP_ROWS = SSD_HPG * M_HEADDIM
CONV_PAD = 8


def _transpose_cols(x):
    return jnp.concatenate([x[:, j * LANES:(j + 1) * LANES].T for j in range(x.shape[1] // LANES)], axis=0)


def _transpose_rows(x):
    return jnp.concatenate([x[j * LANES:(j + 1) * LANES, :].T for j in range(x.shape[0] // LANES)], axis=1)


def _ssd_body(z_ref, xbc_ref, dtr_ref, cw_ref, cb_ref, dtb_ref, a_ref, dcol_ref, ng_ref, ltri_ref,
              y_ref, conv_ref, h_ref, xbuf, *, q):
    c = pl.program_id(1)

    @pl.when(c == 0)
    def _():
        h_ref[...] = jnp.zeros_like(h_ref)
        xbuf[0:CONV_PAD, :] = jnp.zeros((CONV_PAD, M_CONV_DIM), F32)

    xbuf[CONV_PAD:CONV_PAD + q, :] = xbc_ref[...]
    conv = cb_ref[...]
    for k in range(M_CONV):
        start = CONV_PAD - (M_CONV - 1) + k
        conv = conv + xbuf[start:start + q, :] * cw_ref[k:k + 1, :]
    tail = xbuf[CONV_PAD + q - (M_CONV - 1):CONV_PAD + q, :]
    xbuf[CONV_PAD - (M_CONV - 1):CONV_PAD, :] = tail
    conv_ref[0] = tail
    xc = conv * jax.nn.sigmoid(conv)
    xs = xc[:, :M_DINNER]
    xs_t = _transpose_cols(xs)

    pre = dtr_ref[...] + dtb_ref[...]
    dt = jnp.maximum(pre, 0.0) + jnp.log1p(jnp.exp(-jnp.abs(pre)))
    da = dt * a_ref[...]
    ltri = ltri_ref[...]
    acum = sum(jnp.dot(ltri, part, preferred_element_type=F32) for part in _split3(da))
    dt_t = dt.T
    acum_t = acum.T
    li = lax.broadcasted_iota(jnp.int32, (q, q), 0)
    si = lax.broadcasted_iota(jnp.int32, (q, q), 1)
    causal = li >= si

    y_t = []
    for g in range(M_GROUPS):
        bm = xc[:, M_DINNER + g * M_DSTATE:M_DINNER + (g + 1) * M_DSTATE].astype(BF16)
        cm = xc[:, M_DINNER + M_GN + g * M_DSTATE:M_DINNER + M_GN + (g + 1) * M_DSTATE].astype(BF16)
        cb = _nt_dot(cm, bm)
        r0 = g * SSD_GROUP_ROWS
        h_prev = h_ref[0, r0:r0 + SSD_GROUP_ROWS, :]
        y_off = _nt_dot(h_prev.astype(BF16), cm)
        x_dec, scale = [], []
        for hh in range(g * SSD_HPG, (g + 1) * SSD_HPG):
            a_row = acum_t[hh:hh + 1, :]
            a_col = acum[:, hh:hh + 1]
            decay = jnp.where(causal, jnp.exp(a_col - a_row), 0.0)
            m = (cb * decay).astype(BF16)
            rows = slice(hh * M_HEADDIM, (hh + 1) * M_HEADDIM)
            xs_h = xs_t[rows]
            xdt = xs_h * dt_t[hh:hh + 1, :]
            y_h = _nt_dot(xdt.astype(BF16), m)
            y_h = y_h + y_off[rows.start - r0:rows.stop - r0] * jnp.exp(a_row) + xs_h * dcol_ref[hh:hh + 1, :]
            y_t.append(y_h)
            a_last = a_row[:, q - 1:q]
            x_dec.append(xdt * jnp.exp(a_last - a_row))
            scale.append(jnp.broadcast_to(jnp.exp(a_last), (M_HEADDIM, 1)))
        upd = jnp.dot(jnp.concatenate(x_dec, axis=0).astype(BF16), bm, preferred_element_type=F32)
        h_ref[0, r0:r0 + SSD_GROUP_ROWS, :] = jnp.concatenate(scale, axis=0) * h_prev + upd

    y = _transpose_rows(jnp.concatenate(y_t, axis=0))
    zz = z_ref[...]
    yg = y * (zz * jax.nn.sigmoid(zz))
    yg = yg * lax.rsqrt(jnp.mean(yg * yg, axis=-1, keepdims=True) + EPS) * ng_ref[...]
    y_ref[...] = yg.astype(BF16)


def ssd_prompt(z, xbc, dtr, conv_w, conv_b, dt_bias, a_log, d_skip, norm_g, bsz, L):
    q = M_CHUNK
    assert L % q == 0 and q == LANES
    nc = L // q
    padl = lambda v: jnp.pad(v, (0, LANES - v.shape[0]))
    dtb = padl(dt_bias).reshape(1, LANES)
    a_row = padl(-jnp.exp(a_log)).reshape(1, LANES)
    dcol = jnp.broadcast_to(padl(d_skip).reshape(LANES, 1), (LANES, LANES))
    ltri = jnp.asarray(np.tril(np.ones((q, q), np.float32)), BF16)
    const = lambda a: pl.BlockSpec(a.shape, lambda b, c: (0,) * a.ndim)
    tok = lambda w: pl.BlockSpec((q, w), lambda b, c: (b * nc + c, 0))
    cb2 = conv_b.reshape(1, M_CONV_DIM)
    ng2 = norm_g.reshape(1, M_DINNER)
    y, conv_tail, h = pl.pallas_call(
        functools.partial(_ssd_body, q=q),
        name="ssd_chunks",
        grid=(bsz, nc),
        in_specs=[tok(M_DINNER), tok(M_CONV_DIM), tok(LANES), const(conv_w), const(cb2), const(dtb),
                  const(a_row), const(dcol), const(ng2), const(ltri)],
        out_specs=[tok(M_DINNER),
                   pl.BlockSpec((1, M_CONV - 1, M_CONV_DIM), lambda b, c: (b, 0, 0)),
                   pl.BlockSpec((1, M_DINNER, M_DSTATE), lambda b, c: (b, 0, 0))],
        out_shape=[jax.ShapeDtypeStruct((bsz * L, M_DINNER), BF16),
                   jax.ShapeDtypeStruct((bsz, M_CONV - 1, M_CONV_DIM), F32),
                   jax.ShapeDtypeStruct((bsz, M_DINNER, M_DSTATE), F32)],
        scratch_shapes=[pltpu.VMEM((CONV_PAD + q, M_CONV_DIM), F32)],
        compiler_params=pltpu.CompilerParams(
            dimension_semantics=("parallel", "arbitrary"),
            vmem_limit_bytes=VMEM_LIMIT_BYTES),
    )(z, xbc, dtr, conv_w, cb2, dtb, a_row, dcol, ng2, ltri)
    return y, conv_tail, h.reshape(bsz, M_HEADS, M_HEADDIM, M_DSTATE)


def mamba_prompt_mixer(x, g, w_in, conv_w, conv_b, dt_bias, a_log, d_skip, norm_g, w_out):
    bsz, L, _ = x.shape
    x2 = x.reshape(bsz * L, D_MODEL)
    z = norm_matmul(x2, g, w_in[:, :M_DINNER])
    xbc = norm_matmul(x2, g, w_in[:, M_DINNER:M_DINNER + M_CONV_DIM])
    dtr = norm_matmul(x2, g, jnp.pad(w_in[:, M_DINNER + M_CONV_DIM:], ((0, 0), (0, LANES - M_HEADS))))
    y, conv_tail, h = ssd_prompt(z, xbc, dtr, conv_w, conv_b, dt_bias, a_log, d_skip, norm_g, bsz, L)
    out = norm_matmul(y, None, w_out, normalize=False, res=x2)
    return out.reshape(bsz, L, D_MODEL), conv_tail, h


def _rel_bucket(dist):
    exact = REL_BUCKETS // 2
    d = jnp.maximum(dist, 0)
    ratio = jnp.log(jnp.maximum(d, 1).astype(F32) / exact) / math.log(REL_MAX_DIST / exact)
    large = jnp.minimum(exact + (ratio * (REL_BUCKETS - exact)).astype(jnp.int32), REL_BUCKETS - 1)
    return jnp.where(d < exact, d, large)


def _head_bias(rel_table, dist):
    b = rel_table[_rel_bucket(dist)].astype(F32)
    return jnp.moveaxis(b, -1, 0).reshape(KV_HEADS, HPG, dist.shape[0], dist.shape[1])


def _masked_softmax(s, valid):
    s = jnp.where(valid, s, -1e30)
    e = jnp.where(valid, jnp.exp(s - jnp.max(s, axis=-1, keepdims=True)), 0.0)
    return e / jnp.maximum(jnp.sum(e, axis=-1, keepdims=True), 1e-30)


def _group_attend(qg, qpos, k, v, kpos, valid, rel_table):
    s = jnp.einsum('bqghd,bkgd->bghqk', qg, k).astype(F32) * SCALE
    s = s + _head_bias(rel_table, qpos[:, None] - kpos[None, :])
    p = _masked_softmax(s, valid)
    o = jnp.einsum('bghqk,bkgd->bqghd', p.astype(v.dtype), v)
    return o, p


def _compress(rows, pe, w1, w2):
    bsz, T = rows.shape[0], rows.shape[1]
    nc = (T - CMP_BLOCK) // CMP_STRIDE + 1
    idx = jnp.arange(nc)[:, None] * CMP_STRIDE + jnp.arange(CMP_BLOCK)[None, :]
    blk = rows[:, idx] + pe[:, None, :]
    blk = jnp.moveaxis(blk, 2, 3).reshape(bsz, nc, KV_HEADS, CMP_BLOCK * HEAD_DIM)
    return jax.nn.silu(blk @ w1) @ w2


def _select_blocks(p_cmp, qpos, n_sel):
    nc = p_cmp.shape[-1]
    cstart = jnp.arange(nc) * CMP_STRIDE
    sstart = jnp.arange(n_sel) * SEL_BLOCK
    overlap = ((cstart[:, None] < sstart[None, :] + SEL_BLOCK) &
               (cstart[:, None] + CMP_BLOCK > sstart[None, :])).astype(F32)
    imp = jnp.einsum('bghqc,cs->bgqs', p_cmp, overlap)
    qblk = qpos // SEL_BLOCK
    j = jnp.arange(n_sel)
    forced = (j[None, :] == qblk[:, None]) | (j[None, :] == 0)
    score = jnp.where(forced, 1e9, jnp.where(j[None, :] <= qblk[:, None], imp, -1e30))
    _, idx = lax.top_k(score, min(SEL_TOPK, n_sel))
    return idx


def _sel_attend(qg, qpos, idx, kblk, vblk, rel_table):
    bsz, lq = qg.shape[0], qg.shape[1]
    bi = jnp.arange(bsz)[:, None, None, None]
    gi = jnp.arange(KV_HEADS)[None, :, None, None]
    kg = kblk[bi, gi, idx]
    vg = vblk[bi, gi, idx]
    s = jnp.einsum('bqghd,bgqkrd->bghqkr', qg, kg).astype(F32) * SCALE
    kpos = idx[..., None] * SEL_BLOCK + jnp.arange(SEL_BLOCK)
    dist = qpos[:, None, None] - kpos
    g5 = jnp.arange(KV_HEADS)[None, :, None, None, None]
    bias = rel_table.reshape(REL_BUCKETS, KV_HEADS, HPG)[_rel_bucket(dist), g5].astype(F32)
    s = s + jnp.moveaxis(bias, -1, 2)
    nk = idx.shape[-1]
    valid = (dist >= 0)[:, :, None].reshape(bsz, KV_HEADS, 1, lq, nk * SEL_BLOCK)
    p = _masked_softmax(s.reshape(bsz, KV_HEADS, HPG, lq, nk * SEL_BLOCK), valid)
    p = p.reshape(bsz, KV_HEADS, HPG, lq, nk, SEL_BLOCK)
    return jnp.einsum('bghqkr,bgqkrd->bqghd', p.astype(vg.dtype), vg)


def nsa_mixer(x, g, pos0, kv_cmp_past, kv_sel_past, kv_win_past, n_keep,
              w_in, q_g, k_g, cmp_pe, cmp_w1, cmp_w2, w_out, rel_table):
    bsz, L, _ = x.shape
    sizes = [N_HEADS * HEAD_DIM] + [KV_DIM] * 6 + [3 * N_HEADS]
    proj = norm_matmul(x.reshape(bsz * L, D_MODEL), g, w_in).reshape(bsz, L, -1)
    q, kc, vc, ks, vs, kw, vw, gl = jnp.split(proj, np.cumsum(sizes)[:-1].tolist(), axis=-1)
    q = _rmsnorm(q.reshape(bsz, L, KV_HEADS, HPG, HEAD_DIM), q_g)

    def kv(t):
        return t.reshape(bsz, L, KV_HEADS, HEAD_DIM)

    new_cmp = jnp.stack([kv(kc), kv(vc)], axis=2)
    new_sel = jnp.stack([_rmsnorm(kv(ks), k_g[1]), kv(vs)], axis=2)
    new_win = jnp.stack([_rmsnorm(kv(kw), k_g[2]), kv(vw)], axis=2)
    qpos = pos0 + jnp.arange(L)

    crows = jnp.concatenate([kv_cmp_past.astype(x.dtype), new_cmp], axis=1)
    kcmp = _rmsnorm(_compress(crows[:, :, 0], cmp_pe[0], cmp_w1[0], cmp_w2[0]), k_g[0])
    vcmp = _compress(crows[:, :, 1], cmp_pe[1], cmp_w1[1], cmp_w2[1])
    ends = jnp.arange(kcmp.shape[1]) * CMP_STRIDE + CMP_BLOCK - 1
    o_cmp, p_cmp = _group_attend(q, qpos, kcmp, vcmp, ends, qpos[:, None] >= ends[None, :], rel_table)

    srows = jnp.concatenate([kv_sel_past.astype(x.dtype), new_sel], axis=1)
    T = srows.shape[1]
    n_sel = -(-T // SEL_BLOCK)
    srows = jnp.pad(srows, ((0, 0), (0, n_sel * SEL_BLOCK - T), (0, 0), (0, 0), (0, 0)))
    blocks = srows.reshape(bsz, n_sel, SEL_BLOCK, 2, KV_HEADS, HEAD_DIM).transpose(3, 0, 4, 1, 2, 5)
    idx = _select_blocks(p_cmp, qpos, n_sel)
    qb = SEL_QBLK if L % SEL_QBLK == 0 else L
    nqb = L // qb

    def sel_block(args):
        qgi, qposi, idxi = args
        return _sel_attend(qgi, qposi, idxi, blocks[0], blocks[1], rel_table)

    o_sel = lax.map(sel_block, (q.reshape(bsz, nqb, qb, KV_HEADS, HPG, HEAD_DIM).swapaxes(0, 1),
                                qpos.reshape(nqb, qb),
                                idx.reshape(bsz, KV_HEADS, nqb, qb, -1).transpose(2, 0, 1, 3, 4)))
    o_sel = o_sel.swapaxes(0, 1).reshape(bsz, L, KV_HEADS, HPG, HEAD_DIM)

    p_win = kv_win_past.shape[1]
    wrows = jnp.concatenate([kv_win_past.astype(x.dtype), new_win], axis=1)
    wpad = jnp.pad(wrows, ((0, 0), (WINDOW, 0), (0, 0), (0, 0), (0, 0)))
    n_all = WINDOW + p_win + L
    kpos_all = pos0 - p_win - WINDOW + jnp.arange(n_all)
    kvalid_all = jnp.arange(n_all) >= WINDOW
    wq = WIN_QBLK if L % WIN_QBLK == 0 else L

    def win_block(i):
        start = p_win + i * wq
        qgi = lax.dynamic_slice_in_dim(q, i * wq, wq, axis=1)
        kvi = lax.dynamic_slice_in_dim(wpad, start, WINDOW + wq, axis=1)
        kposi = lax.dynamic_slice_in_dim(kpos_all, start, WINDOW + wq)
        kvalidi = lax.dynamic_slice_in_dim(kvalid_all, start, WINDOW + wq)
        qposi = pos0 + i * wq + jnp.arange(wq)
        dist = qposi[:, None] - kposi[None, :]
        valid = kvalidi[None, :] & (dist >= 0) & (dist <= WINDOW)
        o, _ = _group_attend(qgi, qposi, kvi[:, :, 0], kvi[:, :, 1], kposi, valid, rel_table)
        return o

    o_win = lax.map(win_block, jnp.arange(L // wq)).swapaxes(0, 1).reshape(bsz, L, KV_HEADS, HPG, HEAD_DIM)

    gate = jax.nn.sigmoid(gl).reshape(bsz, L, KV_HEADS, HPG, 3)
    o = gate[..., 0:1] * o_cmp + gate[..., 1:2] * o_sel + gate[..., 2:3] * o_win
    y = norm_matmul(o.reshape(bsz * L, N_HEADS * HEAD_DIM), None, w_out, normalize=False)
    return y.reshape(bsz, L, D_MODEL), new_cmp, new_sel, wrows[:, -n_keep:]


ATT_TQ = 128
ATT_TK = 128
NEG = -1e30


def _nt_dot(a, b):
    return lax.dot_general(a, b, (((1,), (1,)), ((), ())), preferred_element_type=F32)


def _compress_body(rk_ref, rv_ref, pe_ref, w1_ref, w2_ref, kg_ref, kc_ref, vc_ref, *, nb):
    half = (CMP_BLOCK // 2) * HEAD_DIM
    for kv, (r_ref, o_ref) in enumerate(((rk_ref, kc_ref), (rv_ref, vc_ref))):
        lo = (r_ref[0, 0, 0:nb, :] + pe_ref[kv, 0:1, :]).astype(BF16)
        hi = (r_ref[0, 0, 1:nb + 1, :] + pe_ref[kv, 1:2, :]).astype(BF16)
        h = (jnp.dot(lo, w1_ref[kv, :half, :].astype(BF16), preferred_element_type=F32)
             + jnp.dot(hi, w1_ref[kv, half:, :].astype(BF16), preferred_element_type=F32))
        h = h * jax.nn.sigmoid(h)
        o = jnp.dot(h.astype(BF16), w2_ref[kv].astype(BF16), preferred_element_type=F32)
        if kv == 0:
            o = o * lax.rsqrt(jnp.mean(o * o, axis=-1, keepdims=True) + EPS) * kg_ref[...]
        o_ref[0, 0] = o


def compress_rows(rk, rv, pe, w1, w2, kg):
    bsz, g, nbp, width = rk.shape
    nb = nbp - 8
    strip = pl.BlockSpec((1, 1, nbp, width), lambda b, j: (b, j, 0, 0))
    out = pl.BlockSpec((1, 1, nb, HEAD_DIM), lambda b, j: (b, j, 0, 0))
    full = lambda a: pl.BlockSpec(a.shape, lambda b, j: (0,) * a.ndim)
    pe2 = pe.reshape(2, 2, width)
    kg2 = kg.reshape(1, HEAD_DIM)
    return pl.pallas_call(
        functools.partial(_compress_body, nb=nb),
        name="nsa_compress",
        grid=(bsz, g),
        in_specs=[strip, strip, full(pe2), full(w1), full(w2), full(kg2)],
        out_specs=[out, out],
        out_shape=[jax.ShapeDtypeStruct((bsz, g, nb, HEAD_DIM), F32)] * 2,
        compiler_params=pltpu.CompilerParams(
            dimension_semantics=("parallel", "parallel"),
            vmem_limit_bytes=VMEM_LIMIT_BYTES),
    )(rk, rv, pe2, w1, w2, kg2)


def _split3(x):
    a = x.astype(BF16)
    r = x - a.astype(F32)
    b = r.astype(BF16)
    c = (r - b.astype(F32)).astype(BF16)
    return a, b, c


def _nsa_attn_body(q_ref, kc_ref, vc_ref, ks_ref, vs_ref, kw_ref, vw_ref, gate_ref, bcmp_ref, btile_ref,
                   ovl_ref, o_ref, *, n_cmp, n_sel, topk):
    i = pl.program_id(2)
    q0 = i * ATT_TQ
    rows = HPG * ATT_TQ
    qt = q_ref[0]
    qb = jnp.concatenate([qt[:, h * HEAD_DIM:(h + 1) * HEAD_DIM] for h in range(HPG)], axis=0).astype(BF16)
    row = lax.broadcasted_iota(jnp.int32, (rows, ATT_TK), 0)
    qpos = q0 + (row & (ATT_TQ - 1))
    col = lax.broadcasted_iota(jnp.int32, (rows, ATT_TK), 1)

    s = _nt_dot(qb, kc_ref[0, 0].astype(BF16)) * SCALE + bcmp_ref[0, 0]
    valid = (qpos >= col * CMP_STRIDE + (CMP_BLOCK - 1)) & (col < n_cmp)
    s = jnp.where(valid, s, NEG)
    e = jnp.where(valid, jnp.exp(s - jnp.max(s, axis=-1, keepdims=True)), 0.0)
    p = e / jnp.maximum(jnp.sum(e, axis=-1, keepdims=True), 1e-30)
    pb = p.astype(BF16)
    o_cmp = jnp.dot(pb, vc_ref[0, 0].astype(BF16), preferred_element_type=F32)

    psum = p[0:ATT_TQ]
    for h in range(1, HPG):
        psum = psum + p[h * ATT_TQ:(h + 1) * ATT_TQ]
    ovl = ovl_ref[...]
    imp_t = _nt_dot(ovl, psum.astype(BF16))
    nblk = imp_t.shape[0]
    j = lax.broadcasted_iota(jnp.int32, (nblk, ATT_TQ), 0)
    qblk = (q0 + lax.broadcasted_iota(jnp.int32, (nblk, ATT_TQ), 1)) // SEL_BLOCK
    forced = (j == qblk) | (j == 0)
    score = jnp.where(forced, 1e9, jnp.where(j <= qblk, imp_t, NEG))
    score = jnp.where(j < n_sel, score, -3e38)
    rank = jnp.zeros((nblk, ATT_TQ), F32)
    for jp in range(n_sel):
        other = score[jp:jp + 1, :]
        beats = (other > score) | ((other == score) & (jp < j))
        rank = rank + jnp.where(beats, 1.0, 0.0)
    sel = jnp.where((rank < topk) & (j < n_sel), 1.0, 0.0).T.astype(BF16)

    jj = lax.broadcasted_iota(jnp.int32, (nblk, ATT_TK), 0)
    kk = lax.broadcasted_iota(jnp.int32, (nblk, ATT_TK), 1) // SEL_BLOCK
    blocks_per_step = ATT_TK // SEL_BLOCK

    def flash(k_ref, v_ref, lo, hi, valid_fn):
        def step(kc, carry):
            m, l, acc = carry
            start = pl.multiple_of(kc * ATT_TK, ATT_TK)
            kblk = k_ref[0, 0, pl.ds(start, ATT_TK), :].astype(BF16)
            vblk = v_ref[0, 0, pl.ds(start, ATT_TK), :].astype(BF16)
            sc = _nt_dot(qb, kblk) * SCALE + btile_ref[0, jnp.minimum(i - kc, 2)]
            ok = valid_fn(kc, qpos - (kc * ATT_TK + col))
            sc = jnp.where(ok, sc, NEG)
            m_new = jnp.maximum(m, jnp.max(sc, axis=-1, keepdims=True))
            alpha = jnp.exp(m - m_new)
            pr = jnp.where(ok, jnp.exp(sc - m_new), 0.0)
            l = alpha * l + jnp.sum(pr, axis=-1, keepdims=True)
            acc = alpha * acc + jnp.dot(pr.astype(BF16), vblk, preferred_element_type=F32)
            return m_new, l, acc

        init = (jnp.full((rows, 1), NEG, F32), jnp.zeros((rows, 1), F32), jnp.zeros((rows, HEAD_DIM), F32))
        _, l, acc = lax.fori_loop(lo, hi, step, init)
        return acc / jnp.maximum(l, 1e-30)

    def sel_valid(kc, dist):
        expand = jnp.where(jj == kc * blocks_per_step + kk, 1.0, 0.0).astype(BF16)
        chosen = jnp.dot(sel, expand, preferred_element_type=F32)
        chosen = jnp.concatenate([chosen] * HPG, axis=0)
        return (chosen > 0.5) & (dist >= 0)

    def win_valid(kc, dist):
        return (dist >= 0) & (dist <= WINDOW)

    o_sel = flash(ks_ref, vs_ref, 0, i + 1, sel_valid)
    o_win = flash(kw_ref, vw_ref, jnp.maximum(i - WINDOW // ATT_TK, 0), i + 1, win_valid)

    gate = jax.nn.sigmoid(gate_ref[0, 0])

    def gcol(br):
        return jnp.concatenate([gate[:, h * 3 + br:h * 3 + br + 1] for h in range(HPG)], axis=0)

    o = gcol(0) * o_cmp + gcol(1) * o_sel + gcol(2) * o_win
    o_ref[0] = jnp.concatenate([o[h * ATT_TQ:(h + 1) * ATT_TQ] for h in range(HPG)], axis=1)


def _bias_tables(rel_table, n_qtiles):
    tab = rel_table[_rel_bucket(jnp.arange(REL_MAX_DIST + 1))].astype(F32)
    tab = tab.T.reshape(KV_HEADS, HPG, REL_MAX_DIST + 1)

    def skew(v, rows, width, step):
        out = jnp.broadcast_to(v[..., None, :], v.shape[:-1] + (rows, width + step))
        out = out.reshape(v.shape[:-1] + (rows * (width + step),))[..., :rows * width]
        return out.reshape(v.shape[:-1] + (rows, width))

    width = 2 * ATT_TK
    w = np.arange(width + 1)
    s_minus_t = np.where(w <= ATT_TK, w, w - (width + 1))
    d_idx = np.stack([np.clip(delta * ATT_TK - s_minus_t, 0, REL_MAX_DIST) for delta in range(3)])
    btile = skew(tab[:, :, d_idx], ATT_TQ, width, 1)[..., :ATT_TK]
    btile = btile.transpose(0, 2, 1, 3, 4).reshape(KV_HEADS, 3, HPG * ATT_TQ, ATT_TK)

    n_q = n_qtiles * ATT_TQ
    u = np.arange(n_q + CMP_STRIDE)
    v_cmp = tab[:, :, np.clip(u - (CMP_BLOCK - 1), 0, REL_MAX_DIST)]
    bcmp = skew(v_cmp, ATT_TK, n_q, CMP_STRIDE)
    bcmp = bcmp.reshape(KV_HEADS, HPG, ATT_TK, n_qtiles, ATT_TQ).transpose(3, 0, 1, 4, 2)
    return btile, bcmp.reshape(n_qtiles, KV_HEADS, HPG * ATT_TQ, ATT_TK)


def nsa_prompt_attention(q, kcmp, vcmp, ks, vs, kw, vw, gl, rel_table):
    bsz, L, _ = q.shape
    assert L % ATT_TQ == 0 and kcmp.shape[2] == ATT_TK
    nq = L // ATT_TQ
    n_cmp = (L - CMP_BLOCK) // CMP_STRIDE + 1
    n_sel = L // SEL_BLOCK
    assert n_sel <= ATT_TK
    btile, bcmp = _bias_tables(rel_table, nq)
    c = np.arange(ATT_TK)[None, :] * CMP_STRIDE
    sb = np.arange(ATT_TK)[:, None] * SEL_BLOCK
    ovl = ((c < sb + SEL_BLOCK) & (c + CMP_BLOCK > sb) & (np.arange(ATT_TK)[None, :] < n_cmp)
           & (np.arange(ATT_TK)[:, None] < n_sel))
    ovl = jnp.asarray(ovl, BF16)
    width = HPG * HEAD_DIM
    kvspec = lambda n: pl.BlockSpec((1, 1, n, HEAD_DIM), lambda b, g, i: (b, g, 0, 0))
    return pl.pallas_call(
        functools.partial(_nsa_attn_body, n_cmp=n_cmp, n_sel=n_sel, topk=min(SEL_TOPK, n_sel)),
        name="nsa_attention",
        grid=(bsz, KV_HEADS, nq),
        in_specs=[
            pl.BlockSpec((1, ATT_TQ, width), lambda b, g, i: (b, i, g)),
            kvspec(ATT_TK), kvspec(ATT_TK), kvspec(L), kvspec(L), kvspec(L), kvspec(L),
            pl.BlockSpec((1, 1, ATT_TQ, HPG * 3), lambda b, g, i: (b, g, i, 0)),
            pl.BlockSpec((1, 1, HPG * ATT_TQ, ATT_TK), lambda b, g, i: (i, g, 0, 0)),
            pl.BlockSpec((1, 3, HPG * ATT_TQ, ATT_TK), lambda b, g, i: (g, 0, 0, 0)),
            pl.BlockSpec((ATT_TK, ATT_TK), lambda b, g, i: (0, 0)),
        ],
        out_specs=pl.BlockSpec((1, ATT_TQ, width), lambda b, g, i: (b, i, g)),
        out_shape=jax.ShapeDtypeStruct((bsz, L, N_HEADS * HEAD_DIM), F32),
        compiler_params=pltpu.CompilerParams(
            dimension_semantics=("parallel", "parallel", "arbitrary"),
            vmem_limit_bytes=VMEM_LIMIT_BYTES),
    )(q, kcmp, vcmp, ks, vs, kw, vw, gl, bcmp, btile, ovl)


def nsa_prompt_mixer(x, g, w_in, q_g, k_g, cmp_pe, cmp_w1, cmp_w2, w_out, rel_table):
    bsz, L, _ = x.shape
    t = bsz * L
    proj = norm_matmul(x.reshape(t, D_MODEL), g, w_in)
    c0 = N_HEADS * HEAD_DIM
    q = _rmsnorm(proj[:, :c0].reshape(t, N_HEADS, HEAD_DIM), q_g).reshape(bsz, L, c0)

    def kvpair(k, normed_g):
        kcols = proj[:, c0 + 2 * k * KV_DIM:c0 + (2 * k + 1) * KV_DIM]
        vcols = proj[:, c0 + (2 * k + 1) * KV_DIM:c0 + (2 * k + 2) * KV_DIM]
        if normed_g is not None:
            kcols = _rmsnorm(kcols.reshape(t, KV_HEADS, HEAD_DIM), normed_g).reshape(t, KV_DIM)
        rows = jnp.stack([kcols, vcols], axis=1).reshape(bsz, L, 2, KV_HEADS, HEAD_DIM)
        per_group = rows.transpose(2, 0, 3, 1, 4)
        return rows, per_group[0], per_group[1]

    new_cmp, kc_rows, vc_rows = kvpair(0, None)
    new_sel, ks, vs = kvpair(1, k_g[1])
    new_win, kw, vw = kvpair(2, k_g[2])
    gl = proj[:, c0 + 6 * KV_DIM:].reshape(bsz, L, KV_HEADS, HPG * 3).transpose(0, 2, 1, 3)

    nb = L // CMP_STRIDE

    def strips(r):
        r = r.reshape(bsz, KV_HEADS, nb, CMP_STRIDE * HEAD_DIM)
        return jnp.pad(r, ((0, 0), (0, 0), (0, ATT_TK + 8 - nb), (0, 0)))

    kcmp, vcmp = compress_rows(strips(kc_rows), strips(vc_rows), cmp_pe, cmp_w1, cmp_w2, k_g[0])
    o = nsa_prompt_attention(q, kcmp, vcmp, ks, vs, kw, vw, gl, rel_table)
    y = norm_matmul(o.reshape(t, c0), None, w_out, normalize=False, res=x.reshape(t, D_MODEL))
    return y.reshape(bsz, L, D_MODEL), new_cmp, new_sel, new_win[:, -min(WINDOW, L):]


DEC_PAGES_PER_STEP = 8
STRIP = CMP_STRIDE
ROW_LANES = 2 * KV_DIM
SEL_LANES = 256


def _group_rmsnorm(x, gain_row):
    lane = lax.broadcasted_iota(jnp.int32, x.shape, 1) // HEAD_DIM
    sq = x * x
    ms = jnp.zeros_like(x)
    for grp in range(KV_HEADS):
        tot = jnp.sum(jnp.where(lane == grp, sq, 0.0), axis=-1, keepdims=True)
        ms = jnp.where(lane == grp, tot, ms)
    return x * lax.rsqrt(ms / HEAD_DIM + EPS) * gain_row


def _decode_cmp_body(pt_ref, *refs, n_strips, n_sel, qblk, topk):
    pages = refs[:DEC_PAGES_PER_STEP]
    (bd_ref, peterm_ref, w2bd_ref, kg_ref, qbd_ref, bias_ref, ovl_ref, upper_ref,
     ocmp_ref, idx_ref, seq, hibuf) = refs[DEC_PAGES_PER_STEP:]
    s = pl.program_id(1)
    strips_per_page = pages[0].shape[1]
    for r in range(DEC_PAGES_PER_STEP):
        row0 = pl.multiple_of((s * DEC_PAGES_PER_STEP + r) * strips_per_page, strips_per_page)
        seq[pl.ds(row0, strips_per_page), :] = pages[r][0]

    @pl.when(s == pl.num_programs(1) - 1)
    def _():
        n_cmp = n_strips - 1
        summaries = []
        for kv in range(2):
            halves = []
            for half in range(2):
                acc = jnp.zeros((n_strips, KV_DIM), F32)
                for l in range(STRIP):
                    c0 = l * ROW_LANES + kv * KV_DIM
                    w_idx = (kv * 2 + half) * STRIP + l
                    xl = (seq[:, c0:c0 + KV_DIM] + peterm_ref[w_idx:w_idx + 1, :]).astype(BF16)
                    acc = acc + jnp.dot(xl, bd_ref[w_idx], preferred_element_type=F32)
                halves.append(acc)
            hibuf[0:n_strips, :] = halves[1]
            hibuf[n_strips:n_strips + 8, :] = jnp.zeros((8, KV_DIM), F32)
            h = halves[0] + hibuf[1:n_strips + 1, :]
            h = (h * jax.nn.sigmoid(h)).astype(BF16)
            o = jnp.dot(h, w2bd_ref[kv], preferred_element_type=F32)
            if kv == 0:
                o = _group_rmsnorm(o, kg_ref[...])
            summaries.append(o.astype(BF16))
        kcmp, vcmp = summaries

        qbd = qbd_ref[0].astype(BF16)
        sc = _nt_dot(qbd, kcmp) * SCALE + bias_ref[...]
        col = lax.broadcasted_iota(jnp.int32, sc.shape, 1)
        valid = col < n_cmp
        sc = jnp.where(valid, sc, NEG)
        e = jnp.where(valid, jnp.exp(sc - jnp.max(sc, axis=-1, keepdims=True)), 0.0)
        p = e / jnp.maximum(jnp.sum(e, axis=-1, keepdims=True), 1e-30)
        pb = p.astype(BF16)
        o_cmp = jnp.dot(pb, vcmp, preferred_element_type=F32)
        head_grp = lax.broadcasted_iota(jnp.int32, o_cmp.shape, 0) // HPG
        lane_grp = lax.broadcasted_iota(jnp.int32, o_cmp.shape, 1) // HEAD_DIM
        ocmp_ref[0] = jnp.where(head_grp == lane_grp, o_cmp, 0.0)

        psum = jnp.concatenate(
            [jnp.sum(p[grp * HPG:(grp + 1) * HPG], axis=0, keepdims=True) for grp in range(KV_HEADS)]
            + [jnp.zeros((8 - KV_HEADS, n_strips), F32)], axis=0)
        imp = jnp.dot(psum.astype(BF16), ovl_ref[...], preferred_element_type=F32)
        j = lax.broadcasted_iota(jnp.int32, imp.shape, 1)
        forced = (j == qblk) | (j == 0)
        score = jnp.where(forced, 1e9, jnp.where(j <= qblk, imp, NEG))
        score = jnp.where(j < n_sel, score, -3e38)
        rank = jnp.zeros(imp.shape, F32)
        for jp in range(n_sel):
            other = score[:, jp:jp + 1]
            beats = (other > score) | ((other == score) & (jp < j))
            rank = rank + jnp.where(beats, 1.0, 0.0)
        chosen = (rank < topk) & (j < n_sel)
        cum = jnp.dot(jnp.where(chosen, 1.0, 0.0).astype(BF16), upper_ref[...], preferred_element_type=F32)
        jf = j.astype(F32)
        out_lane = lax.broadcasted_iota(jnp.int32, (8, LANES), 1)
        out = jnp.zeros((8, LANES), F32)
        for k in range(topk):
            pick = jnp.sum(jnp.where(chosen & (cum == k + 1.0), jf, 0.0), axis=-1, keepdims=True)
            out = jnp.where(out_lane == k, pick, out)
        idx_ref[0] = out


def _decode_attn_body(tbl_ref, idx_ref, *refs, n_blk, qblk):
    blks = refs[:n_blk]
    (win_ref, qbd_ref, knew_ref, vnew_ref, wknew_ref, wvnew_ref, bsel_ref, bwin_ref, ocmp_ref, gate_ref,
     o_ref) = refs[n_blk:]
    b, g = pl.program_id(0), pl.program_id(1)
    qbd = qbd_ref[0, 0].astype(BF16)
    qf = qbd.astype(F32)
    own = lax.broadcasted_iota(jnp.int32, (HPG, KV_DIM), 1) // HEAD_DIM == g

    def attend(parts, k_new, v_new, bias_new):
        s_new = jnp.sum(qf * k_new.astype(BF16).astype(F32), axis=-1, keepdims=True) * SCALE + bias_new
        m = s_new
        for sc, ok, _ in parts:
            m = jnp.maximum(m, jnp.max(jnp.where(ok, sc, NEG), axis=-1, keepdims=True))
        e_new = jnp.exp(s_new - m)
        es = [jnp.where(ok, jnp.exp(sc - m), 0.0) for sc, ok, _ in parts]
        den = e_new
        for e in es:
            den = den + jnp.sum(e, axis=-1, keepdims=True)
        den = jnp.maximum(den, 1e-30)
        acc = (e_new / den).astype(BF16).astype(F32) * v_new.astype(BF16).astype(F32)
        for e, (_, _, v) in zip(es, parts):
            acc = acc + jnp.dot((e / den).astype(BF16), v, preferred_element_type=F32)
        return acc

    sel_parts = []
    for k in range(n_blk):
        blk = blks[k][0]
        bidx = idx_ref[b, g, k]
        sc = _nt_dot(qbd, blk[:, :KV_DIM].astype(BF16)) * SCALE + bsel_ref[jnp.minimum(bidx, qblk), 0]
        ok = jnp.broadcast_to(bidx < qblk, sc.shape)
        sel_parts.append((sc, ok, blk[:, KV_DIM:].astype(BF16)))
    bias0 = bsel_ref[qblk, 0][:, 0:1]
    o_sel = attend(sel_parts, knew_ref[0], vnew_ref[0], bias0)

    win = win_ref[0]
    sc = _nt_dot(qbd, win[:, :KV_DIM].astype(BF16)) * SCALE + bwin_ref[0]
    o_win = attend([(sc, jnp.full(sc.shape, True), win[:, KV_DIM:].astype(BF16))], wknew_ref[0], wvnew_ref[0], bias0)

    gate = jax.nn.sigmoid(gate_ref[0, 0])
    o = gate[:, 0:1] * ocmp_ref[0, 0] + gate[:, 1:2] * o_sel + gate[:, 2:3] * o_win
    o_ref[0, 0] = jnp.where(own, o, 0.0)


def nsa_decode_mixer(x, g, past_len, cache_cmp, cache_sel, win_past, page_table,
                     w_in, q_g, k_g, cmp_pe, cmp_w1, cmp_w2, w_out, rel_table):
    bsz = x.shape[0]
    n_pool, page = cache_cmp.shape[0], cache_cmp.shape[1]
    n_pages = page_table.shape[1]
    assert past_len == n_pages * page and page % STRIP == 0 and n_pages % DEC_PAGES_PER_STEP == 0
    assert win_past.shape[1] == WINDOW and past_len >= WINDOW and page % SEL_BLOCK == 0
    x2 = x.reshape(bsz, D_MODEL)
    proj = norm_matmul(x2, g, w_in)
    c0 = N_HEADS * HEAD_DIM
    q = _rmsnorm(proj[:, :c0].reshape(bsz, N_HEADS, HEAD_DIM), q_g)

    def rows(k, gain):
        kcols = proj[:, c0 + 2 * k * KV_DIM:c0 + (2 * k + 1) * KV_DIM]
        vcols = proj[:, c0 + (2 * k + 1) * KV_DIM:c0 + (2 * k + 2) * KV_DIM]
        if gain is not None:
            kcols = _rmsnorm(kcols.reshape(bsz, KV_HEADS, HEAD_DIM), gain).reshape(bsz, KV_DIM)
        return kcols, vcols

    kc_new, vc_new = rows(0, None)
    ks_new, vs_new = rows(1, k_g[1])
    kw_new, vw_new = rows(2, k_g[2])
    gl = proj[:, c0 + 6 * KV_DIM:].reshape(bsz, KV_HEADS, HPG, 3)
    as_row = lambda kk, vv: jnp.stack([kk, vv], axis=1).reshape(bsz, 1, 2, KV_HEADS, HEAD_DIM)
    new_cmp, new_sel, new_win = as_row(kc_new, vc_new), as_row(ks_new, vs_new), as_row(kw_new, vw_new)

    eye = jnp.eye(KV_HEADS, dtype=F32)
    qbd = jnp.einsum('bghd,gk->bghkd', q.reshape(bsz, KV_HEADS, HPG, HEAD_DIM), eye).reshape(bsz, N_HEADS, KV_DIM)
    w1r = cmp_w1.reshape(2, 2, STRIP, HEAD_DIM, HEAD_DIM)
    bd = jnp.einsum('khlio,gj->khlgijo', w1r, eye).reshape(2 * 2 * STRIP, KV_DIM, KV_DIM).astype(BF16)
    w2bd = jnp.einsum('kio,gj->kgijo', cmp_w2, eye).reshape(2, KV_DIM, KV_DIM).astype(BF16)
    peterm = jnp.tile(cmp_pe.reshape(2 * 2 * STRIP, HEAD_DIM), (1, KV_HEADS))
    kg_row = jnp.tile(k_g[0], KV_HEADS).reshape(1, KV_DIM)

    n_strips = past_len // STRIP
    n_cmp = n_strips - 1
    qblk = past_len // SEL_BLOCK
    n_sel = qblk + 1
    topk = min(SEL_TOPK, n_sel)
    assert n_sel <= SEL_LANES and n_strips % 8 == 0
    tab = rel_table[_rel_bucket(jnp.arange(REL_MAX_DIST + 1))].astype(F32).T
    ends = np.arange(n_strips) * CMP_STRIDE + CMP_BLOCK - 1
    bias_cmp = tab[:, np.clip(past_len - ends, 0, REL_MAX_DIST)]
    cs = np.arange(n_strips)[:, None] * CMP_STRIDE
    ss = np.arange(SEL_LANES)[None, :] * SEL_BLOCK
    ovl = (cs < ss + SEL_BLOCK) & (cs + CMP_BLOCK > ss) & (np.arange(n_strips)[:, None] < n_cmp) \
        & (np.arange(SEL_LANES)[None, :] < n_sel)
    ovl = jnp.asarray(ovl, BF16)
    upper = jnp.asarray(np.triu(np.ones((SEL_LANES, SEL_LANES), np.float32)), BF16)

    strips_per_page = page // STRIP
    cmp_view = cache_cmp.reshape(n_pool, strips_per_page, STRIP * ROW_LANES)
    steps = n_pages // DEC_PAGES_PER_STEP
    const = lambda a: pl.BlockSpec(a.shape, lambda b, s, pt: (0,) * a.ndim)
    page_spec = lambda r: pl.BlockSpec((1, strips_per_page, STRIP * ROW_LANES),
                                       lambda b, s, pt: (pt[b, s * DEC_PAGES_PER_STEP + r], 0, 0))
    o_cmp, idx = pl.pallas_call(
        functools.partial(_decode_cmp_body, n_strips=n_strips, n_sel=n_sel, qblk=qblk, topk=topk),
        name="nsa_decode_cmp",
        grid_spec=pltpu.PrefetchScalarGridSpec(
            num_scalar_prefetch=1,
            grid=(bsz, steps),
            in_specs=[page_spec(r) for r in range(DEC_PAGES_PER_STEP)]
            + [const(bd), const(peterm), const(w2bd), const(kg_row),
               pl.BlockSpec((1, N_HEADS, KV_DIM), lambda b, s, pt: (b, 0, 0)),
               const(bias_cmp), const(ovl), const(upper)],
            out_specs=[pl.BlockSpec((1, N_HEADS, KV_DIM), lambda b, s, pt: (b, 0, 0)),
                       pl.BlockSpec((1, 8, LANES), lambda b, s, pt: (b, 0, 0))],
            scratch_shapes=[pltpu.VMEM((n_strips, STRIP * ROW_LANES), F32),
                            pltpu.VMEM((n_strips + 8, KV_DIM), F32)],
        ),
        out_shape=[jax.ShapeDtypeStruct((bsz, N_HEADS, KV_DIM), F32),
                   jax.ShapeDtypeStruct((bsz, 8, LANES), F32)],
        compiler_params=pltpu.CompilerParams(
            dimension_semantics=("parallel", "arbitrary"),
            vmem_limit_bytes=DECODE_VMEM_LIMIT_BYTES),
    )(page_table, *([cmp_view] * DEC_PAGES_PER_STEP), bd, peterm, w2bd, kg_row, qbd, bias_cmp, ovl, upper)

    blk_idx = idx[:, :KV_HEADS, :topk].astype(jnp.int32)
    per_page = page // SEL_BLOCK
    safe = jnp.minimum(blk_idx, qblk - 1)
    page_of = jnp.take_along_axis(page_table, (safe // per_page).reshape(bsz, -1), axis=1).reshape(safe.shape)
    tbl = page_of * per_page + safe % per_page
    sel_view = cache_sel.reshape(n_pool, page, ROW_LANES)
    win_view = win_past.reshape(bsz, WINDOW, ROW_LANES)

    dist_sel = past_len - (np.arange(n_sel)[:, None] * SEL_BLOCK + np.arange(SEL_BLOCK)[None, :])
    bsel = tab[:, np.clip(dist_sel, 0, REL_MAX_DIST)]
    bsel = bsel.reshape(KV_HEADS, HPG, n_sel, SEL_BLOCK).transpose(2, 0, 1, 3)
    bwin = tab[:, np.clip(past_len - (past_len - WINDOW + np.arange(WINDOW)), 0, REL_MAX_DIST)]
    bwin = bwin.reshape(KV_HEADS, HPG, WINDOW)

    def spread(a):
        return a.reshape(bsz, 1, KV_DIM)

    qbd4 = qbd.reshape(bsz, KV_HEADS, HPG, KV_DIM)
    ocmp4 = o_cmp.reshape(bsz, KV_HEADS, HPG, KV_DIM)
    blk_spec = lambda k: pl.BlockSpec(
        (1, SEL_BLOCK, ROW_LANES), lambda b, g, t, i: (t[b, g, k] // per_page, t[b, g, k] % per_page, 0))
    row_spec = pl.BlockSpec((1, 1, KV_DIM), lambda b, g, t, i: (b, 0, 0))
    per_bg = lambda *shape: pl.BlockSpec((1, 1) + shape, lambda b, g, t, i: (b, g) + (0,) * len(shape))
    o_full = pl.pallas_call(
        functools.partial(_decode_attn_body, n_blk=topk, qblk=qblk),
        name="nsa_decode_attn",
        grid_spec=pltpu.PrefetchScalarGridSpec(
            num_scalar_prefetch=2,
            grid=(bsz, KV_HEADS),
            in_specs=[blk_spec(k) for k in range(topk)]
            + [pl.BlockSpec((1, WINDOW, ROW_LANES), lambda b, g, t, i: (b, 0, 0)),
               per_bg(HPG, KV_DIM), row_spec, row_spec, row_spec, row_spec,
               pl.BlockSpec((n_sel, 1, HPG, SEL_BLOCK), lambda b, g, t, i: (0, g, 0, 0)),
               pl.BlockSpec((1, HPG, WINDOW), lambda b, g, t, i: (g, 0, 0)),
               per_bg(HPG, KV_DIM), per_bg(HPG, 3)],
            out_specs=per_bg(HPG, KV_DIM),
        ),
        out_shape=jax.ShapeDtypeStruct((bsz, KV_HEADS, HPG, KV_DIM), F32),
        compiler_params=pltpu.CompilerParams(
            dimension_semantics=("parallel", "arbitrary"),
            vmem_limit_bytes=VMEM_LIMIT_BYTES),
    )(tbl, blk_idx, *([sel_view] * topk), win_view, qbd4, spread(ks_new), spread(vs_new), spread(kw_new),
      spread(vw_new), bsel, bwin, ocmp4, gl)
    o = o_full.reshape(bsz, KV_HEADS, HPG, KV_HEADS, HEAD_DIM).sum(axis=3).reshape(bsz, c0)
    y = norm_matmul(o, None, w_out, normalize=False, res=x2)
    new_win_buf = jnp.concatenate([win_past[:, 1:], new_win], axis=1)
    return y.reshape(x.shape), new_cmp, new_sel, new_win_buf


def _gla_chunked(q, k, v, logf, s0):
    bsz, L = q.shape[0], q.shape[1]
    nc = -(-L // H_CHUNK)
    pad = nc * H_CHUNK - L

    def prep(t):
        t = jnp.pad(t, ((0, 0), (0, pad), (0, 0), (0, 0)))
        return t.reshape(bsz, nc, H_CHUNK, H_HEADS, t.shape[-1])

    q, k, v, logf = prep(q), prep(k), prep(v), prep(logf)
    acum = jnp.cumsum(logf, axis=2)
    alast = acum[:, :, -1:]
    qe = q * jnp.exp(acum)
    ke = k * jnp.exp(-acum)
    kd = k * jnp.exp(alast - acum)
    tri = jnp.tril(jnp.ones((H_CHUNK, H_CHUNK), bool))
    att = jnp.where(tri, jnp.einsum('bcthk,bcshk->bchts', qe, ke), 0.0)
    o_intra = jnp.einsum('bchts,bcshv->bcthv', att, v)
    upd = jnp.einsum('bcshk,bcshv->bchkv', kd, v)

    def step(s, inp):
        dec, up = inp
        return dec[..., None] * s + up, s

    s_last, s_prev = lax.scan(step, s0, (jnp.exp(alast[:, :, 0]).swapaxes(0, 1), upd.swapaxes(0, 1)))
    o_inter = jnp.einsum('bcthk,bchkv->bcthv', qe, s_prev.swapaxes(0, 1))
    o = (o_intra + o_inter).reshape(bsz, nc * H_CHUNK, H_HEADS, H_DV)[:, :L]
    return o, s_last


HGRN_TOKENS = 128


def _hgrn_body(q_ref, f_ref, v_ref, gate_ref, lb_ref, ng_ref, lcum_ref, lsum_ref, o_ref, s_ref):
    @pl.when(pl.program_id(1) == 0)
    def _():
        s_ref[...] = jnp.zeros_like(s_ref)

    n = HGRN_TOKENS
    qr = q_ref[...]
    q = qr * jax.nn.sigmoid(qr)
    lb = lb_ref[...]
    forget = lb + (1.0 - lb) * jax.nn.sigmoid(f_ref[...])
    logf = jnp.log(forget)
    k = 1.0 - forget
    parts = _split3(logf)
    acum = sum(jnp.dot(lcum_ref[...], p, preferred_element_type=F32) for p in parts)
    atot = sum(jnp.dot(lsum_ref[...], p, preferred_element_type=F32) for p in parts)
    qe = (q * jnp.exp(acum)).astype(BF16)
    ke = (k * jnp.exp(-acum)).astype(BF16)
    kd = (k * jnp.exp(atot - acum)).astype(BF16)
    vb = v_ref[...].astype(BF16)
    ti = lax.broadcasted_iota(jnp.int32, (n, n), 0)
    si = lax.broadcasted_iota(jnp.int32, (n, n), 1)
    intra = (ti // H_CHUNK == si // H_CHUNK) & (ti >= si)
    gate = gate_ref[...]
    gate = gate * jax.nn.sigmoid(gate)
    outs = []
    for h in range(H_HEADS):
        cols = slice(h * H_DK, (h + 1) * H_DK)
        att = jnp.where(intra, _nt_dot(qe[:, cols], ke[:, cols]), 0.0)
        o_h = jnp.dot(att.astype(BF16), vb[:, cols], preferred_element_type=F32)
        decay_t = jnp.exp(atot[:, cols]).T
        state = s_ref[0, cols, :]
        inter = []
        for j in range(n // H_CHUNK):
            rows = slice(j * H_CHUNK, (j + 1) * H_CHUNK)
            inter.append(jnp.dot(qe[rows, cols], state.astype(BF16), preferred_element_type=F32))
            upd = lax.dot_general(kd[rows, cols], vb[rows, cols], (((0,), (0,)), ((), ())),
                                  preferred_element_type=F32)
            state = decay_t[:, j * H_CHUNK:j * H_CHUNK + 1] * state + upd
        s_ref[0, cols, :] = state
        o_h = o_h + jnp.concatenate(inter, axis=0)
        o_h = o_h * lax.rsqrt(jnp.mean(o_h * o_h, axis=-1, keepdims=True) + EPS) * ng_ref[...]
        outs.append(o_h * gate[:, cols])
    o_ref[...] = jnp.concatenate(outs, axis=1).astype(BF16)


def hgrn2_prompt_mixer(x, g, lb, w_in, norm_g, w_out):
    bsz, L, _ = x.shape
    n = HGRN_TOKENS
    assert L % n == 0 and n % H_CHUNK == 0 and H_DK == LANES and H_DV == LANES
    t = bsz * L
    x2 = x.reshape(t, D_MODEL)
    proj = norm_matmul(x2, g, w_in)
    nb = L // n
    r = np.arange(n)
    same = (r[:, None] // H_CHUNK) == (r[None, :] // H_CHUNK)
    lcum = jnp.asarray(same & (r[None, :] <= r[:, None]), BF16)
    lsum = jnp.asarray(same, BF16)
    col = lambda kk: pl.BlockSpec((n, D_MODEL), lambda b, c: (b * nb + c, kk))
    const = lambda a: pl.BlockSpec(a.shape, lambda b, c: (0,) * a.ndim)
    lb2 = lb.reshape(1, D_MODEL)
    ng2 = norm_g.reshape(1, H_DV)
    o, s = pl.pallas_call(
        _hgrn_body,
        name="hgrn_blocks",
        grid=(bsz, nb),
        in_specs=[col(0), col(1), col(2), col(3), const(lb2), const(ng2), const(lcum), const(lsum)],
        out_specs=[pl.BlockSpec((n, D_MODEL), lambda b, c: (b * nb + c, 0)),
                   pl.BlockSpec((1, H_HEADS * H_DK, H_DV), lambda b, c: (b, 0, 0))],
        out_shape=[jax.ShapeDtypeStruct((t, D_MODEL), BF16),
                   jax.ShapeDtypeStruct((bsz, H_HEADS * H_DK, H_DV), F32)],
        compiler_params=pltpu.CompilerParams(
            dimension_semantics=("parallel", "arbitrary"),
            vmem_limit_bytes=VMEM_LIMIT_BYTES),
    )(proj, proj, proj, proj, lb2, ng2, lcum, lsum)
    y = norm_matmul(o, None, w_out, normalize=False, res=x2)
    return y.reshape(bsz, L, D_MODEL), s.reshape(bsz, H_HEADS, H_DK, H_DV)


def hgrn2_mixer(x, g, s0, lb, w_in, norm_g, w_out):
    bsz, L, _ = x.shape
    proj = norm_matmul(x.reshape(bsz * L, D_MODEL), g, w_in).reshape(bsz, L, -1)
    q, fr, iv, gate = jnp.split(proj, 4, axis=-1)
    q = jax.nn.silu(q).reshape(bsz, L, H_HEADS, H_DK)
    forget = lb + (1.0 - lb) * jax.nn.sigmoid(fr)
    logf = jnp.log(forget).reshape(bsz, L, H_HEADS, H_DK)
    k = (1.0 - forget).reshape(bsz, L, H_HEADS, H_DK)
    v = iv.reshape(bsz, L, H_HEADS, H_DV)
    o, s = _gla_chunked(q, k, v, logf, s0)
    o = _rmsnorm(o, norm_g) * jax.nn.silu(gate.reshape(bsz, L, H_HEADS, H_DV))
    y = norm_matmul(o.reshape(bsz * L, D_MODEL), None, w_out, normalize=False)
    return y.reshape(bsz, L, D_MODEL), s


def _gather_pages(cache, page_table):
    pages = cache[page_table]
    b, n, p = pages.shape[0], pages.shape[1], pages.shape[2]
    return pages.reshape(b, n * p, *pages.shape[3:])


def kernel(x_prompt, x_sample, state_ssm, state_conv, cache_kv_cmp, cache_kv_sel, cache_kv_win, state_hgrn, page_table, norm_g, rel_table, m_w_in, m_conv_w, m_conv_b, m_dt_bias, m_a_log, m_d, m_norm_g, m_w_out, n_w_in, n_q_g, n_k_g, n_cmp_pe, n_cmp_w1, n_cmp_w2, n_w_out, h_w_in, h_lb, h_norm_g, h_w_out, moe_w_rg, moe_b_rg, moe_w_re, moe_b_re, moe_w1, moe_w3, moe_w2):
    bp, lp = x_prompt.shape[0], x_prompt.shape[1]
    past_len = page_table.shape[1] * cache_kv_cmp.shape[2]
    dt = x_prompt.dtype
    lbs = jax.nn.softmax(h_lb.astype(F32), axis=0)
    lbs = jnp.cumsum(lbs, axis=0) - lbs[0]
    xp, xs = x_prompt, x_sample
    ssm_p, conv_p, cmp_p, sel_p, win_p, hg_p = [], [], [], [], [], []
    ssm_s, conv_s, cmp_s, sel_s, win_s, hg_s = [], [], [], [], [], []
    for i in range(DEPTH):
        kind, j = i % N_MIXERS, i // N_MIXERS
        g0 = norm_g[i, 0]
        if kind == 0:
            w = (m_w_in[j], m_conv_w[j], m_conv_b[j], m_dt_bias[j], m_a_log[j], m_d[j], m_norm_g[j], m_w_out[j])
            xp, cbuf, hh = mamba_prompt_mixer(xp, g0, *w)
            ssm_p.append(hh)
            conv_p.append(cbuf)
            ys, cbuf, hh = mamba_mixer(xs, g0, state_conv[j], state_ssm[j], *w)
            xs = xs + ys
            ssm_s.append(hh)
            conv_s.append(cbuf)
        elif kind == 1:
            w = (n_w_in[j], n_q_g[j], n_k_g[j], n_cmp_pe[j], n_cmp_w1[j], n_cmp_w2[j], n_w_out[j], rel_table)
            xp, rc, rs, wb = nsa_prompt_mixer(xp, g0, *w)
            cmp_p.append(rc)
            sel_p.append(rs)
            win_p.append(wb)
            xs, rc, rs, wb = nsa_decode_mixer(xs, g0, past_len, cache_kv_cmp[j], cache_kv_sel[j],
                                              cache_kv_win[j], page_table, *w)
            cmp_s.append(rc)
            sel_s.append(rs)
            win_s.append(wb)
        else:
            w = (lbs[i], h_w_in[j], h_norm_g[j], h_w_out[j])
            xp, st = hgrn2_prompt_mixer(xp, g0, *w)
            hg_p.append(st)
            ys, st = hgrn2_mixer(xs, g0, state_hgrn[j], *w)
            xs = xs + ys
            hg_s.append(st)
        mw = (moe_w_rg[i], moe_b_rg[i], moe_w_re[i], moe_b_re[i], moe_w1[i], moe_w3[i], moe_w2[i])
        xp = hier_moe_residual(xp.reshape(-1, D_MODEL), norm_g[i, 1], *mw).reshape(xp.shape)
        xs = hier_moe_residual(xs.reshape(-1, D_MODEL), norm_g[i, 1], *mw).reshape(xs.shape)
    return (xp, xs,
            jnp.stack(ssm_p), jnp.stack(conv_p), jnp.stack(cmp_p), jnp.stack(sel_p), jnp.stack(win_p), jnp.stack(hg_p),
            jnp.stack(ssm_s), jnp.stack(conv_s), jnp.stack(cmp_s), jnp.stack(sel_s), jnp.stack(win_s), jnp.stack(hg_s))
```

```python
import functools
import math

import jax
import jax.numpy as jnp
import numpy as np
from jax import lax
from jax.experimental import pallas as pl
from jax.experimental.pallas import tpu as pltpu

F32 = jnp.float32
BF16 = jnp.bfloat16
EPS = 1e-6

D_MODEL = 1024
DEPTH = 4
N_MIXERS = 3

M_DINNER = 2 * D_MODEL
M_HEADDIM = 64
M_HEADS = M_DINNER // M_HEADDIM
M_GROUPS = 4
M_DSTATE = 128
M_CONV = 4
M_GN = M_GROUPS * M_DSTATE
M_CONV_DIM = M_DINNER + 2 * M_GN
M_CHUNK = 128

N_HEADS = 16
HEAD_DIM = D_MODEL // N_HEADS
KV_HEADS = 4
HPG = N_HEADS // KV_HEADS
KV_DIM = KV_HEADS * HEAD_DIM
CMP_BLOCK = 32
CMP_STRIDE = 16
SEL_BLOCK = 64
SEL_TOPK = 16
WINDOW = 512
SEL_QBLK = 32
WIN_QBLK = 128
SCALE = HEAD_DIM ** -0.5
REL_BUCKETS = 32
REL_MAX_DIST = 128

H_DK = 128
H_HEADS = D_MODEL // H_DK
H_DV = D_MODEL // H_HEADS
H_CHUNK = 32

MOE_GROUPS = 4
MOE_EPG = 4
MOE_EXPERTS = MOE_GROUPS * MOE_EPG
MOE_TOPK = 2
MOE_FF = 512

VMEM_LIMIT_BYTES = 48 * 1024 * 1024
DECODE_VMEM_LIMIT_BYTES = 56 * 1024 * 1024


def _rmsnorm(x, g):
    xf = x.astype(F32)
    y = xf * lax.rsqrt(jnp.mean(xf * xf, axis=-1, keepdims=True) + EPS)
    return (y * g.astype(F32)).astype(x.dtype)


LANES = 128
MXU_WIDTH = 256


def _norm_matmul_body(*refs, normalize, has_res):
    if has_res:
        x_ref, g_ref, w_ref, res_ref, o_ref, xb_ref = refs
    else:
        x_ref, g_ref, w_ref, o_ref, xb_ref = refs

    @pl.when(pl.program_id(1) == 0)
    def _():
        x = x_ref[...].astype(F32)
        if normalize:
            x = x * lax.rsqrt(jnp.mean(x * x, axis=-1, keepdims=True) + EPS) * g_ref[...]
        xb_ref[...] = x.astype(BF16)

    acc = jnp.dot(xb_ref[...], w_ref[...].astype(BF16), preferred_element_type=F32)
    if has_res:
        acc = acc + res_ref[...]
    o_ref[...] = acc


def _pick_tile(n, pref):
    t = min(n, pref)
    while n % t:
        t //= 2
    return t


def norm_matmul(x, g, w, *, normalize=True, res=None, tm=None, tn=512):
    t, k = x.shape
    n_true = w.shape[1]
    tm = _pick_tile(t, tm or (1024 if k <= 1024 else 512))
    col_tile = MXU_WIDTH if n_true > MXU_WIDTH else LANES
    if n_true % col_tile:
        assert res is None
        w = jnp.pad(w, ((0, 0), (0, col_tile - n_true % col_tile)))
    n = w.shape[1]
    tn = _pick_tile(n, tn)
    if g is None:
        g = jnp.ones((k,), F32)
    in_specs = [
        pl.BlockSpec((tm, k), lambda i, j: (i, 0)),
        pl.BlockSpec((1, k), lambda i, j: (0, 0)),
        pl.BlockSpec((k, tn), lambda i, j: (0, j)),
    ]
    args = [x, g.reshape(1, k), w]
    if res is not None:
        in_specs.append(pl.BlockSpec((tm, tn), lambda i, j: (i, j)))
        args.append(res)
    out = pl.pallas_call(
        functools.partial(_norm_matmul_body, normalize=normalize, has_res=res is not None),
        name="norm_matmul",
        grid=(t // tm, n // tn),
        in_specs=in_specs,
        out_specs=pl.BlockSpec((tm, tn), lambda i, j: (i, j)),
        out_shape=jax.ShapeDtypeStruct((t, n), F32),
        scratch_shapes=[pltpu.VMEM((tm, k), BF16)],
        compiler_params=pltpu.CompilerParams(
            dimension_semantics=("parallel", "arbitrary"),
            vmem_limit_bytes=VMEM_LIMIT_BYTES),
    )(*args)
    return out if n == n_true else out[:, :n_true]


ROUTE_LANES = 128
MOE_TILE = 1024
MOE_ROWS = 160
NEG = -1e30


def _router_body(x_ref, g_ref, w_ref, b_ref, u_ref, xn_ref, rank_ref, wt_ref):
    x = x_ref[...]
    tm = x.shape[0]
    xn = x * lax.rsqrt(jnp.mean(x * x, axis=-1, keepdims=True) + EPS) * g_ref[...]
    xb = xn.astype(BF16)
    xn_ref[...] = xb
    logits = jnp.dot(xb, w_ref[...].astype(BF16), preferred_element_type=F32) + b_ref[...]
    lane = lax.broadcasted_iota(jnp.int32, (tm, ROUTE_LANES), 1).astype(F32)

    def first_max(mask):
        v = jnp.max(jnp.where(mask, logits, NEG), axis=-1, keepdims=True)
        i = jnp.min(jnp.where(mask & (logits == v), lane, float(ROUTE_LANES)), axis=-1, keepdims=True)
        return v, i

    is_group = lane < MOE_GROUPS
    mg, g_idx = first_max(is_group)
    pg_top = 1.0 / jnp.sum(jnp.where(is_group, jnp.exp(logits - mg), 0.0), axis=-1, keepdims=True)
    lo = MOE_GROUPS + MOE_EPG * g_idx
    in_group = (lane >= lo) & (lane < lo + MOE_EPG)
    v1, i1 = first_max(in_group)
    v2, i2 = first_max(in_group & (lane != i1))
    e2 = jnp.exp(v2 - v1)
    w_a = pg_top / (1.0 + e2)
    w_b = pg_top * e2 / (1.0 + e2)
    info = jnp.where(lane == 0, i1 - MOE_GROUPS, jnp.where(lane == 1, i2 - MOE_GROUPS,
                     jnp.where(lane == 2, w_a, jnp.where(lane == 3, w_b, 0.0))))
    info_t = info.T
    e_a, e_b, w_at, w_bt = info_t[0:1], info_t[1:2], info_t[2:3], info_t[3:4]
    expert = lax.broadcasted_iota(jnp.int32, (MOE_EXPERTS, tm), 0).astype(F32)
    m_a = e_a == expert
    m_b = e_b == expert
    onehot = jnp.concatenate([jnp.where(m_a, 1.0, 0.0), jnp.where(m_b, 1.0, 0.0)], axis=0).astype(BF16)
    cum = jnp.dot(onehot, u_ref[...], preferred_element_type=F32)
    cum_a, cum_b = cum[:MOE_EXPERTS], cum[MOE_EXPERTS:]
    n_a = cum_a[:, tm - 1:tm]
    rank_ref[...] = jnp.where(m_a, cum_a - 1.0, jnp.where(m_b, n_a + cum_b - 1.0, -1.0))
    wt_ref[...] = jnp.where(m_a, w_at, jnp.where(m_b, w_bt, 0.0))


def moe_route(x, g, w_rg, b_rg, w_re, b_re, tm):
    t = x.shape[0]
    pad = ROUTE_LANES - MOE_GROUPS - MOE_EXPERTS
    w = jnp.pad(jnp.concatenate([w_rg, w_re], axis=1), ((0, 0), (0, pad)))
    b = jnp.pad(jnp.concatenate([b_rg, b_re]), (0, pad)).reshape(1, ROUTE_LANES)
    upper = jnp.asarray(np.triu(np.ones((tm, tm), np.float32)), BF16)
    const = lambda a: pl.BlockSpec(a.shape, lambda i: (0,) * a.ndim)
    g2 = g.reshape(1, D_MODEL)
    return pl.pallas_call(
        _router_body,
        name="moe_router",
        grid=(t // tm,),
        in_specs=[pl.BlockSpec((tm, D_MODEL), lambda i: (i, 0)), const(g2), const(w), const(b), const(upper)],
        out_specs=[pl.BlockSpec((tm, D_MODEL), lambda i: (i, 0)),
                   pl.BlockSpec((MOE_EXPERTS, tm), lambda i: (0, i)),
                   pl.BlockSpec((MOE_EXPERTS, tm), lambda i: (0, i))],
        out_shape=[jax.ShapeDtypeStruct((t, D_MODEL), BF16),
                   jax.ShapeDtypeStruct((MOE_EXPERTS, t), F32),
                   jax.ShapeDtypeStruct((MOE_EXPERTS, t), F32)],
        compiler_params=pltpu.CompilerParams(
            dimension_semantics=("parallel",), vmem_limit_bytes=VMEM_LIMIT_BYTES),
    )(x, g2, w, b, upper)


def _moe_expert_body(cnt_ref, xn_ref, rank_ref, wt_ref, w1_ref, w3_ref, w2_ref, res_ref, o_ref, *, rows):
    ti, e = pl.program_id(0), pl.program_id(1)

    @pl.when(e == 0)
    def _():
        o_ref[...] = res_ref[...]

    tm = xn_ref.shape[0]
    rank = rank_ref[pl.ds(e, 1), :]
    wt = wt_ref[pl.ds(e, 1), :]
    n_chunks = (cnt_ref[e, ti] + rows - 1) // rows

    def chunk(c, carry):
        r = (lax.broadcasted_iota(jnp.int32, (rows, tm), 0) + c * rows).astype(F32)
        sel = jnp.where(rank == r, 1.0, 0.0)
        selb = sel.astype(BF16)
        xs = jnp.dot(selb, xn_ref[...], preferred_element_type=F32).astype(BF16)
        a = jnp.dot(xs, w1_ref[0].astype(BF16), preferred_element_type=F32)
        b = jnp.dot(xs, w3_ref[0].astype(BF16), preferred_element_type=F32)
        h = (a * jax.nn.sigmoid(a) * b * jnp.sum(sel * wt, axis=1, keepdims=True)).astype(BF16)
        y = jnp.dot(h, w2_ref[0].astype(BF16), preferred_element_type=F32)
        y_hi = y.astype(BF16)
        y_lo = (y - y_hi.astype(F32)).astype(BF16)
        tn = (((0,), (0,)), ((), ()))
        o_ref[...] += (lax.dot_general(selb, y_hi, tn, preferred_element_type=F32)
                       + lax.dot_general(selb, y_lo, tn, preferred_element_type=F32))
        return carry

    lax.fori_loop(0, n_chunks, chunk, 0)


def moe_experts(xn, rank, wt, counts, w1, w3, w2, res, tm):
    t = xn.shape[0]
    rows = min(MOE_ROWS, tm)
    grid_spec = pltpu.PrefetchScalarGridSpec(
        num_scalar_prefetch=1,
        grid=(t // tm, MOE_EXPERTS),
        in_specs=[
            pl.BlockSpec((tm, D_MODEL), lambda i, e, c: (i, 0)),
            pl.BlockSpec((MOE_EXPERTS, tm), lambda i, e, c: (0, i)),
            pl.BlockSpec((MOE_EXPERTS, tm), lambda i, e, c: (0, i)),
            pl.BlockSpec((1, D_MODEL, MOE_FF), lambda i, e, c: (e, 0, 0)),
            pl.BlockSpec((1, D_MODEL, MOE_FF), lambda i, e, c: (e, 0, 0)),
            pl.BlockSpec((1, MOE_FF, D_MODEL), lambda i, e, c: (e, 0, 0)),
            pl.BlockSpec((tm, D_MODEL), lambda i, e, c: (i, 0)),
        ],
        out_specs=pl.BlockSpec((tm, D_MODEL), lambda i, e, c: (i, 0)),
    )
    return pl.pallas_call(
        functools.partial(_moe_expert_body, rows=rows),
        name="moe_experts",
        grid_spec=grid_spec,
        out_shape=jax.ShapeDtypeStruct((t, D_MODEL), F32),
        compiler_params=pltpu.CompilerParams(
            dimension_semantics=("parallel", "arbitrary"),
            vmem_limit_bytes=VMEM_LIMIT_BYTES),
    )(counts, xn, rank, wt, w1, w3, w2, res)


def hier_moe_residual(x, g, w_rg, b_rg, w_re, b_re, w1, w3, w2):
    t_true = x.shape[0]
    tm = MOE_TILE if t_true % MOE_TILE == 0 else ROUTE_LANES
    if t_true % tm:
        x = jnp.pad(x, ((0, tm - t_true % tm), (0, 0)))
    t = x.shape[0]
    xn, rank, wt = moe_route(x, g, w_rg, b_rg, w_re, b_re, tm)
    counts = jnp.sum((rank >= 0).reshape(MOE_EXPERTS, t // tm, tm), axis=-1, dtype=jnp.int32)
    return moe_experts(xn, rank, wt, counts, w1, w3, w2, x, tm)[:t_true]


def _causal_dwconv(u, buf, w, b):
    L = u.shape[1]
    ext = jnp.concatenate([buf.astype(u.dtype), u], axis=1)
    out = b + sum(ext[:, k:k + L] * w[k] for k in range(M_CONV))
    return out, ext[:, L:]


def _ssd_scan(x, dt, a, bm, cm, h0):
    bsz, L = x.shape[0], x.shape[1]
    q = M_CHUNK if L % M_CHUNK == 0 else L
    nc = L // q
    hpg = M_HEADS // M_GROUPS
    xdt = (x * dt[..., None]).reshape(bsz, nc, q, M_GROUPS, hpg, M_HEADDIM)
    acum = jnp.cumsum((dt * a).reshape(bsz, nc, q, M_GROUPS, hpg), axis=2)
    bc = bm.reshape(bsz, nc, q, M_GROUPS, M_DSTATE)
    cc = cm.reshape(bsz, nc, q, M_GROUPS, M_DSTATE)
    tri = jnp.tril(jnp.ones((q, q), bool))[:, :, None, None]
    seg = acum[:, :, :, None] - acum[:, :, None]
    decay = jnp.exp(jnp.where(tri, seg, -jnp.inf))
    cb = jnp.einsum('bclgn,bcsgn->bclsg', cc, bc)
    y_diag = jnp.einsum('bclsg,bclsgh,bcsghp->bclghp', cb, decay, xdt)
    decay_end = jnp.exp(acum[:, :, -1:] - acum)
    states = jnp.einsum('bcsgn,bcsgh,bcsghp->bcghpn', bc, decay_end, xdt)
    chunk_decay = jnp.exp(acum[:, :, -1])

    def step(h, inp):
        cd, st = inp
        return cd[..., None, None] * h + st, h

    h_last, h_prev = lax.scan(step, h0.reshape(bsz, M_GROUPS, hpg, M_HEADDIM, M_DSTATE),
                              (chunk_decay.swapaxes(0, 1), states.swapaxes(0, 1)))
    y_off = jnp.einsum('bclgn,bclgh,bcghpn->bclghp', cc, jnp.exp(acum), h_prev.swapaxes(0, 1))
    y = (y_diag + y_off).reshape(bsz, L, M_HEADS, M_HEADDIM)
    return y, h_last.reshape(bsz, M_HEADS, M_HEADDIM, M_DSTATE)


def mamba_mixer(x, g, conv_buf, h0, w_in, conv_w, conv_b, dt_bias, a_log, d_skip, norm_g, w_out):
    bsz, L, _ = x.shape
    proj = norm_matmul(x.reshape(bsz * L, D_MODEL), g, w_in).reshape(bsz, L, -1)
    z = proj[..., :M_DINNER]
    xbc = proj[..., M_DINNER:M_DINNER + M_CONV_DIM]
    dt_raw = proj[..., M_DINNER + M_CONV_DIM:]
    xbc, new_buf = _causal_dwconv(xbc, conv_buf, conv_w, conv_b)
    xbc = jax.nn.silu(xbc)
    xs = xbc[..., :M_DINNER].reshape(bsz, L, M_HEADS, M_HEADDIM)
    bm = xbc[..., M_DINNER:M_DINNER + M_GN].reshape(bsz, L, M_GROUPS, M_DSTATE)
    cm = xbc[..., M_DINNER + M_GN:].reshape(bsz, L, M_GROUPS, M_DSTATE)
    dt = jax.nn.softplus(dt_raw + dt_bias)
    a = -jnp.exp(a_log)
    y, h = _ssd_scan(xs, dt, a, bm, cm, h0)
    y = y + xs * d_skip[:, None]
    y = y.reshape(bsz, L, M_DINNER)
    y = _rmsnorm(y * jax.nn.silu(z), norm_g)
    out = norm_matmul(y.reshape(bsz * L, M_DINNER), None, w_out, normalize=False)
    return out.reshape(bsz, L, D_MODEL), new_buf, h


SSD_HPG = M_HEADS // M_GROUPS
SSD_GROUP_ROWS = SSD_HPG * M_HEADDIM
CONV_PAD = 8


def _transpose_cols(x):
    return jnp.concatenate([x[:, j * LANES:(j + 1) * LANES].T for j in range(x.shape[1] // LANES)], axis=0)


def _transpose_rows(x):
    return jnp.concatenate([x[j * LANES:(j + 1) * LANES, :].T for j in range(x.shape[0] // LANES)], axis=1)


def _ssd_body(z_ref, xbc_ref, dtr_ref, cw_ref, cb_ref, dtb_ref, a_ref, dcol_ref, ng_ref, ltri_ref,
              y_ref, conv_ref, h_ref, xbuf, *, q):
    c = pl.program_id(1)

    @pl.when(c == 0)
    def _():
        h_ref[...] = jnp.zeros_like(h_ref)
        xbuf[0:CONV_PAD, :] = jnp.zeros((CONV_PAD, M_CONV_DIM), F32)

    xbuf[CONV_PAD:CONV_PAD + q, :] = xbc_ref[...]
    conv = cb_ref[...]
    for k in range(M_CONV):
        start = CONV_PAD - (M_CONV - 1) + k
        conv = conv + xbuf[start:start + q, :] * cw_ref[k:k + 1, :]
    tail = xbuf[CONV_PAD + q - (M_CONV - 1):CONV_PAD + q, :]
    xbuf[CONV_PAD - (M_CONV - 1):CONV_PAD, :] = tail
    conv_ref[0] = tail
    xc = conv * jax.nn.sigmoid(conv)
    xs = xc[:, :M_DINNER]
    xs_t = _transpose_cols(xs)

    pre = dtr_ref[...] + dtb_ref[...]
    dt = jnp.maximum(pre, 0.0) + jnp.log1p(jnp.exp(-jnp.abs(pre)))
    da = dt * a_ref[...]
    ltri = ltri_ref[...]
    acum = sum(jnp.dot(ltri, part, preferred_element_type=F32) for part in _split3(da))
    dt_t = dt.T
    acum_t = acum.T
    li = lax.broadcasted_iota(jnp.int32, (q, q), 0)
    si = lax.broadcasted_iota(jnp.int32, (q, q), 1)
    causal = li >= si

    y_t = []
    for g in range(M_GROUPS):
        bm = xc[:, M_DINNER + g * M_DSTATE:M_DINNER + (g + 1) * M_DSTATE].astype(BF16)
        cm = xc[:, M_DINNER + M_GN + g * M_DSTATE:M_DINNER + M_GN + (g + 1) * M_DSTATE].astype(BF16)
        cb = _nt_dot(cm, bm)
        r0 = g * SSD_GROUP_ROWS
        h_prev = h_ref[0, r0:r0 + SSD_GROUP_ROWS, :]
        y_off = _nt_dot(h_prev.astype(BF16), cm)
        x_dec, scale = [], []
        for hh in range(g * SSD_HPG, (g + 1) * SSD_HPG):
            a_row = acum_t[hh:hh + 1, :]
            a_col = acum[:, hh:hh + 1]
            decay = jnp.where(causal, jnp.exp(a_col - a_row), 0.0)
            m = (cb * decay).astype(BF16)
            rows = slice(hh * M_HEADDIM, (hh + 1) * M_HEADDIM)
            xs_h = xs_t[rows]
            xdt = xs_h * dt_t[hh:hh + 1, :]
            y_h = _nt_dot(xdt.astype(BF16), m)
            y_h = y_h + y_off[rows.start - r0:rows.stop - r0] * jnp.exp(a_row) + xs_h * dcol_ref[hh:hh + 1, :]
            y_t.append(y_h)
            a_last = a_row[:, q - 1:q]
            x_dec.append(xdt * jnp.exp(a_last - a_row))
            scale.append(jnp.broadcast_to(jnp.exp(a_last), (M_HEADDIM, 1)))
        upd = jnp.dot(jnp.concatenate(x_dec, axis=0).astype(BF16), bm, preferred_element_type=F32)
        h_ref[0, r0:r0 + SSD_GROUP_ROWS, :] = jnp.concatenate(scale, axis=0) * h_prev + upd

    y = _transpose_rows(jnp.concatenate(y_t, axis=0))
    zz = z_ref[...]
    yg = y * (zz * jax.nn.sigmoid(zz))
    yg = yg * lax.rsqrt(jnp.mean(yg * yg, axis=-1, keepdims=True) + EPS) * ng_ref[...]
    y_ref[...] = yg.astype(BF16)


def ssd_prompt(z, xbc, dtr, conv_w, conv_b, dt_bias, a_log, d_skip, norm_g, bsz, L):
    q = M_CHUNK
    assert L % q == 0 and q == LANES
    nc = L // q
    padl = lambda v: jnp.pad(v, (0, LANES - v.shape[0]))
    dtb = padl(dt_bias).reshape(1, LANES)
    a_row = padl(-jnp.exp(a_log)).reshape(1, LANES)
    dcol = jnp.broadcast_to(padl(d_skip).reshape(LANES, 1), (LANES, LANES))
    ltri = jnp.asarray(np.tril(np.ones((q, q), np.float32)), BF16)
    const = lambda a: pl.BlockSpec(a.shape, lambda b, c: (0,) * a.ndim)
    tok = lambda w: pl.BlockSpec((q, w), lambda b, c: (b * nc + c, 0))
    cb2 = conv_b.reshape(1, M_CONV_DIM)
    ng2 = norm_g.reshape(1, M_DINNER)
    y, conv_tail, h = pl.pallas_call(
        functools.partial(_ssd_body, q=q),
        name="ssd_chunks",
        grid=(bsz, nc),
        in_specs=[tok(M_DINNER), tok(M_CONV_DIM), tok(LANES), const(conv_w), const(cb2), const(dtb),
                  const(a_row), const(dcol), const(ng2), const(ltri)],
        out_specs=[tok(M_DINNER),
                   pl.BlockSpec((1, M_CONV - 1, M_CONV_DIM), lambda b, c: (b, 0, 0)),
                   pl.BlockSpec((1, M_DINNER, M_DSTATE), lambda b, c: (b, 0, 0))],
        out_shape=[jax.ShapeDtypeStruct((bsz * L, M_DINNER), BF16),
                   jax.ShapeDtypeStruct((bsz, M_CONV - 1, M_CONV_DIM), F32),
                   jax.ShapeDtypeStruct((bsz, M_DINNER, M_DSTATE), F32)],
        scratch_shapes=[pltpu.VMEM((CONV_PAD + q, M_CONV_DIM), F32)],
        compiler_params=pltpu.CompilerParams(
            dimension_semantics=("parallel", "arbitrary"),
            vmem_limit_bytes=VMEM_LIMIT_BYTES),
    )(z, xbc, dtr, conv_w, cb2, dtb, a_row, dcol, ng2, ltri)
    return y, conv_tail, h.reshape(bsz, M_HEADS, M_HEADDIM, M_DSTATE)


def mamba_prompt_mixer(x, g, w_in, conv_w, conv_b, dt_bias, a_log, d_skip, norm_g, w_out):
    bsz, L, _ = x.shape
    x2 = x.reshape(bsz * L, D_MODEL)
    z = norm_matmul(x2, g, w_in[:, :M_DINNER])
    xbc = norm_matmul(x2, g, w_in[:, M_DINNER:M_DINNER + M_CONV_DIM])
    dtr = norm_matmul(x2, g, jnp.pad(w_in[:, M_DINNER + M_CONV_DIM:], ((0, 0), (0, LANES - M_HEADS))))
    y, conv_tail, h = ssd_prompt(z, xbc, dtr, conv_w, conv_b, dt_bias, a_log, d_skip, norm_g, bsz, L)
    out = norm_matmul(y, None, w_out, normalize=False, res=x2)
    return out.reshape(bsz, L, D_MODEL), conv_tail, h


def _rel_bucket(dist):
    exact = REL_BUCKETS // 2
    d = jnp.maximum(dist, 0)
    ratio = jnp.log(jnp.maximum(d, 1).astype(F32) / exact) / math.log(REL_MAX_DIST / exact)
    large = jnp.minimum(exact + (ratio * (REL_BUCKETS - exact)).astype(jnp.int32), REL_BUCKETS - 1)
    return jnp.where(d < exact, d, large)


def _head_bias(rel_table, dist):
    b = rel_table[_rel_bucket(dist)].astype(F32)
    return jnp.moveaxis(b, -1, 0).reshape(KV_HEADS, HPG, dist.shape[0], dist.shape[1])


def _masked_softmax(s, valid):
    s = jnp.where(valid, s, -1e30)
    e = jnp.where(valid, jnp.exp(s - jnp.max(s, axis=-1, keepdims=True)), 0.0)
    return e / jnp.maximum(jnp.sum(e, axis=-1, keepdims=True), 1e-30)


def _group_attend(qg, qpos, k, v, kpos, valid, rel_table):
    s = jnp.einsum('bqghd,bkgd->bghqk', qg, k).astype(F32) * SCALE
    s = s + _head_bias(rel_table, qpos[:, None] - kpos[None, :])
    p = _masked_softmax(s, valid)
    o = jnp.einsum('bghqk,bkgd->bqghd', p.astype(v.dtype), v)
    return o, p


def _compress(rows, pe, w1, w2):
    bsz, T = rows.shape[0], rows.shape[1]
    nc = (T - CMP_BLOCK) // CMP_STRIDE + 1
    idx = jnp.arange(nc)[:, None] * CMP_STRIDE + jnp.arange(CMP_BLOCK)[None, :]
    blk = rows[:, idx] + pe[:, None, :]
    blk = jnp.moveaxis(blk, 2, 3).reshape(bsz, nc, KV_HEADS, CMP_BLOCK * HEAD_DIM)
    return jax.nn.silu(blk @ w1) @ w2


def _select_blocks(p_cmp, qpos, n_sel):
    nc = p_cmp.shape[-1]
    cstart = jnp.arange(nc) * CMP_STRIDE
    sstart = jnp.arange(n_sel) * SEL_BLOCK
    overlap = ((cstart[:, None] < sstart[None, :] + SEL_BLOCK) &
               (cstart[:, None] + CMP_BLOCK > sstart[None, :])).astype(F32)
    imp = jnp.einsum('bghqc,cs->bgqs', p_cmp, overlap)
    qblk = qpos // SEL_BLOCK
    j = jnp.arange(n_sel)
    forced = (j[None, :] == qblk[:, None]) | (j[None, :] == 0)
    score = jnp.where(forced, 1e9, jnp.where(j[None, :] <= qblk[:, None], imp, -1e30))
    _, idx = lax.top_k(score, min(SEL_TOPK, n_sel))
    return idx


def _sel_attend(qg, qpos, idx, kblk, vblk, rel_table):
    bsz, lq = qg.shape[0], qg.shape[1]
    bi = jnp.arange(bsz)[:, None, None, None]
    gi = jnp.arange(KV_HEADS)[None, :, None, None]
    kg = kblk[bi, gi, idx]
    vg = vblk[bi, gi, idx]
    s = jnp.einsum('bqghd,bgqkrd->bghqkr', qg, kg).astype(F32) * SCALE
    kpos = idx[..., None] * SEL_BLOCK + jnp.arange(SEL_BLOCK)
    dist = qpos[:, None, None] - kpos
    g5 = jnp.arange(KV_HEADS)[None, :, None, None, None]
    bias = rel_table.reshape(REL_BUCKETS, KV_HEADS, HPG)[_rel_bucket(dist), g5].astype(F32)
    s = s + jnp.moveaxis(bias, -1, 2)
    nk = idx.shape[-1]
    valid = (dist >= 0)[:, :, None].reshape(bsz, KV_HEADS, 1, lq, nk * SEL_BLOCK)
    p = _masked_softmax(s.reshape(bsz, KV_HEADS, HPG, lq, nk * SEL_BLOCK), valid)
    p = p.reshape(bsz, KV_HEADS, HPG, lq, nk, SEL_BLOCK)
    return jnp.einsum('bghqkr,bgqkrd->bqghd', p.astype(vg.dtype), vg)


def nsa_mixer(x, g, pos0, kv_cmp_past, kv_sel_past, kv_win_past, n_keep,
              w_in, q_g, k_g, cmp_pe, cmp_w1, cmp_w2, w_out, rel_table):
    bsz, L, _ = x.shape
    sizes = [N_HEADS * HEAD_DIM] + [KV_DIM] * 6 + [3 * N_HEADS]
    proj = norm_matmul(x.reshape(bsz * L, D_MODEL), g, w_in).reshape(bsz, L, -1)
    q, kc, vc, ks, vs, kw, vw, gl = jnp.split(proj, np.cumsum(sizes)[:-1].tolist(), axis=-1)
    q = _rmsnorm(q.reshape(bsz, L, KV_HEADS, HPG, HEAD_DIM), q_g)

    def kv(t):
        return t.reshape(bsz, L, KV_HEADS, HEAD_DIM)

    new_cmp = jnp.stack([kv(kc), kv(vc)], axis=2)
    new_sel = jnp.stack([_rmsnorm(kv(ks), k_g[1]), kv(vs)], axis=2)
    new_win = jnp.stack([_rmsnorm(kv(kw), k_g[2]), kv(vw)], axis=2)
    qpos = pos0 + jnp.arange(L)

    crows = jnp.concatenate([kv_cmp_past.astype(x.dtype), new_cmp], axis=1)
    kcmp = _rmsnorm(_compress(crows[:, :, 0], cmp_pe[0], cmp_w1[0], cmp_w2[0]), k_g[0])
    vcmp = _compress(crows[:, :, 1], cmp_pe[1], cmp_w1[1], cmp_w2[1])
    ends = jnp.arange(kcmp.shape[1]) * CMP_STRIDE + CMP_BLOCK - 1
    o_cmp, p_cmp = _group_attend(q, qpos, kcmp, vcmp, ends, qpos[:, None] >= ends[None, :], rel_table)

    srows = jnp.concatenate([kv_sel_past.astype(x.dtype), new_sel], axis=1)
    T = srows.shape[1]
    n_sel = -(-T // SEL_BLOCK)
    srows = jnp.pad(srows, ((0, 0), (0, n_sel * SEL_BLOCK - T), (0, 0), (0, 0), (0, 0)))
    blocks = srows.reshape(bsz, n_sel, SEL_BLOCK, 2, KV_HEADS, HEAD_DIM).transpose(3, 0, 4, 1, 2, 5)
    idx = _select_blocks(p_cmp, qpos, n_sel)
    qb = SEL_QBLK if L % SEL_QBLK == 0 else L
    nqb = L // qb

    def sel_block(args):
        qgi, qposi, idxi = args
        return _sel_attend(qgi, qposi, idxi, blocks[0], blocks[1], rel_table)

    o_sel = lax.map(sel_block, (q.reshape(bsz, nqb, qb, KV_HEADS, HPG, HEAD_DIM).swapaxes(0, 1),
                                qpos.reshape(nqb, qb),
                                idx.reshape(bsz, KV_HEADS, nqb, qb, -1).transpose(2, 0, 1, 3, 4)))
    o_sel = o_sel.swapaxes(0, 1).reshape(bsz, L, KV_HEADS, HPG, HEAD_DIM)

    p_win = kv_win_past.shape[1]
    wrows = jnp.concatenate([kv_win_past.astype(x.dtype), new_win], axis=1)
    wpad = jnp.pad(wrows, ((0, 0), (WINDOW, 0), (0, 0), (0, 0), (0, 0)))
    n_all = WINDOW + p_win + L
    kpos_all = pos0 - p_win - WINDOW + jnp.arange(n_all)
    kvalid_all = jnp.arange(n_all) >= WINDOW
    wq = WIN_QBLK if L % WIN_QBLK == 0 else L

    def win_block(i):
        start = p_win + i * wq
        qgi = lax.dynamic_slice_in_dim(q, i * wq, wq, axis=1)
        kvi = lax.dynamic_slice_in_dim(wpad, start, WINDOW + wq, axis=1)
        kposi = lax.dynamic_slice_in_dim(kpos_all, start, WINDOW + wq)
        kvalidi = lax.dynamic_slice_in_dim(kvalid_all, start, WINDOW + wq)
        qposi = pos0 + i * wq + jnp.arange(wq)
        dist = qposi[:, None] - kposi[None, :]
        valid = kvalidi[None, :] & (dist >= 0) & (dist <= WINDOW)
        o, _ = _group_attend(qgi, qposi, kvi[:, :, 0], kvi[:, :, 1], kposi, valid, rel_table)
        return o

    o_win = lax.map(win_block, jnp.arange(L // wq)).swapaxes(0, 1).reshape(bsz, L, KV_HEADS, HPG, HEAD_DIM)

    gate = jax.nn.sigmoid(gl).reshape(bsz, L, KV_HEADS, HPG, 3)
    o = gate[..., 0:1] * o_cmp + gate[..., 1:2] * o_sel + gate[..., 2:3] * o_win
    y = norm_matmul(o.reshape(bsz * L, N_HEADS * HEAD_DIM), None, w_out, normalize=False)
    return y.reshape(bsz, L, D_MODEL), new_cmp, new_sel, wrows[:, -n_keep:]


ATT_TQ = 128
ATT_TK = 128
NEG = -1e30


def _nt_dot(a, b):
    return lax.dot_general(a, b, (((1,), (1,)), ((), ())), preferred_element_type=F32)


def _compress_body(rk_ref, rv_ref, pe_ref, w1_ref, w2_ref, kg_ref, kc_ref, vc_ref, *, nb):
    half = (CMP_BLOCK // 2) * HEAD_DIM
    for kv, (r_ref, o_ref) in enumerate(((rk_ref, kc_ref), (rv_ref, vc_ref))):
        lo = (r_ref[0, 0, 0:nb, :] + pe_ref[kv, 0:1, :]).astype(BF16)
        hi = (r_ref[0, 0, 1:nb + 1, :] + pe_ref[kv, 1:2, :]).astype(BF16)
        h = (jnp.dot(lo, w1_ref[kv, :half, :].astype(BF16), preferred_element_type=F32)
             + jnp.dot(hi, w1_ref[kv, half:, :].astype(BF16), preferred_element_type=F32))
        h = h * jax.nn.sigmoid(h)
        o = jnp.dot(h.astype(BF16), w2_ref[kv].astype(BF16), preferred_element_type=F32)
        if kv == 0:
            o = o * lax.rsqrt(jnp.mean(o * o, axis=-1, keepdims=True) + EPS) * kg_ref[...]
        o_ref[0, 0] = o


def compress_rows(rk, rv, pe, w1, w2, kg):
    bsz, g, nbp, width = rk.shape
    nb = nbp - 8
    strip = pl.BlockSpec((1, 1, nbp, width), lambda b, j: (b, j, 0, 0))
    out = pl.BlockSpec((1, 1, nb, HEAD_DIM), lambda b, j: (b, j, 0, 0))
    full = lambda a: pl.BlockSpec(a.shape, lambda b, j: (0,) * a.ndim)
    pe2 = pe.reshape(2, 2, width)
    kg2 = kg.reshape(1, HEAD_DIM)
    return pl.pallas_call(
        functools.partial(_compress_body, nb=nb),
        name="nsa_compress",
        grid=(bsz, g),
        in_specs=[strip, strip, full(pe2), full(w1), full(w2), full(kg2)],
        out_specs=[out, out],
        out_shape=[jax.ShapeDtypeStruct((bsz, g, nb, HEAD_DIM), F32)] * 2,
        compiler_params=pltpu.CompilerParams(
            dimension_semantics=("parallel", "parallel"),
            vmem_limit_bytes=VMEM_LIMIT_BYTES),
    )(rk, rv, pe2, w1, w2, kg2)


def _split3(x):
    a = x.astype(BF16)
    r = x - a.astype(F32)
    b = r.astype(BF16)
    c = (r - b.astype(F32)).astype(BF16)
    return a, b, c


def _nsa_attn_body(q_ref, kc_ref, vc_ref, ks_ref, vs_ref, kw_ref, vw_ref, gate_ref, bcmp_ref, btile_ref,
                   ovl_ref, o_ref, s_scr, *, n_cmp, n_sel, topk):
    i = pl.program_id(2)
    q0 = i * ATT_TQ
    rows = HPG * ATT_TQ
    qt = q_ref[0]
    qs = (jnp.concatenate([qt[:, h * HEAD_DIM:(h + 1) * HEAD_DIM] for h in range(HPG)], axis=0) * SCALE).astype(BF16)
    row = lax.broadcasted_iota(jnp.int32, (rows, ATT_TK), 0)
    qpos = q0 + (row & (ATT_TQ - 1))
    col = lax.broadcasted_iota(jnp.int32, (rows, ATT_TK), 1)

    s = _nt_dot(qs, kc_ref[0, 0].astype(BF16)) + bcmp_ref[0, 0]
    valid = (qpos >= col * CMP_STRIDE + (CMP_BLOCK - 1)) & (col < n_cmp)
    s = jnp.where(valid, s, NEG)
    e = jnp.where(valid, jnp.exp(s - jnp.max(s, axis=-1, keepdims=True)), 0.0)
    p = e / jnp.maximum(jnp.sum(e, axis=-1, keepdims=True), 1e-30)
    pb = p.astype(BF16)
    o_cmp = jnp.dot(pb, vc_ref[0, 0].astype(BF16), preferred_element_type=F32)

    psum = p[0:ATT_TQ]
    for h in range(1, HPG):
        psum = psum + p[h * ATT_TQ:(h + 1) * ATT_TQ]
    ovl = ovl_ref[...]
    imp_t = _nt_dot(ovl, psum.astype(BF16))
    nblk = imp_t.shape[0]
    nrank = -(-n_sel // 8) * 8
    j = lax.broadcasted_iota(jnp.int32, (nrank, ATT_TQ), 0)
    qblk = (q0 + lax.broadcasted_iota(jnp.int32, (nrank, ATT_TQ), 1)) // SEL_BLOCK
    forced = (j == qblk) | (j == 0)
    score = jnp.where(forced, 1e9, jnp.where(j <= qblk, imp_t[:nrank], NEG))
    score = jnp.where(j < n_sel, score, -3e38)
    rank = jnp.zeros((nrank, ATT_TQ), F32)
    for jp in range(n_sel):
        other = score[jp:jp + 1, :]
        beats = (other > score) | ((other == score) & (jp < j))
        rank = rank + jnp.where(beats, 1.0, 0.0)
    sel_t = jnp.where((rank < topk) & (j < n_sel), 1.0, 0.0)
    if nrank < nblk:
        sel_t = jnp.concatenate([sel_t, jnp.zeros((nblk - nrank, ATT_TQ), F32)], axis=0)
    sel = sel_t.T.astype(BF16)

    jj = lax.broadcasted_iota(jnp.int32, (nblk, ATT_TK), 0)
    kk = lax.broadcasted_iota(jnp.int32, (nblk, ATT_TK), 1) // SEL_BLOCK
    blocks_per_step = ATT_TK // SEL_BLOCK

    def attend(k_ref, v_ref, lo, hi, penalty_fn):
        def tile(kc):
            return pl.ds(pl.multiple_of(kc * ATT_TK, ATT_TK), ATT_TK)

        def sweep(fn, init):
            n = hi - lo

            def pair(p, carry):
                kc = lo + 2 * p
                return fn(kc + 1, fn(kc, carry))

            carry = lax.fori_loop(0, lax.shift_right_logical(n, 1), pair, init)
            return lax.cond((n & 1) == 1, lambda c: fn(hi - 1, c), lambda c: c, carry)

        def scores(kc, m_run):
            kblk = k_ref[0, 0, tile(kc), :].astype(BF16)
            sc = _nt_dot(qs, kblk) + btile_ref[0, jnp.minimum(i - kc, 2)] + penalty_fn(kc)
            s_scr[:, tile(kc)] = sc
            return jnp.maximum(m_run, sc)

        m_run = sweep(scores, jnp.full((rows, ATT_TK), NEG, F32))
        m = jnp.max(m_run, axis=-1, keepdims=True)

        def exps(kc, l_run):
            e = jnp.exp(s_scr[:, tile(kc)] - m)
            s_scr[:, tile(kc)] = e
            return l_run + e

        l_run = sweep(exps, jnp.zeros((rows, ATT_TK), F32))
        inv = 1.0 / jnp.maximum(jnp.sum(l_run, axis=-1, keepdims=True), 1e-30)

        def weighted(kc, acc):
            pr = (s_scr[:, tile(kc)] * inv).astype(BF16)
            return acc + jnp.dot(pr, v_ref[0, 0, tile(kc), :].astype(BF16), preferred_element_type=F32)

        return sweep(weighted, jnp.zeros((rows, HEAD_DIM), F32))

    def sel_penalty(kc):
        expand = jnp.where(jj == kc * blocks_per_step + kk, 1.0, 0.0).astype(BF16)
        chosen = jnp.dot(sel, expand, preferred_element_type=F32)
        pen = jnp.concatenate([(chosen - 1.0) * (-NEG)] * HPG, axis=0)
        return jnp.where(qpos >= kc * ATT_TK + col, pen, NEG)

    def win_penalty(kc):
        dist = qpos - (kc * ATT_TK + col)
        return jnp.where((dist >= 0) & (dist <= WINDOW), 0.0, NEG)

    o_sel = attend(ks_ref, vs_ref, 0, i + 1, sel_penalty)
    o_win = attend(kw_ref, vw_ref, jnp.maximum(i - WINDOW // ATT_TK, 0), i + 1, win_penalty)

    gate = jax.nn.sigmoid(gate_ref[0, 0])

    def gcol(br):
        return jnp.concatenate([gate[:, h * 3 + br:h * 3 + br + 1] for h in range(HPG)], axis=0)

    o = gcol(0) * o_cmp + gcol(1) * o_sel + gcol(2) * o_win
    o_ref[0] = jnp.concatenate([o[h * ATT_TQ:(h + 1) * ATT_TQ] for h in range(HPG)], axis=1)


def _bias_tables(rel_table, n_qtiles):
    tab = rel_table[_rel_bucket(jnp.arange(REL_MAX_DIST + 1))].astype(F32)
    tab = tab.T.reshape(KV_HEADS, HPG, REL_MAX_DIST + 1)

    def skew(v, rows, width, step):
        out = jnp.broadcast_to(v[..., None, :], v.shape[:-1] + (rows, width + step))
        out = out.reshape(v.shape[:-1] + (rows * (width + step),))[..., :rows * width]
        return out.reshape(v.shape[:-1] + (rows, width))

    width = 2 * ATT_TK
    w = np.arange(width + 1)
    s_minus_t = np.where(w <= ATT_TK, w, w - (width + 1))
    d_idx = np.stack([np.clip(delta * ATT_TK - s_minus_t, 0, REL_MAX_DIST) for delta in range(3)])
    btile = skew(tab[:, :, d_idx], ATT_TQ, width, 1)[..., :ATT_TK]
    btile = btile.transpose(0, 2, 1, 3, 4).reshape(KV_HEADS, 3, HPG * ATT_TQ, ATT_TK)

    n_q = n_qtiles * ATT_TQ
    u = np.arange(n_q + CMP_STRIDE)
    v_cmp = tab[:, :, np.clip(u - (CMP_BLOCK - 1), 0, REL_MAX_DIST)]
    bcmp = skew(v_cmp, ATT_TK, n_q, CMP_STRIDE)
    bcmp = bcmp.reshape(KV_HEADS, HPG, ATT_TK, n_qtiles, ATT_TQ).transpose(3, 0, 1, 4, 2)
    return btile, bcmp.reshape(n_qtiles, KV_HEADS, HPG * ATT_TQ, ATT_TK)


def nsa_prompt_attention(q, kcmp, vcmp, ks, vs, kw, vw, gl, rel_table):
    bsz, L, _ = q.shape
    assert L % ATT_TQ == 0 and kcmp.shape[2] == ATT_TK
    assert math.frexp(SCALE)[0] == 0.5, "the kernel folds SCALE into q, exact only for powers of two"
    nq = L // ATT_TQ
    n_cmp = (L - CMP_BLOCK) // CMP_STRIDE + 1
    n_sel = L // SEL_BLOCK
    assert n_sel <= ATT_TK
    btile, bcmp = _bias_tables(rel_table, nq)
    c = np.arange(ATT_TK)[None, :] * CMP_STRIDE
    sb = np.arange(ATT_TK)[:, None] * SEL_BLOCK
    ovl = ((c < sb + SEL_BLOCK) & (c + CMP_BLOCK > sb) & (np.arange(ATT_TK)[None, :] < n_cmp)
           & (np.arange(ATT_TK)[:, None] < n_sel))
    ovl = jnp.asarray(ovl, BF16)
    width = HPG * HEAD_DIM
    kvspec = lambda n: pl.BlockSpec((1, 1, n, HEAD_DIM), lambda b, g, i: (b, g, 0, 0))
    return pl.pallas_call(
        functools.partial(_nsa_attn_body, n_cmp=n_cmp, n_sel=n_sel, topk=min(SEL_TOPK, n_sel)),
        name="nsa_attention",
        grid=(bsz, KV_HEADS, nq),
        in_specs=[
            pl.BlockSpec((1, ATT_TQ, width), lambda b, g, i: (b, i, g)),
            kvspec(ATT_TK), kvspec(ATT_TK), kvspec(L), kvspec(L), kvspec(L), kvspec(L),
            pl.BlockSpec((1, 1, ATT_TQ, HPG * 3), lambda b, g, i: (b, g, i, 0)),
            pl.BlockSpec((1, 1, HPG * ATT_TQ, ATT_TK), lambda b, g, i: (i, g, 0, 0)),
            pl.BlockSpec((1, 3, HPG * ATT_TQ, ATT_TK), lambda b, g, i: (g, 0, 0, 0)),
            pl.BlockSpec((ATT_TK, ATT_TK), lambda b, g, i: (0, 0)),
        ],
        out_specs=pl.BlockSpec((1, ATT_TQ, width), lambda b, g, i: (b, i, g)),
        out_shape=jax.ShapeDtypeStruct((bsz, L, N_HEADS * HEAD_DIM), F32),
        scratch_shapes=[pltpu.VMEM((HPG * ATT_TQ, L), F32)],
        compiler_params=pltpu.CompilerParams(
            dimension_semantics=("parallel", "parallel", "arbitrary"),
            vmem_limit_bytes=VMEM_LIMIT_BYTES),
    )(q, kcmp, vcmp, ks, vs, kw, vw, gl, bcmp, btile, ovl)


def nsa_prompt_mixer(x, g, w_in, q_g, k_g, cmp_pe, cmp_w1, cmp_w2, w_out, rel_table):
    bsz, L, _ = x.shape
    t = bsz * L
    proj = norm_matmul(x.reshape(t, D_MODEL), g, w_in)
    c0 = N_HEADS * HEAD_DIM
    q = _rmsnorm(proj[:, :c0].reshape(t, N_HEADS, HEAD_DIM), q_g).reshape(bsz, L, c0)

    def kvpair(k, normed_g):
        kcols = proj[:, c0 + 2 * k * KV_DIM:c0 + (2 * k + 1) * KV_DIM]
        vcols = proj[:, c0 + (2 * k + 1) * KV_DIM:c0 + (2 * k + 2) * KV_DIM]
        if normed_g is not None:
            kcols = _rmsnorm(kcols.reshape(t, KV_HEADS, HEAD_DIM), normed_g).reshape(t, KV_DIM)
        rows = jnp.stack([kcols, vcols], axis=1).reshape(bsz, L, 2, KV_HEADS, HEAD_DIM)
        per_group = rows.transpose(2, 0, 3, 1, 4)
        return rows, per_group[0], per_group[1]

    new_cmp, kc_rows, vc_rows = kvpair(0, None)
    new_sel, ks, vs = kvpair(1, k_g[1])
    new_win, kw, vw = kvpair(2, k_g[2])
    gl = proj[:, c0 + 6 * KV_DIM:].reshape(bsz, L, KV_HEADS, HPG * 3).transpose(0, 2, 1, 3)

    nb = L // CMP_STRIDE

    def strips(r):
        r = r.reshape(bsz, KV_HEADS, nb, CMP_STRIDE * HEAD_DIM)
        return jnp.pad(r, ((0, 0), (0, 0), (0, ATT_TK + 8 - nb), (0, 0)))

    kcmp, vcmp = compress_rows(strips(kc_rows), strips(vc_rows), cmp_pe, cmp_w1, cmp_w2, k_g[0])
    o = nsa_prompt_attention(q, kcmp, vcmp, ks, vs, kw, vw, gl, rel_table)
    y = norm_matmul(o.reshape(t, c0), None, w_out, normalize=False, res=x.reshape(t, D_MODEL))
    return y.reshape(bsz, L, D_MODEL), new_cmp, new_sel, new_win[:, -min(WINDOW, L):]


DEC_PAGES_PER_STEP = 8
STRIP = CMP_STRIDE
ROW_LANES = 2 * KV_DIM
SEL_LANES = 256


def _group_rmsnorm(x, gain_row):
    lane = lax.broadcasted_iota(jnp.int32, x.shape, 1) // HEAD_DIM
    sq = x * x
    ms = jnp.zeros_like(x)
    for grp in range(KV_HEADS):
        tot = jnp.sum(jnp.where(lane == grp, sq, 0.0), axis=-1, keepdims=True)
        ms = jnp.where(lane == grp, tot, ms)
    return x * lax.rsqrt(ms / HEAD_DIM + EPS) * gain_row


def _decode_cmp_body(pt_ref, *refs, n_strips, n_sel, qblk, topk):
    pages = refs[:DEC_PAGES_PER_STEP]
    (bd_ref, peterm_ref, w2bd_ref, kg_ref, qbd_ref, bias_ref, ovl_ref, upper_ref,
     ocmp_ref, idx_ref, seq, hibuf) = refs[DEC_PAGES_PER_STEP:]
    s = pl.program_id(1)
    strips_per_page = pages[0].shape[1]
    for r in range(DEC_PAGES_PER_STEP):
        row0 = pl.multiple_of((s * DEC_PAGES_PER_STEP + r) * strips_per_page, strips_per_page)
        seq[pl.ds(row0, strips_per_page), :] = pages[r][0]

    @pl.when(s == pl.num_programs(1) - 1)
    def _():
        n_cmp = n_strips - 1
        summaries = []
        for kv in range(2):
            halves = []
            for half in range(2):
                acc = jnp.zeros((n_strips, KV_DIM), F32)
                for l in range(STRIP):
                    c0 = l * ROW_LANES + kv * KV_DIM
                    w_idx = (kv * 2 + half) * STRIP + l
                    xl = (seq[:, c0:c0 + KV_DIM] + peterm_ref[w_idx:w_idx + 1, :]).astype(BF16)
                    acc = acc + jnp.dot(xl, bd_ref[w_idx], preferred_element_type=F32)
                halves.append(acc)
            hibuf[0:n_strips, :] = halves[1]
            hibuf[n_strips:n_strips + 8, :] = jnp.zeros((8, KV_DIM), F32)
            h = halves[0] + hibuf[1:n_strips + 1, :]
            h = (h * jax.nn.sigmoid(h)).astype(BF16)
            o = jnp.dot(h, w2bd_ref[kv], preferred_element_type=F32)
            if kv == 0:
                o = _group_rmsnorm(o, kg_ref[...])
            summaries.append(o.astype(BF16))
        kcmp, vcmp = summaries

        qbd = qbd_ref[0].astype(BF16)
        sc = _nt_dot(qbd, kcmp) * SCALE + bias_ref[...]
        col = lax.broadcasted_iota(jnp.int32, sc.shape, 1)
        valid = col < n_cmp
        sc = jnp.where(valid, sc, NEG)
        e = jnp.where(valid, jnp.exp(sc - jnp.max(sc, axis=-1, keepdims=True)), 0.0)
        p = e / jnp.maximum(jnp.sum(e, axis=-1, keepdims=True), 1e-30)
        pb = p.astype(BF16)
        o_cmp = jnp.dot(pb, vcmp, preferred_element_type=F32)
        head_grp = lax.broadcasted_iota(jnp.int32, o_cmp.shape, 0) // HPG
        lane_grp = lax.broadcasted_iota(jnp.int32, o_cmp.shape, 1) // HEAD_DIM
        ocmp_ref[0] = jnp.where(head_grp == lane_grp, o_cmp, 0.0)

        psum = jnp.concatenate(
            [jnp.sum(p[grp * HPG:(grp + 1) * HPG], axis=0, keepdims=True) for grp in range(KV_HEADS)]
            + [jnp.zeros((8 - KV_HEADS, n_strips), F32)], axis=0)
        imp = jnp.dot(psum.astype(BF16), ovl_ref[...], preferred_element_type=F32)
        j = lax.broadcasted_iota(jnp.int32, imp.shape, 1)
        forced = (j == qblk) | (j == 0)
        score = jnp.where(forced, 1e9, jnp.where(j <= qblk, imp, NEG))
        score = jnp.where(j < n_sel, score, -3e38)
        rank = jnp.zeros(imp.shape, F32)
        for jp in range(n_sel):
            other = score[:, jp:jp + 1]
            beats = (other > score) | ((other == score) & (jp < j))
            rank = rank + jnp.where(beats, 1.0, 0.0)
        chosen = (rank < topk) & (j < n_sel)
        cum = jnp.dot(jnp.where(chosen, 1.0, 0.0).astype(BF16), upper_ref[...], preferred_element_type=F32)
        jf = j.astype(F32)
        out_lane = lax.broadcasted_iota(jnp.int32, (8, LANES), 1)
        out = jnp.zeros((8, LANES), F32)
        for k in range(topk):
            pick = jnp.sum(jnp.where(chosen & (cum == k + 1.0), jf, 0.0), axis=-1, keepdims=True)
            out = jnp.where(out_lane == k, pick, out)
        idx_ref[0] = out


def _decode_attn_body(tbl_ref, idx_ref, *refs, n_blk, qblk):
    blks = refs[:n_blk]
    (win_ref, qbd_ref, knew_ref, vnew_ref, wknew_ref, wvnew_ref, bsel_ref, bwin_ref, ocmp_ref, gate_ref,
     o_ref) = refs[n_blk:]
    b, g = pl.program_id(0), pl.program_id(1)
    qbd = qbd_ref[0, 0].astype(BF16)
    qf = qbd.astype(F32)
    own = lax.broadcasted_iota(jnp.int32, (HPG, KV_DIM), 1) // HEAD_DIM == g

    def attend(parts, k_new, v_new, bias_new):
        s_new = jnp.sum(qf * k_new.astype(BF16).astype(F32), axis=-1, keepdims=True) * SCALE + bias_new
        m = s_new
        for sc, ok, _ in parts:
            m = jnp.maximum(m, jnp.max(jnp.where(ok, sc, NEG), axis=-1, keepdims=True))
        e_new = jnp.exp(s_new - m)
        es = [jnp.where(ok, jnp.exp(sc - m), 0.0) for sc, ok, _ in parts]
        den = e_new
        for e in es:
            den = den + jnp.sum(e, axis=-1, keepdims=True)
        den = jnp.maximum(den, 1e-30)
        acc = (e_new / den).astype(BF16).astype(F32) * v_new.astype(BF16).astype(F32)
        for e, (_, _, v) in zip(es, parts):
            acc = acc + jnp.dot((e / den).astype(BF16), v, preferred_element_type=F32)
        return acc

    sel_parts = []
    for k in range(n_blk):
        blk = blks[k][0]
        bidx = idx_ref[b, g, k]
        sc = _nt_dot(qbd, blk[:, :KV_DIM].astype(BF16)) * SCALE + bsel_ref[jnp.minimum(bidx, qblk), 0]
        ok = jnp.broadcast_to(bidx < qblk, sc.shape)
        sel_parts.append((sc, ok, blk[:, KV_DIM:].astype(BF16)))
    bias0 = bsel_ref[qblk, 0][:, 0:1]
    o_sel = attend(sel_parts, knew_ref[0], vnew_ref[0], bias0)

    win = win_ref[0]
    sc = _nt_dot(qbd, win[:, :KV_DIM].astype(BF16)) * SCALE + bwin_ref[0]
    o_win = attend([(sc, jnp.full(sc.shape, True), win[:, KV_DIM:].astype(BF16))], wknew_ref[0], wvnew_ref[0], bias0)

    gate = jax.nn.sigmoid(gate_ref[0, 0])
    o = gate[:, 0:1] * ocmp_ref[0, 0] + gate[:, 1:2] * o_sel + gate[:, 2:3] * o_win
    o_ref[0, 0] = jnp.where(own, o, 0.0)


def nsa_decode_mixer(x, g, past_len, cache_cmp, cache_sel, win_past, page_table,
                     w_in, q_g, k_g, cmp_pe, cmp_w1, cmp_w2, w_out, rel_table):
    bsz = x.shape[0]
    n_pool, page = cache_cmp.shape[0], cache_cmp.shape[1]
    n_pages = page_table.shape[1]
    assert past_len == n_pages * page and page % STRIP == 0 and n_pages % DEC_PAGES_PER_STEP == 0
    assert win_past.shape[1] == WINDOW and past_len >= WINDOW and page % SEL_BLOCK == 0
    x2 = x.reshape(bsz, D_MODEL)
    proj = norm_matmul(x2, g, w_in)
    c0 = N_HEADS * HEAD_DIM
    q = _rmsnorm(proj[:, :c0].reshape(bsz, N_HEADS, HEAD_DIM), q_g)

    def rows(k, gain):
        kcols = proj[:, c0 + 2 * k * KV_DIM:c0 + (2 * k + 1) * KV_DIM]
        vcols = proj[:, c0 + (2 * k + 1) * KV_DIM:c0 + (2 * k + 2) * KV_DIM]
        if gain is not None:
            kcols = _rmsnorm(kcols.reshape(bsz, KV_HEADS, HEAD_DIM), gain).reshape(bsz, KV_DIM)
        return kcols, vcols

    kc_new, vc_new = rows(0, None)
    ks_new, vs_new = rows(1, k_g[1])
    kw_new, vw_new = rows(2, k_g[2])
    gl = proj[:, c0 + 6 * KV_DIM:].reshape(bsz, KV_HEADS, HPG, 3)
    as_row = lambda kk, vv: jnp.stack([kk, vv], axis=1).reshape(bsz, 1, 2, KV_HEADS, HEAD_DIM)
    new_cmp, new_sel, new_win = as_row(kc_new, vc_new), as_row(ks_new, vs_new), as_row(kw_new, vw_new)

    eye = jnp.eye(KV_HEADS, dtype=F32)
    qbd = jnp.einsum('bghd,gk->bghkd', q.reshape(bsz, KV_HEADS, HPG, HEAD_DIM), eye).reshape(bsz, N_HEADS, KV_DIM)
    w1r = cmp_w1.reshape(2, 2, STRIP, HEAD_DIM, HEAD_DIM)
    bd = jnp.einsum('khlio,gj->khlgijo', w1r, eye).reshape(2 * 2 * STRIP, KV_DIM, KV_DIM).astype(BF16)
    w2bd = jnp.einsum('kio,gj->kgijo', cmp_w2, eye).reshape(2, KV_DIM, KV_DIM).astype(BF16)
    peterm = jnp.tile(cmp_pe.reshape(2 * 2 * STRIP, HEAD_DIM), (1, KV_HEADS))
    kg_row = jnp.tile(k_g[0], KV_HEADS).reshape(1, KV_DIM)

    n_strips = past_len // STRIP
    n_cmp = n_strips - 1
    qblk = past_len // SEL_BLOCK
    n_sel = qblk + 1
    topk = min(SEL_TOPK, n_sel)
    assert n_sel <= SEL_LANES and n_strips % 8 == 0
    tab = rel_table[_rel_bucket(jnp.arange(REL_MAX_DIST + 1))].astype(F32).T
    ends = np.arange(n_strips) * CMP_STRIDE + CMP_BLOCK - 1
    bias_cmp = tab[:, np.clip(past_len - ends, 0, REL_MAX_DIST)]
    cs = np.arange(n_strips)[:, None] * CMP_STRIDE
    ss = np.arange(SEL_LANES)[None, :] * SEL_BLOCK
    ovl = (cs < ss + SEL_BLOCK) & (cs + CMP_BLOCK > ss) & (np.arange(n_strips)[:, None] < n_cmp) \
        & (np.arange(SEL_LANES)[None, :] < n_sel)
    ovl = jnp.asarray(ovl, BF16)
    upper = jnp.asarray(np.triu(np.ones((SEL_LANES, SEL_LANES), np.float32)), BF16)

    strips_per_page = page // STRIP
    cmp_view = cache_cmp.reshape(n_pool, strips_per_page, STRIP * ROW_LANES)
    steps = n_pages // DEC_PAGES_PER_STEP
    const = lambda a: pl.BlockSpec(a.shape, lambda b, s, pt: (0,) * a.ndim)
    page_spec = lambda r: pl.BlockSpec((1, strips_per_page, STRIP * ROW_LANES),
                                       lambda b, s, pt: (pt[b, s * DEC_PAGES_PER_STEP + r], 0, 0))
    o_cmp, idx = pl.pallas_call(
        functools.partial(_decode_cmp_body, n_strips=n_strips, n_sel=n_sel, qblk=qblk, topk=topk),
        name="nsa_decode_cmp",
        grid_spec=pltpu.PrefetchScalarGridSpec(
            num_scalar_prefetch=1,
            grid=(bsz, steps),
            in_specs=[page_spec(r) for r in range(DEC_PAGES_PER_STEP)]
            + [const(bd), const(peterm), const(w2bd), const(kg_row),
               pl.BlockSpec((1, N_HEADS, KV_DIM), lambda b, s, pt: (b, 0, 0)),
               const(bias_cmp), const(ovl), const(upper)],
            out_specs=[pl.BlockSpec((1, N_HEADS, KV_DIM), lambda b, s, pt: (b, 0, 0)),
                       pl.BlockSpec((1, 8, LANES), lambda b, s, pt: (b, 0, 0))],
            scratch_shapes=[pltpu.VMEM((n_strips, STRIP * ROW_LANES), F32),
                            pltpu.VMEM((n_strips + 8, KV_DIM), F32)],
        ),
        out_shape=[jax.ShapeDtypeStruct((bsz, N_HEADS, KV_DIM), F32),
                   jax.ShapeDtypeStruct((bsz, 8, LANES), F32)],
        compiler_params=pltpu.CompilerParams(
            dimension_semantics=("parallel", "arbitrary"),
            vmem_limit_bytes=DECODE_VMEM_LIMIT_BYTES),
    )(page_table, *([cmp_view] * DEC_PAGES_PER_STEP), bd, peterm, w2bd, kg_row, qbd, bias_cmp, ovl, upper)

    blk_idx = idx[:, :KV_HEADS, :topk].astype(jnp.int32)
    per_page = page // SEL_BLOCK
    safe = jnp.minimum(blk_idx, qblk - 1)
    page_of = jnp.take_along_axis(page_table, (safe // per_page).reshape(bsz, -1), axis=1).reshape(safe.shape)
    tbl = page_of * per_page + safe % per_page
    sel_view = cache_sel.reshape(n_pool, page, ROW_LANES)
    win_view = win_past.reshape(bsz, WINDOW, ROW_LANES)

    dist_sel = past_len - (np.arange(n_sel)[:, None] * SEL_BLOCK + np.arange(SEL_BLOCK)[None, :])
    bsel = tab[:, np.clip(dist_sel, 0, REL_MAX_DIST)]
    bsel = bsel.reshape(KV_HEADS, HPG, n_sel, SEL_BLOCK).transpose(2, 0, 1, 3)
    bwin = tab[:, np.clip(past_len - (past_len - WINDOW + np.arange(WINDOW)), 0, REL_MAX_DIST)]
    bwin = bwin.reshape(KV_HEADS, HPG, WINDOW)

    def spread(a):
        return a.reshape(bsz, 1, KV_DIM)

    qbd4 = qbd.reshape(bsz, KV_HEADS, HPG, KV_DIM)
    ocmp4 = o_cmp.reshape(bsz, KV_HEADS, HPG, KV_DIM)
    blk_spec = lambda k: pl.BlockSpec(
        (1, SEL_BLOCK, ROW_LANES), lambda b, g, t, i: (t[b, g, k] // per_page, t[b, g, k] % per_page, 0))
    row_spec = pl.BlockSpec((1, 1, KV_DIM), lambda b, g, t, i: (b, 0, 0))
    per_bg = lambda *shape: pl.BlockSpec((1, 1) + shape, lambda b, g, t, i: (b, g) + (0,) * len(shape))
    o_full = pl.pallas_call(
        functools.partial(_decode_attn_body, n_blk=topk, qblk=qblk),
        name="nsa_decode_attn",
        grid_spec=pltpu.PrefetchScalarGridSpec(
            num_scalar_prefetch=2,
            grid=(bsz, KV_HEADS),
            in_specs=[blk_spec(k) for k in range(topk)]
            + [pl.BlockSpec((1, WINDOW, ROW_LANES), lambda b, g, t, i: (b, 0, 0)),
               per_bg(HPG, KV_DIM), row_spec, row_spec, row_spec, row_spec,
               pl.BlockSpec((n_sel, 1, HPG, SEL_BLOCK), lambda b, g, t, i: (0, g, 0, 0)),
               pl.BlockSpec((1, HPG, WINDOW), lambda b, g, t, i: (g, 0, 0)),
               per_bg(HPG, KV_DIM), per_bg(HPG, 3)],
            out_specs=per_bg(HPG, KV_DIM),
        ),
        out_shape=jax.ShapeDtypeStruct((bsz, KV_HEADS, HPG, KV_DIM), F32),
        compiler_params=pltpu.CompilerParams(
            dimension_semantics=("parallel", "arbitrary"),
            vmem_limit_bytes=VMEM_LIMIT_BYTES),
    )(tbl, blk_idx, *([sel_view] * topk), win_view, qbd4, spread(ks_new), spread(vs_new), spread(kw_new),
      spread(vw_new), bsel, bwin, ocmp4, gl)
    o = o_full.reshape(bsz, KV_HEADS, HPG, KV_HEADS, HEAD_DIM).sum(axis=3).reshape(bsz, c0)
    y = norm_matmul(o, None, w_out, normalize=False, res=x2)
    new_win_buf = jnp.concatenate([win_past[:, 1:], new_win], axis=1)
    return y.reshape(x.shape), new_cmp, new_sel, new_win_buf


def _gla_chunked(q, k, v, logf, s0):
    bsz, L = q.shape[0], q.shape[1]
    nc = -(-L // H_CHUNK)
    pad = nc * H_CHUNK - L

    def prep(t):
        t = jnp.pad(t, ((0, 0), (0, pad), (0, 0), (0, 0)))
        return t.reshape(bsz, nc, H_CHUNK, H_HEADS, t.shape[-1])

    q, k, v, logf = prep(q), prep(k), prep(v), prep(logf)
    acum = jnp.cumsum(logf, axis=2)
    alast = acum[:, :, -1:]
    qe = q * jnp.exp(acum)
    ke = k * jnp.exp(-acum)
    kd = k * jnp.exp(alast - acum)
    tri = jnp.tril(jnp.ones((H_CHUNK, H_CHUNK), bool))
    att = jnp.where(tri, jnp.einsum('bcthk,bcshk->bchts', qe, ke), 0.0)
    o_intra = jnp.einsum('bchts,bcshv->bcthv', att, v)
    upd = jnp.einsum('bcshk,bcshv->bchkv', kd, v)

    def step(s, inp):
        dec, up = inp
        return dec[..., None] * s + up, s

    s_last, s_prev = lax.scan(step, s0, (jnp.exp(alast[:, :, 0]).swapaxes(0, 1), upd.swapaxes(0, 1)))
    o_inter = jnp.einsum('bcthk,bchkv->bcthv', qe, s_prev.swapaxes(0, 1))
    o = (o_intra + o_inter).reshape(bsz, nc * H_CHUNK, H_HEADS, H_DV)[:, :L]
    return o, s_last


HGRN_TOKENS = 128


def _hgrn_body(q_ref, f_ref, v_ref, gate_ref, lb_ref, ng_ref, lcum_ref, lsum_ref, o_ref, s_ref):
    @pl.when(pl.program_id(1) == 0)
    def _():
        s_ref[...] = jnp.zeros_like(s_ref)

    n = HGRN_TOKENS
    qr = q_ref[...]
    q = qr * jax.nn.sigmoid(qr)
    lb = lb_ref[...]
    forget = lb + (1.0 - lb) * jax.nn.sigmoid(f_ref[...])
    logf = jnp.log(forget)
    k = 1.0 - forget
    parts = _split3(logf)
    acum = sum(jnp.dot(lcum_ref[...], p, preferred_element_type=F32) for p in parts)
    atot = sum(jnp.dot(lsum_ref[...], p, preferred_element_type=F32) for p in parts)
    qe = (q * jnp.exp(acum)).astype(BF16)
    ke = (k * jnp.exp(-acum)).astype(BF16)
    kd = (k * jnp.exp(atot - acum)).astype(BF16)
    vb = v_ref[...].astype(BF16)
    ti = lax.broadcasted_iota(jnp.int32, (n, n), 0)
    si = lax.broadcasted_iota(jnp.int32, (n, n), 1)
    intra = (ti // H_CHUNK == si // H_CHUNK) & (ti >= si)
    gate = gate_ref[...]
    gate = gate * jax.nn.sigmoid(gate)
    outs = []
    for h in range(H_HEADS):
        cols = slice(h * H_DK, (h + 1) * H_DK)
        att = jnp.where(intra, _nt_dot(qe[:, cols], ke[:, cols]), 0.0)
        o_h = jnp.dot(att.astype(BF16), vb[:, cols], preferred_element_type=F32)
        decay_t = jnp.exp(atot[:, cols]).T
        state = s_ref[0, cols, :]
        inter = []
        for j in range(n // H_CHUNK):
            rows = slice(j * H_CHUNK, (j + 1) * H_CHUNK)
            inter.append(jnp.dot(qe[rows, cols], state.astype(BF16), preferred_element_type=F32))
            upd = lax.dot_general(kd[rows, cols], vb[rows, cols], (((0,), (0,)), ((), ())),
                                  preferred_element_type=F32)
            state = decay_t[:, j * H_CHUNK:j * H_CHUNK + 1] * state + upd
        s_ref[0, cols, :] = state
        o_h = o_h + jnp.concatenate(inter, axis=0)
        o_h = o_h * lax.rsqrt(jnp.mean(o_h * o_h, axis=-1, keepdims=True) + EPS) * ng_ref[...]
        outs.append(o_h * gate[:, cols])
    o_ref[...] = jnp.concatenate(outs, axis=1).astype(BF16)


def hgrn2_prompt_mixer(x, g, lb, w_in, norm_g, w_out):
    bsz, L, _ = x.shape
    n = HGRN_TOKENS
    assert L % n == 0 and n % H_CHUNK == 0 and H_DK == LANES and H_DV == LANES
    t = bsz * L
    x2 = x.reshape(t, D_MODEL)
    proj = norm_matmul(x2, g, w_in)
    nb = L // n
    r = np.arange(n)
    same = (r[:, None] // H_CHUNK) == (r[None, :] // H_CHUNK)
    lcum = jnp.asarray(same & (r[None, :] <= r[:, None]), BF16)
    lsum = jnp.asarray(same, BF16)
    col = lambda kk: pl.BlockSpec((n, D_MODEL), lambda b, c: (b * nb + c, kk))
    const = lambda a: pl.BlockSpec(a.shape, lambda b, c: (0,) * a.ndim)
    lb2 = lb.reshape(1, D_MODEL)
    ng2 = norm_g.reshape(1, H_DV)
    o, s = pl.pallas_call(
        _hgrn_body,
        name="hgrn_blocks",
        grid=(bsz, nb),
        in_specs=[col(0), col(1), col(2), col(3), const(lb2), const(ng2), const(lcum), const(lsum)],
        out_specs=[pl.BlockSpec((n, D_MODEL), lambda b, c: (b * nb + c, 0)),
                   pl.BlockSpec((1, H_HEADS * H_DK, H_DV), lambda b, c: (b, 0, 0))],
        out_shape=[jax.ShapeDtypeStruct((t, D_MODEL), BF16),
                   jax.ShapeDtypeStruct((bsz, H_HEADS * H_DK, H_DV), F32)],
        compiler_params=pltpu.CompilerParams(
            dimension_semantics=("parallel", "arbitrary"),
            vmem_limit_bytes=VMEM_LIMIT_BYTES),
    )(proj, proj, proj, proj, lb2, ng2, lcum, lsum)
    y = norm_matmul(o, None, w_out, normalize=False, res=x2)
    return y.reshape(bsz, L, D_MODEL), s.reshape(bsz, H_HEADS, H_DK, H_DV)


def hgrn2_mixer(x, g, s0, lb, w_in, norm_g, w_out):
    bsz, L, _ = x.shape
    proj = norm_matmul(x.reshape(bsz * L, D_MODEL), g, w_in).reshape(bsz, L, -1)
    q, fr, iv, gate = jnp.split(proj, 4, axis=-1)
    q = jax.nn.silu(q).reshape(bsz, L, H_HEADS, H_DK)
    forget = lb + (1.0 - lb) * jax.nn.sigmoid(fr)
    logf = jnp.log(forget).reshape(bsz, L, H_HEADS, H_DK)
    k = (1.0 - forget).reshape(bsz, L, H_HEADS, H_DK)
    v = iv.reshape(bsz, L, H_HEADS, H_DV)
    o, s = _gla_chunked(q, k, v, logf, s0)
    o = _rmsnorm(o, norm_g) * jax.nn.silu(gate.reshape(bsz, L, H_HEADS, H_DV))
    y = norm_matmul(o.reshape(bsz * L, D_MODEL), None, w_out, normalize=False)
    return y.reshape(bsz, L, D_MODEL), s


def _gather_pages(cache, page_table):
    pages = cache[page_table]
    b, n, p = pages.shape[0], pages.shape[1], pages.shape[2]
    return pages.reshape(b, n * p, *pages.shape[3:])


def kernel(x_prompt, x_sample, state_ssm, state_conv, cache_kv_cmp, cache_kv_sel, cache_kv_win, state_hgrn, page_table, norm_g, rel_table, m_w_in, m_conv_w, m_conv_b, m_dt_bias, m_a_log, m_d, m_norm_g, m_w_out, n_w_in, n_q_g, n_k_g, n_cmp_pe, n_cmp_w1, n_cmp_w2, n_w_out, h_w_in, h_lb, h_norm_g, h_w_out, moe_w_rg, moe_b_rg, moe_w_re, moe_b_re, moe_w1, moe_w3, moe_w2):
    bp, lp = x_prompt.shape[0], x_prompt.shape[1]
    past_len = page_table.shape[1] * cache_kv_cmp.shape[2]
    dt = x_prompt.dtype
    m_w_in, m_w_out, n_w_in, n_w_out, h_w_in, h_w_out, moe_w1, moe_w3, moe_w2 = (
        w.astype(BF16) for w in (m_w_in, m_w_out, n_w_in, n_w_out, h_w_in, h_w_out, moe_w1, moe_w3, moe_w2))
    lbs = jax.nn.softmax(h_lb.astype(F32), axis=0)
    lbs = jnp.cumsum(lbs, axis=0) - lbs[0]
    xp, xs = x_prompt, x_sample
    ssm_p, conv_p, cmp_p, sel_p, win_p, hg_p = [], [], [], [], [], []
    ssm_s, conv_s, cmp_s, sel_s, win_s, hg_s = [], [], [], [], [], []
    for i in range(DEPTH):
        kind, j = i % N_MIXERS, i // N_MIXERS
        g0 = norm_g[i, 0]
        if kind == 0:
            w = (m_w_in[j], m_conv_w[j], m_conv_b[j], m_dt_bias[j], m_a_log[j], m_d[j], m_norm_g[j], m_w_out[j])
            xp, cbuf, hh = mamba_prompt_mixer(xp, g0, *w)
            ssm_p.append(hh)
            conv_p.append(cbuf)
            ys, cbuf, hh = mamba_mixer(xs, g0, state_conv[j], state_ssm[j], *w)
            xs = xs + ys
            ssm_s.append(hh)
            conv_s.append(cbuf)
        elif kind == 1:
            w = (n_w_in[j], n_q_g[j], n_k_g[j], n_cmp_pe[j], n_cmp_w1[j], n_cmp_w2[j], n_w_out[j], rel_table)
            xp, rc, rs, wb = nsa_prompt_mixer(xp, g0, *w)
            cmp_p.append(rc)
            sel_p.append(rs)
            win_p.append(wb)
            xs, rc, rs, wb = nsa_decode_mixer(xs, g0, past_len, cache_kv_cmp[j], cache_kv_sel[j],
                                              cache_kv_win[j], page_table, *w)
            cmp_s.append(rc)
            sel_s.append(rs)
            win_s.append(wb)
        else:
            w = (lbs[i], h_w_in[j], h_norm_g[j], h_w_out[j])
            xp, st = hgrn2_prompt_mixer(xp, g0, *w)
            hg_p.append(st)
            ys, st = hgrn2_mixer(xs, g0, state_hgrn[j], *w)
            xs = xs + ys
            hg_s.append(st)
        mw = (moe_w_rg[i], moe_b_rg[i], moe_w_re[i], moe_b_re[i], moe_w1[i], moe_w3[i], moe_w2[i])
        xp = hier_moe_residual(xp.reshape(-1, D_MODEL), norm_g[i, 1], *mw).reshape(xp.shape)
        xs = hier_moe_residual(xs.reshape(-1, D_MODEL), norm_g[i, 1], *mw).reshape(xs.shape)
    return (xp, xs,
            jnp.stack(ssm_p), jnp.stack(conv_p), jnp.stack(cmp_p), jnp.stack(sel_p), jnp.stack(win_p), jnp.stack(hg_p),
            jnp.stack(ssm_s), jnp.stack(conv_s), jnp.stack(cmp_s), jnp.stack(sel_s), jnp.stack(win_s), jnp.stack(hg_s))
```

```python
import functools
import math

import jax
import jax.numpy as jnp
import numpy as np
from jax import lax
from jax.experimental import pallas as pl
from jax.experimental.pallas import tpu as pltpu

F32 = jnp.float32
BF16 = jnp.bfloat16
EPS = 1e-6

D_MODEL = 1024
DEPTH = 4
N_MIXERS = 3

M_DINNER = 2 * D_MODEL
M_HEADDIM = 64
M_HEADS = M_DINNER // M_HEADDIM
M_GROUPS = 4
M_DSTATE = 128
M_CONV = 4
M_GN = M_GROUPS * M_DSTATE
M_CONV_DIM = M_DINNER + 2 * M_GN
M_CHUNK = 128

N_HEADS = 16
HEAD_DIM = D_MODEL // N_HEADS
KV_HEADS = 4
HPG = N_HEADS // KV_HEADS
KV_DIM = KV_HEADS * HEAD_DIM
CMP_BLOCK = 32
CMP_STRIDE = 16
SEL_BLOCK = 64
SEL_TOPK = 16
WINDOW = 512
SEL_QBLK = 32
WIN_QBLK = 128
SCALE = HEAD_DIM ** -0.5
REL_BUCKETS = 32
REL_MAX_DIST = 128

H_DK = 128
H_HEADS = D_MODEL // H_DK
H_DV = D_MODEL // H_HEADS
H_CHUNK = 32

MOE_GROUPS = 4
MOE_EPG = 4
MOE_EXPERTS = MOE_GROUPS * MOE_EPG
MOE_TOPK = 2
MOE_FF = 512

VMEM_LIMIT_BYTES = 48 * 1024 * 1024
DECODE_VMEM_LIMIT_BYTES = 56 * 1024 * 1024


def _rmsnorm(x, g):
    xf = x.astype(F32)
    y = xf * lax.rsqrt(jnp.mean(xf * xf, axis=-1, keepdims=True) + EPS)
    return (y * g.astype(F32)).astype(x.dtype)


LANES = 128
MXU_WIDTH = 256


def _norm_matmul_body(*refs, normalize, has_res):
    if has_res:
        x_ref, g_ref, w_ref, res_ref, o_ref, xb_ref = refs
    else:
        x_ref, g_ref, w_ref, o_ref, xb_ref = refs

    @pl.when(pl.program_id(1) == 0)
    def _():
        x = x_ref[...].astype(F32)
        if normalize:
            x = x * lax.rsqrt(jnp.mean(x * x, axis=-1, keepdims=True) + EPS) * g_ref[...]
        xb_ref[...] = x.astype(BF16)

    acc = jnp.dot(xb_ref[...], w_ref[...].astype(BF16), preferred_element_type=F32)
    if has_res:
        acc = acc + res_ref[...]
    o_ref[...] = acc


def _pick_tile(n, pref):
    t = min(n, pref)
    while n % t:
        t //= 2
    return t


def norm_matmul(x, g, w, *, normalize=True, res=None, tm=None, tn=512):
    t, k = x.shape
    n_true = w.shape[1]
    tm = _pick_tile(t, tm or (1024 if k <= 1024 else 512))
    col_tile = MXU_WIDTH if n_true > MXU_WIDTH else LANES
    if n_true % col_tile:
        assert res is None
        w = jnp.pad(w, ((0, 0), (0, col_tile - n_true % col_tile)))
    n = w.shape[1]
    tn = _pick_tile(n, tn)
    if g is None:
        g = jnp.ones((k,), F32)
    in_specs = [
        pl.BlockSpec((tm, k), lambda i, j: (i, 0)),
        pl.BlockSpec((1, k), lambda i, j: (0, 0)),
        pl.BlockSpec((k, tn), lambda i, j: (0, j)),
    ]
    args = [x, g.reshape(1, k), w]
    if res is not None:
        in_specs.append(pl.BlockSpec((tm, tn), lambda i, j: (i, j)))
        args.append(res)
    out = pl.pallas_call(
        functools.partial(_norm_matmul_body, normalize=normalize, has_res=res is not None),
        name="norm_matmul",
        grid=(t // tm, n // tn),
        in_specs=in_specs,
        out_specs=pl.BlockSpec((tm, tn), lambda i, j: (i, j)),
        out_shape=jax.ShapeDtypeStruct((t, n), F32),
        scratch_shapes=[pltpu.VMEM((tm, k), BF16)],
        compiler_params=pltpu.CompilerParams(
            dimension_semantics=("parallel", "arbitrary"),
            vmem_limit_bytes=VMEM_LIMIT_BYTES),
    )(*args)
    return out if n == n_true else out[:, :n_true]


ROUTE_LANES = 128
MOE_TILE = 1024
MOE_ROWS = 160
NEG = -1e30


def _router_body(x_ref, g_ref, w_ref, b_ref, u_ref, xn_ref, rank_ref, wt_ref):
    x = x_ref[...]
    tm = x.shape[0]
    xn = x * lax.rsqrt(jnp.mean(x * x, axis=-1, keepdims=True) + EPS) * g_ref[...]
    xb = xn.astype(BF16)
    xn_ref[...] = xb
    logits = jnp.dot(xb, w_ref[...].astype(BF16), preferred_element_type=F32) + b_ref[...]
    lane = lax.broadcasted_iota(jnp.int32, (tm, ROUTE_LANES), 1).astype(F32)

    def first_max(mask):
        v = jnp.max(jnp.where(mask, logits, NEG), axis=-1, keepdims=True)
        i = jnp.min(jnp.where(mask & (logits == v), lane, float(ROUTE_LANES)), axis=-1, keepdims=True)
        return v, i

    is_group = lane < MOE_GROUPS
    mg, g_idx = first_max(is_group)
    pg_top = 1.0 / jnp.sum(jnp.where(is_group, jnp.exp(logits - mg), 0.0), axis=-1, keepdims=True)
    lo = MOE_GROUPS + MOE_EPG * g_idx
    in_group = (lane >= lo) & (lane < lo + MOE_EPG)
    v1, i1 = first_max(in_group)
    v2, i2 = first_max(in_group & (lane != i1))
    e2 = jnp.exp(v2 - v1)
    w_a = pg_top / (1.0 + e2)
    w_b = pg_top * e2 / (1.0 + e2)
    info = jnp.where(lane == 0, i1 - MOE_GROUPS, jnp.where(lane == 1, i2 - MOE_GROUPS,
                     jnp.where(lane == 2, w_a, jnp.where(lane == 3, w_b, 0.0))))
    info_t = info.T
    e_a, e_b, w_at, w_bt = info_t[0:1], info_t[1:2], info_t[2:3], info_t[3:4]
    expert = lax.broadcasted_iota(jnp.int32, (MOE_EXPERTS, tm), 0).astype(F32)
    m_a = e_a == expert
    m_b = e_b == expert
    onehot = jnp.concatenate([jnp.where(m_a, 1.0, 0.0), jnp.where(m_b, 1.0, 0.0)], axis=0).astype(BF16)
    cum = jnp.dot(onehot, u_ref[...], preferred_element_type=F32)
    cum_a, cum_b = cum[:MOE_EXPERTS], cum[MOE_EXPERTS:]
    n_a = cum_a[:, tm - 1:tm]
    rank_ref[...] = jnp.where(m_a, cum_a - 1.0, jnp.where(m_b, n_a + cum_b - 1.0, -1.0))
    wt_ref[...] = jnp.where(m_a, w_at, jnp.where(m_b, w_bt, 0.0))


def moe_route(x, g, w_rg, b_rg, w_re, b_re, tm):
    t = x.shape[0]
    pad = ROUTE_LANES - MOE_GROUPS - MOE_EXPERTS
    w = jnp.pad(jnp.concatenate([w_rg, w_re], axis=1), ((0, 0), (0, pad)))
    b = jnp.pad(jnp.concatenate([b_rg, b_re]), (0, pad)).reshape(1, ROUTE_LANES)
    upper = jnp.asarray(np.triu(np.ones((tm, tm), np.float32)), BF16)
    const = lambda a: pl.BlockSpec(a.shape, lambda i: (0,) * a.ndim)
    g2 = g.reshape(1, D_MODEL)
    return pl.pallas_call(
        _router_body,
        name="moe_router",
        grid=(t // tm,),
        in_specs=[pl.BlockSpec((tm, D_MODEL), lambda i: (i, 0)), const(g2), const(w), const(b), const(upper)],
        out_specs=[pl.BlockSpec((tm, D_MODEL), lambda i: (i, 0)),
                   pl.BlockSpec((MOE_EXPERTS, tm), lambda i: (0, i)),
                   pl.BlockSpec((MOE_EXPERTS, tm), lambda i: (0, i))],
        out_shape=[jax.ShapeDtypeStruct((t, D_MODEL), BF16),
                   jax.ShapeDtypeStruct((MOE_EXPERTS, t), F32),
                   jax.ShapeDtypeStruct((MOE_EXPERTS, t), F32)],
        compiler_params=pltpu.CompilerParams(
            dimension_semantics=("parallel",), vmem_limit_bytes=VMEM_LIMIT_BYTES),
    )(x, g2, w, b, upper)


def _moe_expert_body(cnt_ref, xn_ref, rank_ref, wt_ref, w1_ref, w3_ref, w2_ref, res_ref, o_ref, *, rows):
    ti, e = pl.program_id(0), pl.program_id(1)

    @pl.when(e == 0)
    def _():
        o_ref[...] = res_ref[...]

    tm = xn_ref.shape[0]
    rank = rank_ref[pl.ds(e, 1), :]
    wt = wt_ref[pl.ds(e, 1), :]
    n_chunks = (cnt_ref[e, ti] + rows - 1) // rows

    def chunk(c, carry):
        r = (lax.broadcasted_iota(jnp.int32, (rows, tm), 0) + c * rows).astype(F32)
        sel = jnp.where(rank == r, 1.0, 0.0)
        selb = sel.astype(BF16)
        xs = jnp.dot(selb, xn_ref[...], preferred_element_type=F32).astype(BF16)
        a = jnp.dot(xs, w1_ref[0].astype(BF16), preferred_element_type=F32)
        b = jnp.dot(xs, w3_ref[0].astype(BF16), preferred_element_type=F32)
        h = (a * jax.nn.sigmoid(a) * b * jnp.sum(sel * wt, axis=1, keepdims=True)).astype(BF16)
        y = jnp.dot(h, w2_ref[0].astype(BF16), preferred_element_type=F32)
        y_hi = y.astype(BF16)
        y_lo = (y - y_hi.astype(F32)).astype(BF16)
        tn = (((0,), (0,)), ((), ()))
        o_ref[...] += (lax.dot_general(selb, y_hi, tn, preferred_element_type=F32)
                       + lax.dot_general(selb, y_lo, tn, preferred_element_type=F32))
        return carry

    lax.fori_loop(0, n_chunks, chunk, 0)


def moe_experts(xn, rank, wt, counts, w1, w3, w2, res, tm):
    t = xn.shape[0]
    rows = min(MOE_ROWS, tm)
    grid_spec = pltpu.PrefetchScalarGridSpec(
        num_scalar_prefetch=1,
        grid=(t // tm, MOE_EXPERTS),
        in_specs=[
            pl.BlockSpec((tm, D_MODEL), lambda i, e, c: (i, 0)),
            pl.BlockSpec((MOE_EXPERTS, tm), lambda i, e, c: (0, i)),
            pl.BlockSpec((MOE_EXPERTS, tm), lambda i, e, c: (0, i)),
            pl.BlockSpec((1, D_MODEL, MOE_FF), lambda i, e, c: (e, 0, 0)),
            pl.BlockSpec((1, D_MODEL, MOE_FF), lambda i, e, c: (e, 0, 0)),
            pl.BlockSpec((1, MOE_FF, D_MODEL), lambda i, e, c: (e, 0, 0)),
            pl.BlockSpec((tm, D_MODEL), lambda i, e, c: (i, 0)),
        ],
        out_specs=pl.BlockSpec((tm, D_MODEL), lambda i, e, c: (i, 0)),
    )
    return pl.pallas_call(
        functools.partial(_moe_expert_body, rows=rows),
        name="moe_experts",
        grid_spec=grid_spec,
        out_shape=jax.ShapeDtypeStruct((t, D_MODEL), F32),
        compiler_params=pltpu.CompilerParams(
            dimension_semantics=("parallel", "arbitrary"),
            vmem_limit_bytes=VMEM_LIMIT_BYTES),
    )(counts, xn, rank, wt, w1, w3, w2, res)


def hier_moe_residual(x, g, w_rg, b_rg, w_re, b_re, w1, w3, w2):
    t_true = x.shape[0]
    tm = MOE_TILE if t_true % MOE_TILE == 0 else ROUTE_LANES
    if t_true % tm:
        x = jnp.pad(x, ((0, tm - t_true % tm), (0, 0)))
    t = x.shape[0]
    xn, rank, wt = moe_route(x, g, w_rg, b_rg, w_re, b_re, tm)
    counts = jnp.sum((rank >= 0).reshape(MOE_EXPERTS, t // tm, tm), axis=-1, dtype=jnp.int32)
    return moe_experts(xn, rank, wt, counts, w1, w3, w2, x, tm)[:t_true]


def _causal_dwconv(u, buf, w, b):
    L = u.shape[1]
    ext = jnp.concatenate([buf.astype(u.dtype), u], axis=1)
    out = b + sum(ext[:, k:k + L] * w[k] for k in range(M_CONV))
    return out, ext[:, L:]


def _ssd_scan(x, dt, a, bm, cm, h0):
    bsz, L = x.shape[0], x.shape[1]
    q = M_CHUNK if L % M_CHUNK == 0 else L
    nc = L // q
    hpg = M_HEADS // M_GROUPS
    xdt = (x * dt[..., None]).reshape(bsz, nc, q, M_GROUPS, hpg, M_HEADDIM)
    acum = jnp.cumsum((dt * a).reshape(bsz, nc, q, M_GROUPS, hpg), axis=2)
    bc = bm.reshape(bsz, nc, q, M_GROUPS, M_DSTATE)
    cc = cm.reshape(bsz, nc, q, M_GROUPS, M_DSTATE)
    tri = jnp.tril(jnp.ones((q, q), bool))[:, :, None, None]
    seg = acum[:, :, :, None] - acum[:, :, None]
    decay = jnp.exp(jnp.where(tri, seg, -jnp.inf))
    cb = jnp.einsum('bclgn,bcsgn->bclsg', cc, bc)
    y_diag = jnp.einsum('bclsg,bclsgh,bcsghp->bclghp', cb, decay, xdt)
    decay_end = jnp.exp(acum[:, :, -1:] - acum)
    states = jnp.einsum('bcsgn,bcsgh,bcsghp->bcghpn', bc, decay_end, xdt)
    chunk_decay = jnp.exp(acum[:, :, -1])

    def step(h, inp):
        cd, st = inp
        return cd[..., None, None] * h + st, h

    h_last, h_prev = lax.scan(step, h0.reshape(bsz, M_GROUPS, hpg, M_HEADDIM, M_DSTATE),
                              (chunk_decay.swapaxes(0, 1), states.swapaxes(0, 1)))
    y_off = jnp.einsum('bclgn,bclgh,bcghpn->bclghp', cc, jnp.exp(acum), h_prev.swapaxes(0, 1))
    y = (y_diag + y_off).reshape(bsz, L, M_HEADS, M_HEADDIM)
    return y, h_last.reshape(bsz, M_HEADS, M_HEADDIM, M_DSTATE)


def mamba_mixer(x, g, conv_buf, h0, w_in, conv_w, conv_b, dt_bias, a_log, d_skip, norm_g, w_out):
    bsz, L, _ = x.shape
    proj = norm_matmul(x.reshape(bsz * L, D_MODEL), g, w_in).reshape(bsz, L, -1)
    z = proj[..., :M_DINNER]
    xbc = proj[..., M_DINNER:M_DINNER + M_CONV_DIM]
    dt_raw = proj[..., M_DINNER + M_CONV_DIM:]
    xbc, new_buf = _causal_dwconv(xbc, conv_buf, conv_w, conv_b)
    xbc = jax.nn.silu(xbc)
    xs = xbc[..., :M_DINNER].reshape(bsz, L, M_HEADS, M_HEADDIM)
    bm = xbc[..., M_DINNER:M_DINNER + M_GN].reshape(bsz, L, M_GROUPS, M_DSTATE)
    cm = xbc[..., M_DINNER + M_GN:].reshape(bsz, L, M_GROUPS, M_DSTATE)
    dt = jax.nn.softplus(dt_raw + dt_bias)
    a = -jnp.exp(a_log)
    y, h = _ssd_scan(xs, dt, a, bm, cm, h0)
    y = y + xs * d_skip[:, None]
    y = y.reshape(bsz, L, M_DINNER)
    y = _rmsnorm(y * jax.nn.silu(z), norm_g)
    out = norm_matmul(y.reshape(bsz * L, M_DINNER), None, w_out, normalize=False)
    return out.reshape(bsz, L, D_MODEL), new_buf, h


SSD_HPG = M_HEADS // M_GROUPS
SSD_GROUP_ROWS = SSD_HPG * M_HEADDIM
CONV_PAD = 8


def _transpose_cols(x):
    return jnp.concatenate([x[:, j * LANES:(j + 1) * LANES].T for j in range(x.shape[1] // LANES)], axis=0)


def _transpose_rows(x):
    return jnp.concatenate([x[j * LANES:(j + 1) * LANES, :].T for j in range(x.shape[0] // LANES)], axis=1)


def _ssd_body(z_ref, xbc_ref, dtr_ref, cw_ref, cb_ref, dtb_ref, a_ref, dcol_ref, ng_ref, ltri_ref,
              y_ref, conv_ref, h_ref, xbuf, *, q):
    c = pl.program_id(1)

    @pl.when(c == 0)
    def _():
        h_ref[...] = jnp.zeros_like(h_ref)
        xbuf[0:CONV_PAD, :] = jnp.zeros((CONV_PAD, M_CONV_DIM), F32)

    xbuf[CONV_PAD:CONV_PAD + q, :] = xbc_ref[...]
    conv = cb_ref[...]
    for k in range(M_CONV):
        start = CONV_PAD - (M_CONV - 1) + k
        conv = conv + xbuf[start:start + q, :] * cw_ref[k:k + 1, :]
    tail = xbuf[CONV_PAD + q - (M_CONV - 1):CONV_PAD + q, :]
    xbuf[CONV_PAD - (M_CONV - 1):CONV_PAD, :] = tail
    conv_ref[0] = tail
    xc = conv * jax.nn.sigmoid(conv)
    xs = xc[:, :M_DINNER]
    xs_t = _transpose_cols(xs)

    pre = dtr_ref[...] + dtb_ref[...]
    dt = jnp.maximum(pre, 0.0) + jnp.log1p(jnp.exp(-jnp.abs(pre)))
    da = dt * a_ref[...]
    ltri = ltri_ref[...]
    acum = sum(jnp.dot(ltri, part, preferred_element_type=F32) for part in _split3(da))
    dt_t = dt.T
    acum_t = acum.T
    li = lax.broadcasted_iota(jnp.int32, (q, q), 0)
    si = lax.broadcasted_iota(jnp.int32, (q, q), 1)
    causal = li >= si

    y_t = []
    for g in range(M_GROUPS):
        bm = xc[:, M_DINNER + g * M_DSTATE:M_DINNER + (g + 1) * M_DSTATE].astype(BF16)
        cm = xc[:, M_DINNER + M_GN + g * M_DSTATE:M_DINNER + M_GN + (g + 1) * M_DSTATE].astype(BF16)
        cb = _nt_dot(cm, bm)
        r0 = g * SSD_GROUP_ROWS
        h_prev = h_ref[0, r0:r0 + SSD_GROUP_ROWS, :]
        y_off = _nt_dot(h_prev.astype(BF16), cm)
        x_dec, scale = [], []
        for hh in range(g * SSD_HPG, (g + 1) * SSD_HPG):
            a_row = acum_t[hh:hh + 1, :]
            a_col = acum[:, hh:hh + 1]
            decay = jnp.where(causal, jnp.exp(a_col - a_row), 0.0)
            m = (cb * decay).astype(BF16)
            rows = slice(hh * M_HEADDIM, (hh + 1) * M_HEADDIM)
            xs_h = xs_t[rows]
            xdt = xs_h * dt_t[hh:hh + 1, :]
            y_h = _nt_dot(xdt.astype(BF16), m)
            y_h = y_h + y_off[rows.start - r0:rows.stop - r0] * jnp.exp(a_row) + xs_h * dcol_ref[hh:hh + 1, :]
            y_t.append(y_h)
            a_last = a_row[:, q - 1:q]
            x_dec.append(xdt * jnp.exp(a_last - a_row))
            scale.append(jnp.broadcast_to(jnp.exp(a_last), (M_HEADDIM, 1)))
        upd = jnp.dot(jnp.concatenate(x_dec, axis=0).astype(BF16), bm, preferred_element_type=F32)
        h_ref[0, r0:r0 + SSD_GROUP_ROWS, :] = jnp.concatenate(scale, axis=0) * h_prev + upd

    y = _transpose_rows(jnp.concatenate(y_t, axis=0))
    zz = z_ref[...]
    yg = y * (zz * jax.nn.sigmoid(zz))
    yg = yg * lax.rsqrt(jnp.mean(yg * yg, axis=-1, keepdims=True) + EPS) * ng_ref[...]
    y_ref[...] = yg.astype(BF16)


def ssd_prompt(z, xbc, dtr, conv_w, conv_b, dt_bias, a_log, d_skip, norm_g, bsz, L):
    q = M_CHUNK
    assert L % q == 0 and q == LANES
    nc = L // q
    padl = lambda v: jnp.pad(v, (0, LANES - v.shape[0]))
    dtb = padl(dt_bias).reshape(1, LANES)
    a_row = padl(-jnp.exp(a_log)).reshape(1, LANES)
    dcol = jnp.broadcast_to(padl(d_skip).reshape(LANES, 1), (LANES, LANES))
    ltri = jnp.asarray(np.tril(np.ones((q, q), np.float32)), BF16)
    const = lambda a: pl.BlockSpec(a.shape, lambda b, c: (0,) * a.ndim)
    tok = lambda w: pl.BlockSpec((q, w), lambda b, c: (b * nc + c, 0))
    cb2 = conv_b.reshape(1, M_CONV_DIM)
    ng2 = norm_g.reshape(1, M_DINNER)
    y, conv_tail, h = pl.pallas_call(
        functools.partial(_ssd_body, q=q),
        name="ssd_chunks",
        grid=(bsz, nc),
        in_specs=[tok(M_DINNER), tok(M_CONV_DIM), tok(LANES), const(conv_w), const(cb2), const(dtb),
                  const(a_row), const(dcol), const(ng2), const(ltri)],
        out_specs=[tok(M_DINNER),
                   pl.BlockSpec((1, M_CONV - 1, M_CONV_DIM), lambda b, c: (b, 0, 0)),
                   pl.BlockSpec((1, M_DINNER, M_DSTATE), lambda b, c: (b, 0, 0))],
        out_shape=[jax.ShapeDtypeStruct((bsz * L, M_DINNER), BF16),
                   jax.ShapeDtypeStruct((bsz, M_CONV - 1, M_CONV_DIM), F32),
                   jax.ShapeDtypeStruct((bsz, M_DINNER, M_DSTATE), F32)],
        scratch_shapes=[pltpu.VMEM((CONV_PAD + q, M_CONV_DIM), F32)],
        compiler_params=pltpu.CompilerParams(
            dimension_semantics=("parallel", "arbitrary"),
            vmem_limit_bytes=VMEM_LIMIT_BYTES),
    )(z, xbc, dtr, conv_w, cb2, dtb, a_row, dcol, ng2, ltri)
    return y, conv_tail, h.reshape(bsz, M_HEADS, M_HEADDIM, M_DSTATE)


def mamba_prompt_mixer(x, g, w_in, conv_w, conv_b, dt_bias, a_log, d_skip, norm_g, w_out):
    bsz, L, _ = x.shape
    x2 = x.reshape(bsz * L, D_MODEL)
    z = norm_matmul(x2, g, w_in[:, :M_DINNER])
    xbc = norm_matmul(x2, g, w_in[:, M_DINNER:M_DINNER + M_CONV_DIM])
    dtr = norm_matmul(x2, g, jnp.pad(w_in[:, M_DINNER + M_CONV_DIM:], ((0, 0), (0, LANES - M_HEADS))))
    y, conv_tail, h = ssd_prompt(z, xbc, dtr, conv_w, conv_b, dt_bias, a_log, d_skip, norm_g, bsz, L)
    out = norm_matmul(y, None, w_out, normalize=False, res=x2)
    return out.reshape(bsz, L, D_MODEL), conv_tail, h


def _rel_bucket(dist):
    exact = REL_BUCKETS // 2
    d = jnp.maximum(dist, 0)
    ratio = jnp.log(jnp.maximum(d, 1).astype(F32) / exact) / math.log(REL_MAX_DIST / exact)
    large = jnp.minimum(exact + (ratio * (REL_BUCKETS - exact)).astype(jnp.int32), REL_BUCKETS - 1)
    return jnp.where(d < exact, d, large)


def _head_bias(rel_table, dist):
    b = rel_table[_rel_bucket(dist)].astype(F32)
    return jnp.moveaxis(b, -1, 0).reshape(KV_HEADS, HPG, dist.shape[0], dist.shape[1])


def _masked_softmax(s, valid):
    s = jnp.where(valid, s, -1e30)
    e = jnp.where(valid, jnp.exp(s - jnp.max(s, axis=-1, keepdims=True)), 0.0)
    return e / jnp.maximum(jnp.sum(e, axis=-1, keepdims=True), 1e-30)


def _group_attend(qg, qpos, k, v, kpos, valid, rel_table):
    s = jnp.einsum('bqghd,bkgd->bghqk', qg, k).astype(F32) * SCALE
    s = s + _head_bias(rel_table, qpos[:, None] - kpos[None, :])
    p = _masked_softmax(s, valid)
    o = jnp.einsum('bghqk,bkgd->bqghd', p.astype(v.dtype), v)
    return o, p


def _compress(rows, pe, w1, w2):
    bsz, T = rows.shape[0], rows.shape[1]
    nc = (T - CMP_BLOCK) // CMP_STRIDE + 1
    idx = jnp.arange(nc)[:, None] * CMP_STRIDE + jnp.arange(CMP_BLOCK)[None, :]
    blk = rows[:, idx] + pe[:, None, :]
    blk = jnp.moveaxis(blk, 2, 3).reshape(bsz, nc, KV_HEADS, CMP_BLOCK * HEAD_DIM)
    return jax.nn.silu(blk @ w1) @ w2


def _select_blocks(p_cmp, qpos, n_sel):
    nc = p_cmp.shape[-1]
    cstart = jnp.arange(nc) * CMP_STRIDE
    sstart = jnp.arange(n_sel) * SEL_BLOCK
    overlap = ((cstart[:, None] < sstart[None, :] + SEL_BLOCK) &
               (cstart[:, None] + CMP_BLOCK > sstart[None, :])).astype(F32)
    imp = jnp.einsum('bghqc,cs->bgqs', p_cmp, overlap)
    qblk = qpos // SEL_BLOCK
    j = jnp.arange(n_sel)
    forced = (j[None, :] == qblk[:, None]) | (j[None, :] == 0)
    score = jnp.where(forced, 1e9, jnp.where(j[None, :] <= qblk[:, None], imp, -1e30))
    _, idx = lax.top_k(score, min(SEL_TOPK, n_sel))
    return idx


def _sel_attend(qg, qpos, idx, kblk, vblk, rel_table):
    bsz, lq = qg.shape[0], qg.shape[1]
    bi = jnp.arange(bsz)[:, None, None, None]
    gi = jnp.arange(KV_HEADS)[None, :, None, None]
    kg = kblk[bi, gi, idx]
    vg = vblk[bi, gi, idx]
    s = jnp.einsum('bqghd,bgqkrd->bghqkr', qg, kg).astype(F32) * SCALE
    kpos = idx[..., None] * SEL_BLOCK + jnp.arange(SEL_BLOCK)
    dist = qpos[:, None, None] - kpos
    g5 = jnp.arange(KV_HEADS)[None, :, None, None, None]
    bias = rel_table.reshape(REL_BUCKETS, KV_HEADS, HPG)[_rel_bucket(dist), g5].astype(F32)
    s = s + jnp.moveaxis(bias, -1, 2)
    nk = idx.shape[-1]
    valid = (dist >= 0)[:, :, None].reshape(bsz, KV_HEADS, 1, lq, nk * SEL_BLOCK)
    p = _masked_softmax(s.reshape(bsz, KV_HEADS, HPG, lq, nk * SEL_BLOCK), valid)
    p = p.reshape(bsz, KV_HEADS, HPG, lq, nk, SEL_BLOCK)
    return jnp.einsum('bghqkr,bgqkrd->bqghd', p.astype(vg.dtype), vg)


def nsa_mixer(x, g, pos0, kv_cmp_past, kv_sel_past, kv_win_past, n_keep,
              w_in, q_g, k_g, cmp_pe, cmp_w1, cmp_w2, w_out, rel_table):
    bsz, L, _ = x.shape
    sizes = [N_HEADS * HEAD_DIM] + [KV_DIM] * 6 + [3 * N_HEADS]
    proj = norm_matmul(x.reshape(bsz * L, D_MODEL), g, w_in).reshape(bsz, L, -1)
    q, kc, vc, ks, vs, kw, vw, gl = jnp.split(proj, np.cumsum(sizes)[:-1].tolist(), axis=-1)
    q = _rmsnorm(q.reshape(bsz, L, KV_HEADS, HPG, HEAD_DIM), q_g)

    def kv(t):
        return t.reshape(bsz, L, KV_HEADS, HEAD_DIM)

    new_cmp = jnp.stack([kv(kc), kv(vc)], axis=2)
    new_sel = jnp.stack([_rmsnorm(kv(ks), k_g[1]), kv(vs)], axis=2)
    new_win = jnp.stack([_rmsnorm(kv(kw), k_g[2]), kv(vw)], axis=2)
    qpos = pos0 + jnp.arange(L)

    crows = jnp.concatenate([kv_cmp_past.astype(x.dtype), new_cmp], axis=1)
    kcmp = _rmsnorm(_compress(crows[:, :, 0], cmp_pe[0], cmp_w1[0], cmp_w2[0]), k_g[0])
    vcmp = _compress(crows[:, :, 1], cmp_pe[1], cmp_w1[1], cmp_w2[1])
    ends = jnp.arange(kcmp.shape[1]) * CMP_STRIDE + CMP_BLOCK - 1
    o_cmp, p_cmp = _group_attend(q, qpos, kcmp, vcmp, ends, qpos[:, None] >= ends[None, :], rel_table)

    srows = jnp.concatenate([kv_sel_past.astype(x.dtype), new_sel], axis=1)
    T = srows.shape[1]
    n_sel = -(-T // SEL_BLOCK)
    srows = jnp.pad(srows, ((0, 0), (0, n_sel * SEL_BLOCK - T), (0, 0), (0, 0), (0, 0)))
    blocks = srows.reshape(bsz, n_sel, SEL_BLOCK, 2, KV_HEADS, HEAD_DIM).transpose(3, 0, 4, 1, 2, 5)
    idx = _select_blocks(p_cmp, qpos, n_sel)
    qb = SEL_QBLK if L % SEL_QBLK == 0 else L
    nqb = L // qb

    def sel_block(args):
        qgi, qposi, idxi = args
        return _sel_attend(qgi, qposi, idxi, blocks[0], blocks[1], rel_table)

    o_sel = lax.map(sel_block, (q.reshape(bsz, nqb, qb, KV_HEADS, HPG, HEAD_DIM).swapaxes(0, 1),
                                qpos.reshape(nqb, qb),
                                idx.reshape(bsz, KV_HEADS, nqb, qb, -1).transpose(2, 0, 1, 3, 4)))
    o_sel = o_sel.swapaxes(0, 1).reshape(bsz, L, KV_HEADS, HPG, HEAD_DIM)

    p_win = kv_win_past.shape[1]
    wrows = jnp.concatenate([kv_win_past.astype(x.dtype), new_win], axis=1)
    wpad = jnp.pad(wrows, ((0, 0), (WINDOW, 0), (0, 0), (0, 0), (0, 0)))
    n_all = WINDOW + p_win + L
    kpos_all = pos0 - p_win - WINDOW + jnp.arange(n_all)
    kvalid_all = jnp.arange(n_all) >= WINDOW
    wq = WIN_QBLK if L % WIN_QBLK == 0 else L

    def win_block(i):
        start = p_win + i * wq
        qgi = lax.dynamic_slice_in_dim(q, i * wq, wq, axis=1)
        kvi = lax.dynamic_slice_in_dim(wpad, start, WINDOW + wq, axis=1)
        kposi = lax.dynamic_slice_in_dim(kpos_all, start, WINDOW + wq)
        kvalidi = lax.dynamic_slice_in_dim(kvalid_all, start, WINDOW + wq)
        qposi = pos0 + i * wq + jnp.arange(wq)
        dist = qposi[:, None] - kposi[None, :]
        valid = kvalidi[None, :] & (dist >= 0) & (dist <= WINDOW)
        o, _ = _group_attend(qgi, qposi, kvi[:, :, 0], kvi[:, :, 1], kposi, valid, rel_table)
        return o

    o_win = lax.map(win_block, jnp.arange(L // wq)).swapaxes(0, 1).reshape(bsz, L, KV_HEADS, HPG, HEAD_DIM)

    gate = jax.nn.sigmoid(gl).reshape(bsz, L, KV_HEADS, HPG, 3)
    o = gate[..., 0:1] * o_cmp + gate[..., 1:2] * o_sel + gate[..., 2:3] * o_win
    y = norm_matmul(o.reshape(bsz * L, N_HEADS * HEAD_DIM), None, w_out, normalize=False)
    return y.reshape(bsz, L, D_MODEL), new_cmp, new_sel, wrows[:, -n_keep:]


ATT_TQ = 128
ATT_TK = 128
NEG = -1e30


def _nt_dot(a, b):
    return lax.dot_general(a, b, (((1,), (1,)), ((), ())), preferred_element_type=F32)


def _compress_body(rk_ref, rv_ref, pe_ref, w1_ref, w2_ref, kg_ref, kc_ref, vc_ref, *, nb):
    half = (CMP_BLOCK // 2) * HEAD_DIM
    for kv, (r_ref, o_ref) in enumerate(((rk_ref, kc_ref), (rv_ref, vc_ref))):
        lo = (r_ref[0, 0, 0:nb, :] + pe_ref[kv, 0:1, :]).astype(BF16)
        hi = (r_ref[0, 0, 1:nb + 1, :] + pe_ref[kv, 1:2, :]).astype(BF16)
        h = (jnp.dot(lo, w1_ref[kv, :half, :].astype(BF16), preferred_element_type=F32)
             + jnp.dot(hi, w1_ref[kv, half:, :].astype(BF16), preferred_element_type=F32))
        h = h * jax.nn.sigmoid(h)
        o = jnp.dot(h.astype(BF16), w2_ref[kv].astype(BF16), preferred_element_type=F32)
        if kv == 0:
            o = o * lax.rsqrt(jnp.mean(o * o, axis=-1, keepdims=True) + EPS) * kg_ref[...]
        o_ref[0, 0] = o


def compress_rows(rk, rv, pe, w1, w2, kg):
    bsz, g, nbp, width = rk.shape
    nb = nbp - 8
    strip = pl.BlockSpec((1, 1, nbp, width), lambda b, j: (b, j, 0, 0))
    out = pl.BlockSpec((1, 1, nb, HEAD_DIM), lambda b, j: (b, j, 0, 0))
    full = lambda a: pl.BlockSpec(a.shape, lambda b, j: (0,) * a.ndim)
    pe2 = pe.reshape(2, 2, width)
    kg2 = kg.reshape(1, HEAD_DIM)
    return pl.pallas_call(
        functools.partial(_compress_body, nb=nb),
        name="nsa_compress",
        grid=(bsz, g),
        in_specs=[strip, strip, full(pe2), full(w1), full(w2), full(kg2)],
        out_specs=[out, out],
        out_shape=[jax.ShapeDtypeStruct((bsz, g, nb, HEAD_DIM), F32)] * 2,
        compiler_params=pltpu.CompilerParams(
            dimension_semantics=("parallel", "parallel"),
            vmem_limit_bytes=VMEM_LIMIT_BYTES),
    )(rk, rv, pe2, w1, w2, kg2)


def _split3(x):
    a = x.astype(BF16)
    r = x - a.astype(F32)
    b = r.astype(BF16)
    c = (r - b.astype(F32)).astype(BF16)
    return a, b, c


def _nsa_attn_body(q_ref, kc_ref, vc_ref, ks_ref, vs_ref, kw_ref, vw_ref, gate_ref, bcmp_ref, btile_ref,
                   ovl_ref, o_ref, s_scr, *, n_cmp, n_sel, topk):
    i = pl.program_id(2)
    q0 = i * ATT_TQ
    rows = HPG * ATT_TQ
    qt = q_ref[0]
    qs = (jnp.concatenate([qt[:, h * HEAD_DIM:(h + 1) * HEAD_DIM] for h in range(HPG)], axis=0) * SCALE).astype(BF16)
    row = lax.broadcasted_iota(jnp.int32, (rows, ATT_TK), 0)
    qpos = q0 + (row & (ATT_TQ - 1))
    col = lax.broadcasted_iota(jnp.int32, (rows, ATT_TK), 1)

    s = _nt_dot(qs, kc_ref[0, 0].astype(BF16)) + bcmp_ref[0, 0]
    valid = (qpos >= col * CMP_STRIDE + (CMP_BLOCK - 1)) & (col < n_cmp)
    s = jnp.where(valid, s, NEG)
    e = jnp.where(valid, jnp.exp(s - jnp.max(s, axis=-1, keepdims=True)), 0.0)
    p = e / jnp.maximum(jnp.sum(e, axis=-1, keepdims=True), 1e-30)
    pb = p.astype(BF16)
    o_cmp = jnp.dot(pb, vc_ref[0, 0].astype(BF16), preferred_element_type=F32)

    psum = p[0:ATT_TQ]
    for h in range(1, HPG):
        psum = psum + p[h * ATT_TQ:(h + 1) * ATT_TQ]
    ovl = ovl_ref[...]
    imp_t = _nt_dot(ovl, psum.astype(BF16))
    nblk = imp_t.shape[0]
    nrank = -(-n_sel // 8) * 8
    j = lax.broadcasted_iota(jnp.int32, (nrank, ATT_TQ), 0)
    qblk = (q0 + lax.broadcasted_iota(jnp.int32, (nrank, ATT_TQ), 1)) // SEL_BLOCK
    forced = (j == qblk) | (j == 0)
    score = jnp.where(forced, 1e9, jnp.where(j <= qblk, imp_t[:nrank], NEG))
    score = jnp.where(j < n_sel, score, -3e38)
    rank = jnp.zeros((nrank, ATT_TQ), F32)
    for jp in range(n_sel):
        other = score[jp:jp + 1, :]
        beats = (other > score) | ((other == score) & (jp < j))
        rank = rank + jnp.where(beats, 1.0, 0.0)
    sel_t = jnp.where((rank < topk) & (j < n_sel), 1.0, 0.0)
    if nrank < nblk:
        sel_t = jnp.concatenate([sel_t, jnp.zeros((nblk - nrank, ATT_TQ), F32)], axis=0)
    sel = sel_t.T.astype(BF16)

    jj = lax.broadcasted_iota(jnp.int32, (nblk, ATT_TK), 0)
    kk = lax.broadcasted_iota(jnp.int32, (nblk, ATT_TK), 1) // SEL_BLOCK
    blocks_per_step = ATT_TK // SEL_BLOCK

    def attend(k_ref, v_ref, lo, hi, penalty_fn):
        def tile(kc):
            return pl.ds(pl.multiple_of(kc * ATT_TK, ATT_TK), ATT_TK)

        def sweep(fn, init):
            n = hi - lo

            def pair(p, carry):
                kc = lo + 2 * p
                return fn(kc + 1, fn(kc, carry))

            carry = lax.fori_loop(0, lax.shift_right_logical(n, 1), pair, init)
            return lax.cond((n & 1) == 1, lambda c: fn(hi - 1, c), lambda c: c, carry)

        def scores(kc, m_run):
            kblk = k_ref[0, 0, tile(kc), :].astype(BF16)
            sc = _nt_dot(qs, kblk) + btile_ref[0, jnp.minimum(i - kc, 2)] + penalty_fn(kc)
            s_scr[:, tile(kc)] = sc
            return jnp.maximum(m_run, sc)

        m_run = sweep(scores, jnp.full((rows, ATT_TK), NEG, F32))
        m = jnp.max(m_run, axis=-1, keepdims=True)

        def exps(kc, l_run):
            e = jnp.exp(s_scr[:, tile(kc)] - m)
            s_scr[:, tile(kc)] = e
            return l_run + e

        l_run = sweep(exps, jnp.zeros((rows, ATT_TK), F32))
        inv = 1.0 / jnp.maximum(jnp.sum(l_run, axis=-1, keepdims=True), 1e-30)

        def weighted(kc, acc):
            pr = (s_scr[:, tile(kc)] * inv).astype(BF16)
            return acc + jnp.dot(pr, v_ref[0, 0, tile(kc), :].astype(BF16), preferred_element_type=F32)

        return sweep(weighted, jnp.zeros((rows, HEAD_DIM), F32))

    def sel_penalty(kc):
        expand = jnp.where(jj == kc * blocks_per_step + kk, 1.0, 0.0).astype(BF16)
        chosen = jnp.dot(sel, expand, preferred_element_type=F32)
        pen = jnp.concatenate([(chosen - 1.0) * (-NEG)] * HPG, axis=0)
        return jnp.where(qpos >= kc * ATT_TK + col, pen, NEG)

    def win_penalty(kc):
        dist = qpos - (kc * ATT_TK + col)
        return jnp.where((dist >= 0) & (dist <= WINDOW), 0.0, NEG)

    o_sel = attend(ks_ref, vs_ref, 0, i + 1, sel_penalty)
    o_win = attend(kw_ref, vw_ref, jnp.maximum(i - WINDOW // ATT_TK, 0), i + 1, win_penalty)

    gate = jax.nn.sigmoid(gate_ref[0, 0])

    def gcol(br):
        return jnp.concatenate([gate[:, h * 3 + br:h * 3 + br + 1] for h in range(HPG)], axis=0)

    o = gcol(0) * o_cmp + gcol(1) * o_sel + gcol(2) * o_win
    o_ref[0] = jnp.concatenate([o[h * ATT_TQ:(h + 1) * ATT_TQ] for h in range(HPG)], axis=1)


def _bias_tables(rel_table, n_qtiles):
    tab = rel_table[_rel_bucket(jnp.arange(REL_MAX_DIST + 1))].astype(F32)
    tab = tab.T.reshape(KV_HEADS, HPG, REL_MAX_DIST + 1)

    def skew(v, rows, width, step):
        out = jnp.broadcast_to(v[..., None, :], v.shape[:-1] + (rows, width + step))
        out = out.reshape(v.shape[:-1] + (rows * (width + step),))[..., :rows * width]
        return out.reshape(v.shape[:-1] + (rows, width))

    width = 2 * ATT_TK
    w = np.arange(width + 1)
    s_minus_t = np.where(w <= ATT_TK, w, w - (width + 1))
    d_idx = np.stack([np.clip(delta * ATT_TK - s_minus_t, 0, REL_MAX_DIST) for delta in range(3)])
    btile = skew(tab[:, :, d_idx], ATT_TQ, width, 1)[..., :ATT_TK]
    btile = btile.transpose(0, 2, 1, 3, 4).reshape(KV_HEADS, 3, HPG * ATT_TQ, ATT_TK)

    n_q = n_qtiles * ATT_TQ
    u = np.arange(n_q + CMP_STRIDE)
    v_cmp = tab[:, :, np.clip(u - (CMP_BLOCK - 1), 0, REL_MAX_DIST)]
    bcmp = skew(v_cmp, ATT_TK, n_q, CMP_STRIDE)
    bcmp = bcmp.reshape(KV_HEADS, HPG, ATT_TK, n_qtiles, ATT_TQ).transpose(3, 0, 1, 4, 2)
    return btile, bcmp.reshape(n_qtiles, KV_HEADS, HPG * ATT_TQ, ATT_TK)


def nsa_prompt_attention(q, kcmp, vcmp, ks, vs, kw, vw, gl, rel_table):
    bsz, L, _ = q.shape
    assert L % ATT_TQ == 0 and kcmp.shape[2] == ATT_TK
    assert math.frexp(SCALE)[0] == 0.5, "the kernel folds SCALE into q, exact only for powers of two"
    nq = L // ATT_TQ
    n_cmp = (L - CMP_BLOCK) // CMP_STRIDE + 1
    n_sel = L // SEL_BLOCK
    assert n_sel <= ATT_TK
    btile, bcmp = _bias_tables(rel_table, nq)
    c = np.arange(ATT_TK)[None, :] * CMP_STRIDE
    sb = np.arange(ATT_TK)[:, None] * SEL_BLOCK
    ovl = ((c < sb + SEL_BLOCK) & (c + CMP_BLOCK > sb) & (np.arange(ATT_TK)[None, :] < n_cmp)
           & (np.arange(ATT_TK)[:, None] < n_sel))
    ovl = jnp.asarray(ovl, BF16)
    width = HPG * HEAD_DIM
    kvspec = lambda n: pl.BlockSpec((1, 1, n, HEAD_DIM), lambda b, g, i: (b, g, 0, 0))
    return pl.pallas_call(
        functools.partial(_nsa_attn_body, n_cmp=n_cmp, n_sel=n_sel, topk=min(SEL_TOPK, n_sel)),
        name="nsa_attention",
        grid=(bsz, KV_HEADS, nq),
        in_specs=[
            pl.BlockSpec((1, ATT_TQ, width), lambda b, g, i: (b, i, g)),
            kvspec(ATT_TK), kvspec(ATT_TK), kvspec(L), kvspec(L), kvspec(L), kvspec(L),
            pl.BlockSpec((1, 1, ATT_TQ, HPG * 3), lambda b, g, i: (b, g, i, 0)),
            pl.BlockSpec((1, 1, HPG * ATT_TQ, ATT_TK), lambda b, g, i: (i, g, 0, 0)),
            pl.BlockSpec((1, 3, HPG * ATT_TQ, ATT_TK), lambda b, g, i: (g, 0, 0, 0)),
            pl.BlockSpec((ATT_TK, ATT_TK), lambda b, g, i: (0, 0)),
        ],
        out_specs=pl.BlockSpec((1, ATT_TQ, width), lambda b, g, i: (b, i, g)),
        out_shape=jax.ShapeDtypeStruct((bsz, L, N_HEADS * HEAD_DIM), F32),
        scratch_shapes=[pltpu.VMEM((HPG * ATT_TQ, L), F32)],
        compiler_params=pltpu.CompilerParams(
            dimension_semantics=("parallel", "parallel", "arbitrary"),
            vmem_limit_bytes=VMEM_LIMIT_BYTES),
    )(q, kcmp, vcmp, ks, vs, kw, vw, gl, bcmp, btile, ovl)


def nsa_prompt_mixer(x, g, w_in, q_g, k_g, cmp_pe, cmp_w1, cmp_w2, w_out, rel_table):
    bsz, L, _ = x.shape
    t = bsz * L
    proj = norm_matmul(x.reshape(t, D_MODEL), g, w_in)
    c0 = N_HEADS * HEAD_DIM
    q = _rmsnorm(proj[:, :c0].reshape(t, N_HEADS, HEAD_DIM), q_g).reshape(bsz, L, c0)

    def kvpair(k, normed_g):
        kcols = proj[:, c0 + 2 * k * KV_DIM:c0 + (2 * k + 1) * KV_DIM]
        vcols = proj[:, c0 + (2 * k + 1) * KV_DIM:c0 + (2 * k + 2) * KV_DIM]
        if normed_g is not None:
            kcols = _rmsnorm(kcols.reshape(t, KV_HEADS, HEAD_DIM), normed_g).reshape(t, KV_DIM)
        rows = jnp.stack([kcols, vcols], axis=1).reshape(bsz, L, 2, KV_HEADS, HEAD_DIM)
        per_group = rows.transpose(2, 0, 3, 1, 4)
        return rows, per_group[0], per_group[1]

    new_cmp, kc_rows, vc_rows = kvpair(0, None)
    new_sel, ks, vs = kvpair(1, k_g[1])
    new_win, kw, vw = kvpair(2, k_g[2])
    gl = proj[:, c0 + 6 * KV_DIM:].reshape(bsz, L, KV_HEADS, HPG * 3).transpose(0, 2, 1, 3)

    nb = L // CMP_STRIDE

    def strips(r):
        r = r.reshape(bsz, KV_HEADS, nb, CMP_STRIDE * HEAD_DIM)
        return jnp.pad(r, ((0, 0), (0, 0), (0, ATT_TK + 8 - nb), (0, 0)))

    kcmp, vcmp = compress_rows(strips(kc_rows), strips(vc_rows), cmp_pe, cmp_w1, cmp_w2, k_g[0])
    o = nsa_prompt_attention(q, kcmp, vcmp, ks, vs, kw, vw, gl, rel_table)
    y = norm_matmul(o.reshape(t, c0), None, w_out, normalize=False, res=x.reshape(t, D_MODEL))
    return y.reshape(bsz, L, D_MODEL), new_cmp, new_sel, new_win[:, -min(WINDOW, L):]


DEC_PAGES_PER_STEP = 8
STRIP = CMP_STRIDE
ROW_LANES = 2 * KV_DIM
SEL_LANES = 256


def _group_rmsnorm(x, gain_row):
    lane = lax.broadcasted_iota(jnp.int32, x.shape, 1) // HEAD_DIM
    sq = x * x
    ms = jnp.zeros_like(x)
    for grp in range(KV_HEADS):
        tot = jnp.sum(jnp.where(lane == grp, sq, 0.0), axis=-1, keepdims=True)
        ms = jnp.where(lane == grp, tot, ms)
    return x * lax.rsqrt(ms / HEAD_DIM + EPS) * gain_row


def _decode_cmp_body(pt_ref, *refs, n_strips, n_sel, qblk, topk):
    pages = refs[:DEC_PAGES_PER_STEP]
    (bd_ref, peterm_ref, w2bd_ref, kg_ref, qbd_ref, bias_ref, ovl_ref, upper_ref,
     ocmp_ref, idx_ref, seq, tbuf, hibuf) = refs[DEC_PAGES_PER_STEP:]
    s = pl.program_id(1)
    strips_per_page = pages[0].shape[2] // STRIP
    for r in range(DEC_PAGES_PER_STEP):
        row0 = pl.multiple_of((s * DEC_PAGES_PER_STEP + r) * strips_per_page, strips_per_page)
        for c in range(ROW_LANES // LANES):
            lanes = slice(c * LANES, (c + 1) * LANES)
            tbuf[c] = pages[r][0, lanes, :].T
            for l in range(STRIP):
                seq[l, pl.ds(row0, strips_per_page), lanes] = tbuf[c, pl.ds(l, strips_per_page, stride=STRIP), :]

    @pl.when(s == pl.num_programs(1) - 1)
    def _():
        n_cmp = n_strips - 1
        summaries = []
        for kv in range(2):
            halves = []
            for half in range(2):
                acc = jnp.zeros((n_strips, KV_DIM), F32)
                for l in range(STRIP):
                    w_idx = (kv * 2 + half) * STRIP + l
                    xl = (seq[l, :, kv * KV_DIM:(kv + 1) * KV_DIM] + peterm_ref[w_idx:w_idx + 1, :]).astype(BF16)
                    acc = acc + jnp.dot(xl, bd_ref[w_idx], preferred_element_type=F32)
                halves.append(acc)
            hibuf[0:n_strips, :] = halves[1]
            hibuf[n_strips:n_strips + 8, :] = jnp.zeros((8, KV_DIM), F32)
            h = halves[0] + hibuf[1:n_strips + 1, :]
            h = (h * jax.nn.sigmoid(h)).astype(BF16)
            o = jnp.dot(h, w2bd_ref[kv], preferred_element_type=F32)
            if kv == 0:
                o = _group_rmsnorm(o, kg_ref[...])
            summaries.append(o.astype(BF16))
        kcmp, vcmp = summaries

        qbd = qbd_ref[0].astype(BF16)
        sc = _nt_dot(qbd, kcmp) * SCALE + bias_ref[...]
        col = lax.broadcasted_iota(jnp.int32, sc.shape, 1)
        valid = col < n_cmp
        sc = jnp.where(valid, sc, NEG)
        e = jnp.where(valid, jnp.exp(sc - jnp.max(sc, axis=-1, keepdims=True)), 0.0)
        p = e / jnp.maximum(jnp.sum(e, axis=-1, keepdims=True), 1e-30)
        pb = p.astype(BF16)
        o_cmp = jnp.dot(pb, vcmp, preferred_element_type=F32)
        head_grp = lax.broadcasted_iota(jnp.int32, o_cmp.shape, 0) // HPG
        lane_grp = lax.broadcasted_iota(jnp.int32, o_cmp.shape, 1) // HEAD_DIM
        ocmp_ref[0] = jnp.where(head_grp == lane_grp, o_cmp, 0.0)

        psum = jnp.concatenate(
            [jnp.sum(p[grp * HPG:(grp + 1) * HPG], axis=0, keepdims=True) for grp in range(KV_HEADS)]
            + [jnp.zeros((8 - KV_HEADS, n_strips), F32)], axis=0)
        imp = jnp.dot(psum.astype(BF16), ovl_ref[...], preferred_element_type=F32)
        j = lax.broadcasted_iota(jnp.int32, imp.shape, 1)
        forced = (j == qblk) | (j == 0)
        score = jnp.where(forced, 1e9, jnp.where(j <= qblk, imp, NEG))
        score = jnp.where(j < n_sel, score, -3e38)
        rank = jnp.zeros(imp.shape, F32)
        for jp in range(n_sel):
            other = score[:, jp:jp + 1]
            beats = (other > score) | ((other == score) & (jp < j))
            rank = rank + jnp.where(beats, 1.0, 0.0)
        chosen = (rank < topk) & (j < n_sel)
        cum = jnp.dot(jnp.where(chosen, 1.0, 0.0).astype(BF16), upper_ref[...], preferred_element_type=F32)
        jf = j.astype(F32)
        out_lane = lax.broadcasted_iota(jnp.int32, (8, LANES), 1)
        out = jnp.zeros((8, LANES), F32)
        for k in range(topk):
            pick = jnp.sum(jnp.where(chosen & (cum == k + 1.0), jf, 0.0), axis=-1, keepdims=True)
            out = jnp.where(out_lane == k, pick, out)
        idx_ref[0] = out


def _decode_attn_body(tbl_ref, idx_ref, *refs, n_blk, qblk, per_page):
    pages = refs[:n_blk]
    (win_ref, q_ref, knew_ref, vnew_ref, wknew_ref, wvnew_ref, bsel_ref, bwin_ref, ocmp_ref, gate_ref,
     o_ref) = refs[n_blk:]
    b, g = pl.program_id(0), pl.program_id(1)
    qb = q_ref[0, 0].astype(BF16)
    qf = qb.astype(F32)
    k_rows = pl.ds(pl.multiple_of(g * HEAD_DIM, HEAD_DIM), HEAD_DIM)
    v_rows = pl.ds(pl.multiple_of(KV_DIM + g * HEAD_DIM, HEAD_DIM), HEAD_DIM)

    def attend(parts, k_new, v_new, bias_new):
        s_new = jnp.sum(qf * k_new.astype(BF16).astype(F32), axis=-1, keepdims=True) * SCALE + bias_new
        m = s_new
        for sc, ok, _ in parts:
            m = jnp.maximum(m, jnp.max(jnp.where(ok, sc, NEG), axis=-1, keepdims=True))
        e_new = jnp.exp(s_new - m)
        es = [jnp.where(ok, jnp.exp(sc - m), 0.0) for sc, ok, _ in parts]
        den = e_new
        for e in es:
            den = den + jnp.sum(e, axis=-1, keepdims=True)
        den = jnp.maximum(den, 1e-30)
        acc = (e_new / den).astype(BF16).astype(F32) * v_new.astype(BF16).astype(F32)
        for e, (_, _, v_t) in zip(es, parts):
            acc = acc + _nt_dot((e / den).astype(BF16), v_t)
        return acc

    sel_parts = []
    for k in range(n_blk):
        bidx = idx_ref[b, g, k]
        k_t = pages[k][0, k_rows, :].astype(BF16)
        v_t = pages[k][0, v_rows, :].astype(BF16)
        pg = jnp.minimum(bidx, qblk) // per_page
        sc = jnp.dot(qb, k_t, preferred_element_type=F32) * SCALE + bsel_ref[pg, 0]
        lane_blk = lax.broadcasted_iota(jnp.int32, sc.shape, 1) // SEL_BLOCK
        ok = (lane_blk == bidx % per_page) & (bidx < qblk)
        sel_parts.append((sc, ok, v_t))
    bias0 = bsel_ref[qblk // per_page, 0][:, 0:1]
    o_sel = attend(sel_parts, knew_ref[0, 0], vnew_ref[0, 0], bias0)

    sc = jnp.dot(qb, win_ref[0, k_rows, :].astype(BF16), preferred_element_type=F32) * SCALE + bwin_ref[0]
    o_win = attend([(sc, jnp.full(sc.shape, True), win_ref[0, v_rows, :].astype(BF16))],
                   wknew_ref[0, 0], wvnew_ref[0, 0], bias0)

    gate = jax.nn.sigmoid(gate_ref[0, 0])
    o_ref[0, 0] = gate[:, 0:1] * ocmp_ref[0, 0] + gate[:, 1:2] * o_sel + gate[:, 2:3] * o_win


def nsa_decode_mixer(x, g, past_len, cache_cmp, cache_sel, win_past, page_table,
                     w_in, q_g, k_g, cmp_pe, cmp_w1, cmp_w2, w_out, rel_table):
    bsz = x.shape[0]
    n_pool, page = cache_cmp.shape[0], cache_cmp.shape[1]
    n_pages = page_table.shape[1]
    assert past_len == n_pages * page and page % STRIP == 0 and n_pages % DEC_PAGES_PER_STEP == 0
    assert win_past.shape[1] == WINDOW and past_len >= WINDOW and page % SEL_BLOCK == 0
    x2 = x.reshape(bsz, D_MODEL)
    proj = norm_matmul(x2, g, w_in)
    c0 = N_HEADS * HEAD_DIM
    q = _rmsnorm(proj[:, :c0].reshape(bsz, N_HEADS, HEAD_DIM), q_g)

    def rows(k, gain):
        kcols = proj[:, c0 + 2 * k * KV_DIM:c0 + (2 * k + 1) * KV_DIM]
        vcols = proj[:, c0 + (2 * k + 1) * KV_DIM:c0 + (2 * k + 2) * KV_DIM]
        if gain is not None:
            kcols = _rmsnorm(kcols.reshape(bsz, KV_HEADS, HEAD_DIM), gain).reshape(bsz, KV_DIM)
        return kcols, vcols

    kc_new, vc_new = rows(0, None)
    ks_new, vs_new = rows(1, k_g[1])
    kw_new, vw_new = rows(2, k_g[2])
    gl = proj[:, c0 + 6 * KV_DIM:].reshape(bsz, KV_HEADS, HPG, 3)
    as_row = lambda kk, vv: jnp.stack([kk, vv], axis=1).reshape(bsz, 1, 2, KV_HEADS, HEAD_DIM)
    new_cmp, new_sel, new_win = as_row(kc_new, vc_new), as_row(ks_new, vs_new), as_row(kw_new, vw_new)

    eye = jnp.eye(KV_HEADS, dtype=F32)
    qbd = jnp.einsum('bghd,gk->bghkd', q.reshape(bsz, KV_HEADS, HPG, HEAD_DIM), eye).reshape(bsz, N_HEADS, KV_DIM)
    w1r = cmp_w1.reshape(2, 2, STRIP, HEAD_DIM, HEAD_DIM)
    bd = jnp.einsum('khlio,gj->khlgijo', w1r, eye).reshape(2 * 2 * STRIP, KV_DIM, KV_DIM).astype(BF16)
    w2bd = jnp.einsum('kio,gj->kgijo', cmp_w2, eye).reshape(2, KV_DIM, KV_DIM).astype(BF16)
    peterm = jnp.tile(cmp_pe.reshape(2 * 2 * STRIP, HEAD_DIM), (1, KV_HEADS))
    kg_row = jnp.tile(k_g[0], KV_HEADS).reshape(1, KV_DIM)

    n_strips = past_len // STRIP
    n_cmp = n_strips - 1
    qblk = past_len // SEL_BLOCK
    n_sel = qblk + 1
    topk = min(SEL_TOPK, n_sel)
    assert n_sel <= SEL_LANES and n_strips % 8 == 0
    tab = rel_table[_rel_bucket(jnp.arange(REL_MAX_DIST + 1))].astype(F32).T
    ends = np.arange(n_strips) * CMP_STRIDE + CMP_BLOCK - 1
    bias_cmp = tab[:, np.clip(past_len - ends, 0, REL_MAX_DIST)]
    cs = np.arange(n_strips)[:, None] * CMP_STRIDE
    ss = np.arange(SEL_LANES)[None, :] * SEL_BLOCK
    ovl = (cs < ss + SEL_BLOCK) & (cs + CMP_BLOCK > ss) & (np.arange(n_strips)[:, None] < n_cmp) \
        & (np.arange(SEL_LANES)[None, :] < n_sel)
    ovl = jnp.asarray(ovl, BF16)
    upper = jnp.asarray(np.triu(np.ones((SEL_LANES, SEL_LANES), np.float32)), BF16)

    strips_per_page = page // STRIP
    cmp_view = cache_cmp.reshape(n_pool, page, ROW_LANES).transpose(0, 2, 1)
    steps = n_pages // DEC_PAGES_PER_STEP
    const = lambda a: pl.BlockSpec(a.shape, lambda b, s, pt: (0,) * a.ndim)
    page_spec = lambda r: pl.BlockSpec((1, ROW_LANES, page),
                                       lambda b, s, pt: (pt[b, s * DEC_PAGES_PER_STEP + r], 0, 0))
    o_cmp, idx = pl.pallas_call(
        functools.partial(_decode_cmp_body, n_strips=n_strips, n_sel=n_sel, qblk=qblk, topk=topk),
        name="nsa_decode_cmp",
        grid_spec=pltpu.PrefetchScalarGridSpec(
            num_scalar_prefetch=1,
            grid=(bsz, steps),
            in_specs=[page_spec(r) for r in range(DEC_PAGES_PER_STEP)]
            + [const(bd), const(peterm), const(w2bd), const(kg_row),
               pl.BlockSpec((1, N_HEADS, KV_DIM), lambda b, s, pt: (b, 0, 0)),
               const(bias_cmp), const(ovl), const(upper)],
            out_specs=[pl.BlockSpec((1, N_HEADS, KV_DIM), lambda b, s, pt: (b, 0, 0)),
                       pl.BlockSpec((1, 8, LANES), lambda b, s, pt: (b, 0, 0))],
            scratch_shapes=[pltpu.VMEM((STRIP, n_strips, ROW_LANES), F32),
                            pltpu.VMEM((ROW_LANES // LANES, page, LANES), F32),
                            pltpu.VMEM((n_strips + 8, KV_DIM), F32)],
        ),
        out_shape=[jax.ShapeDtypeStruct((bsz, N_HEADS, KV_DIM), F32),
                   jax.ShapeDtypeStruct((bsz, 8, LANES), F32)],
        compiler_params=pltpu.CompilerParams(
            dimension_semantics=("parallel", "arbitrary"),
            vmem_limit_bytes=DECODE_VMEM_LIMIT_BYTES),
    )(page_table, *([cmp_view] * DEC_PAGES_PER_STEP), bd, peterm, w2bd, kg_row, qbd, bias_cmp, ovl, upper)

    blk_idx = idx[:, :KV_HEADS, :topk].astype(jnp.int32)
    per_page = page // SEL_BLOCK
    safe = jnp.minimum(blk_idx, qblk - 1)
    page_of = jnp.take_along_axis(page_table, (safe // per_page).reshape(bsz, -1), axis=1).reshape(safe.shape)
    sel_view = cache_sel.reshape(n_pool, page, ROW_LANES).transpose(0, 2, 1)
    win_view = win_past.reshape(bsz, WINDOW, ROW_LANES).transpose(0, 2, 1)

    n_pg = n_pages + 1
    dist_sel = past_len - (np.arange(n_pg)[:, None] * page + np.arange(page)[None, :])
    bsel = tab[:, np.clip(dist_sel, 0, REL_MAX_DIST)]
    bsel = bsel.reshape(KV_HEADS, HPG, n_pg, page).transpose(2, 0, 1, 3)
    bwin = tab[:, np.clip(past_len - (past_len - WINDOW + np.arange(WINDOW)), 0, REL_MAX_DIST)]
    bwin = bwin.reshape(KV_HEADS, HPG, WINDOW)

    per_group = lambda a: a.reshape(bsz, KV_HEADS, 1, HEAD_DIM)
    q4 = q.reshape(bsz, KV_HEADS, HPG, HEAD_DIM)
    ocmp4 = o_cmp.reshape(bsz, KV_HEADS, HPG, KV_HEADS, HEAD_DIM).sum(axis=3)
    page_spec2 = lambda k: pl.BlockSpec((1, ROW_LANES, page), lambda b, g, t, i: (t[b, g, k], 0, 0))
    per_bg = lambda *shape: pl.BlockSpec((1, 1) + shape, lambda b, g, t, i: (b, g) + (0,) * len(shape))
    o4 = pl.pallas_call(
        functools.partial(_decode_attn_body, n_blk=topk, qblk=qblk, per_page=per_page),
        name="nsa_decode_attn",
        grid_spec=pltpu.PrefetchScalarGridSpec(
            num_scalar_prefetch=2,
            grid=(bsz, KV_HEADS),
            in_specs=[page_spec2(k) for k in range(topk)]
            + [pl.BlockSpec((1, ROW_LANES, WINDOW), lambda b, g, t, i: (b, 0, 0)),
               per_bg(HPG, HEAD_DIM), per_bg(1, HEAD_DIM), per_bg(1, HEAD_DIM), per_bg(1, HEAD_DIM),
               per_bg(1, HEAD_DIM),
               pl.BlockSpec((n_pg, 1, HPG, page), lambda b, g, t, i: (0, g, 0, 0)),
               pl.BlockSpec((1, HPG, WINDOW), lambda b, g, t, i: (g, 0, 0)),
               per_bg(HPG, HEAD_DIM), per_bg(HPG, 3)],
            out_specs=per_bg(HPG, HEAD_DIM),
        ),
        out_shape=jax.ShapeDtypeStruct((bsz, KV_HEADS, HPG, HEAD_DIM), F32),
        compiler_params=pltpu.CompilerParams(
            dimension_semantics=("parallel", "arbitrary"),
            vmem_limit_bytes=VMEM_LIMIT_BYTES),
    )(page_of, blk_idx, *([sel_view] * topk), win_view, q4, per_group(ks_new), per_group(vs_new),
      per_group(kw_new), per_group(vw_new), bsel, bwin, ocmp4, gl)
    y = norm_matmul(o4.reshape(bsz, c0), None, w_out, normalize=False, res=x2)
    new_win_buf = jnp.concatenate([win_past[:, 1:], new_win], axis=1)
    return y.reshape(x.shape), new_cmp, new_sel, new_win_buf


def _gla_chunked(q, k, v, logf, s0):
    bsz, L = q.shape[0], q.shape[1]
    nc = -(-L // H_CHUNK)
    pad = nc * H_CHUNK - L

    def prep(t):
        t = jnp.pad(t, ((0, 0), (0, pad), (0, 0), (0, 0)))
        return t.reshape(bsz, nc, H_CHUNK, H_HEADS, t.shape[-1])

    q, k, v, logf = prep(q), prep(k), prep(v), prep(logf)
    acum = jnp.cumsum(logf, axis=2)
    alast = acum[:, :, -1:]
    qe = q * jnp.exp(acum)
    ke = k * jnp.exp(-acum)
    kd = k * jnp.exp(alast - acum)
    tri = jnp.tril(jnp.ones((H_CHUNK, H_CHUNK), bool))
    att = jnp.where(tri, jnp.einsum('bcthk,bcshk->bchts', qe, ke), 0.0)
    o_intra = jnp.einsum('bchts,bcshv->bcthv', att, v)
    upd = jnp.einsum('bcshk,bcshv->bchkv', kd, v)

    def step(s, inp):
        dec, up = inp
        return dec[..., None] * s + up, s

    s_last, s_prev = lax.scan(step, s0, (jnp.exp(alast[:, :, 0]).swapaxes(0, 1), upd.swapaxes(0, 1)))
    o_inter = jnp.einsum('bcthk,bchkv->bcthv', qe, s_prev.swapaxes(0, 1))
    o = (o_intra + o_inter).reshape(bsz, nc * H_CHUNK, H_HEADS, H_DV)[:, :L]
    return o, s_last


HGRN_TOKENS = 128


def _hgrn_body(q_ref, f_ref, v_ref, gate_ref, lb_ref, ng_ref, lcum_ref, lsum_ref, o_ref, s_ref):
    @pl.when(pl.program_id(1) == 0)
    def _():
        s_ref[...] = jnp.zeros_like(s_ref)

    n = HGRN_TOKENS
    qr = q_ref[...]
    q = qr * jax.nn.sigmoid(qr)
    lb = lb_ref[...]
    forget = lb + (1.0 - lb) * jax.nn.sigmoid(f_ref[...])
    logf = jnp.log(forget)
    k = 1.0 - forget
    parts = _split3(logf)
    acum = sum(jnp.dot(lcum_ref[...], p, preferred_element_type=F32) for p in parts)
    atot = sum(jnp.dot(lsum_ref[...], p, preferred_element_type=F32) for p in parts)
    qe = (q * jnp.exp(acum)).astype(BF16)
    ke = (k * jnp.exp(-acum)).astype(BF16)
    kd = (k * jnp.exp(atot - acum)).astype(BF16)
    vb = v_ref[...].astype(BF16)
    ti = lax.broadcasted_iota(jnp.int32, (n, n), 0)
    si = lax.broadcasted_iota(jnp.int32, (n, n), 1)
    intra = (ti // H_CHUNK == si // H_CHUNK) & (ti >= si)
    gate = gate_ref[...]
    gate = gate * jax.nn.sigmoid(gate)
    outs = []
    for h in range(H_HEADS):
        cols = slice(h * H_DK, (h + 1) * H_DK)
        att = jnp.where(intra, _nt_dot(qe[:, cols], ke[:, cols]), 0.0)
        o_h = jnp.dot(att.astype(BF16), vb[:, cols], preferred_element_type=F32)
        decay_t = jnp.exp(atot[:, cols]).T
        state = s_ref[0, cols, :]
        inter = []
        for j in range(n // H_CHUNK):
            rows = slice(j * H_CHUNK, (j + 1) * H_CHUNK)
            inter.append(jnp.dot(qe[rows, cols], state.astype(BF16), preferred_element_type=F32))
            upd = lax.dot_general(kd[rows, cols], vb[rows, cols], (((0,), (0,)), ((), ())),
                                  preferred_element_type=F32)
            state = decay_t[:, j * H_CHUNK:j * H_CHUNK + 1] * state + upd
        s_ref[0, cols, :] = state
        o_h = o_h + jnp.concatenate(inter, axis=0)
        o_h = o_h * lax.rsqrt(jnp.mean(o_h * o_h, axis=-1, keepdims=True) + EPS) * ng_ref[...]
        outs.append(o_h * gate[:, cols])
    o_ref[...] = jnp.concatenate(outs, axis=1).astype(BF16)


def hgrn2_prompt_mixer(x, g, lb, w_in, norm_g, w_out):
    bsz, L, _ = x.shape
    n = HGRN_TOKENS
    assert L % n == 0 and n % H_CHUNK == 0 and H_DK == LANES and H_DV == LANES
    t = bsz * L
    x2 = x.reshape(t, D_MODEL)
    proj = norm_matmul(x2, g, w_in)
    nb = L // n
    r = np.arange(n)
    same = (r[:, None] // H_CHUNK) == (r[None, :] // H_CHUNK)
    lcum = jnp.asarray(same & (r[None, :] <= r[:, None]), BF16)
    lsum = jnp.asarray(same, BF16)
    col = lambda kk: pl.BlockSpec((n, D_MODEL), lambda b, c: (b * nb + c, kk))
    const = lambda a: pl.BlockSpec(a.shape, lambda b, c: (0,) * a.ndim)
    lb2 = lb.reshape(1, D_MODEL)
    ng2 = norm_g.reshape(1, H_DV)
    o, s = pl.pallas_call(
        _hgrn_body,
        name="hgrn_blocks",
        grid=(bsz, nb),
        in_specs=[col(0), col(1), col(2), col(3), const(lb2), const(ng2), const(lcum), const(lsum)],
        out_specs=[pl.BlockSpec((n, D_MODEL), lambda b, c: (b * nb + c, 0)),
                   pl.BlockSpec((1, H_HEADS * H_DK, H_DV), lambda b, c: (b, 0, 0))],
        out_shape=[jax.ShapeDtypeStruct((t, D_MODEL), BF16),
                   jax.ShapeDtypeStruct((bsz, H_HEADS * H_DK, H_DV), F32)],
        compiler_params=pltpu.CompilerParams(
            dimension_semantics=("parallel", "arbitrary"),
            vmem_limit_bytes=VMEM_LIMIT_BYTES),
    )(proj, proj, proj, proj, lb2, ng2, lcum, lsum)
    y = norm_matmul(o, None, w_out, normalize=False, res=x2)
    return y.reshape(bsz, L, D_MODEL), s.reshape(bsz, H_HEADS, H_DK, H_DV)


def hgrn2_mixer(x, g, s0, lb, w_in, norm_g, w_out):
    bsz, L, _ = x.shape
    proj = norm_matmul(x.reshape(bsz * L, D_MODEL), g, w_in).reshape(bsz, L, -1)
    q, fr, iv, gate = jnp.split(proj, 4, axis=-1)
    q = jax.nn.silu(q).reshape(bsz, L, H_HEADS, H_DK)
    forget = lb + (1.0 - lb) * jax.nn.sigmoid(fr)
    logf = jnp.log(forget).reshape(bsz, L, H_HEADS, H_DK)
    k = (1.0 - forget).reshape(bsz, L, H_HEADS, H_DK)
    v = iv.reshape(bsz, L, H_HEADS, H_DV)
    o, s = _gla_chunked(q, k, v, logf, s0)
    o = _rmsnorm(o, norm_g) * jax.nn.silu(gate.reshape(bsz, L, H_HEADS, H_DV))
    y = norm_matmul(o.reshape(bsz * L, D_MODEL), None, w_out, normalize=False)
    return y.reshape(bsz, L, D_MODEL), s


def _gather_pages(cache, page_table):
    pages = cache[page_table]
    b, n, p = pages.shape[0], pages.shape[1], pages.shape[2]
    return pages.reshape(b, n * p, *pages.shape[3:])


def kernel(x_prompt, x_sample, state_ssm, state_conv, cache_kv_cmp, cache_kv_sel, cache_kv_win, state_hgrn, page_table, norm_g, rel_table, m_w_in, m_conv_w, m_conv_b, m_dt_bias, m_a_log, m_d, m_norm_g, m_w_out, n_w_in, n_q_g, n_k_g, n_cmp_pe, n_cmp_w1, n_cmp_w2, n_w_out, h_w_in, h_lb, h_norm_g, h_w_out, moe_w_rg, moe_b_rg, moe_w_re, moe_b_re, moe_w1, moe_w3, moe_w2):
    bp, lp = x_prompt.shape[0], x_prompt.shape[1]
    past_len = page_table.shape[1] * cache_kv_cmp.shape[2]
    dt = x_prompt.dtype
    m_w_in, m_w_out, n_w_in, n_w_out, h_w_in, h_w_out, moe_w1, moe_w3, moe_w2 = (
        w.astype(BF16) for w in (m_w_in, m_w_out, n_w_in, n_w_out, h_w_in, h_w_out, moe_w1, moe_w3, moe_w2))
    lbs = jax.nn.softmax(h_lb.astype(F32), axis=0)
    lbs = jnp.cumsum(lbs, axis=0) - lbs[0]
    xp, xs = x_prompt, x_sample
    ssm_p, conv_p, cmp_p, sel_p, win_p, hg_p = [], [], [], [], [], []
    ssm_s, conv_s, cmp_s, sel_s, win_s, hg_s = [], [], [], [], [], []
    for i in range(DEPTH):
        kind, j = i % N_MIXERS, i // N_MIXERS
        g0 = norm_g[i, 0]
        if kind == 0:
            w = (m_w_in[j], m_conv_w[j], m_conv_b[j], m_dt_bias[j], m_a_log[j], m_d[j], m_norm_g[j], m_w_out[j])
            xp, cbuf, hh = mamba_prompt_mixer(xp, g0, *w)
            ssm_p.append(hh)
            conv_p.append(cbuf)
            ys, cbuf, hh = mamba_mixer(xs, g0, state_conv[j], state_ssm[j], *w)
            xs = xs + ys
            ssm_s.append(hh)
            conv_s.append(cbuf)
        elif kind == 1:
            w = (n_w_in[j], n_q_g[j], n_k_g[j], n_cmp_pe[j], n_cmp_w1[j], n_cmp_w2[j], n_w_out[j], rel_table)
            xp, rc, rs, wb = nsa_prompt_mixer(xp, g0, *w)
            cmp_p.append(rc)
            sel_p.append(rs)
            win_p.append(wb)
            xs, rc, rs, wb = nsa_decode_mixer(xs, g0, past_len, cache_kv_cmp[j], cache_kv_sel[j],
                                              cache_kv_win[j], page_table, *w)
            cmp_s.append(rc)
            sel_s.append(rs)
            win_s.append(wb)
        else:
            w = (lbs[i], h_w_in[j], h_norm_g[j], h_w_out[j])
            xp, st = hgrn2_prompt_mixer(xp, g0, *w)
            hg_p.append(st)
            ys, st = hgrn2_mixer(xs, g0, state_hgrn[j], *w)
            xs = xs + ys
            hg_s.append(st)
        mw = (moe_w_rg[i], moe_b_rg[i], moe_w_re[i], moe_b_re[i], moe_w1[i], moe_w3[i], moe_w2[i])
        xp = hier_moe_residual(xp.reshape(-1, D_MODEL), norm_g[i, 1], *mw).reshape(xp.shape)
        xs = hier_moe_residual(xs.reshape(-1, D_MODEL), norm_g[i, 1], *mw).reshape(xs.shape)
    return (xp, xs,
            jnp.stack(ssm_p), jnp.stack(conv_p), jnp.stack(cmp_p), jnp.stack(sel_p), jnp.stack(win_p), jnp.stack(hg_p),
            jnp.stack(ssm_s), jnp.stack(conv_s), jnp.stack(cmp_s), jnp.stack(sel_s), jnp.stack(win_s), jnp.stack(hg_s))
```

```python
import functools
import math

import jax
import jax.numpy as jnp
import numpy as np
from jax import lax
from jax.experimental import pallas as pl
from jax.experimental.pallas import tpu as pltpu

F32 = jnp.float32
BF16 = jnp.bfloat16
EPS = 1e-6

D_MODEL = 1024
DEPTH = 4
N_MIXERS = 3

M_DINNER = 2 * D_MODEL
M_HEADDIM = 64
M_HEADS = M_DINNER // M_HEADDIM
M_GROUPS = 4
M_DSTATE = 128
M_CONV = 4
M_GN = M_GROUPS * M_DSTATE
M_CONV_DIM = M_DINNER + 2 * M_GN
M_CHUNK = 128

N_HEADS = 16
HEAD_DIM = D_MODEL // N_HEADS
KV_HEADS = 4
HPG = N_HEADS // KV_HEADS
KV_DIM = KV_HEADS * HEAD_DIM
CMP_BLOCK = 32
CMP_STRIDE = 16
SEL_BLOCK = 64
SEL_TOPK = 16
WINDOW = 512
SEL_QBLK = 32
WIN_QBLK = 128
SCALE = HEAD_DIM ** -0.5
REL_BUCKETS = 32
REL_MAX_DIST = 128

H_DK = 128
H_HEADS = D_MODEL // H_DK
H_DV = D_MODEL // H_HEADS
H_CHUNK = 32

MOE_GROUPS = 4
MOE_EPG = 4
MOE_EXPERTS = MOE_GROUPS * MOE_EPG
MOE_TOPK = 2
MOE_FF = 512

VMEM_LIMIT_BYTES = 48 * 1024 * 1024
DECODE_VMEM_LIMIT_BYTES = 56 * 1024 * 1024


def _rmsnorm(x, g):
    xf = x.astype(F32)
    y = xf * lax.rsqrt(jnp.mean(xf * xf, axis=-1, keepdims=True) + EPS)
    return (y * g.astype(F32)).astype(x.dtype)


LANES = 128
MXU_WIDTH = 256


def _norm_matmul_body(*refs, normalize, has_res):
    if has_res:
        x_ref, g_ref, w_ref, res_ref, o_ref, xb_ref = refs
    else:
        x_ref, g_ref, w_ref, o_ref, xb_ref = refs

    @pl.when(pl.program_id(1) == 0)
    def _():
        x = x_ref[...].astype(F32)
        if normalize:
            x = x * lax.rsqrt(jnp.mean(x * x, axis=-1, keepdims=True) + EPS) * g_ref[...]
        xb_ref[...] = x.astype(BF16)

    acc = jnp.dot(xb_ref[...], w_ref[...].astype(BF16), preferred_element_type=F32)
    if has_res:
        acc = acc + res_ref[...]
    o_ref[...] = acc


def _pick_tile(n, pref):
    t = min(n, pref)
    while n % t:
        t //= 2
    return t


def norm_matmul(x, g, w, *, normalize=True, res=None, tm=None, tn=512):
    t, k = x.shape
    n_true = w.shape[1]
    tm = _pick_tile(t, tm or (1024 if k <= 1024 else 512))
    col_tile = MXU_WIDTH if n_true > MXU_WIDTH else LANES
    if n_true % col_tile:
        assert res is None
        w = jnp.pad(w, ((0, 0), (0, col_tile - n_true % col_tile)))
    n = w.shape[1]
    tn = _pick_tile(n, tn)
    if g is None:
        g = jnp.ones((k,), F32)
    in_specs = [
        pl.BlockSpec((tm, k), lambda i, j: (i, 0)),
        pl.BlockSpec((1, k), lambda i, j: (0, 0)),
        pl.BlockSpec((k, tn), lambda i, j: (0, j)),
    ]
    args = [x, g.reshape(1, k), w]
    if res is not None:
        in_specs.append(pl.BlockSpec((tm, tn), lambda i, j: (i, j)))
        args.append(res)
    out = pl.pallas_call(
        functools.partial(_norm_matmul_body, normalize=normalize, has_res=res is not None),
        name="norm_matmul",
        grid=(t // tm, n // tn),
        in_specs=in_specs,
        out_specs=pl.BlockSpec((tm, tn), lambda i, j: (i, j)),
        out_shape=jax.ShapeDtypeStruct((t, n), F32),
        scratch_shapes=[pltpu.VMEM((tm, k), BF16)],
        compiler_params=pltpu.CompilerParams(
            dimension_semantics=("parallel", "arbitrary"),
            vmem_limit_bytes=VMEM_LIMIT_BYTES),
    )(*args)
    return out if n == n_true else out[:, :n_true]


ROUTE_LANES = 128
MOE_TILE = 1024
MOE_ROWS = 160
MOE_EXPERTS_PER_STEP = 2
NEG = -1e30


def _router_body(x_ref, g_ref, w_ref, b_ref, u_ref, xn_ref, rank_ref, wt_ref):
    x = x_ref[...]
    tm = x.shape[0]
    xn = x * lax.rsqrt(jnp.mean(x * x, axis=-1, keepdims=True) + EPS) * g_ref[...]
    xb = xn.astype(BF16)
    xn_ref[...] = xb
    logits = jnp.dot(xb, w_ref[...].astype(BF16), preferred_element_type=F32) + b_ref[...]
    lane = lax.broadcasted_iota(jnp.int32, (tm, ROUTE_LANES), 1).astype(F32)

    def first_max(mask):
        v = jnp.max(jnp.where(mask, logits, NEG), axis=-1, keepdims=True)
        i = jnp.min(jnp.where(mask & (logits == v), lane, float(ROUTE_LANES)), axis=-1, keepdims=True)
        return v, i

    is_group = lane < MOE_GROUPS
    mg, g_idx = first_max(is_group)
    pg_top = 1.0 / jnp.sum(jnp.where(is_group, jnp.exp(logits - mg), 0.0), axis=-1, keepdims=True)
    lo = MOE_GROUPS + MOE_EPG * g_idx
    in_group = (lane >= lo) & (lane < lo + MOE_EPG)
    v1, i1 = first_max(in_group)
    v2, i2 = first_max(in_group & (lane != i1))
    e2 = jnp.exp(v2 - v1)
    w_a = pg_top / (1.0 + e2)
    w_b = pg_top * e2 / (1.0 + e2)
    info = jnp.where(lane == 0, i1 - MOE_GROUPS, jnp.where(lane == 1, i2 - MOE_GROUPS,
                     jnp.where(lane == 2, w_a, jnp.where(lane == 3, w_b, 0.0))))
    info_t = info.T
    e_a, e_b, w_at, w_bt = info_t[0:1], info_t[1:2], info_t[2:3], info_t[3:4]
    expert = lax.broadcasted_iota(jnp.int32, (MOE_EXPERTS, tm), 0).astype(F32)
    m_a = e_a == expert
    m_b = e_b == expert
    onehot = jnp.concatenate([jnp.where(m_a, 1.0, 0.0), jnp.where(m_b, 1.0, 0.0)], axis=0).astype(BF16)
    cum = jnp.dot(onehot, u_ref[...], preferred_element_type=F32)
    cum_a, cum_b = cum[:MOE_EXPERTS], cum[MOE_EXPERTS:]
    n_a = cum_a[:, tm - 1:tm]
    rank_ref[...] = jnp.where(m_a, cum_a - 1.0, jnp.where(m_b, n_a + cum_b - 1.0, -1.0))
    wt_ref[...] = jnp.where(m_a, w_at, jnp.where(m_b, w_bt, 0.0))


def moe_route(x, g, w_rg, b_rg, w_re, b_re, tm):
    t = x.shape[0]
    pad = ROUTE_LANES - MOE_GROUPS - MOE_EXPERTS
    w = jnp.pad(jnp.concatenate([w_rg, w_re], axis=1), ((0, 0), (0, pad)))
    b = jnp.pad(jnp.concatenate([b_rg, b_re]), (0, pad)).reshape(1, ROUTE_LANES)
    upper = jnp.asarray(np.triu(np.ones((tm, tm), np.float32)), BF16)
    const = lambda a: pl.BlockSpec(a.shape, lambda i: (0,) * a.ndim)
    g2 = g.reshape(1, D_MODEL)
    return pl.pallas_call(
        _router_body,
        name="moe_router",
        grid=(t // tm,),
        in_specs=[pl.BlockSpec((tm, D_MODEL), lambda i: (i, 0)), const(g2), const(w), const(b), const(upper)],
        out_specs=[pl.BlockSpec((tm, D_MODEL), lambda i: (i, 0)),
                   pl.BlockSpec((MOE_EXPERTS, tm), lambda i: (0, i)),
                   pl.BlockSpec((MOE_EXPERTS, tm), lambda i: (0, i))],
        out_shape=[jax.ShapeDtypeStruct((t, D_MODEL), BF16),
                   jax.ShapeDtypeStruct((MOE_EXPERTS, t), F32),
                   jax.ShapeDtypeStruct((MOE_EXPERTS, t), F32)],
        compiler_params=pltpu.CompilerParams(
            dimension_semantics=("parallel",), vmem_limit_bytes=VMEM_LIMIT_BYTES),
    )(x, g2, w, b, upper)


def _moe_expert_body(cnt_ref, xn_ref, rank_ref, wt_ref, w1_ref, w3_ref, w2_ref, res_ref, o_ref, *, rows):
    ti, step = pl.program_id(0), pl.program_id(1)

    @pl.when(step == 0)
    def _():
        o_ref[...] = res_ref[...]

    tm = xn_ref.shape[0]
    experts = [step * MOE_EXPERTS_PER_STEP + k for k in range(MOE_EXPERTS_PER_STEP)]
    ranks = [rank_ref[pl.ds(e, 1), :] for e in experts]
    wts = [wt_ref[pl.ds(e, 1), :] for e in experts]
    n_chunks = functools.reduce(jnp.maximum, [(cnt_ref[e, ti] + rows - 1) // rows for e in experts])

    def chunk(c, carry):
        r = (lax.broadcasted_iota(jnp.int32, (rows, tm), 0) + c * rows).astype(F32)
        total = None
        for k in range(MOE_EXPERTS_PER_STEP):
            sel = jnp.where(ranks[k] == r, 1.0, 0.0)
            selb = sel.astype(BF16)
            xs = jnp.dot(selb, xn_ref[...], preferred_element_type=F32).astype(BF16)
            a = jnp.dot(xs, w1_ref[k].astype(BF16), preferred_element_type=F32)
            b = jnp.dot(xs, w3_ref[k].astype(BF16), preferred_element_type=F32)
            h = (a * jax.nn.sigmoid(a) * b * jnp.sum(sel * wts[k], axis=1, keepdims=True)).astype(BF16)
            y = jnp.dot(h, w2_ref[k].astype(BF16), preferred_element_type=F32)
            y_hi = y.astype(BF16)
            y_lo = (y - y_hi.astype(F32)).astype(BF16)
            tn = (((0,), (0,)), ((), ()))
            back = (lax.dot_general(selb, y_hi, tn, preferred_element_type=F32)
                    + lax.dot_general(selb, y_lo, tn, preferred_element_type=F32))
            total = back if total is None else total + back
        o_ref[...] += total
        return carry

    lax.fori_loop(0, n_chunks, chunk, 0)


def moe_experts(xn, rank, wt, counts, w1, w3, w2, res, tm):
    t = xn.shape[0]
    rows = min(MOE_ROWS, tm)
    grid_spec = pltpu.PrefetchScalarGridSpec(
        num_scalar_prefetch=1,
        grid=(t // tm, MOE_EXPERTS // MOE_EXPERTS_PER_STEP),
        in_specs=[
            pl.BlockSpec((tm, D_MODEL), lambda i, e, c: (i, 0)),
            pl.BlockSpec((MOE_EXPERTS, tm), lambda i, e, c: (0, i)),
            pl.BlockSpec((MOE_EXPERTS, tm), lambda i, e, c: (0, i)),
            pl.BlockSpec((MOE_EXPERTS_PER_STEP, D_MODEL, MOE_FF), lambda i, e, c: (e, 0, 0)),
            pl.BlockSpec((MOE_EXPERTS_PER_STEP, D_MODEL, MOE_FF), lambda i, e, c: (e, 0, 0)),
            pl.BlockSpec((MOE_EXPERTS_PER_STEP, MOE_FF, D_MODEL), lambda i, e, c: (e, 0, 0)),
            pl.BlockSpec((tm, D_MODEL), lambda i, e, c: (i, 0)),
        ],
        out_specs=pl.BlockSpec((tm, D_MODEL), lambda i, e, c: (i, 0)),
    )
    return pl.pallas_call(
        functools.partial(_moe_expert_body, rows=rows),
        name="moe_experts",
        grid_spec=grid_spec,
        out_shape=jax.ShapeDtypeStruct((t, D_MODEL), F32),
        compiler_params=pltpu.CompilerParams(
            dimension_semantics=("parallel", "arbitrary"),
            vmem_limit_bytes=VMEM_LIMIT_BYTES),
    )(counts, xn, rank, wt, w1, w3, w2, res)


def hier_moe_residual(x, g, w_rg, b_rg, w_re, b_re, w1, w3, w2):
    t_true = x.shape[0]
    tm = MOE_TILE if t_true % MOE_TILE == 0 else ROUTE_LANES
    if t_true % tm:
        x = jnp.pad(x, ((0, tm - t_true % tm), (0, 0)))
    t = x.shape[0]
    xn, rank, wt = moe_route(x, g, w_rg, b_rg, w_re, b_re, tm)
    counts = jnp.sum((rank >= 0).reshape(MOE_EXPERTS, t // tm, tm), axis=-1, dtype=jnp.int32)
    return moe_experts(xn, rank, wt, counts, w1, w3, w2, x, tm)[:t_true]


def _causal_dwconv(u, buf, w, b):
    L = u.shape[1]
    ext = jnp.concatenate([buf.astype(u.dtype), u], axis=1)
    out = b + sum(ext[:, k:k + L] * w[k] for k in range(M_CONV))
    return out, ext[:, L:]


def _ssd_scan(x, dt, a, bm, cm, h0):
    bsz, L = x.shape[0], x.shape[1]
    q = M_CHUNK if L % M_CHUNK == 0 else L
    nc = L // q
    hpg = M_HEADS // M_GROUPS
    xdt = (x * dt[..., None]).reshape(bsz, nc, q, M_GROUPS, hpg, M_HEADDIM)
    acum = jnp.cumsum((dt * a).reshape(bsz, nc, q, M_GROUPS, hpg), axis=2)
    bc = bm.reshape(bsz, nc, q, M_GROUPS, M_DSTATE)
    cc = cm.reshape(bsz, nc, q, M_GROUPS, M_DSTATE)
    tri = jnp.tril(jnp.ones((q, q), bool))[:, :, None, None]
    seg = acum[:, :, :, None] - acum[:, :, None]
    decay = jnp.exp(jnp.where(tri, seg, -jnp.inf))
    cb = jnp.einsum('bclgn,bcsgn->bclsg', cc, bc)
    y_diag = jnp.einsum('bclsg,bclsgh,bcsghp->bclghp', cb, decay, xdt)
    decay_end = jnp.exp(acum[:, :, -1:] - acum)
    states = jnp.einsum('bcsgn,bcsgh,bcsghp->bcghpn', bc, decay_end, xdt)
    chunk_decay = jnp.exp(acum[:, :, -1])

    def step(h, inp):
        cd, st = inp
        return cd[..., None, None] * h + st, h

    h_last, h_prev = lax.scan(step, h0.reshape(bsz, M_GROUPS, hpg, M_HEADDIM, M_DSTATE),
                              (chunk_decay.swapaxes(0, 1), states.swapaxes(0, 1)))
    y_off = jnp.einsum('bclgn,bclgh,bcghpn->bclghp', cc, jnp.exp(acum), h_prev.swapaxes(0, 1))
    y = (y_diag + y_off).reshape(bsz, L, M_HEADS, M_HEADDIM)
    return y, h_last.reshape(bsz, M_HEADS, M_HEADDIM, M_DSTATE)


def mamba_mixer(x, g, conv_buf, h0, w_in, conv_w, conv_b, dt_bias, a_log, d_skip, norm_g, w_out):
    bsz, L, _ = x.shape
    proj = norm_matmul(x.reshape(bsz * L, D_MODEL), g, w_in).reshape(bsz, L, -1)
    z = proj[..., :M_DINNER]
    xbc = proj[..., M_DINNER:M_DINNER + M_CONV_DIM]
    dt_raw = proj[..., M_DINNER + M_CONV_DIM:]
    xbc, new_buf = _causal_dwconv(xbc, conv_buf, conv_w, conv_b)
    xbc = jax.nn.silu(xbc)
    xs = xbc[..., :M_DINNER].reshape(bsz, L, M_HEADS, M_HEADDIM)
    bm = xbc[..., M_DINNER:M_DINNER + M_GN].reshape(bsz, L, M_GROUPS, M_DSTATE)
    cm = xbc[..., M_DINNER + M_GN:].reshape(bsz, L, M_GROUPS, M_DSTATE)
    dt = jax.nn.softplus(dt_raw + dt_bias)
    a = -jnp.exp(a_log)
    y, h = _ssd_scan(xs, dt, a, bm, cm, h0)
    y = y + xs * d_skip[:, None]
    y = y.reshape(bsz, L, M_DINNER)
    y = _rmsnorm(y * jax.nn.silu(z), norm_g)
    out = norm_matmul(y.reshape(bsz * L, M_DINNER), None, w_out, normalize=False)
    return out.reshape(bsz, L, D_MODEL), new_buf, h


SSD_HPG = M_HEADS // M_GROUPS
SSD_GROUP_ROWS = SSD_HPG * M_HEADDIM
CONV_PAD = 8


def _transpose_cols(x):
    return jnp.concatenate([x[:, j * LANES:(j + 1) * LANES].T for j in range(x.shape[1] // LANES)], axis=0)


def _transpose_rows(x):
    return jnp.concatenate([x[j * LANES:(j + 1) * LANES, :].T for j in range(x.shape[0] // LANES)], axis=1)


def _ssd_body(z_ref, xbc_ref, dtr_ref, cw_ref, cb_ref, dtb_ref, a_ref, dcol_ref, ng_ref, ltri_ref,
              y_ref, conv_ref, h_ref, xbuf, *, q):
    c = pl.program_id(1)

    @pl.when(c == 0)
    def _():
        h_ref[...] = jnp.zeros_like(h_ref)
        xbuf[0:CONV_PAD, :] = jnp.zeros((CONV_PAD, M_CONV_DIM), F32)

    xbuf[CONV_PAD:CONV_PAD + q, :] = xbc_ref[...]
    conv = cb_ref[...]
    for k in range(M_CONV):
        start = CONV_PAD - (M_CONV - 1) + k
        conv = conv + xbuf[start:start + q, :] * cw_ref[k:k + 1, :]
    tail = xbuf[CONV_PAD + q - (M_CONV - 1):CONV_PAD + q, :]
    xbuf[CONV_PAD - (M_CONV - 1):CONV_PAD, :] = tail
    conv_ref[0] = tail
    xc = conv * jax.nn.sigmoid(conv)
    xs = xc[:, :M_DINNER]
    xs_t = _transpose_cols(xs)

    pre = dtr_ref[...] + dtb_ref[...]
    dt = jnp.maximum(pre, 0.0) + jnp.log1p(jnp.exp(-jnp.abs(pre)))
    da = dt * a_ref[...]
    ltri = ltri_ref[...]
    acum = sum(jnp.dot(ltri, part, preferred_element_type=F32) for part in _split3(da))
    dt_t = dt.T
    acum_t = acum.T
    li = lax.broadcasted_iota(jnp.int32, (q, q), 0)
    si = lax.broadcasted_iota(jnp.int32, (q, q), 1)
    causal = li >= si

    y_t = []
    for g in range(M_GROUPS):
        bm = xc[:, M_DINNER + g * M_DSTATE:M_DINNER + (g + 1) * M_DSTATE].astype(BF16)
        cm = xc[:, M_DINNER + M_GN + g * M_DSTATE:M_DINNER + M_GN + (g + 1) * M_DSTATE].astype(BF16)
        cb = _nt_dot(cm, bm)
        r0 = g * SSD_GROUP_ROWS
        h_prev = h_ref[0, r0:r0 + SSD_GROUP_ROWS, :]
        y_off = _nt_dot(h_prev.astype(BF16), cm)
        x_dec, scale = [], []
        for hh in range(g * SSD_HPG, (g + 1) * SSD_HPG):
            a_row = acum_t[hh:hh + 1, :]
            a_col = acum[:, hh:hh + 1]
            decay = jnp.where(causal, jnp.exp(a_col - a_row), 0.0)
            m = (cb * decay).astype(BF16)
            rows = slice(hh * M_HEADDIM, (hh + 1) * M_HEADDIM)
            xs_h = xs_t[rows]
            xdt = xs_h * dt_t[hh:hh + 1, :]
            y_h = _nt_dot(xdt.astype(BF16), m)
            y_h = y_h + y_off[rows.start - r0:rows.stop - r0] * jnp.exp(a_row) + xs_h * dcol_ref[hh:hh + 1, :]
            y_t.append(y_h)
            a_last = a_row[:, q - 1:q]
            x_dec.append(xdt * jnp.exp(a_last - a_row))
            scale.append(jnp.broadcast_to(jnp.exp(a_last), (M_HEADDIM, 1)))
        upd = jnp.dot(jnp.concatenate(x_dec, axis=0).astype(BF16), bm, preferred_element_type=F32)
        h_ref[0, r0:r0 + SSD_GROUP_ROWS, :] = jnp.concatenate(scale, axis=0) * h_prev + upd

    y = _transpose_rows(jnp.concatenate(y_t, axis=0))
    zz = z_ref[...]
    yg = y * (zz * jax.nn.sigmoid(zz))
    yg = yg * lax.rsqrt(jnp.mean(yg * yg, axis=-1, keepdims=True) + EPS) * ng_ref[...]
    y_ref[...] = yg.astype(BF16)


def ssd_prompt(z, xbc, dtr, conv_w, conv_b, dt_bias, a_log, d_skip, norm_g, bsz, L):
    q = M_CHUNK
    assert L % q == 0 and q == LANES
    nc = L // q
    padl = lambda v: jnp.pad(v, (0, LANES - v.shape[0]))
    dtb = padl(dt_bias).reshape(1, LANES)
    a_row = padl(-jnp.exp(a_log)).reshape(1, LANES)
    dcol = jnp.broadcast_to(padl(d_skip).reshape(LANES, 1), (LANES, LANES))
    ltri = jnp.asarray(np.tril(np.ones((q, q), np.float32)), BF16)
    const = lambda a: pl.BlockSpec(a.shape, lambda b, c: (0,) * a.ndim)
    tok = lambda w: pl.BlockSpec((q, w), lambda b, c: (b * nc + c, 0))
    cb2 = conv_b.reshape(1, M_CONV_DIM)
    ng2 = norm_g.reshape(1, M_DINNER)
    y, conv_tail, h = pl.pallas_call(
        functools.partial(_ssd_body, q=q),
        name="ssd_chunks",
        grid=(bsz, nc),
        in_specs=[tok(M_DINNER), tok(M_CONV_DIM), tok(LANES), const(conv_w), const(cb2), const(dtb),
                  const(a_row), const(dcol), const(ng2), const(ltri)],
        out_specs=[tok(M_DINNER),
                   pl.BlockSpec((1, M_CONV - 1, M_CONV_DIM), lambda b, c: (b, 0, 0)),
                   pl.BlockSpec((1, M_DINNER, M_DSTATE), lambda b, c: (b, 0, 0))],
        out_shape=[jax.ShapeDtypeStruct((bsz * L, M_DINNER), BF16),
                   jax.ShapeDtypeStruct((bsz, M_CONV - 1, M_CONV_DIM), F32),
                   jax.ShapeDtypeStruct((bsz, M_DINNER, M_DSTATE), F32)],
        scratch_shapes=[pltpu.VMEM((CONV_PAD + q, M_CONV_DIM), F32)],
        compiler_params=pltpu.CompilerParams(
            dimension_semantics=("parallel", "arbitrary"),
            vmem_limit_bytes=VMEM_LIMIT_BYTES),
    )(z, xbc, dtr, conv_w, cb2, dtb, a_row, dcol, ng2, ltri)
    return y, conv_tail, h.reshape(bsz, M_HEADS, M_HEADDIM, M_DSTATE)


def mamba_prompt_mixer(x, g, w_in, conv_w, conv_b, dt_bias, a_log, d_skip, norm_g, w_out):
    bsz, L, _ = x.shape
    x2 = x.reshape(bsz * L, D_MODEL)
    z = norm_matmul(x2, g, w_in[:, :M_DINNER])
    xbc = norm_matmul(x2, g, w_in[:, M_DINNER:M_DINNER + M_CONV_DIM])
    dtr = norm_matmul(x2, g, jnp.pad(w_in[:, M_DINNER + M_CONV_DIM:], ((0, 0), (0, LANES - M_HEADS))))
    y, conv_tail, h = ssd_prompt(z, xbc, dtr, conv_w, conv_b, dt_bias, a_log, d_skip, norm_g, bsz, L)
    out = norm_matmul(y, None, w_out, normalize=False, res=x2)
    return out.reshape(bsz, L, D_MODEL), conv_tail, h


def _rel_bucket(dist):
    exact = REL_BUCKETS // 2
    d = jnp.maximum(dist, 0)
    ratio = jnp.log(jnp.maximum(d, 1).astype(F32) / exact) / math.log(REL_MAX_DIST / exact)
    large = jnp.minimum(exact + (ratio * (REL_BUCKETS - exact)).astype(jnp.int32), REL_BUCKETS - 1)
    return jnp.where(d < exact, d, large)


def _head_bias(rel_table, dist):
    b = rel_table[_rel_bucket(dist)].astype(F32)
    return jnp.moveaxis(b, -1, 0).reshape(KV_HEADS, HPG, dist.shape[0], dist.shape[1])


def _masked_softmax(s, valid):
    s = jnp.where(valid, s, -1e30)
    e = jnp.where(valid, jnp.exp(s - jnp.max(s, axis=-1, keepdims=True)), 0.0)
    return e / jnp.maximum(jnp.sum(e, axis=-1, keepdims=True), 1e-30)


def _group_attend(qg, qpos, k, v, kpos, valid, rel_table):
    s = jnp.einsum('bqghd,bkgd->bghqk', qg, k).astype(F32) * SCALE
    s = s + _head_bias(rel_table, qpos[:, None] - kpos[None, :])
    p = _masked_softmax(s, valid)
    o = jnp.einsum('bghqk,bkgd->bqghd', p.astype(v.dtype), v)
    return o, p


def _compress(rows, pe, w1, w2):
    bsz, T = rows.shape[0], rows.shape[1]
    nc = (T - CMP_BLOCK) // CMP_STRIDE + 1
    idx = jnp.arange(nc)[:, None] * CMP_STRIDE + jnp.arange(CMP_BLOCK)[None, :]
    blk = rows[:, idx] + pe[:, None, :]
    blk = jnp.moveaxis(blk, 2, 3).reshape(bsz, nc, KV_HEADS, CMP_BLOCK * HEAD_DIM)
    return jax.nn.silu(blk @ w1) @ w2


def _select_blocks(p_cmp, qpos, n_sel):
    nc = p_cmp.shape[-1]
    cstart = jnp.arange(nc) * CMP_STRIDE
    sstart = jnp.arange(n_sel) * SEL_BLOCK
    overlap = ((cstart[:, None] < sstart[None, :] + SEL_BLOCK) &
               (cstart[:, None] + CMP_BLOCK > sstart[None, :])).astype(F32)
    imp = jnp.einsum('bghqc,cs->bgqs', p_cmp, overlap)
    qblk = qpos // SEL_BLOCK
    j = jnp.arange(n_sel)
    forced = (j[None, :] == qblk[:, None]) | (j[None, :] == 0)
    score = jnp.where(forced, 1e9, jnp.where(j[None, :] <= qblk[:, None], imp, -1e30))
    _, idx = lax.top_k(score, min(SEL_TOPK, n_sel))
    return idx


def _sel_attend(qg, qpos, idx, kblk, vblk, rel_table):
    bsz, lq = qg.shape[0], qg.shape[1]
    bi = jnp.arange(bsz)[:, None, None, None]
    gi = jnp.arange(KV_HEADS)[None, :, None, None]
    kg = kblk[bi, gi, idx]
    vg = vblk[bi, gi, idx]
    s = jnp.einsum('bqghd,bgqkrd->bghqkr', qg, kg).astype(F32) * SCALE
    kpos = idx[..., None] * SEL_BLOCK + jnp.arange(SEL_BLOCK)
    dist = qpos[:, None, None] - kpos
    g5 = jnp.arange(KV_HEADS)[None, :, None, None, None]
    bias = rel_table.reshape(REL_BUCKETS, KV_HEADS, HPG)[_rel_bucket(dist), g5].astype(F32)
    s = s + jnp.moveaxis(bias, -1, 2)
    nk = idx.shape[-1]
    valid = (dist >= 0)[:, :, None].reshape(bsz, KV_HEADS, 1, lq, nk * SEL_BLOCK)
    p = _masked_softmax(s.reshape(bsz, KV_HEADS, HPG, lq, nk * SEL_BLOCK), valid)
    p = p.reshape(bsz, KV_HEADS, HPG, lq, nk, SEL_BLOCK)
    return jnp.einsum('bghqkr,bgqkrd->bqghd', p.astype(vg.dtype), vg)


def nsa_mixer(x, g, pos0, kv_cmp_past, kv_sel_past, kv_win_past, n_keep,
              w_in, q_g, k_g, cmp_pe, cmp_w1, cmp_w2, w_out, rel_table):
    bsz, L, _ = x.shape
    sizes = [N_HEADS * HEAD_DIM] + [KV_DIM] * 6 + [3 * N_HEADS]
    proj = norm_matmul(x.reshape(bsz * L, D_MODEL), g, w_in).reshape(bsz, L, -1)
    q, kc, vc, ks, vs, kw, vw, gl = jnp.split(proj, np.cumsum(sizes)[:-1].tolist(), axis=-1)
    q = _rmsnorm(q.reshape(bsz, L, KV_HEADS, HPG, HEAD_DIM), q_g)

    def kv(t):
        return t.reshape(bsz, L, KV_HEADS, HEAD_DIM)

    new_cmp = jnp.stack([kv(kc), kv(vc)], axis=2)
    new_sel = jnp.stack([_rmsnorm(kv(ks), k_g[1]), kv(vs)], axis=2)
    new_win = jnp.stack([_rmsnorm(kv(kw), k_g[2]), kv(vw)], axis=2)
    qpos = pos0 + jnp.arange(L)

    crows = jnp.concatenate([kv_cmp_past.astype(x.dtype), new_cmp], axis=1)
    kcmp = _rmsnorm(_compress(crows[:, :, 0], cmp_pe[0], cmp_w1[0], cmp_w2[0]), k_g[0])
    vcmp = _compress(crows[:, :, 1], cmp_pe[1], cmp_w1[1], cmp_w2[1])
    ends = jnp.arange(kcmp.shape[1]) * CMP_STRIDE + CMP_BLOCK - 1
    o_cmp, p_cmp = _group_attend(q, qpos, kcmp, vcmp, ends, qpos[:, None] >= ends[None, :], rel_table)

    srows = jnp.concatenate([kv_sel_past.astype(x.dtype), new_sel], axis=1)
    T = srows.shape[1]
    n_sel = -(-T // SEL_BLOCK)
    srows = jnp.pad(srows, ((0, 0), (0, n_sel * SEL_BLOCK - T), (0, 0), (0, 0), (0, 0)))
    blocks = srows.reshape(bsz, n_sel, SEL_BLOCK, 2, KV_HEADS, HEAD_DIM).transpose(3, 0, 4, 1, 2, 5)
    idx = _select_blocks(p_cmp, qpos, n_sel)
    qb = SEL_QBLK if L % SEL_QBLK == 0 else L
    nqb = L // qb

    def sel_block(args):
        qgi, qposi, idxi = args
        return _sel_attend(qgi, qposi, idxi, blocks[0], blocks[1], rel_table)

    o_sel = lax.map(sel_block, (q.reshape(bsz, nqb, qb, KV_HEADS, HPG, HEAD_DIM).swapaxes(0, 1),
                                qpos.reshape(nqb, qb),
                                idx.reshape(bsz, KV_HEADS, nqb, qb, -1).transpose(2, 0, 1, 3, 4)))
    o_sel = o_sel.swapaxes(0, 1).reshape(bsz, L, KV_HEADS, HPG, HEAD_DIM)

    p_win = kv_win_past.shape[1]
    wrows = jnp.concatenate([kv_win_past.astype(x.dtype), new_win], axis=1)
    wpad = jnp.pad(wrows, ((0, 0), (WINDOW, 0), (0, 0), (0, 0), (0, 0)))
    n_all = WINDOW + p_win + L
    kpos_all = pos0 - p_win - WINDOW + jnp.arange(n_all)
    kvalid_all = jnp.arange(n_all) >= WINDOW
    wq = WIN_QBLK if L % WIN_QBLK == 0 else L

    def win_block(i):
        start = p_win + i * wq
        qgi = lax.dynamic_slice_in_dim(q, i * wq, wq, axis=1)
        kvi = lax.dynamic_slice_in_dim(wpad, start, WINDOW + wq, axis=1)
        kposi = lax.dynamic_slice_in_dim(kpos_all, start, WINDOW + wq)
        kvalidi = lax.dynamic_slice_in_dim(kvalid_all, start, WINDOW + wq)
        qposi = pos0 + i * wq + jnp.arange(wq)
        dist = qposi[:, None] - kposi[None, :]
        valid = kvalidi[None, :] & (dist >= 0) & (dist <= WINDOW)
        o, _ = _group_attend(qgi, qposi, kvi[:, :, 0], kvi[:, :, 1], kposi, valid, rel_table)
        return o

    o_win = lax.map(win_block, jnp.arange(L // wq)).swapaxes(0, 1).reshape(bsz, L, KV_HEADS, HPG, HEAD_DIM)

    gate = jax.nn.sigmoid(gl).reshape(bsz, L, KV_HEADS, HPG, 3)
    o = gate[..., 0:1] * o_cmp + gate[..., 1:2] * o_sel + gate[..., 2:3] * o_win
    y = norm_matmul(o.reshape(bsz * L, N_HEADS * HEAD_DIM), None, w_out, normalize=False)
    return y.reshape(bsz, L, D_MODEL), new_cmp, new_sel, wrows[:, -n_keep:]


ATT_TQ = 128
ATT_TK = 128
NEG = -1e30


def _nt_dot(a, b):
    return lax.dot_general(a, b, (((1,), (1,)), ((), ())), preferred_element_type=F32)


def _compress_body(rk_ref, rv_ref, pe_ref, w1_ref, w2_ref, kg_ref, kc_ref, vc_ref, *, nb):
    half = (CMP_BLOCK // 2) * HEAD_DIM
    for kv, (r_ref, o_ref) in enumerate(((rk_ref, kc_ref), (rv_ref, vc_ref))):
        lo = (r_ref[0, 0, 0:nb, :] + pe_ref[kv, 0:1, :]).astype(BF16)
        hi = (r_ref[0, 0, 1:nb + 1, :] + pe_ref[kv, 1:2, :]).astype(BF16)
        h = (jnp.dot(lo, w1_ref[kv, :half, :].astype(BF16), preferred_element_type=F32)
             + jnp.dot(hi, w1_ref[kv, half:, :].astype(BF16), preferred_element_type=F32))
        h = h * jax.nn.sigmoid(h)
        o = jnp.dot(h.astype(BF16), w2_ref[kv].astype(BF16), preferred_element_type=F32)
        if kv == 0:
            o = o * lax.rsqrt(jnp.mean(o * o, axis=-1, keepdims=True) + EPS) * kg_ref[...]
        o_ref[0, 0] = o


def compress_rows(rk, rv, pe, w1, w2, kg):
    bsz, g, nbp, width = rk.shape
    nb = nbp - 8
    strip = pl.BlockSpec((1, 1, nbp, width), lambda b, j: (b, j, 0, 0))
    out = pl.BlockSpec((1, 1, nb, HEAD_DIM), lambda b, j: (b, j, 0, 0))
    full = lambda a: pl.BlockSpec(a.shape, lambda b, j: (0,) * a.ndim)
    pe2 = pe.reshape(2, 2, width)
    kg2 = kg.reshape(1, HEAD_DIM)
    return pl.pallas_call(
        functools.partial(_compress_body, nb=nb),
        name="nsa_compress",
        grid=(bsz, g),
        in_specs=[strip, strip, full(pe2), full(w1), full(w2), full(kg2)],
        out_specs=[out, out],
        out_shape=[jax.ShapeDtypeStruct((bsz, g, nb, HEAD_DIM), F32)] * 2,
        compiler_params=pltpu.CompilerParams(
            dimension_semantics=("parallel", "parallel"),
            vmem_limit_bytes=VMEM_LIMIT_BYTES),
    )(rk, rv, pe2, w1, w2, kg2)


def _split3(x):
    a = x.astype(BF16)
    r = x - a.astype(F32)
    b = r.astype(BF16)
    c = (r - b.astype(F32)).astype(BF16)
    return a, b, c


def _nsa_attn_body(q_ref, kc_ref, vc_ref, ks_ref, vs_ref, kw_ref, vw_ref, gate_ref, bcmp_ref, btile_ref,
                   ovl_ref, o_ref, s_scr, *, n_cmp, n_sel, topk):
    i = pl.program_id(2)
    q0 = i * ATT_TQ
    rows = HPG * ATT_TQ
    qt = q_ref[0]
    qs = (jnp.concatenate([qt[:, h * HEAD_DIM:(h + 1) * HEAD_DIM] for h in range(HPG)], axis=0) * SCALE).astype(BF16)
    row = lax.broadcasted_iota(jnp.int32, (rows, ATT_TK), 0)
    qpos = q0 + (row & (ATT_TQ - 1))
    col = lax.broadcasted_iota(jnp.int32, (rows, ATT_TK), 1)

    s = _nt_dot(qs, kc_ref[0, 0].astype(BF16)) + bcmp_ref[0, 0]
    valid = (qpos >= col * CMP_STRIDE + (CMP_BLOCK - 1)) & (col < n_cmp)
    s = jnp.where(valid, s, NEG)
    e = jnp.where(valid, jnp.exp(s - jnp.max(s, axis=-1, keepdims=True)), 0.0)
    p = e / jnp.maximum(jnp.sum(e, axis=-1, keepdims=True), 1e-30)
    pb = p.astype(BF16)
    o_cmp = jnp.dot(pb, vc_ref[0, 0].astype(BF16), preferred_element_type=F32)

    psum = p[0:ATT_TQ]
    for h in range(1, HPG):
        psum = psum + p[h * ATT_TQ:(h + 1) * ATT_TQ]
    ovl = ovl_ref[...]
    imp_t = _nt_dot(ovl, psum.astype(BF16))
    nblk = imp_t.shape[0]
    nrank = -(-n_sel // 8) * 8
    j = lax.broadcasted_iota(jnp.int32, (nrank, ATT_TQ), 0)
    qblk = (q0 + lax.broadcasted_iota(jnp.int32, (nrank, ATT_TQ), 1)) // SEL_BLOCK
    forced = (j == qblk) | (j == 0)
    score = jnp.where(forced, 1e9, jnp.where(j <= qblk, imp_t[:nrank], NEG))
    score = jnp.where(j < n_sel, score, -3e38)
    rank = jnp.zeros((nrank, ATT_TQ), F32)
    for jp in range(n_sel):
        other = score[jp:jp + 1, :]
        beats = (other > score) | ((other == score) & (jp < j))
        rank = rank + jnp.where(beats, 1.0, 0.0)
    sel_t = jnp.where((rank < topk) & (j < n_sel), 1.0, 0.0)
    if nrank < nblk:
        sel_t = jnp.concatenate([sel_t, jnp.zeros((nblk - nrank, ATT_TQ), F32)], axis=0)
    sel = sel_t.T.astype(BF16)

    jj = lax.broadcasted_iota(jnp.int32, (nblk, ATT_TK), 0)
    kk = lax.broadcasted_iota(jnp.int32, (nblk, ATT_TK), 1) // SEL_BLOCK
    blocks_per_step = ATT_TK // SEL_BLOCK

    def attend(k_ref, v_ref, lo, hi, penalty_fn):
        def tile(kc):
            return pl.ds(pl.multiple_of(kc * ATT_TK, ATT_TK), ATT_TK)

        def sweep(fn, init):
            n = hi - lo

            def pair(p, carry):
                kc = lo + 2 * p
                return fn(kc + 1, fn(kc, carry))

            carry = lax.fori_loop(0, lax.shift_right_logical(n, 1), pair, init)
            return lax.cond((n & 1) == 1, lambda c: fn(hi - 1, c), lambda c: c, carry)

        def scores(kc, m_run):
            kblk = k_ref[0, 0, tile(kc), :].astype(BF16)
            sc = _nt_dot(qs, kblk) + btile_ref[0, jnp.minimum(i - kc, 2)] + penalty_fn(kc)
            s_scr[:, tile(kc)] = sc
            return jnp.maximum(m_run, sc)

        m_run = sweep(scores, jnp.full((rows, ATT_TK), NEG, F32))
        m = jnp.max(m_run, axis=-1, keepdims=True)

        def exps(kc, l_run):
            e = jnp.exp(s_scr[:, tile(kc)] - m)
            s_scr[:, tile(kc)] = e
            return l_run + e

        l_run = sweep(exps, jnp.zeros((rows, ATT_TK), F32))
        inv = 1.0 / jnp.maximum(jnp.sum(l_run, axis=-1, keepdims=True), 1e-30)

        def weighted(kc, acc):
            pr = (s_scr[:, tile(kc)] * inv).astype(BF16)
            return acc + jnp.dot(pr, v_ref[0, 0, tile(kc), :].astype(BF16), preferred_element_type=F32)

        return sweep(weighted, jnp.zeros((rows, HEAD_DIM), F32))

    def sel_penalty(kc):
        expand = jnp.where(jj == kc * blocks_per_step + kk, 1.0, 0.0).astype(BF16)
        chosen = jnp.dot(sel, expand, preferred_element_type=F32)
        pen = jnp.concatenate([(chosen - 1.0) * (-NEG)] * HPG, axis=0)
        return jnp.where(qpos >= kc * ATT_TK + col, pen, NEG)

    def win_penalty(kc):
        dist = qpos - (kc * ATT_TK + col)
        return jnp.where((dist >= 0) & (dist <= WINDOW), 0.0, NEG)

    o_sel = attend(ks_ref, vs_ref, 0, i + 1, sel_penalty)
    o_win = attend(kw_ref, vw_ref, jnp.maximum(i - WINDOW // ATT_TK, 0), i + 1, win_penalty)

    gate = jax.nn.sigmoid(gate_ref[0, 0])

    def gcol(br):
        return jnp.concatenate([gate[:, h * 3 + br:h * 3 + br + 1] for h in range(HPG)], axis=0)

    o = gcol(0) * o_cmp + gcol(1) * o_sel + gcol(2) * o_win
    o_ref[0] = jnp.concatenate([o[h * ATT_TQ:(h + 1) * ATT_TQ] for h in range(HPG)], axis=1)


def _bias_tables(rel_table, n_qtiles):
    tab = rel_table[_rel_bucket(jnp.arange(REL_MAX_DIST + 1))].astype(F32)
    tab = tab.T.reshape(KV_HEADS, HPG, REL_MAX_DIST + 1)

    def skew(v, rows, width, step):
        out = jnp.broadcast_to(v[..., None, :], v.shape[:-1] + (rows, width + step))
        out = out.reshape(v.shape[:-1] + (rows * (width + step),))[..., :rows * width]
        return out.reshape(v.shape[:-1] + (rows, width))

    width = 2 * ATT_TK
    w = np.arange(width + 1)
    s_minus_t = np.where(w <= ATT_TK, w, w - (width + 1))
    d_idx = np.stack([np.clip(delta * ATT_TK - s_minus_t, 0, REL_MAX_DIST) for delta in range(3)])
    btile = skew(tab[:, :, d_idx], ATT_TQ, width, 1)[..., :ATT_TK]
    btile = btile.transpose(0, 2, 1, 3, 4).reshape(KV_HEADS, 3, HPG * ATT_TQ, ATT_TK)

    n_q = n_qtiles * ATT_TQ
    u = np.arange(n_q + CMP_STRIDE)
    v_cmp = tab[:, :, np.clip(u - (CMP_BLOCK - 1), 0, REL_MAX_DIST)]
    bcmp = skew(v_cmp, ATT_TK, n_q, CMP_STRIDE)
    bcmp = bcmp.reshape(KV_HEADS, HPG, ATT_TK, n_qtiles, ATT_TQ).transpose(3, 0, 1, 4, 2)
    return btile, bcmp.reshape(n_qtiles, KV_HEADS, HPG * ATT_TQ, ATT_TK)


def _nsa_attn_t_body(qt_ref, kc_ref, vct_ref, ks_ref, vst_ref, kw_ref, vwt_ref, gate_ref, bcmp_ref, btile_ref,
                     ovl_ref, o_ref, s_scr, *, n_cmp, n_sel, topk):
    i = pl.program_id(2)
    q0 = i * ATT_TQ
    cols = HPG * ATT_TQ
    qs = (qt_ref[0, 0, 0] * SCALE).astype(BF16)
    key = lax.broadcasted_iota(jnp.int32, (ATT_TK, cols), 0)
    qpos = q0 + (lax.broadcasted_iota(jnp.int32, (ATT_TK, cols), 1) & (ATT_TQ - 1))

    def fold(x, op):
        return op(x.reshape(ATT_TK // 8, 8, cols), axis=0)

    s = jnp.dot(kc_ref[0, 0].astype(BF16), qs, preferred_element_type=F32) + bcmp_ref[0, 0]
    valid = (qpos >= key * CMP_STRIDE + (CMP_BLOCK - 1)) & (key < n_cmp)
    s = jnp.where(valid, s, NEG)
    e = jnp.where(valid, jnp.exp(s - jnp.max(s, axis=0, keepdims=True)), 0.0)
    p = e / jnp.maximum(jnp.sum(e, axis=0, keepdims=True), 1e-30)
    o_cmp = jnp.dot(vct_ref[0, 0].astype(BF16), p.astype(BF16), preferred_element_type=F32)

    psum = p[:, 0:ATT_TQ]
    for h in range(1, HPG):
        psum = psum + p[:, h * ATT_TQ:(h + 1) * ATT_TQ]
    imp_t = jnp.dot(ovl_ref[...], psum.astype(BF16), preferred_element_type=F32)
    nblk = imp_t.shape[0]
    nrank = -(-n_sel // 8) * 8
    j = lax.broadcasted_iota(jnp.int32, (nrank, ATT_TQ), 0)
    qblk = (q0 + lax.broadcasted_iota(jnp.int32, (nrank, ATT_TQ), 1)) // SEL_BLOCK
    forced = (j == qblk) | (j == 0)
    score = jnp.where(forced, 1e9, jnp.where(j <= qblk, imp_t[:nrank], NEG))
    score = jnp.where(j < n_sel, score, -3e38)
    rank = jnp.zeros((nrank, ATT_TQ), F32)
    for jp in range(n_sel):
        other = score[jp:jp + 1, :]
        beats = (other > score) | ((other == score) & (jp < j))
        rank = rank + jnp.where(beats, 1.0, 0.0)
    sel_t = jnp.where((rank < topk) & (j < n_sel), 1.0, 0.0)
    if nrank < nblk:
        sel_t = jnp.concatenate([sel_t, jnp.zeros((nblk - nrank, ATT_TQ), F32)], axis=0)
    sel_t = sel_t.astype(BF16)

    kk = lax.broadcasted_iota(jnp.int32, (ATT_TK, nblk), 0) // SEL_BLOCK
    jj = lax.broadcasted_iota(jnp.int32, (ATT_TK, nblk), 1)
    blocks_per_step = ATT_TK // SEL_BLOCK

    def attend(k_ref, vt_ref, lo, hi, penalty_fn):
        def tile(kc):
            return pl.ds(pl.multiple_of(kc * ATT_TK, ATT_TK), ATT_TK)

        def sweep(fn, init):
            n = hi - lo

            def pair(pi, carry):
                kc = lo + 2 * pi
                return fn(kc + 1, fn(kc, carry))

            carry = lax.fori_loop(0, lax.shift_right_logical(n, 1), pair, init)
            return lax.cond((n & 1) == 1, lambda c: fn(hi - 1, c), lambda c: c, carry)

        def scores(kc, m_run):
            kblk = k_ref[0, 0, tile(kc), :].astype(BF16)
            sc = (jnp.dot(kblk, qs, preferred_element_type=F32) + btile_ref[0, jnp.minimum(i - kc, 2)]
                  + penalty_fn(kc))
            s_scr[tile(kc), :] = sc
            return jnp.maximum(m_run, fold(sc, jnp.max))

        m = jnp.max(sweep(scores, jnp.full((8, cols), NEG, F32)), axis=0, keepdims=True)

        def exps(kc, l_run):
            ex = jnp.exp(s_scr[tile(kc), :] - m)
            s_scr[tile(kc), :] = ex
            return l_run + fold(ex, jnp.sum)

        den = jnp.sum(sweep(exps, jnp.zeros((8, cols), F32)), axis=0, keepdims=True)
        inv = 1.0 / jnp.maximum(den, 1e-30)

        def weighted(kc, acc):
            pr = (s_scr[tile(kc), :] * inv).astype(BF16)
            return acc + jnp.dot(vt_ref[0, 0, :, tile(kc)].astype(BF16), pr, preferred_element_type=F32)

        return sweep(weighted, jnp.zeros((HEAD_DIM, cols), F32))

    def sel_penalty(kc):
        expand = jnp.where(jj == kc * blocks_per_step + kk, 1.0, 0.0).astype(BF16)
        chosen = jnp.dot(expand, sel_t, preferred_element_type=F32)
        pen = jnp.concatenate([(chosen - 1.0) * (-NEG)] * HPG, axis=1)
        return jnp.where(qpos >= kc * ATT_TK + key, pen, NEG)

    def win_penalty(kc):
        dist = qpos - (kc * ATT_TK + key)
        return jnp.where((dist >= 0) & (dist <= WINDOW), 0.0, NEG)

    o_sel = attend(ks_ref, vst_ref, 0, i + 1, sel_penalty)
    o_win = attend(kw_ref, vwt_ref, jnp.maximum(i - WINDOW // ATT_TK, 0), i + 1, win_penalty)
    gate = jax.nn.sigmoid(gate_ref[0, 0, 0])
    o_ref[0, 0, 0] = gate[0:1] * o_cmp + gate[1:2] * o_sel + gate[2:3] * o_win


def nsa_prompt_attention_t(q, kcmp, vcmp, ks, vs_t, kw, vw_t, gl, rel_table):
    bsz, L = q.shape[0], q.shape[1]
    assert L % ATT_TQ == 0 and kcmp.shape[2] == ATT_TK
    assert math.frexp(SCALE)[0] == 0.5, "the kernel folds SCALE into q, exact only for powers of two"
    nq = L // ATT_TQ
    n_cmp = (L - CMP_BLOCK) // CMP_STRIDE + 1
    n_sel = L // SEL_BLOCK
    assert n_sel <= ATT_TK
    cols = HPG * ATT_TQ
    btile, bcmp = _bias_tables(rel_table, nq)
    btile = btile.transpose(0, 1, 3, 2)
    bcmp = bcmp.transpose(0, 1, 3, 2)
    c = np.arange(ATT_TK)[None, :] * CMP_STRIDE
    sb = np.arange(ATT_TK)[:, None] * SEL_BLOCK
    ovl = ((c < sb + SEL_BLOCK) & (c + CMP_BLOCK > sb) & (np.arange(ATT_TK)[None, :] < n_cmp)
           & (np.arange(ATT_TK)[:, None] < n_sel))
    ovl = jnp.asarray(ovl, BF16)
    q_t = q.reshape(bsz, nq, ATT_TQ, KV_HEADS, HPG, HEAD_DIM).transpose(0, 3, 1, 5, 4, 2)
    q_t = q_t.reshape(bsz, KV_HEADS, nq, HEAD_DIM, cols)
    gate_t = gl.reshape(bsz, nq, ATT_TQ, KV_HEADS, HPG, 3).transpose(0, 3, 1, 5, 4, 2)
    gate_t = gate_t.reshape(bsz, KV_HEADS, nq, 3, cols)
    vc_t = vcmp.transpose(0, 1, 3, 2)
    rows_spec = lambda n: pl.BlockSpec((1, 1, n, HEAD_DIM), lambda b, g, i: (b, g, 0, 0))
    cols_spec = lambda n: pl.BlockSpec((1, 1, HEAD_DIM, n), lambda b, g, i: (b, g, 0, 0))
    tile_spec = lambda r: pl.BlockSpec((1, 1, 1, r, cols), lambda b, g, i: (b, g, i, 0, 0))
    o_t = pl.pallas_call(
        functools.partial(_nsa_attn_t_body, n_cmp=n_cmp, n_sel=n_sel, topk=min(SEL_TOPK, n_sel)),
        name="nsa_attention",
        grid=(bsz, KV_HEADS, nq),
        in_specs=[
            tile_spec(HEAD_DIM),
            rows_spec(ATT_TK), cols_spec(ATT_TK), rows_spec(L), cols_spec(L), rows_spec(L), cols_spec(L),
            tile_spec(3),
            pl.BlockSpec((1, 1, ATT_TK, cols), lambda b, g, i: (i, g, 0, 0)),
            pl.BlockSpec((1, 3, ATT_TK, cols), lambda b, g, i: (g, 0, 0, 0)),
            pl.BlockSpec((ATT_TK, ATT_TK), lambda b, g, i: (0, 0)),
        ],
        out_specs=tile_spec(HEAD_DIM),
        out_shape=jax.ShapeDtypeStruct((bsz, KV_HEADS, nq, HEAD_DIM, cols), F32),
        scratch_shapes=[pltpu.VMEM((L, cols), F32)],
        compiler_params=pltpu.CompilerParams(
            dimension_semantics=("parallel", "parallel", "arbitrary"),
            vmem_limit_bytes=VMEM_LIMIT_BYTES),
    )(q_t, kcmp, vc_t, ks, vs_t, kw, vw_t, gate_t, bcmp, btile, ovl)
    o = o_t.reshape(bsz, KV_HEADS, nq, HEAD_DIM, HPG, ATT_TQ).transpose(0, 2, 5, 1, 4, 3)
    return o.reshape(bsz, L, N_HEADS * HEAD_DIM)


def nsa_prompt_mixer(x, g, w_in, q_g, k_g, cmp_pe, cmp_w1, cmp_w2, w_out, rel_table):
    bsz, L, _ = x.shape
    t = bsz * L
    proj = norm_matmul(x.reshape(t, D_MODEL), g, w_in)
    c0 = N_HEADS * HEAD_DIM
    q = _rmsnorm(proj[:, :c0].reshape(bsz, L, N_HEADS, HEAD_DIM), q_g)

    def kvpair(k, normed_g):
        kcols = proj[:, c0 + 2 * k * KV_DIM:c0 + (2 * k + 1) * KV_DIM]
        vcols = proj[:, c0 + (2 * k + 1) * KV_DIM:c0 + (2 * k + 2) * KV_DIM]
        if normed_g is not None:
            kcols = _rmsnorm(kcols.reshape(t, KV_HEADS, HEAD_DIM), normed_g).reshape(t, KV_DIM)
        rows = jnp.stack([kcols, vcols], axis=1).reshape(bsz, L, 2, KV_HEADS, HEAD_DIM)
        k4 = kcols.reshape(bsz, L, KV_HEADS, HEAD_DIM).transpose(0, 2, 1, 3)
        v4 = vcols.reshape(bsz, L, KV_HEADS, HEAD_DIM)
        return rows, k4, v4

    new_cmp, kc_rows, vc_rows = kvpair(0, None)
    new_sel, ks, vs = kvpair(1, k_g[1])
    new_win, kw, vw = kvpair(2, k_g[2])
    gl = proj[:, c0 + 6 * KV_DIM:].reshape(bsz, L, KV_HEADS, HPG, 3)
    vc_rows = vc_rows.transpose(0, 2, 1, 3)
    vs_t, vw_t = vs.transpose(0, 2, 3, 1), vw.transpose(0, 2, 3, 1)

    nb = L // CMP_STRIDE

    def strips(r):
        r = r.reshape(bsz, KV_HEADS, nb, CMP_STRIDE * HEAD_DIM)
        return jnp.pad(r, ((0, 0), (0, 0), (0, ATT_TK + 8 - nb), (0, 0)))

    kcmp, vcmp = compress_rows(strips(kc_rows), strips(vc_rows), cmp_pe, cmp_w1, cmp_w2, k_g[0])
    o = nsa_prompt_attention_t(q, kcmp, vcmp, ks, vs_t, kw, vw_t, gl, rel_table)
    y = norm_matmul(o.reshape(t, c0), None, w_out, normalize=False, res=x.reshape(t, D_MODEL))
    return y.reshape(bsz, L, D_MODEL), new_cmp, new_sel, new_win[:, -min(WINDOW, L):]


DEC_PAGES_PER_STEP = 8
STRIP = CMP_STRIDE
ROW_LANES = 2 * KV_DIM
SEL_LANES = 256


def _group_rmsnorm(x, gain_row):
    lane = lax.broadcasted_iota(jnp.int32, x.shape, 1) // HEAD_DIM
    sq = x * x
    ms = jnp.zeros_like(x)
    for grp in range(KV_HEADS):
        tot = jnp.sum(jnp.where(lane == grp, sq, 0.0), axis=-1, keepdims=True)
        ms = jnp.where(lane == grp, tot, ms)
    return x * lax.rsqrt(ms / HEAD_DIM + EPS) * gain_row


def _decode_cmp_body(pt_ref, *refs, n_strips, n_sel, qblk, topk):
    pages = refs[:DEC_PAGES_PER_STEP]
    (bd_ref, peterm_ref, w2bd_ref, kg_ref, qbd_ref, bias_ref, ovl_ref, upper_ref,
     ocmp_ref, idx_ref, seq, tbuf, hibuf) = refs[DEC_PAGES_PER_STEP:]
    s = pl.program_id(1)
    strips_per_page = pages[0].shape[2] // STRIP
    for r in range(DEC_PAGES_PER_STEP):
        row0 = pl.multiple_of((s * DEC_PAGES_PER_STEP + r) * strips_per_page, strips_per_page)
        for c in range(ROW_LANES // LANES):
            lanes = slice(c * LANES, (c + 1) * LANES)
            tbuf[c] = pages[r][0, lanes, :].T
            for l in range(STRIP):
                seq[l, pl.ds(row0, strips_per_page), lanes] = tbuf[c, pl.ds(l, strips_per_page, stride=STRIP), :]

    @pl.when(s == pl.num_programs(1) - 1)
    def _():
        n_cmp = n_strips - 1
        summaries = []
        for kv in range(2):
            halves = []
            for half in range(2):
                acc = jnp.zeros((n_strips, KV_DIM), F32)
                for l in range(STRIP):
                    w_idx = (kv * 2 + half) * STRIP + l
                    xl = (seq[l, :, kv * KV_DIM:(kv + 1) * KV_DIM] + peterm_ref[w_idx:w_idx + 1, :]).astype(BF16)
                    acc = acc + jnp.dot(xl, bd_ref[w_idx], preferred_element_type=F32)
                halves.append(acc)
            hibuf[0:n_strips, :] = halves[1]
            hibuf[n_strips:n_strips + 8, :] = jnp.zeros((8, KV_DIM), F32)
            h = halves[0] + hibuf[1:n_strips + 1, :]
            h = (h * jax.nn.sigmoid(h)).astype(BF16)
            o = jnp.dot(h, w2bd_ref[kv], preferred_element_type=F32)
            if kv == 0:
                o = _group_rmsnorm(o, kg_ref[...])
            summaries.append(o.astype(BF16))
        kcmp, vcmp = summaries

        qbd = qbd_ref[0].astype(BF16)
        sc = _nt_dot(qbd, kcmp) * SCALE + bias_ref[...]
        col = lax.broadcasted_iota(jnp.int32, sc.shape, 1)
        valid = col < n_cmp
        sc = jnp.where(valid, sc, NEG)
        e = jnp.where(valid, jnp.exp(sc - jnp.max(sc, axis=-1, keepdims=True)), 0.0)
        p = e / jnp.maximum(jnp.sum(e, axis=-1, keepdims=True), 1e-30)
        pb = p.astype(BF16)
        o_cmp = jnp.dot(pb, vcmp, preferred_element_type=F32)
        head_grp = lax.broadcasted_iota(jnp.int32, o_cmp.shape, 0) // HPG
        lane_grp = lax.broadcasted_iota(jnp.int32, o_cmp.shape, 1) // HEAD_DIM
        ocmp_ref[0] = jnp.where(head_grp == lane_grp, o_cmp, 0.0)

        psum = jnp.concatenate(
            [jnp.sum(p[grp * HPG:(grp + 1) * HPG], axis=0, keepdims=True) for grp in range(KV_HEADS)]
            + [jnp.zeros((8 - KV_HEADS, n_strips), F32)], axis=0)
        imp = jnp.dot(psum.astype(BF16), ovl_ref[...], preferred_element_type=F32)
        j = lax.broadcasted_iota(jnp.int32, imp.shape, 1)
        forced = (j == qblk) | (j == 0)
        score = jnp.where(forced, 1e9, jnp.where(j <= qblk, imp, NEG))
        score = jnp.where(j < n_sel, score, -3e38)
        rank = jnp.zeros(imp.shape, F32)
        for jp in range(n_sel):
            other = score[:, jp:jp + 1]
            beats = (other > score) | ((other == score) & (jp < j))
            rank = rank + jnp.where(beats, 1.0, 0.0)
        chosen = (rank < topk) & (j < n_sel)
        cum = jnp.dot(jnp.where(chosen, 1.0, 0.0).astype(BF16), upper_ref[...], preferred_element_type=F32)
        jf = j.astype(F32)
        out_lane = lax.broadcasted_iota(jnp.int32, (8, LANES), 1)
        out = jnp.zeros((8, LANES), F32)
        for k in range(topk):
            pick = jnp.sum(jnp.where(chosen & (cum == k + 1.0), jf, 0.0), axis=-1, keepdims=True)
            out = jnp.where(out_lane == k, pick, out)
        idx_ref[0] = out


def _decode_attn_body(tbl_ref, idx_ref, *refs, n_blk, qblk, per_page):
    pages = refs[:n_blk]
    (win_ref, q_ref, knew_ref, vnew_ref, wknew_ref, wvnew_ref, bsel_ref, bwin_ref, ocmp_ref, gate_ref,
     o_ref) = refs[n_blk:]
    b, g = pl.program_id(0), pl.program_id(1)
    qb = q_ref[0, 0].astype(BF16)
    qf = qb.astype(F32)
    k_rows = pl.ds(pl.multiple_of(g * HEAD_DIM, HEAD_DIM), HEAD_DIM)
    v_rows = pl.ds(pl.multiple_of(KV_DIM + g * HEAD_DIM, HEAD_DIM), HEAD_DIM)

    def attend(parts, k_new, v_new, bias_new):
        s_new = jnp.sum(qf * k_new.astype(BF16).astype(F32), axis=-1, keepdims=True) * SCALE + bias_new
        m = s_new
        for sc, ok, _ in parts:
            m = jnp.maximum(m, jnp.max(jnp.where(ok, sc, NEG), axis=-1, keepdims=True))
        e_new = jnp.exp(s_new - m)
        es = [jnp.where(ok, jnp.exp(sc - m), 0.0) for sc, ok, _ in parts]
        den = e_new
        for e in es:
            den = den + jnp.sum(e, axis=-1, keepdims=True)
        den = jnp.maximum(den, 1e-30)
        acc = (e_new / den).astype(BF16).astype(F32) * v_new.astype(BF16).astype(F32)
        for e, (_, _, v_t) in zip(es, parts):
            acc = acc + _nt_dot((e / den).astype(BF16), v_t)
        return acc

    sel_parts = []
    for k in range(n_blk):
        bidx = idx_ref[b, g, k]
        k_t = pages[k][0, k_rows, :].astype(BF16)
        v_t = pages[k][0, v_rows, :].astype(BF16)
        pg = jnp.minimum(bidx, qblk) // per_page
        sc = jnp.dot(qb, k_t, preferred_element_type=F32) * SCALE + bsel_ref[pg, 0]
        lane_blk = lax.broadcasted_iota(jnp.int32, sc.shape, 1) // SEL_BLOCK
        ok = (lane_blk == bidx % per_page) & (bidx < qblk)
        sel_parts.append((sc, ok, v_t))
    bias0 = bsel_ref[qblk // per_page, 0][:, 0:1]
    o_sel = attend(sel_parts, knew_ref[0, 0], vnew_ref[0, 0], bias0)

    sc = jnp.dot(qb, win_ref[0, k_rows, :].astype(BF16), preferred_element_type=F32) * SCALE + bwin_ref[0]
    o_win = attend([(sc, jnp.full(sc.shape, True), win_ref[0, v_rows, :].astype(BF16))],
                   wknew_ref[0, 0], wvnew_ref[0, 0], bias0)

    gate = jax.nn.sigmoid(gate_ref[0, 0])
    o_ref[0, 0] = gate[:, 0:1] * ocmp_ref[0, 0] + gate[:, 1:2] * o_sel + gate[:, 2:3] * o_win


def nsa_decode_mixer(x, g, past_len, cache_cmp, cache_sel, win_past, page_table,
                     w_in, q_g, k_g, cmp_pe, cmp_w1, cmp_w2, w_out, rel_table):
    bsz = x.shape[0]
    n_pool, page = cache_cmp.shape[0], cache_cmp.shape[1]
    n_pages = page_table.shape[1]
    assert past_len == n_pages * page and page % STRIP == 0 and n_pages % DEC_PAGES_PER_STEP == 0
    assert win_past.shape[1] == WINDOW and past_len >= WINDOW and page % SEL_BLOCK == 0
    x2 = x.reshape(bsz, D_MODEL)
    proj = norm_matmul(x2, g, w_in)
    c0 = N_HEADS * HEAD_DIM
    q = _rmsnorm(proj[:, :c0].reshape(bsz, N_HEADS, HEAD_DIM), q_g)

    def rows(k, gain):
        kcols = proj[:, c0 + 2 * k * KV_DIM:c0 + (2 * k + 1) * KV_DIM]
        vcols = proj[:, c0 + (2 * k + 1) * KV_DIM:c0 + (2 * k + 2) * KV_DIM]
        if gain is not None:
            kcols = _rmsnorm(kcols.reshape(bsz, KV_HEADS, HEAD_DIM), gain).reshape(bsz, KV_DIM)
        return kcols, vcols

    kc_new, vc_new = rows(0, None)
    ks_new, vs_new = rows(1, k_g[1])
    kw_new, vw_new = rows(2, k_g[2])
    gl = proj[:, c0 + 6 * KV_DIM:].reshape(bsz, KV_HEADS, HPG, 3)
    as_row = lambda kk, vv: jnp.stack([kk, vv], axis=1).reshape(bsz, 1, 2, KV_HEADS, HEAD_DIM)
    new_cmp, new_sel, new_win = as_row(kc_new, vc_new), as_row(ks_new, vs_new), as_row(kw_new, vw_new)

    eye = jnp.eye(KV_HEADS, dtype=F32)
    qbd = jnp.einsum('bghd,gk->bghkd', q.reshape(bsz, KV_HEADS, HPG, HEAD_DIM), eye).reshape(bsz, N_HEADS, KV_DIM)
    w1r = cmp_w1.reshape(2, 2, STRIP, HEAD_DIM, HEAD_DIM)
    bd = jnp.einsum('khlio,gj->khlgijo', w1r, eye).reshape(2 * 2 * STRIP, KV_DIM, KV_DIM).astype(BF16)
    w2bd = jnp.einsum('kio,gj->kgijo', cmp_w2, eye).reshape(2, KV_DIM, KV_DIM).astype(BF16)
    peterm = jnp.tile(cmp_pe.reshape(2 * 2 * STRIP, HEAD_DIM), (1, KV_HEADS))
    kg_row = jnp.tile(k_g[0], KV_HEADS).reshape(1, KV_DIM)

    n_strips = past_len // STRIP
    n_cmp = n_strips - 1
    qblk = past_len // SEL_BLOCK
    n_sel = qblk + 1
    topk = min(SEL_TOPK, n_sel)
    assert n_sel <= SEL_LANES and n_strips % 8 == 0
    tab = rel_table[_rel_bucket(jnp.arange(REL_MAX_DIST + 1))].astype(F32).T
    ends = np.arange(n_strips) * CMP_STRIDE + CMP_BLOCK - 1
    bias_cmp = tab[:, np.clip(past_len - ends, 0, REL_MAX_DIST)]
    cs = np.arange(n_strips)[:, None] * CMP_STRIDE
    ss = np.arange(SEL_LANES)[None, :] * SEL_BLOCK
    ovl = (cs < ss + SEL_BLOCK) & (cs + CMP_BLOCK > ss) & (np.arange(n_strips)[:, None] < n_cmp) \
        & (np.arange(SEL_LANES)[None, :] < n_sel)
    ovl = jnp.asarray(ovl, BF16)
    upper = jnp.asarray(np.triu(np.ones((SEL_LANES, SEL_LANES), np.float32)), BF16)

    strips_per_page = page // STRIP
    cmp_view = cache_cmp.reshape(n_pool, page, ROW_LANES).transpose(0, 2, 1)
    steps = n_pages // DEC_PAGES_PER_STEP
    const = lambda a: pl.BlockSpec(a.shape, lambda b, s, pt: (0,) * a.ndim)
    page_spec = lambda r: pl.BlockSpec((1, ROW_LANES, page),
                                       lambda b, s, pt: (pt[b, s * DEC_PAGES_PER_STEP + r], 0, 0))
    o_cmp, idx = pl.pallas_call(
        functools.partial(_decode_cmp_body, n_strips=n_strips, n_sel=n_sel, qblk=qblk, topk=topk),
        name="nsa_decode_cmp",
        grid_spec=pltpu.PrefetchScalarGridSpec(
            num_scalar_prefetch=1,
            grid=(bsz, steps),
            in_specs=[page_spec(r) for r in range(DEC_PAGES_PER_STEP)]
            + [const(bd), const(peterm), const(w2bd), const(kg_row),
               pl.BlockSpec((1, N_HEADS, KV_DIM), lambda b, s, pt: (b, 0, 0)),
               const(bias_cmp), const(ovl), const(upper)],
            out_specs=[pl.BlockSpec((1, N_HEADS, KV_DIM), lambda b, s, pt: (b, 0, 0)),
                       pl.BlockSpec((1, 8, LANES), lambda b, s, pt: (b, 0, 0))],
            scratch_shapes=[pltpu.VMEM((STRIP, n_strips, ROW_LANES), F32),
                            pltpu.VMEM((ROW_LANES // LANES, page, LANES), F32),
                            pltpu.VMEM((n_strips + 8, KV_DIM), F32)],
        ),
        out_shape=[jax.ShapeDtypeStruct((bsz, N_HEADS, KV_DIM), F32),
                   jax.ShapeDtypeStruct((bsz, 8, LANES), F32)],
        compiler_params=pltpu.CompilerParams(
            dimension_semantics=("parallel", "arbitrary"),
            vmem_limit_bytes=DECODE_VMEM_LIMIT_BYTES),
    )(page_table, *([cmp_view] * DEC_PAGES_PER_STEP), bd, peterm, w2bd, kg_row, qbd, bias_cmp, ovl, upper)

    blk_idx = idx[:, :KV_HEADS, :topk].astype(jnp.int32)
    per_page = page // SEL_BLOCK
    safe = jnp.minimum(blk_idx, qblk - 1)
    page_of = jnp.take_along_axis(page_table, (safe // per_page).reshape(bsz, -1), axis=1).reshape(safe.shape)
    sel_view = cache_sel.reshape(n_pool, page, ROW_LANES).transpose(0, 2, 1)
    win_view = win_past.reshape(bsz, WINDOW, ROW_LANES).transpose(0, 2, 1)

    n_pg = n_pages + 1
    dist_sel = past_len - (np.arange(n_pg)[:, None] * page + np.arange(page)[None, :])
    bsel = tab[:, np.clip(dist_sel, 0, REL_MAX_DIST)]
    bsel = bsel.reshape(KV_HEADS, HPG, n_pg, page).transpose(2, 0, 1, 3)
    bwin = tab[:, np.clip(past_len - (past_len - WINDOW + np.arange(WINDOW)), 0, REL_MAX_DIST)]
    bwin = bwin.reshape(KV_HEADS, HPG, WINDOW)

    per_group = lambda a: a.reshape(bsz, KV_HEADS, 1, HEAD_DIM)
    q4 = q.reshape(bsz, KV_HEADS, HPG, HEAD_DIM)
    ocmp4 = o_cmp.reshape(bsz, KV_HEADS, HPG, KV_HEADS, HEAD_DIM).sum(axis=3)
    page_spec2 = lambda k: pl.BlockSpec((1, ROW_LANES, page), lambda b, g, t, i: (t[b, g, k], 0, 0))
    per_bg = lambda *shape: pl.BlockSpec((1, 1) + shape, lambda b, g, t, i: (b, g) + (0,) * len(shape))
    o4 = pl.pallas_call(
        functools.partial(_decode_attn_body, n_blk=topk, qblk=qblk, per_page=per_page),
        name="nsa_decode_attn",
        grid_spec=pltpu.PrefetchScalarGridSpec(
            num_scalar_prefetch=2,
            grid=(bsz, KV_HEADS),
            in_specs=[page_spec2(k) for k in range(topk)]
            + [pl.BlockSpec((1, ROW_LANES, WINDOW), lambda b, g, t, i: (b, 0, 0)),
               per_bg(HPG, HEAD_DIM), per_bg(1, HEAD_DIM), per_bg(1, HEAD_DIM), per_bg(1, HEAD_DIM),
               per_bg(1, HEAD_DIM),
               pl.BlockSpec((n_pg, 1, HPG, page), lambda b, g, t, i: (0, g, 0, 0)),
               pl.BlockSpec((1, HPG, WINDOW), lambda b, g, t, i: (g, 0, 0)),
               per_bg(HPG, HEAD_DIM), per_bg(HPG, 3)],
            out_specs=per_bg(HPG, HEAD_DIM),
        ),
        out_shape=jax.ShapeDtypeStruct((bsz, KV_HEADS, HPG, HEAD_DIM), F32),
        compiler_params=pltpu.CompilerParams(
            dimension_semantics=("parallel", "arbitrary"),
            vmem_limit_bytes=VMEM_LIMIT_BYTES),
    )(page_of, blk_idx, *([sel_view] * topk), win_view, q4, per_group(ks_new), per_group(vs_new),
      per_group(kw_new), per_group(vw_new), bsel, bwin, ocmp4, gl)
    y = norm_matmul(o4.reshape(bsz, c0), None, w_out, normalize=False, res=x2)
    new_win_buf = jnp.concatenate([win_past[:, 1:], new_win], axis=1)
    return y.reshape(x.shape), new_cmp, new_sel, new_win_buf


def _gla_chunked(q, k, v, logf, s0):
    bsz, L = q.shape[0], q.shape[1]
    nc = -(-L // H_CHUNK)
    pad = nc * H_CHUNK - L

    def prep(t):
        t = jnp.pad(t, ((0, 0), (0, pad), (0, 0), (0, 0)))
        return t.reshape(bsz, nc, H_CHUNK, H_HEADS, t.shape[-1])

    q, k, v, logf = prep(q), prep(k), prep(v), prep(logf)
    acum = jnp.cumsum(logf, axis=2)
    alast = acum[:, :, -1:]
    qe = q * jnp.exp(acum)
    ke = k * jnp.exp(-acum)
    kd = k * jnp.exp(alast - acum)
    tri = jnp.tril(jnp.ones((H_CHUNK, H_CHUNK), bool))
    att = jnp.where(tri, jnp.einsum('bcthk,bcshk->bchts', qe, ke), 0.0)
    o_intra = jnp.einsum('bchts,bcshv->bcthv', att, v)
    upd = jnp.einsum('bcshk,bcshv->bchkv', kd, v)

    def step(s, inp):
        dec, up = inp
        return dec[..., None] * s + up, s

    s_last, s_prev = lax.scan(step, s0, (jnp.exp(alast[:, :, 0]).swapaxes(0, 1), upd.swapaxes(0, 1)))
    o_inter = jnp.einsum('bcthk,bchkv->bcthv', qe, s_prev.swapaxes(0, 1))
    o = (o_intra + o_inter).reshape(bsz, nc * H_CHUNK, H_HEADS, H_DV)[:, :L]
    return o, s_last


HGRN_TOKENS = 128


def _hgrn_body(q_ref, f_ref, v_ref, gate_ref, lb_ref, ng_ref, lcum_ref, lsum_ref, o_ref, s_ref):
    @pl.when(pl.program_id(1) == 0)
    def _():
        s_ref[...] = jnp.zeros_like(s_ref)

    n = HGRN_TOKENS
    qr = q_ref[...]
    q = qr * jax.nn.sigmoid(qr)
    lb = lb_ref[...]
    forget = lb + (1.0 - lb) * jax.nn.sigmoid(f_ref[...])
    logf = jnp.log(forget)
    k = 1.0 - forget
    parts = _split3(logf)
    acum = sum(jnp.dot(lcum_ref[...], p, preferred_element_type=F32) for p in parts)
    atot = sum(jnp.dot(lsum_ref[...], p, preferred_element_type=F32) for p in parts)
    qe = (q * jnp.exp(acum)).astype(BF16)
    ke = (k * jnp.exp(-acum)).astype(BF16)
    kd = (k * jnp.exp(atot - acum)).astype(BF16)
    vb = v_ref[...].astype(BF16)
    ti = lax.broadcasted_iota(jnp.int32, (n, n), 0)
    si = lax.broadcasted_iota(jnp.int32, (n, n), 1)
    intra = (ti // H_CHUNK == si // H_CHUNK) & (ti >= si)
    gate = gate_ref[...]
    gate = gate * jax.nn.sigmoid(gate)
    outs = []
    for h in range(H_HEADS):
        cols = slice(h * H_DK, (h + 1) * H_DK)
        att = jnp.where(intra, _nt_dot(qe[:, cols], ke[:, cols]), 0.0)
        o_h = jnp.dot(att.astype(BF16), vb[:, cols], preferred_element_type=F32)
        decay_t = jnp.exp(atot[:, cols]).T
        state = s_ref[0, cols, :]
        inter = []
        for j in range(n // H_CHUNK):
            rows = slice(j * H_CHUNK, (j + 1) * H_CHUNK)
            inter.append(jnp.dot(qe[rows, cols], state.astype(BF16), preferred_element_type=F32))
            upd = lax.dot_general(kd[rows, cols], vb[rows, cols], (((0,), (0,)), ((), ())),
                                  preferred_element_type=F32)
            state = decay_t[:, j * H_CHUNK:j * H_CHUNK + 1] * state + upd
        s_ref[0, cols, :] = state
        o_h = o_h + jnp.concatenate(inter, axis=0)
        o_h = o_h * lax.rsqrt(jnp.mean(o_h * o_h, axis=-1, keepdims=True) + EPS) * ng_ref[...]
        outs.append(o_h * gate[:, cols])
    o_ref[...] = jnp.concatenate(outs, axis=1).astype(BF16)


def hgrn2_prompt_mixer(x, g, lb, w_in, norm_g, w_out):
    bsz, L, _ = x.shape
    n = HGRN_TOKENS
    assert L % n == 0 and n % H_CHUNK == 0 and H_DK == LANES and H_DV == LANES
    t = bsz * L
    x2 = x.reshape(t, D_MODEL)
    proj = norm_matmul(x2, g, w_in)
    nb = L // n
    r = np.arange(n)
    same = (r[:, None] // H_CHUNK) == (r[None, :] // H_CHUNK)
    lcum = jnp.asarray(same & (r[None, :] <= r[:, None]), BF16)
    lsum = jnp.asarray(same, BF16)
    col = lambda kk: pl.BlockSpec((n, D_MODEL), lambda b, c: (b * nb + c, kk))
    const = lambda a: pl.BlockSpec(a.shape, lambda b, c: (0,) * a.ndim)
    lb2 = lb.reshape(1, D_MODEL)
    ng2 = norm_g.reshape(1, H_DV)
    o, s = pl.pallas_call(
        _hgrn_body,
        name="hgrn_blocks",
        grid=(bsz, nb),
        in_specs=[col(0), col(1), col(2), col(3), const(lb2), const(ng2), const(lcum), const(lsum)],
        out_specs=[pl.BlockSpec((n, D_MODEL), lambda b, c: (b * nb + c, 0)),
                   pl.BlockSpec((1, H_HEADS * H_DK, H_DV), lambda b, c: (b, 0, 0))],
        out_shape=[jax.ShapeDtypeStruct((t, D_MODEL), BF16),
                   jax.ShapeDtypeStruct((bsz, H_HEADS * H_DK, H_DV), F32)],
        compiler_params=pltpu.CompilerParams(
            dimension_semantics=("parallel", "arbitrary"),
            vmem_limit_bytes=VMEM_LIMIT_BYTES),
    )(proj, proj, proj, proj, lb2, ng2, lcum, lsum)
    y = norm_matmul(o, None, w_out, normalize=False, res=x2)
    return y.reshape(bsz, L, D_MODEL), s.reshape(bsz, H_HEADS, H_DK, H_DV)


def hgrn2_mixer(x, g, s0, lb, w_in, norm_g, w_out):
    bsz, L, _ = x.shape
    proj = norm_matmul(x.reshape(bsz * L, D_MODEL), g, w_in).reshape(bsz, L, -1)
    q, fr, iv, gate = jnp.split(proj, 4, axis=-1)
    q = jax.nn.silu(q).reshape(bsz, L, H_HEADS, H_DK)
    forget = lb + (1.0 - lb) * jax.nn.sigmoid(fr)
    logf = jnp.log(forget).reshape(bsz, L, H_HEADS, H_DK)
    k = (1.0 - forget).reshape(bsz, L, H_HEADS, H_DK)
    v = iv.reshape(bsz, L, H_HEADS, H_DV)
    o, s = _gla_chunked(q, k, v, logf, s0)
    o = _rmsnorm(o, norm_g) * jax.nn.silu(gate.reshape(bsz, L, H_HEADS, H_DV))
    y = norm_matmul(o.reshape(bsz * L, D_MODEL), None, w_out, normalize=False)
    return y.reshape(bsz, L, D_MODEL), s


def _gather_pages(cache, page_table):
    pages = cache[page_table]
    b, n, p = pages.shape[0], pages.shape[1], pages.shape[2]
    return pages.reshape(b, n * p, *pages.shape[3:])


def kernel(x_prompt, x_sample, state_ssm, state_conv, cache_kv_cmp, cache_kv_sel, cache_kv_win, state_hgrn, page_table, norm_g, rel_table, m_w_in, m_conv_w, m_conv_b, m_dt_bias, m_a_log, m_d, m_norm_g, m_w_out, n_w_in, n_q_g, n_k_g, n_cmp_pe, n_cmp_w1, n_cmp_w2, n_w_out, h_w_in, h_lb, h_norm_g, h_w_out, moe_w_rg, moe_b_rg, moe_w_re, moe_b_re, moe_w1, moe_w3, moe_w2):
    bp, lp = x_prompt.shape[0], x_prompt.shape[1]
    past_len = page_table.shape[1] * cache_kv_cmp.shape[2]
    dt = x_prompt.dtype
    m_w_in, m_w_out, n_w_in, n_w_out, h_w_in, h_w_out, moe_w1, moe_w3, moe_w2 = (
        w.astype(BF16) for w in (m_w_in, m_w_out, n_w_in, n_w_out, h_w_in, h_w_out, moe_w1, moe_w3, moe_w2))
    lbs = jax.nn.softmax(h_lb.astype(F32), axis=0)
    lbs = jnp.cumsum(lbs, axis=0) - lbs[0]
    xp, xs = x_prompt, x_sample
    ssm_p, conv_p, cmp_p, sel_p, win_p, hg_p = [], [], [], [], [], []
    ssm_s, conv_s, cmp_s, sel_s, win_s, hg_s = [], [], [], [], [], []
    for i in range(DEPTH):
        kind, j = i % N_MIXERS, i // N_MIXERS
        g0 = norm_g[i, 0]
        if kind == 0:
            w = (m_w_in[j], m_conv_w[j], m_conv_b[j], m_dt_bias[j], m_a_log[j], m_d[j], m_norm_g[j], m_w_out[j])
            xp, cbuf, hh = mamba_prompt_mixer(xp, g0, *w)
            ssm_p.append(hh)
            conv_p.append(cbuf)
            ys, cbuf, hh = mamba_mixer(xs, g0, state_conv[j], state_ssm[j], *w)
            xs = xs + ys
            ssm_s.append(hh)
            conv_s.append(cbuf)
        elif kind == 1:
            w = (n_w_in[j], n_q_g[j], n_k_g[j], n_cmp_pe[j], n_cmp_w1[j], n_cmp_w2[j], n_w_out[j], rel_table)
            xp, rc, rs, wb = nsa_prompt_mixer(xp, g0, *w)
            cmp_p.append(rc)
            sel_p.append(rs)
            win_p.append(wb)
            xs, rc, rs, wb = nsa_decode_mixer(xs, g0, past_len, cache_kv_cmp[j], cache_kv_sel[j],
                                              cache_kv_win[j], page_table, *w)
            cmp_s.append(rc)
            sel_s.append(rs)
            win_s.append(wb)
        else:
            w = (lbs[i], h_w_in[j], h_norm_g[j], h_w_out[j])
            xp, st = hgrn2_prompt_mixer(xp, g0, *w)
            hg_p.append(st)
            ys, st = hgrn2_mixer(xs, g0, state_hgrn[j], *w)
            xs = xs + ys
            hg_s.append(st)
        mw = (moe_w_rg[i], moe_b_rg[i], moe_w_re[i], moe_b_re[i], moe_w1[i], moe_w3[i], moe_w2[i])
        xp = hier_moe_residual(xp.reshape(-1, D_MODEL), norm_g[i, 1], *mw).reshape(xp.shape)
        xs = hier_moe_residual(xs.reshape(-1, D_MODEL), norm_g[i, 1], *mw).reshape(xs.shape)
    return (xp, xs,
            jnp.stack(ssm_p), jnp.stack(conv_p), jnp.stack(cmp_p), jnp.stack(sel_p), jnp.stack(win_p), jnp.stack(hg_p),
            jnp.stack(ssm_s), jnp.stack(conv_s), jnp.stack(cmp_s), jnp.stack(sel_s), jnp.stack(win_s), jnp.stack(hg_s))
```

```python
import functools
import math

import jax
import jax.numpy as jnp
import numpy as np
from jax import lax
from jax.experimental import pallas as pl
from jax.experimental.pallas import tpu as pltpu

F32 = jnp.float32
BF16 = jnp.bfloat16
EPS = 1e-6

D_MODEL = 1024
DEPTH = 4
N_MIXERS = 3

M_DINNER = 2 * D_MODEL
M_HEADDIM = 64
M_HEADS = M_DINNER // M_HEADDIM
M_GROUPS = 4
M_DSTATE = 128
M_CONV = 4
M_GN = M_GROUPS * M_DSTATE
M_CONV_DIM = M_DINNER + 2 * M_GN
M_CHUNK = 128

N_HEADS = 16
HEAD_DIM = D_MODEL // N_HEADS
KV_HEADS = 4
HPG = N_HEADS // KV_HEADS
KV_DIM = KV_HEADS * HEAD_DIM
CMP_BLOCK = 32
CMP_STRIDE = 16
SEL_BLOCK = 64
SEL_TOPK = 16
WINDOW = 512
SEL_QBLK = 32
WIN_QBLK = 128
SCALE = HEAD_DIM ** -0.5
REL_BUCKETS = 32
REL_MAX_DIST = 128

H_DK = 128
H_HEADS = D_MODEL // H_DK
H_DV = D_MODEL // H_HEADS
H_CHUNK = 32

MOE_GROUPS = 4
MOE_EPG = 4
MOE_EXPERTS = MOE_GROUPS * MOE_EPG
MOE_TOPK = 2
MOE_FF = 512

VMEM_LIMIT_BYTES = 48 * 1024 * 1024
DECODE_VMEM_LIMIT_BYTES = 56 * 1024 * 1024


def _rmsnorm(x, g):
    xf = x.astype(F32)
    y = xf * lax.rsqrt(jnp.mean(xf * xf, axis=-1, keepdims=True) + EPS)
    return (y * g.astype(F32)).astype(x.dtype)


LANES = 128
MXU_WIDTH = 256


def _norm_matmul_body(*refs, normalize, has_res):
    if has_res:
        x_ref, g_ref, w_ref, res_ref, o_ref, xb_ref = refs
    else:
        x_ref, g_ref, w_ref, o_ref, xb_ref = refs

    @pl.when(pl.program_id(1) == 0)
    def _():
        x = x_ref[...].astype(F32)
        if normalize:
            x = x * lax.rsqrt(jnp.mean(x * x, axis=-1, keepdims=True) + EPS) * g_ref[...]
        xb_ref[...] = x.astype(BF16)

    acc = jnp.dot(xb_ref[...], w_ref[...].astype(BF16), preferred_element_type=F32)
    if has_res:
        acc = acc + res_ref[...]
    o_ref[...] = acc


def _pick_tile(n, pref):
    t = min(n, pref)
    while n % t:
        t //= 2
    return t


def norm_matmul(x, g, w, *, normalize=True, res=None, tm=None, tn=512):
    t, k = x.shape
    n_true = w.shape[1]
    tm = _pick_tile(t, tm or (2048 if k <= 1024 else 1024))
    col_tile = MXU_WIDTH if n_true > MXU_WIDTH else LANES
    if n_true % col_tile:
        assert res is None
        w = jnp.pad(w, ((0, 0), (0, col_tile - n_true % col_tile)))
    n = w.shape[1]
    tn = _pick_tile(n, tn)
    if g is None:
        g = jnp.ones((k,), F32)
    in_specs = [
        pl.BlockSpec((tm, k), lambda i, j: (i, 0)),
        pl.BlockSpec((1, k), lambda i, j: (0, 0)),
        pl.BlockSpec((k, tn), lambda i, j: (0, j)),
    ]
    args = [x, g.reshape(1, k), w]
    if res is not None:
        in_specs.append(pl.BlockSpec((tm, tn), lambda i, j: (i, j)))
        args.append(res)
    out = pl.pallas_call(
        functools.partial(_norm_matmul_body, normalize=normalize, has_res=res is not None),
        name="norm_matmul",
        grid=(t // tm, n // tn),
        in_specs=in_specs,
        out_specs=pl.BlockSpec((tm, tn), lambda i, j: (i, j)),
        out_shape=jax.ShapeDtypeStruct((t, n), F32),
        scratch_shapes=[pltpu.VMEM((tm, k), BF16)],
        compiler_params=pltpu.CompilerParams(
            dimension_semantics=("parallel", "arbitrary"),
            vmem_limit_bytes=VMEM_LIMIT_BYTES),
    )(*args)
    return out if n == n_true else out[:, :n_true]


ROUTE_LANES = 128
MOE_TILE = 1024
MOE_ROWS = 160
MOE_EXPERTS_PER_STEP = 2
NEG = -1e30


def _router_body(x_ref, g_ref, w_ref, b_ref, u_ref, xn_ref, rank_ref, wt_ref):
    x = x_ref[...]
    tm = x.shape[0]
    xn = x * lax.rsqrt(jnp.mean(x * x, axis=-1, keepdims=True) + EPS) * g_ref[...]
    xb = xn.astype(BF16)
    xn_ref[...] = xb
    logits = jnp.dot(xb, w_ref[...].astype(BF16), preferred_element_type=F32) + b_ref[...]
    lane = lax.broadcasted_iota(jnp.int32, (tm, ROUTE_LANES), 1).astype(F32)

    def first_max(mask):
        v = jnp.max(jnp.where(mask, logits, NEG), axis=-1, keepdims=True)
        i = jnp.min(jnp.where(mask & (logits == v), lane, float(ROUTE_LANES)), axis=-1, keepdims=True)
        return v, i

    is_group = lane < MOE_GROUPS
    mg, g_idx = first_max(is_group)
    pg_top = 1.0 / jnp.sum(jnp.where(is_group, jnp.exp(logits - mg), 0.0), axis=-1, keepdims=True)
    lo = MOE_GROUPS + MOE_EPG * g_idx
    in_group = (lane >= lo) & (lane < lo + MOE_EPG)
    v1, i1 = first_max(in_group)
    v2, i2 = first_max(in_group & (lane != i1))
    e2 = jnp.exp(v2 - v1)
    w_a = pg_top / (1.0 + e2)
    w_b = pg_top * e2 / (1.0 + e2)
    info = jnp.where(lane == 0, i1 - MOE_GROUPS, jnp.where(lane == 1, i2 - MOE_GROUPS,
                     jnp.where(lane == 2, w_a, jnp.where(lane == 3, w_b, 0.0))))
    info_t = info.T
    e_a, e_b, w_at, w_bt = info_t[0:1], info_t[1:2], info_t[2:3], info_t[3:4]
    expert = lax.broadcasted_iota(jnp.int32, (MOE_EXPERTS, tm), 0).astype(F32)
    m_a = e_a == expert
    m_b = e_b == expert
    onehot = jnp.concatenate([jnp.where(m_a, 1.0, 0.0), jnp.where(m_b, 1.0, 0.0)], axis=0).astype(BF16)
    cum = jnp.dot(onehot, u_ref[...], preferred_element_type=F32)
    cum_a, cum_b = cum[:MOE_EXPERTS], cum[MOE_EXPERTS:]
    n_a = cum_a[:, tm - 1:tm]
    rank_ref[...] = jnp.where(m_a, cum_a - 1.0, jnp.where(m_b, n_a + cum_b - 1.0, -1.0))
    wt_ref[...] = jnp.where(m_a, w_at, jnp.where(m_b, w_bt, 0.0))


def moe_route(x, g, w_rg, b_rg, w_re, b_re, tm):
    t = x.shape[0]
    pad = ROUTE_LANES - MOE_GROUPS - MOE_EXPERTS
    w = jnp.pad(jnp.concatenate([w_rg, w_re], axis=1), ((0, 0), (0, pad)))
    b = jnp.pad(jnp.concatenate([b_rg, b_re]), (0, pad)).reshape(1, ROUTE_LANES)
    upper = jnp.asarray(np.triu(np.ones((tm, tm), np.float32)), BF16)
    const = lambda a: pl.BlockSpec(a.shape, lambda i: (0,) * a.ndim)
    g2 = g.reshape(1, D_MODEL)
    return pl.pallas_call(
        _router_body,
        name="moe_router",
        grid=(t // tm,),
        in_specs=[pl.BlockSpec((tm, D_MODEL), lambda i: (i, 0)), const(g2), const(w), const(b), const(upper)],
        out_specs=[pl.BlockSpec((tm, D_MODEL), lambda i: (i, 0)),
                   pl.BlockSpec((MOE_EXPERTS, tm), lambda i: (0, i)),
                   pl.BlockSpec((MOE_EXPERTS, tm), lambda i: (0, i))],
        out_shape=[jax.ShapeDtypeStruct((t, D_MODEL), BF16),
                   jax.ShapeDtypeStruct((MOE_EXPERTS, t), F32),
                   jax.ShapeDtypeStruct((MOE_EXPERTS, t), F32)],
        compiler_params=pltpu.CompilerParams(
            dimension_semantics=("parallel",), vmem_limit_bytes=VMEM_LIMIT_BYTES),
    )(x, g2, w, b, upper)


def _moe_expert_body(cnt_ref, xn_ref, rank_ref, wt_ref, w1_ref, w3_ref, w2_ref, res_ref, o_ref, *, rows):
    ti, step = pl.program_id(0), pl.program_id(1)

    @pl.when(step == 0)
    def _():
        o_ref[...] = res_ref[...]

    tm = xn_ref.shape[0]
    experts = [step * MOE_EXPERTS_PER_STEP + k for k in range(MOE_EXPERTS_PER_STEP)]
    ranks = [rank_ref[pl.ds(e, 1), :] for e in experts]
    wts = [wt_ref[pl.ds(e, 1), :] for e in experts]
    n_chunks = functools.reduce(jnp.maximum, [(cnt_ref[e, ti] + rows - 1) // rows for e in experts])

    def chunk(c, carry):
        r = (lax.broadcasted_iota(jnp.int32, (rows, tm), 0) + c * rows).astype(F32)
        total = None
        for k in range(MOE_EXPERTS_PER_STEP):
            sel = jnp.where(ranks[k] == r, 1.0, 0.0)
            selb = sel.astype(BF16)
            xs = jnp.dot(selb, xn_ref[...], preferred_element_type=F32).astype(BF16)
            a = jnp.dot(xs, w1_ref[k].astype(BF16), preferred_element_type=F32)
            b = jnp.dot(xs, w3_ref[k].astype(BF16), preferred_element_type=F32)
            h = (a * jax.nn.sigmoid(a) * b * jnp.sum(sel * wts[k], axis=1, keepdims=True)).astype(BF16)
            y = jnp.dot(h, w2_ref[k].astype(BF16), preferred_element_type=F32)
            y_hi = y.astype(BF16)
            y_lo = (y - y_hi.astype(F32)).astype(BF16)
            tn = (((0,), (0,)), ((), ()))
            back = (lax.dot_general(selb, y_hi, tn, preferred_element_type=F32)
                    + lax.dot_general(selb, y_lo, tn, preferred_element_type=F32))
            total = back if total is None else total + back
        o_ref[...] += total
        return carry

    lax.fori_loop(0, n_chunks, chunk, 0)


def moe_experts(xn, rank, wt, counts, w1, w3, w2, res, tm):
    t = xn.shape[0]
    rows = min(MOE_ROWS, tm)
    grid_spec = pltpu.PrefetchScalarGridSpec(
        num_scalar_prefetch=1,
        grid=(t // tm, MOE_EXPERTS // MOE_EXPERTS_PER_STEP),
        in_specs=[
            pl.BlockSpec((tm, D_MODEL), lambda i, e, c: (i, 0)),
            pl.BlockSpec((MOE_EXPERTS, tm), lambda i, e, c: (0, i)),
            pl.BlockSpec((MOE_EXPERTS, tm), lambda i, e, c: (0, i)),
            pl.BlockSpec((MOE_EXPERTS_PER_STEP, D_MODEL, MOE_FF), lambda i, e, c: (e, 0, 0)),
            pl.BlockSpec((MOE_EXPERTS_PER_STEP, D_MODEL, MOE_FF), lambda i, e, c: (e, 0, 0)),
            pl.BlockSpec((MOE_EXPERTS_PER_STEP, MOE_FF, D_MODEL), lambda i, e, c: (e, 0, 0)),
            pl.BlockSpec((tm, D_MODEL), lambda i, e, c: (i, 0)),
        ],
        out_specs=pl.BlockSpec((tm, D_MODEL), lambda i, e, c: (i, 0)),
    )
    return pl.pallas_call(
        functools.partial(_moe_expert_body, rows=rows),
        name="moe_experts",
        grid_spec=grid_spec,
        out_shape=jax.ShapeDtypeStruct((t, D_MODEL), F32),
        compiler_params=pltpu.CompilerParams(
            dimension_semantics=("parallel", "arbitrary"),
            vmem_limit_bytes=VMEM_LIMIT_BYTES),
    )(counts, xn, rank, wt, w1, w3, w2, res)


def hier_moe_residual(x, g, w_rg, b_rg, w_re, b_re, w1, w3, w2):
    t_true = x.shape[0]
    tm = MOE_TILE if t_true % MOE_TILE == 0 else ROUTE_LANES
    if t_true % tm:
        x = jnp.pad(x, ((0, tm - t_true % tm), (0, 0)))
    t = x.shape[0]
    xn, rank, wt = moe_route(x, g, w_rg, b_rg, w_re, b_re, tm)
    counts = jnp.sum((rank >= 0).reshape(MOE_EXPERTS, t // tm, tm), axis=-1, dtype=jnp.int32)
    return moe_experts(xn, rank, wt, counts, w1, w3, w2, x, tm)[:t_true]


def _causal_dwconv(u, buf, w, b):
    L = u.shape[1]
    ext = jnp.concatenate([buf.astype(u.dtype), u], axis=1)
    out = b + sum(ext[:, k:k + L] * w[k] for k in range(M_CONV))
    return out, ext[:, L:]


def _ssd_scan(x, dt, a, bm, cm, h0):
    bsz, L = x.shape[0], x.shape[1]
    q = M_CHUNK if L % M_CHUNK == 0 else L
    nc = L // q
    hpg = M_HEADS // M_GROUPS
    xdt = (x * dt[..., None]).reshape(bsz, nc, q, M_GROUPS, hpg, M_HEADDIM)
    acum = jnp.cumsum((dt * a).reshape(bsz, nc, q, M_GROUPS, hpg), axis=2)
    bc = bm.reshape(bsz, nc, q, M_GROUPS, M_DSTATE)
    cc = cm.reshape(bsz, nc, q, M_GROUPS, M_DSTATE)
    tri = jnp.tril(jnp.ones((q, q), bool))[:, :, None, None]
    seg = acum[:, :, :, None] - acum[:, :, None]
    decay = jnp.exp(jnp.where(tri, seg, -jnp.inf))
    cb = jnp.einsum('bclgn,bcsgn->bclsg', cc, bc)
    y_diag = jnp.einsum('bclsg,bclsgh,bcsghp->bclghp', cb, decay, xdt)
    decay_end = jnp.exp(acum[:, :, -1:] - acum)
    states = jnp.einsum('bcsgn,bcsgh,bcsghp->bcghpn', bc, decay_end, xdt)
    chunk_decay = jnp.exp(acum[:, :, -1])

    def step(h, inp):
        cd, st = inp
        return cd[..., None, None] * h + st, h

    h_last, h_prev = lax.scan(step, h0.reshape(bsz, M_GROUPS, hpg, M_HEADDIM, M_DSTATE),
                              (chunk_decay.swapaxes(0, 1), states.swapaxes(0, 1)))
    y_off = jnp.einsum('bclgn,bclgh,bcghpn->bclghp', cc, jnp.exp(acum), h_prev.swapaxes(0, 1))
    y = (y_diag + y_off).reshape(bsz, L, M_HEADS, M_HEADDIM)
    return y, h_last.reshape(bsz, M_HEADS, M_HEADDIM, M_DSTATE)


def mamba_mixer(x, g, conv_buf, h0, w_in, conv_w, conv_b, dt_bias, a_log, d_skip, norm_g, w_out):
    bsz, L, _ = x.shape
    proj = norm_matmul(x.reshape(bsz * L, D_MODEL), g, w_in).reshape(bsz, L, -1)
    z = proj[..., :M_DINNER]
    xbc = proj[..., M_DINNER:M_DINNER + M_CONV_DIM]
    dt_raw = proj[..., M_DINNER + M_CONV_DIM:]
    xbc, new_buf = _causal_dwconv(xbc, conv_buf, conv_w, conv_b)
    xbc = jax.nn.silu(xbc)
    xs = xbc[..., :M_DINNER].reshape(bsz, L, M_HEADS, M_HEADDIM)
    bm = xbc[..., M_DINNER:M_DINNER + M_GN].reshape(bsz, L, M_GROUPS, M_DSTATE)
    cm = xbc[..., M_DINNER + M_GN:].reshape(bsz, L, M_GROUPS, M_DSTATE)
    dt = jax.nn.softplus(dt_raw + dt_bias)
    a = -jnp.exp(a_log)
    y, h = _ssd_scan(xs, dt, a, bm, cm, h0)
    y = y + xs * d_skip[:, None]
    y = y.reshape(bsz, L, M_DINNER)
    y = _rmsnorm(y * jax.nn.silu(z), norm_g)
    out = norm_matmul(y.reshape(bsz * L, M_DINNER), None, w_out, normalize=False)
    return out.reshape(bsz, L, D_MODEL), new_buf, h


SSD_HPG = M_HEADS // M_GROUPS
SSD_GROUP_ROWS = SSD_HPG * M_HEADDIM
CONV_PAD = 8


def _transpose_cols(x):
    return jnp.concatenate([x[:, j * LANES:(j + 1) * LANES].T for j in range(x.shape[1] // LANES)], axis=0)


def _transpose_rows(x):
    return jnp.concatenate([x[j * LANES:(j + 1) * LANES, :].T for j in range(x.shape[0] // LANES)], axis=1)


def _ssd_body(z_ref, xbc_ref, dtr_ref, cw_ref, cb_ref, dtb_ref, a_ref, dcol_ref, ng_ref, ltri_ref,
              y_ref, conv_ref, h_ref, xbuf, *, q):
    c = pl.program_id(1)

    @pl.when(c == 0)
    def _():
        h_ref[...] = jnp.zeros_like(h_ref)
        xbuf[0:CONV_PAD, :] = jnp.zeros((CONV_PAD, M_CONV_DIM), F32)

    xbuf[CONV_PAD:CONV_PAD + q, :] = xbc_ref[...]
    conv = cb_ref[...]
    for k in range(M_CONV):
        start = CONV_PAD - (M_CONV - 1) + k
        conv = conv + xbuf[start:start + q, :] * cw_ref[k:k + 1, :]
    tail = xbuf[CONV_PAD + q - (M_CONV - 1):CONV_PAD + q, :]
    xbuf[CONV_PAD - (M_CONV - 1):CONV_PAD, :] = tail
    conv_ref[0] = tail
    xc = conv * jax.nn.sigmoid(conv)
    xs = xc[:, :M_DINNER]
    xs_t = _transpose_cols(xs)

    pre = dtr_ref[...] + dtb_ref[...]
    dt = jnp.maximum(pre, 0.0) + jnp.log1p(jnp.exp(-jnp.abs(pre)))
    da = dt * a_ref[...]
    ltri = ltri_ref[...]
    acum = sum(jnp.dot(ltri, part, preferred_element_type=F32) for part in _split3(da))
    dt_t = dt.T
    acum_t = acum.T
    li = lax.broadcasted_iota(jnp.int32, (q, q), 0)
    si = lax.broadcasted_iota(jnp.int32, (q, q), 1)
    causal = li >= si

    y_t = []
    for g in range(M_GROUPS):
        bm = xc[:, M_DINNER + g * M_DSTATE:M_DINNER + (g + 1) * M_DSTATE].astype(BF16)
        cm = xc[:, M_DINNER + M_GN + g * M_DSTATE:M_DINNER + M_GN + (g + 1) * M_DSTATE].astype(BF16)
        cb = _nt_dot(cm, bm)
        r0 = g * SSD_GROUP_ROWS
        h_prev = h_ref[0, r0:r0 + SSD_GROUP_ROWS, :]
        y_off = _nt_dot(h_prev.astype(BF16), cm)
        x_dec, scale = [], []
        for hh in range(g * SSD_HPG, (g + 1) * SSD_HPG):
            a_row = acum_t[hh:hh + 1, :]
            a_col = acum[:, hh:hh + 1]
            decay = jnp.where(causal, jnp.exp(a_col - a_row), 0.0)
            m = (cb * decay).astype(BF16)
            rows = slice(hh * M_HEADDIM, (hh + 1) * M_HEADDIM)
            xs_h = xs_t[rows]
            xdt = xs_h * dt_t[hh:hh + 1, :]
            y_h = _nt_dot(xdt.astype(BF16), m)
            y_h = y_h + y_off[rows.start - r0:rows.stop - r0] * jnp.exp(a_row) + xs_h * dcol_ref[hh:hh + 1, :]
            y_t.append(y_h)
            a_last = a_row[:, q - 1:q]
            x_dec.append(xdt * jnp.exp(a_last - a_row))
            scale.append(jnp.broadcast_to(jnp.exp(a_last), (M_HEADDIM, 1)))
        upd = jnp.dot(jnp.concatenate(x_dec, axis=0).astype(BF16), bm, preferred_element_type=F32)
        h_ref[0, r0:r0 + SSD_GROUP_ROWS, :] = jnp.concatenate(scale, axis=0) * h_prev + upd

    y = _transpose_rows(jnp.concatenate(y_t, axis=0))
    zz = z_ref[...]
    yg = y * (zz * jax.nn.sigmoid(zz))
    yg = yg * lax.rsqrt(jnp.mean(yg * yg, axis=-1, keepdims=True) + EPS) * ng_ref[...]
    y_ref[...] = yg.astype(BF16)


def ssd_prompt(z, xbc, dtr, conv_w, conv_b, dt_bias, a_log, d_skip, norm_g, bsz, L):
    q = M_CHUNK
    assert L % q == 0 and q == LANES
    nc = L // q
    padl = lambda v: jnp.pad(v, (0, LANES - v.shape[0]))
    dtb = padl(dt_bias).reshape(1, LANES)
    a_row = padl(-jnp.exp(a_log)).reshape(1, LANES)
    dcol = jnp.broadcast_to(padl(d_skip).reshape(LANES, 1), (LANES, LANES))
    ltri = jnp.asarray(np.tril(np.ones((q, q), np.float32)), BF16)
    const = lambda a: pl.BlockSpec(a.shape, lambda b, c: (0,) * a.ndim)
    tok = lambda w: pl.BlockSpec((q, w), lambda b, c: (b * nc + c, 0))
    cb2 = conv_b.reshape(1, M_CONV_DIM)
    ng2 = norm_g.reshape(1, M_DINNER)
    y, conv_tail, h = pl.pallas_call(
        functools.partial(_ssd_body, q=q),
        name="ssd_chunks",
        grid=(bsz, nc),
        in_specs=[tok(M_DINNER), tok(M_CONV_DIM), tok(LANES), const(conv_w), const(cb2), const(dtb),
                  const(a_row), const(dcol), const(ng2), const(ltri)],
        out_specs=[tok(M_DINNER),
                   pl.BlockSpec((1, M_CONV - 1, M_CONV_DIM), lambda b, c: (b, 0, 0)),
                   pl.BlockSpec((1, M_DINNER, M_DSTATE), lambda b, c: (b, 0, 0))],
        out_shape=[jax.ShapeDtypeStruct((bsz * L, M_DINNER), BF16),
                   jax.ShapeDtypeStruct((bsz, M_CONV - 1, M_CONV_DIM), F32),
                   jax.ShapeDtypeStruct((bsz, M_DINNER, M_DSTATE), F32)],
        scratch_shapes=[pltpu.VMEM((CONV_PAD + q, M_CONV_DIM), F32)],
        compiler_params=pltpu.CompilerParams(
            dimension_semantics=("parallel", "arbitrary"),
            vmem_limit_bytes=VMEM_LIMIT_BYTES),
    )(z, xbc, dtr, conv_w, cb2, dtb, a_row, dcol, ng2, ltri)
    return y, conv_tail, h.reshape(bsz, M_HEADS, M_HEADDIM, M_DSTATE)


def mamba_prompt_mixer(x, g, w_in, conv_w, conv_b, dt_bias, a_log, d_skip, norm_g, w_out):
    bsz, L, _ = x.shape
    x2 = x.reshape(bsz * L, D_MODEL)
    z = norm_matmul(x2, g, w_in[:, :M_DINNER])
    xbc = norm_matmul(x2, g, w_in[:, M_DINNER:M_DINNER + M_CONV_DIM])
    dtr = norm_matmul(x2, g, jnp.pad(w_in[:, M_DINNER + M_CONV_DIM:], ((0, 0), (0, LANES - M_HEADS))))
    y, conv_tail, h = ssd_prompt(z, xbc, dtr, conv_w, conv_b, dt_bias, a_log, d_skip, norm_g, bsz, L)
    out = norm_matmul(y, None, w_out, normalize=False, res=x2)
    return out.reshape(bsz, L, D_MODEL), conv_tail, h


def _rel_bucket(dist):
    exact = REL_BUCKETS // 2
    d = jnp.maximum(dist, 0)
    ratio = jnp.log(jnp.maximum(d, 1).astype(F32) / exact) / math.log(REL_MAX_DIST / exact)
    large = jnp.minimum(exact + (ratio * (REL_BUCKETS - exact)).astype(jnp.int32), REL_BUCKETS - 1)
    return jnp.where(d < exact, d, large)


def _head_bias(rel_table, dist):
    b = rel_table[_rel_bucket(dist)].astype(F32)
    return jnp.moveaxis(b, -1, 0).reshape(KV_HEADS, HPG, dist.shape[0], dist.shape[1])


def _masked_softmax(s, valid):
    s = jnp.where(valid, s, -1e30)
    e = jnp.where(valid, jnp.exp(s - jnp.max(s, axis=-1, keepdims=True)), 0.0)
    return e / jnp.maximum(jnp.sum(e, axis=-1, keepdims=True), 1e-30)


def _group_attend(qg, qpos, k, v, kpos, valid, rel_table):
    s = jnp.einsum('bqghd,bkgd->bghqk', qg, k).astype(F32) * SCALE
    s = s + _head_bias(rel_table, qpos[:, None] - kpos[None, :])
    p = _masked_softmax(s, valid)
    o = jnp.einsum('bghqk,bkgd->bqghd', p.astype(v.dtype), v)
    return o, p


def _compress(rows, pe, w1, w2):
    bsz, T = rows.shape[0], rows.shape[1]
    nc = (T - CMP_BLOCK) // CMP_STRIDE + 1
    idx = jnp.arange(nc)[:, None] * CMP_STRIDE + jnp.arange(CMP_BLOCK)[None, :]
    blk = rows[:, idx] + pe[:, None, :]
    blk = jnp.moveaxis(blk, 2, 3).reshape(bsz, nc, KV_HEADS, CMP_BLOCK * HEAD_DIM)
    return jax.nn.silu(blk @ w1) @ w2


def _select_blocks(p_cmp, qpos, n_sel):
    nc = p_cmp.shape[-1]
    cstart = jnp.arange(nc) * CMP_STRIDE
    sstart = jnp.arange(n_sel) * SEL_BLOCK
    overlap = ((cstart[:, None] < sstart[None, :] + SEL_BLOCK) &
               (cstart[:, None] + CMP_BLOCK > sstart[None, :])).astype(F32)
    imp = jnp.einsum('bghqc,cs->bgqs', p_cmp, overlap)
    qblk = qpos // SEL_BLOCK
    j = jnp.arange(n_sel)
    forced = (j[None, :] == qblk[:, None]) | (j[None, :] == 0)
    score = jnp.where(forced, 1e9, jnp.where(j[None, :] <= qblk[:, None], imp, -1e30))
    _, idx = lax.top_k(score, min(SEL_TOPK, n_sel))
    return idx


def _sel_attend(qg, qpos, idx, kblk, vblk, rel_table):
    bsz, lq = qg.shape[0], qg.shape[1]
    bi = jnp.arange(bsz)[:, None, None, None]
    gi = jnp.arange(KV_HEADS)[None, :, None, None]
    kg = kblk[bi, gi, idx]
    vg = vblk[bi, gi, idx]
    s = jnp.einsum('bqghd,bgqkrd->bghqkr', qg, kg).astype(F32) * SCALE
    kpos = idx[..., None] * SEL_BLOCK + jnp.arange(SEL_BLOCK)
    dist = qpos[:, None, None] - kpos
    g5 = jnp.arange(KV_HEADS)[None, :, None, None, None]
    bias = rel_table.reshape(REL_BUCKETS, KV_HEADS, HPG)[_rel_bucket(dist), g5].astype(F32)
    s = s + jnp.moveaxis(bias, -1, 2)
    nk = idx.shape[-1]
    valid = (dist >= 0)[:, :, None].reshape(bsz, KV_HEADS, 1, lq, nk * SEL_BLOCK)
    p = _masked_softmax(s.reshape(bsz, KV_HEADS, HPG, lq, nk * SEL_BLOCK), valid)
    p = p.reshape(bsz, KV_HEADS, HPG, lq, nk, SEL_BLOCK)
    return jnp.einsum('bghqkr,bgqkrd->bqghd', p.astype(vg.dtype), vg)


def nsa_mixer(x, g, pos0, kv_cmp_past, kv_sel_past, kv_win_past, n_keep,
              w_in, q_g, k_g, cmp_pe, cmp_w1, cmp_w2, w_out, rel_table):
    bsz, L, _ = x.shape
    sizes = [N_HEADS * HEAD_DIM] + [KV_DIM] * 6 + [3 * N_HEADS]
    proj = norm_matmul(x.reshape(bsz * L, D_MODEL), g, w_in).reshape(bsz, L, -1)
    q, kc, vc, ks, vs, kw, vw, gl = jnp.split(proj, np.cumsum(sizes)[:-1].tolist(), axis=-1)
    q = _rmsnorm(q.reshape(bsz, L, KV_HEADS, HPG, HEAD_DIM), q_g)

    def kv(t):
        return t.reshape(bsz, L, KV_HEADS, HEAD_DIM)

    new_cmp = jnp.stack([kv(kc), kv(vc)], axis=2)
    new_sel = jnp.stack([_rmsnorm(kv(ks), k_g[1]), kv(vs)], axis=2)
    new_win = jnp.stack([_rmsnorm(kv(kw), k_g[2]), kv(vw)], axis=2)
    qpos = pos0 + jnp.arange(L)

    crows = jnp.concatenate([kv_cmp_past.astype(x.dtype), new_cmp], axis=1)
    kcmp = _rmsnorm(_compress(crows[:, :, 0], cmp_pe[0], cmp_w1[0], cmp_w2[0]), k_g[0])
    vcmp = _compress(crows[:, :, 1], cmp_pe[1], cmp_w1[1], cmp_w2[1])
    ends = jnp.arange(kcmp.shape[1]) * CMP_STRIDE + CMP_BLOCK - 1
    o_cmp, p_cmp = _group_attend(q, qpos, kcmp, vcmp, ends, qpos[:, None] >= ends[None, :], rel_table)

    srows = jnp.concatenate([kv_sel_past.astype(x.dtype), new_sel], axis=1)
    T = srows.shape[1]
    n_sel = -(-T // SEL_BLOCK)
    srows = jnp.pad(srows, ((0, 0), (0, n_sel * SEL_BLOCK - T), (0, 0), (0, 0), (0, 0)))
    blocks = srows.reshape(bsz, n_sel, SEL_BLOCK, 2, KV_HEADS, HEAD_DIM).transpose(3, 0, 4, 1, 2, 5)
    idx = _select_blocks(p_cmp, qpos, n_sel)
    qb = SEL_QBLK if L % SEL_QBLK == 0 else L
    nqb = L // qb

    def sel_block(args):
        qgi, qposi, idxi = args
        return _sel_attend(qgi, qposi, idxi, blocks[0], blocks[1], rel_table)

    o_sel = lax.map(sel_block, (q.reshape(bsz, nqb, qb, KV_HEADS, HPG, HEAD_DIM).swapaxes(0, 1),
                                qpos.reshape(nqb, qb),
                                idx.reshape(bsz, KV_HEADS, nqb, qb, -1).transpose(2, 0, 1, 3, 4)))
    o_sel = o_sel.swapaxes(0, 1).reshape(bsz, L, KV_HEADS, HPG, HEAD_DIM)

    p_win = kv_win_past.shape[1]
    wrows = jnp.concatenate([kv_win_past.astype(x.dtype), new_win], axis=1)
    wpad = jnp.pad(wrows, ((0, 0), (WINDOW, 0), (0, 0), (0, 0), (0, 0)))
    n_all = WINDOW + p_win + L
    kpos_all = pos0 - p_win - WINDOW + jnp.arange(n_all)
    kvalid_all = jnp.arange(n_all) >= WINDOW
    wq = WIN_QBLK if L % WIN_QBLK == 0 else L

    def win_block(i):
        start = p_win + i * wq
        qgi = lax.dynamic_slice_in_dim(q, i * wq, wq, axis=1)
        kvi = lax.dynamic_slice_in_dim(wpad, start, WINDOW + wq, axis=1)
        kposi = lax.dynamic_slice_in_dim(kpos_all, start, WINDOW + wq)
        kvalidi = lax.dynamic_slice_in_dim(kvalid_all, start, WINDOW + wq)
        qposi = pos0 + i * wq + jnp.arange(wq)
        dist = qposi[:, None] - kposi[None, :]
        valid = kvalidi[None, :] & (dist >= 0) & (dist <= WINDOW)
        o, _ = _group_attend(qgi, qposi, kvi[:, :, 0], kvi[:, :, 1], kposi, valid, rel_table)
        return o

    o_win = lax.map(win_block, jnp.arange(L // wq)).swapaxes(0, 1).reshape(bsz, L, KV_HEADS, HPG, HEAD_DIM)

    gate = jax.nn.sigmoid(gl).reshape(bsz, L, KV_HEADS, HPG, 3)
    o = gate[..., 0:1] * o_cmp + gate[..., 1:2] * o_sel + gate[..., 2:3] * o_win
    y = norm_matmul(o.reshape(bsz * L, N_HEADS * HEAD_DIM), None, w_out, normalize=False)
    return y.reshape(bsz, L, D_MODEL), new_cmp, new_sel, wrows[:, -n_keep:]


ATT_TQ = 128
ATT_TK = 128
ATT_UNROLL_LOG2 = 2
NEG = -1e30


def _nt_dot(a, b):
    return lax.dot_general(a, b, (((1,), (1,)), ((), ())), preferred_element_type=F32)


def _compress_body(rk_ref, rv_ref, pe_ref, w1_ref, w2_ref, kg_ref, kc_ref, vc_ref, *, nb):
    half = (CMP_BLOCK // 2) * HEAD_DIM
    for kv, (r_ref, o_ref) in enumerate(((rk_ref, kc_ref), (rv_ref, vc_ref))):
        lo = (r_ref[0, 0, 0:nb, :] + pe_ref[kv, 0:1, :]).astype(BF16)
        hi = (r_ref[0, 0, 1:nb + 1, :] + pe_ref[kv, 1:2, :]).astype(BF16)
        h = (jnp.dot(lo, w1_ref[kv, :half, :].astype(BF16), preferred_element_type=F32)
             + jnp.dot(hi, w1_ref[kv, half:, :].astype(BF16), preferred_element_type=F32))
        h = h * jax.nn.sigmoid(h)
        o = jnp.dot(h.astype(BF16), w2_ref[kv].astype(BF16), preferred_element_type=F32)
        if kv == 0:
            o = o * lax.rsqrt(jnp.mean(o * o, axis=-1, keepdims=True) + EPS) * kg_ref[...]
        o_ref[0, 0] = o


def compress_rows(rk, rv, pe, w1, w2, kg):
    bsz, g, nbp, width = rk.shape
    nb = nbp - 8
    strip = pl.BlockSpec((1, 1, nbp, width), lambda b, j: (b, j, 0, 0))
    out = pl.BlockSpec((1, 1, nb, HEAD_DIM), lambda b, j: (b, j, 0, 0))
    full = lambda a: pl.BlockSpec(a.shape, lambda b, j: (0,) * a.ndim)
    pe2 = pe.reshape(2, 2, width)
    kg2 = kg.reshape(1, HEAD_DIM)
    return pl.pallas_call(
        functools.partial(_compress_body, nb=nb),
        name="nsa_compress",
        grid=(bsz, g),
        in_specs=[strip, strip, full(pe2), full(w1), full(w2), full(kg2)],
        out_specs=[out, out],
        out_shape=[jax.ShapeDtypeStruct((bsz, g, nb, HEAD_DIM), F32)] * 2,
        compiler_params=pltpu.CompilerParams(
            dimension_semantics=("parallel", "parallel"),
            vmem_limit_bytes=VMEM_LIMIT_BYTES),
    )(rk, rv, pe2, w1, w2, kg2)


def _split3(x):
    a = x.astype(BF16)
    r = x - a.astype(F32)
    b = r.astype(BF16)
    c = (r - b.astype(F32)).astype(BF16)
    return a, b, c


def _bias_tables(rel_table, n_qtiles):
    tab = rel_table[_rel_bucket(jnp.arange(REL_MAX_DIST + 1))].astype(F32)
    tab = tab.T.reshape(KV_HEADS, HPG, REL_MAX_DIST + 1)

    def skew(v, rows, width, step):
        out = jnp.broadcast_to(v[..., None, :], v.shape[:-1] + (rows, width + step))
        out = out.reshape(v.shape[:-1] + (rows * (width + step),))[..., :rows * width]
        return out.reshape(v.shape[:-1] + (rows, width))

    width = 2 * ATT_TK
    w = np.arange(width + 1)
    s_minus_t = np.where(w <= ATT_TK, w, w - (width + 1))
    d_idx = np.stack([np.clip(delta * ATT_TK - s_minus_t, 0, REL_MAX_DIST) for delta in range(3)])
    btile = skew(tab[:, :, d_idx], ATT_TQ, width, 1)[..., :ATT_TK]
    btile = btile.transpose(0, 2, 1, 3, 4).reshape(KV_HEADS, 3, HPG * ATT_TQ, ATT_TK)

    n_q = n_qtiles * ATT_TQ
    u = np.arange(n_q + CMP_STRIDE)
    v_cmp = tab[:, :, np.clip(u - (CMP_BLOCK - 1), 0, REL_MAX_DIST)]
    bcmp = skew(v_cmp, ATT_TK, n_q, CMP_STRIDE)
    bcmp = bcmp.reshape(KV_HEADS, HPG, ATT_TK, n_qtiles, ATT_TQ).transpose(3, 0, 1, 4, 2)
    return btile, bcmp.reshape(n_qtiles, KV_HEADS, HPG * ATT_TQ, ATT_TK)


def _nsa_attn_t_body(qt_ref, kc_ref, vct_ref, ks_ref, vst_ref, kw_ref, vwt_ref, gate_ref, bcmp_ref, btile_ref,
                     ovl_ref, o_ref, s_scr, *, n_cmp, n_sel, topk):
    i = pl.program_id(2)
    q0 = i * ATT_TQ
    cols = HPG * ATT_TQ
    qs = (qt_ref[0, 0, 0] * SCALE).astype(BF16)
    key = lax.broadcasted_iota(jnp.int32, (ATT_TK, cols), 0)
    qpos = q0 + (lax.broadcasted_iota(jnp.int32, (ATT_TK, cols), 1) & (ATT_TQ - 1))

    def fold(x, op):
        return op(x.reshape(ATT_TK // 8, 8, cols), axis=0)

    s = jnp.dot(kc_ref[0, 0].astype(BF16), qs, preferred_element_type=F32) + bcmp_ref[0, 0]
    valid = (qpos >= key * CMP_STRIDE + (CMP_BLOCK - 1)) & (key < n_cmp)
    s = jnp.where(valid, s, NEG)
    e = jnp.where(valid, jnp.exp(s - jnp.max(s, axis=0, keepdims=True)), 0.0)
    p = e / jnp.maximum(jnp.sum(e, axis=0, keepdims=True), 1e-30)
    o_cmp = jnp.dot(vct_ref[0, 0].astype(BF16), p.astype(BF16), preferred_element_type=F32)

    psum = p[:, 0:ATT_TQ]
    for h in range(1, HPG):
        psum = psum + p[:, h * ATT_TQ:(h + 1) * ATT_TQ]
    imp_t = jnp.dot(ovl_ref[...], psum.astype(BF16), preferred_element_type=F32)
    nblk = imp_t.shape[0]
    nrank = -(-n_sel // 8) * 8
    j = lax.broadcasted_iota(jnp.int32, (nrank, ATT_TQ), 0)
    qblk = (q0 + lax.broadcasted_iota(jnp.int32, (nrank, ATT_TQ), 1)) // SEL_BLOCK
    forced = (j == qblk) | (j == 0)
    score = jnp.where(forced, 1e9, jnp.where(j <= qblk, imp_t[:nrank], NEG))
    score = jnp.where(j < n_sel, score, -3e38)
    rank = jnp.zeros((nrank, ATT_TQ), F32)
    for jp in range(n_sel):
        other = score[jp:jp + 1, :]
        beats = (other > score) | ((other == score) & (jp < j))
        rank = rank + jnp.where(beats, 1.0, 0.0)
    sel_t = jnp.where((rank < topk) & (j < n_sel), 1.0, 0.0)
    if nrank < nblk:
        sel_t = jnp.concatenate([sel_t, jnp.zeros((nblk - nrank, ATT_TQ), F32)], axis=0)
    sel_t = sel_t.astype(BF16)

    kk = lax.broadcasted_iota(jnp.int32, (ATT_TK, nblk), 0) // SEL_BLOCK
    jj = lax.broadcasted_iota(jnp.int32, (ATT_TK, nblk), 1)
    blocks_per_step = ATT_TK // SEL_BLOCK

    def attend(k_ref, vt_ref, lo, hi, penalty_fn, whole_groups=False):
        def tile(kc):
            return pl.ds(pl.multiple_of(kc * ATT_TK, ATT_TK), ATT_TK)

        def sweep(fn, init):
            extra = (1 << ATT_UNROLL_LOG2) - 1 if whole_groups else 0
            n_groups = lax.shift_right_logical(hi - lo + extra, ATT_UNROLL_LOG2)

            def group(gi, carry):
                kc = lo + gi * (1 << ATT_UNROLL_LOG2)
                for u in range(1 << ATT_UNROLL_LOG2):
                    carry = fn(kc + u, carry)
                return carry

            carry = lax.fori_loop(0, n_groups, group, init)
            return lax.fori_loop(lo + n_groups * (1 << ATT_UNROLL_LOG2), hi, fn, carry)

        def scores(kc, m_run):
            kblk = k_ref[0, 0, tile(kc), :].astype(BF16)
            sc = (jnp.dot(kblk, qs, preferred_element_type=F32) + btile_ref[0, jnp.clip(i - kc, 0, 2)]
                  + penalty_fn(kc))
            s_scr[tile(kc), :] = sc
            return jnp.maximum(m_run, fold(sc, jnp.max))

        m = jnp.max(sweep(scores, jnp.full((8, cols), NEG, F32)), axis=0, keepdims=True)

        def exps(kc, l_run):
            ex = jnp.exp(s_scr[tile(kc), :] - m)
            s_scr[tile(kc), :] = ex
            return l_run + fold(ex, jnp.sum)

        den = jnp.sum(sweep(exps, jnp.zeros((8, cols), F32)), axis=0, keepdims=True)
        inv = 1.0 / jnp.maximum(den, 1e-30)

        def weighted(kc, acc):
            pr = (s_scr[tile(kc), :] * inv).astype(BF16)
            return acc + jnp.dot(vt_ref[0, 0, :, tile(kc)].astype(BF16), pr, preferred_element_type=F32)

        return sweep(weighted, jnp.zeros((HEAD_DIM, cols), F32))

    def sel_penalty(kc):
        expand = jnp.where(jj == kc * blocks_per_step + kk, 1.0, 0.0).astype(BF16)
        chosen = jnp.dot(expand, sel_t, preferred_element_type=F32)
        pen = jnp.concatenate([(chosen - 1.0) * (-NEG)] * HPG, axis=1)
        return jnp.where(qpos >= kc * ATT_TK + key, pen, NEG)

    def win_penalty(kc):
        dist = qpos - (kc * ATT_TK + key)
        return jnp.where((dist >= 0) & (dist <= WINDOW), 0.0, NEG)

    o_sel = attend(ks_ref, vst_ref, 0, i + 1, sel_penalty, whole_groups=True)
    o_win = attend(kw_ref, vwt_ref, jnp.maximum(i - WINDOW // ATT_TK, 0), i + 1, win_penalty)
    gate = jax.nn.sigmoid(gate_ref[0, 0, 0])
    o_ref[0, 0, 0] = gate[0:1] * o_cmp + gate[1:2] * o_sel + gate[2:3] * o_win


def nsa_prompt_attention_t(q, kcmp, vcmp, ks, vs_t, kw, vw_t, gl, rel_table):
    bsz, L = q.shape[0], q.shape[1]
    assert L % ATT_TQ == 0 and kcmp.shape[2] == ATT_TK
    assert math.frexp(SCALE)[0] == 0.5, "the kernel folds SCALE into q, exact only for powers of two"
    nq = L // ATT_TQ
    n_cmp = (L - CMP_BLOCK) // CMP_STRIDE + 1
    n_sel = L // SEL_BLOCK
    assert n_sel <= ATT_TK and nq % (1 << ATT_UNROLL_LOG2) == 0
    cols = HPG * ATT_TQ
    btile, bcmp = _bias_tables(rel_table, nq)
    btile = btile.transpose(0, 1, 3, 2)
    bcmp = bcmp.transpose(0, 1, 3, 2)
    c = np.arange(ATT_TK)[None, :] * CMP_STRIDE
    sb = np.arange(ATT_TK)[:, None] * SEL_BLOCK
    ovl = ((c < sb + SEL_BLOCK) & (c + CMP_BLOCK > sb) & (np.arange(ATT_TK)[None, :] < n_cmp)
           & (np.arange(ATT_TK)[:, None] < n_sel))
    ovl = jnp.asarray(ovl, BF16)
    q_t = q.reshape(bsz, nq, ATT_TQ, KV_HEADS, HPG, HEAD_DIM).transpose(0, 3, 1, 5, 4, 2)
    q_t = q_t.reshape(bsz, KV_HEADS, nq, HEAD_DIM, cols)
    gate_t = gl.reshape(bsz, nq, ATT_TQ, KV_HEADS, HPG, 3).transpose(0, 3, 1, 5, 4, 2)
    gate_t = gate_t.reshape(bsz, KV_HEADS, nq, 3, cols)
    vc_t = vcmp.transpose(0, 1, 3, 2)
    rows_spec = lambda n: pl.BlockSpec((1, 1, n, HEAD_DIM), lambda b, g, i: (b, g, 0, 0))
    cols_spec = lambda n: pl.BlockSpec((1, 1, HEAD_DIM, n), lambda b, g, i: (b, g, 0, 0))
    tile_spec = lambda r: pl.BlockSpec((1, 1, 1, r, cols), lambda b, g, i: (b, g, i, 0, 0))
    o_t = pl.pallas_call(
        functools.partial(_nsa_attn_t_body, n_cmp=n_cmp, n_sel=n_sel, topk=min(SEL_TOPK, n_sel)),
        name="nsa_attention",
        grid=(bsz, KV_HEADS, nq),
        in_specs=[
            tile_spec(HEAD_DIM),
            rows_spec(ATT_TK), cols_spec(ATT_TK), rows_spec(L), cols_spec(L), rows_spec(L), cols_spec(L),
            tile_spec(3),
            pl.BlockSpec((1, 1, ATT_TK, cols), lambda b, g, i: (i, g, 0, 0)),
            pl.BlockSpec((1, 3, ATT_TK, cols), lambda b, g, i: (g, 0, 0, 0)),
            pl.BlockSpec((ATT_TK, ATT_TK), lambda b, g, i: (0, 0)),
        ],
        out_specs=tile_spec(HEAD_DIM),
        out_shape=jax.ShapeDtypeStruct((bsz, KV_HEADS, nq, HEAD_DIM, cols), F32),
        scratch_shapes=[pltpu.VMEM((L, cols), F32)],
        compiler_params=pltpu.CompilerParams(
            dimension_semantics=("parallel", "parallel", "arbitrary"),
            vmem_limit_bytes=VMEM_LIMIT_BYTES),
    )(q_t, kcmp, vc_t, ks, vs_t, kw, vw_t, gate_t, bcmp, btile, ovl)
    o = o_t.reshape(bsz, KV_HEADS, nq, HEAD_DIM, HPG, ATT_TQ).transpose(0, 2, 5, 1, 4, 3)
    return o.reshape(bsz, L, N_HEADS * HEAD_DIM)


def nsa_prompt_mixer(x, g, w_in, q_g, k_g, cmp_pe, cmp_w1, cmp_w2, w_out, rel_table):
    bsz, L, _ = x.shape
    t = bsz * L
    proj = norm_matmul(x.reshape(t, D_MODEL), g, w_in)
    c0 = N_HEADS * HEAD_DIM
    q = _rmsnorm(proj[:, :c0].reshape(bsz, L, N_HEADS, HEAD_DIM), q_g)

    def kvpair(k, normed_g):
        kcols = proj[:, c0 + 2 * k * KV_DIM:c0 + (2 * k + 1) * KV_DIM]
        vcols = proj[:, c0 + (2 * k + 1) * KV_DIM:c0 + (2 * k + 2) * KV_DIM]
        if normed_g is not None:
            kcols = _rmsnorm(kcols.reshape(t, KV_HEADS, HEAD_DIM), normed_g).reshape(t, KV_DIM)
        rows = jnp.stack([kcols, vcols], axis=1).reshape(bsz, L, 2, KV_HEADS, HEAD_DIM)
        k4 = kcols.reshape(bsz, L, KV_HEADS, HEAD_DIM).transpose(0, 2, 1, 3)
        v4 = vcols.reshape(bsz, L, KV_HEADS, HEAD_DIM)
        return rows, k4, v4

    new_cmp, kc_rows, vc_rows = kvpair(0, None)
    new_sel, ks, vs = kvpair(1, k_g[1])
    new_win, kw, vw = kvpair(2, k_g[2])
    gl = proj[:, c0 + 6 * KV_DIM:].reshape(bsz, L, KV_HEADS, HPG, 3)
    vc_rows = vc_rows.transpose(0, 2, 1, 3)
    vs_t, vw_t = vs.transpose(0, 2, 3, 1), vw.transpose(0, 2, 3, 1)

    nb = L // CMP_STRIDE

    def strips(r):
        r = r.reshape(bsz, KV_HEADS, nb, CMP_STRIDE * HEAD_DIM)
        return jnp.pad(r, ((0, 0), (0, 0), (0, ATT_TK + 8 - nb), (0, 0)))

    kcmp, vcmp = compress_rows(strips(kc_rows), strips(vc_rows), cmp_pe, cmp_w1, cmp_w2, k_g[0])
    o = nsa_prompt_attention_t(q, kcmp, vcmp, ks, vs_t, kw, vw_t, gl, rel_table)
    y = norm_matmul(o.reshape(t, c0), None, w_out, normalize=False, res=x.reshape(t, D_MODEL))
    return y.reshape(bsz, L, D_MODEL), new_cmp, new_sel, new_win[:, -min(WINDOW, L):]


DEC_PAGES_PER_STEP = 8
STRIP = CMP_STRIDE
ROW_LANES = 2 * KV_DIM
SEL_LANES = 256


def _group_rmsnorm(x, gain_row):
    lane = lax.broadcasted_iota(jnp.int32, x.shape, 1) // HEAD_DIM
    sq = x * x
    ms = jnp.zeros_like(x)
    for grp in range(KV_HEADS):
        tot = jnp.sum(jnp.where(lane == grp, sq, 0.0), axis=-1, keepdims=True)
        ms = jnp.where(lane == grp, tot, ms)
    return x * lax.rsqrt(ms / HEAD_DIM + EPS) * gain_row


def _decode_cmp_body(pt_ref, *refs, n_strips, n_sel, qblk, topk):
    pages = refs[:DEC_PAGES_PER_STEP]
    (bd_ref, peterm_ref, w2bd_ref, kg_ref, qbd_ref, bias_ref, ovl_ref, upper_ref,
     ocmp_ref, idx_ref, seq, tbuf, hibuf) = refs[DEC_PAGES_PER_STEP:]
    s = pl.program_id(1)
    strips_per_page = pages[0].shape[2] // STRIP
    for r in range(DEC_PAGES_PER_STEP):
        row0 = pl.multiple_of((s * DEC_PAGES_PER_STEP + r) * strips_per_page, strips_per_page)
        for c in range(ROW_LANES // LANES):
            lanes = slice(c * LANES, (c + 1) * LANES)
            tbuf[c] = pages[r][0, lanes, :].T
            for l in range(STRIP):
                seq[l, pl.ds(row0, strips_per_page), lanes] = tbuf[c, pl.ds(l, strips_per_page, stride=STRIP), :]

    @pl.when(s == pl.num_programs(1) - 1)
    def _():
        n_cmp = n_strips - 1
        summaries = []
        for kv in range(2):
            halves = []
            for half in range(2):
                acc = jnp.zeros((n_strips, KV_DIM), F32)
                for l in range(STRIP):
                    w_idx = (kv * 2 + half) * STRIP + l
                    xl = (seq[l, :, kv * KV_DIM:(kv + 1) * KV_DIM] + peterm_ref[w_idx:w_idx + 1, :]).astype(BF16)
                    acc = acc + jnp.dot(xl, bd_ref[w_idx], preferred_element_type=F32)
                halves.append(acc)
            hibuf[0:n_strips, :] = halves[1]
            hibuf[n_strips:n_strips + 8, :] = jnp.zeros((8, KV_DIM), F32)
            h = halves[0] + hibuf[1:n_strips + 1, :]
            h = (h * jax.nn.sigmoid(h)).astype(BF16)
            o = jnp.dot(h, w2bd_ref[kv], preferred_element_type=F32)
            if kv == 0:
                o = _group_rmsnorm(o, kg_ref[...])
            summaries.append(o.astype(BF16))
        kcmp, vcmp = summaries

        qbd = qbd_ref[0].astype(BF16)
        sc = _nt_dot(qbd, kcmp) * SCALE + bias_ref[...]
        col = lax.broadcasted_iota(jnp.int32, sc.shape, 1)
        valid = col < n_cmp
        sc = jnp.where(valid, sc, NEG)
        e = jnp.where(valid, jnp.exp(sc - jnp.max(sc, axis=-1, keepdims=True)), 0.0)
        p = e / jnp.maximum(jnp.sum(e, axis=-1, keepdims=True), 1e-30)
        pb = p.astype(BF16)
        o_cmp = jnp.dot(pb, vcmp, preferred_element_type=F32)
        head_grp = lax.broadcasted_iota(jnp.int32, o_cmp.shape, 0) // HPG
        lane_grp = lax.broadcasted_iota(jnp.int32, o_cmp.shape, 1) // HEAD_DIM
        ocmp_ref[0] = jnp.where(head_grp == lane_grp, o_cmp, 0.0)

        psum = jnp.concatenate(
            [jnp.sum(p[grp * HPG:(grp + 1) * HPG], axis=0, keepdims=True) for grp in range(KV_HEADS)]
            + [jnp.zeros((8 - KV_HEADS, n_strips), F32)], axis=0)
        imp = jnp.dot(psum.astype(BF16), ovl_ref[...], preferred_element_type=F32)
        j = lax.broadcasted_iota(jnp.int32, imp.shape, 1)
        forced = (j == qblk) | (j == 0)
        score = jnp.where(forced, 1e9, jnp.where(j <= qblk, imp, NEG))
        score = jnp.where(j < n_sel, score, -3e38)
        rank = jnp.zeros(imp.shape, F32)
        for jp in range(n_sel):
            other = score[:, jp:jp + 1]
            beats = (other > score) | ((other == score) & (jp < j))
            rank = rank + jnp.where(beats, 1.0, 0.0)
        chosen = (rank < topk) & (j < n_sel)
        cum = jnp.dot(jnp.where(chosen, 1.0, 0.0).astype(BF16), upper_ref[...], preferred_element_type=F32)
        jf = j.astype(F32)
        out_lane = lax.broadcasted_iota(jnp.int32, (8, LANES), 1)
        out = jnp.zeros((8, LANES), F32)
        for k in range(topk):
            pick = jnp.sum(jnp.where(chosen & (cum == k + 1.0), jf, 0.0), axis=-1, keepdims=True)
            out = jnp.where(out_lane == k, pick, out)
        idx_ref[0] = out


def _decode_attn_body(tbl_ref, idx_ref, *refs, n_blk, qblk, per_page):
    pages = refs[:n_blk]
    (win_ref, q_ref, knew_ref, vnew_ref, wknew_ref, wvnew_ref, bsel_ref, bwin_ref, ocmp_ref, gate_ref,
     o_ref) = refs[n_blk:]
    b, g = pl.program_id(0), pl.program_id(1)
    qb = q_ref[0, 0].astype(BF16)
    qf = qb.astype(F32)
    k_rows = pl.ds(pl.multiple_of(g * HEAD_DIM, HEAD_DIM), HEAD_DIM)
    v_rows = pl.ds(pl.multiple_of(KV_DIM + g * HEAD_DIM, HEAD_DIM), HEAD_DIM)

    def attend(parts, k_new, v_new, bias_new):
        s_new = jnp.sum(qf * k_new.astype(BF16).astype(F32), axis=-1, keepdims=True) * SCALE + bias_new
        m = s_new
        for sc, ok, _ in parts:
            m = jnp.maximum(m, jnp.max(jnp.where(ok, sc, NEG), axis=-1, keepdims=True))
        e_new = jnp.exp(s_new - m)
        es = [jnp.where(ok, jnp.exp(sc - m), 0.0) for sc, ok, _ in parts]
        den = e_new
        for e in es:
            den = den + jnp.sum(e, axis=-1, keepdims=True)
        den = jnp.maximum(den, 1e-30)
        acc = (e_new / den).astype(BF16).astype(F32) * v_new.astype(BF16).astype(F32)
        for e, (_, _, v_t) in zip(es, parts):
            acc = acc + _nt_dot((e / den).astype(BF16), v_t)
        return acc

    sel_parts = []
    for k in range(n_blk):
        bidx = idx_ref[b, g, k]
        k_t = pages[k][0, k_rows, :].astype(BF16)
        v_t = pages[k][0, v_rows, :].astype(BF16)
        pg = jnp.minimum(bidx, qblk) // per_page
        sc = jnp.dot(qb, k_t, preferred_element_type=F32) * SCALE + bsel_ref[pg, 0]
        lane_blk = lax.broadcasted_iota(jnp.int32, sc.shape, 1) // SEL_BLOCK
        ok = (lane_blk == bidx % per_page) & (bidx < qblk)
        sel_parts.append((sc, ok, v_t))
    bias0 = bsel_ref[qblk // per_page, 0][:, 0:1]
    o_sel = attend(sel_parts, knew_ref[0, 0], vnew_ref[0, 0], bias0)

    sc = jnp.dot(qb, win_ref[0, k_rows, :].astype(BF16), preferred_element_type=F32) * SCALE + bwin_ref[0]
    o_win = attend([(sc, jnp.full(sc.shape, True), win_ref[0, v_rows, :].astype(BF16))],
                   wknew_ref[0, 0], wvnew_ref[0, 0], bias0)

    gate = jax.nn.sigmoid(gate_ref[0, 0])
    o_ref[0, 0] = gate[:, 0:1] * ocmp_ref[0, 0] + gate[:, 1:2] * o_sel + gate[:, 2:3] * o_win


def nsa_decode_mixer(x, g, past_len, cache_cmp, cache_sel, win_past, page_table,
                     w_in, q_g, k_g, cmp_pe, cmp_w1, cmp_w2, w_out, rel_table):
    bsz = x.shape[0]
    n_pool, page = cache_cmp.shape[0], cache_cmp.shape[1]
    n_pages = page_table.shape[1]
    assert past_len == n_pages * page and page % STRIP == 0 and n_pages % DEC_PAGES_PER_STEP == 0
    assert win_past.shape[1] == WINDOW and past_len >= WINDOW and page % SEL_BLOCK == 0
    x2 = x.reshape(bsz, D_MODEL)
    proj = norm_matmul(x2, g, w_in)
    c0 = N_HEADS * HEAD_DIM
    q = _rmsnorm(proj[:, :c0].reshape(bsz, N_HEADS, HEAD_DIM), q_g)

    def rows(k, gain):
        kcols = proj[:, c0 + 2 * k * KV_DIM:c0 + (2 * k + 1) * KV_DIM]
        vcols = proj[:, c0 + (2 * k + 1) * KV_DIM:c0 + (2 * k + 2) * KV_DIM]
        if gain is not None:
            kcols = _rmsnorm(kcols.reshape(bsz, KV_HEADS, HEAD_DIM), gain).reshape(bsz, KV_DIM)
        return kcols, vcols

    kc_new, vc_new = rows(0, None)
    ks_new, vs_new = rows(1, k_g[1])
    kw_new, vw_new = rows(2, k_g[2])
    gl = proj[:, c0 + 6 * KV_DIM:].reshape(bsz, KV_HEADS, HPG, 3)
    as_row = lambda kk, vv: jnp.stack([kk, vv], axis=1).reshape(bsz, 1, 2, KV_HEADS, HEAD_DIM)
    new_cmp, new_sel, new_win = as_row(kc_new, vc_new), as_row(ks_new, vs_new), as_row(kw_new, vw_new)

    eye = jnp.eye(KV_HEADS, dtype=F32)
    qbd = jnp.einsum('bghd,gk->bghkd', q.reshape(bsz, KV_HEADS, HPG, HEAD_DIM), eye).reshape(bsz, N_HEADS, KV_DIM)
    w1r = cmp_w1.reshape(2, 2, STRIP, HEAD_DIM, HEAD_DIM)
    bd = jnp.einsum('khlio,gj->khlgijo', w1r, eye).reshape(2 * 2 * STRIP, KV_DIM, KV_DIM).astype(BF16)
    w2bd = jnp.einsum('kio,gj->kgijo', cmp_w2, eye).reshape(2, KV_DIM, KV_DIM).astype(BF16)
    peterm = jnp.tile(cmp_pe.reshape(2 * 2 * STRIP, HEAD_DIM), (1, KV_HEADS))
    kg_row = jnp.tile(k_g[0], KV_HEADS).reshape(1, KV_DIM)

    n_strips = past_len // STRIP
    n_cmp = n_strips - 1
    qblk = past_len // SEL_BLOCK
    n_sel = qblk + 1
    topk = min(SEL_TOPK, n_sel)
    assert n_sel <= SEL_LANES and n_strips % 8 == 0
    tab = rel_table[_rel_bucket(jnp.arange(REL_MAX_DIST + 1))].astype(F32).T
    ends = np.arange(n_strips) * CMP_STRIDE + CMP_BLOCK - 1
    bias_cmp = tab[:, np.clip(past_len - ends, 0, REL_MAX_DIST)]
    cs = np.arange(n_strips)[:, None] * CMP_STRIDE
    ss = np.arange(SEL_LANES)[None, :] * SEL_BLOCK
    ovl = (cs < ss + SEL_BLOCK) & (cs + CMP_BLOCK > ss) & (np.arange(n_strips)[:, None] < n_cmp) \
        & (np.arange(SEL_LANES)[None, :] < n_sel)
    ovl = jnp.asarray(ovl, BF16)
    upper = jnp.asarray(np.triu(np.ones((SEL_LANES, SEL_LANES), np.float32)), BF16)

    strips_per_page = page // STRIP
    cmp_view = cache_cmp.reshape(n_pool, page, ROW_LANES).transpose(0, 2, 1)
    steps = n_pages // DEC_PAGES_PER_STEP
    const = lambda a: pl.BlockSpec(a.shape, lambda b, s, pt: (0,) * a.ndim)
    page_spec = lambda r: pl.BlockSpec((1, ROW_LANES, page),
                                       lambda b, s, pt: (pt[b, s * DEC_PAGES_PER_STEP + r], 0, 0))
    o_cmp, idx = pl.pallas_call(
        functools.partial(_decode_cmp_body, n_strips=n_strips, n_sel=n_sel, qblk=qblk, topk=topk),
        name="nsa_decode_cmp",
        grid_spec=pltpu.PrefetchScalarGridSpec(
            num_scalar_prefetch=1,
            grid=(bsz, steps),
            in_specs=[page_spec(r) for r in range(DEC_PAGES_PER_STEP)]
            + [const(bd), const(peterm), const(w2bd), const(kg_row),
               pl.BlockSpec((1, N_HEADS, KV_DIM), lambda b, s, pt: (b, 0, 0)),
               const(bias_cmp), const(ovl), const(upper)],
            out_specs=[pl.BlockSpec((1, N_HEADS, KV_DIM), lambda b, s, pt: (b, 0, 0)),
                       pl.BlockSpec((1, 8, LANES), lambda b, s, pt: (b, 0, 0))],
            scratch_shapes=[pltpu.VMEM((STRIP, n_strips, ROW_LANES), F32),
                            pltpu.VMEM((ROW_LANES // LANES, page, LANES), F32),
                            pltpu.VMEM((n_strips + 8, KV_DIM), F32)],
        ),
        out_shape=[jax.ShapeDtypeStruct((bsz, N_HEADS, KV_DIM), F32),
                   jax.ShapeDtypeStruct((bsz, 8, LANES), F32)],
        compiler_params=pltpu.CompilerParams(
            dimension_semantics=("parallel", "arbitrary"),
            vmem_limit_bytes=DECODE_VMEM_LIMIT_BYTES),
    )(page_table, *([cmp_view] * DEC_PAGES_PER_STEP), bd, peterm, w2bd, kg_row, qbd, bias_cmp, ovl, upper)

    blk_idx = idx[:, :KV_HEADS, :topk].astype(jnp.int32)
    per_page = page // SEL_BLOCK
    safe = jnp.minimum(blk_idx, qblk - 1)
    page_of = jnp.take_along_axis(page_table, (safe // per_page).reshape(bsz, -1), axis=1).reshape(safe.shape)
    sel_view = cache_sel.reshape(n_pool, page, ROW_LANES).transpose(0, 2, 1)
    win_view = win_past.reshape(bsz, WINDOW, ROW_LANES).transpose(0, 2, 1)

    n_pg = n_pages + 1
    dist_sel = past_len - (np.arange(n_pg)[:, None] * page + np.arange(page)[None, :])
    bsel = tab[:, np.clip(dist_sel, 0, REL_MAX_DIST)]
    bsel = bsel.reshape(KV_HEADS, HPG, n_pg, page).transpose(2, 0, 1, 3)
    bwin = tab[:, np.clip(past_len - (past_len - WINDOW + np.arange(WINDOW)), 0, REL_MAX_DIST)]
    bwin = bwin.reshape(KV_HEADS, HPG, WINDOW)

    per_group = lambda a: a.reshape(bsz, KV_HEADS, 1, HEAD_DIM)
    q4 = q.reshape(bsz, KV_HEADS, HPG, HEAD_DIM)
    ocmp4 = o_cmp.reshape(bsz, KV_HEADS, HPG, KV_HEADS, HEAD_DIM).sum(axis=3)
    page_spec2 = lambda k: pl.BlockSpec((1, ROW_LANES, page), lambda b, g, t, i: (t[b, g, k], 0, 0))
    per_bg = lambda *shape: pl.BlockSpec((1, 1) + shape, lambda b, g, t, i: (b, g) + (0,) * len(shape))
    o4 = pl.pallas_call(
        functools.partial(_decode_attn_body, n_blk=topk, qblk=qblk, per_page=per_page),
        name="nsa_decode_attn",
        grid_spec=pltpu.PrefetchScalarGridSpec(
            num_scalar_prefetch=2,
            grid=(bsz, KV_HEADS),
            in_specs=[page_spec2(k) for k in range(topk)]
            + [pl.BlockSpec((1, ROW_LANES, WINDOW), lambda b, g, t, i: (b, 0, 0)),
               per_bg(HPG, HEAD_DIM), per_bg(1, HEAD_DIM), per_bg(1, HEAD_DIM), per_bg(1, HEAD_DIM),
               per_bg(1, HEAD_DIM),
               pl.BlockSpec((n_pg, 1, HPG, page), lambda b, g, t, i: (0, g, 0, 0)),
               pl.BlockSpec((1, HPG, WINDOW), lambda b, g, t, i: (g, 0, 0)),
               per_bg(HPG, HEAD_DIM), per_bg(HPG, 3)],
            out_specs=per_bg(HPG, HEAD_DIM),
        ),
        out_shape=jax.ShapeDtypeStruct((bsz, KV_HEADS, HPG, HEAD_DIM), F32),
        compiler_params=pltpu.CompilerParams(
            dimension_semantics=("parallel", "arbitrary"),
            vmem_limit_bytes=VMEM_LIMIT_BYTES),
    )(page_of, blk_idx, *([sel_view] * topk), win_view, q4, per_group(ks_new), per_group(vs_new),
      per_group(kw_new), per_group(vw_new), bsel, bwin, ocmp4, gl)
    y = norm_matmul(o4.reshape(bsz, c0), None, w_out, normalize=False, res=x2)
    new_win_buf = jnp.concatenate([win_past[:, 1:], new_win], axis=1)
    return y.reshape(x.shape), new_cmp, new_sel, new_win_buf


def _gla_chunked(q, k, v, logf, s0):
    bsz, L = q.shape[0], q.shape[1]
    nc = -(-L // H_CHUNK)
    pad = nc * H_CHUNK - L

    def prep(t):
        t = jnp.pad(t, ((0, 0), (0, pad), (0, 0), (0, 0)))
        return t.reshape(bsz, nc, H_CHUNK, H_HEADS, t.shape[-1])

    q, k, v, logf = prep(q), prep(k), prep(v), prep(logf)
    acum = jnp.cumsum(logf, axis=2)
    alast = acum[:, :, -1:]
    qe = q * jnp.exp(acum)
    ke = k * jnp.exp(-acum)
    kd = k * jnp.exp(alast - acum)
    tri = jnp.tril(jnp.ones((H_CHUNK, H_CHUNK), bool))
    att = jnp.where(tri, jnp.einsum('bcthk,bcshk->bchts', qe, ke), 0.0)
    o_intra = jnp.einsum('bchts,bcshv->bcthv', att, v)
    upd = jnp.einsum('bcshk,bcshv->bchkv', kd, v)

    def step(s, inp):
        dec, up = inp
        return dec[..., None] * s + up, s

    s_last, s_prev = lax.scan(step, s0, (jnp.exp(alast[:, :, 0]).swapaxes(0, 1), upd.swapaxes(0, 1)))
    o_inter = jnp.einsum('bcthk,bchkv->bcthv', qe, s_prev.swapaxes(0, 1))
    o = (o_intra + o_inter).reshape(bsz, nc * H_CHUNK, H_HEADS, H_DV)[:, :L]
    return o, s_last


HGRN_TOKENS = 128


def _hgrn_body(q_ref, f_ref, v_ref, gate_ref, lb_ref, ng_ref, lcum_ref, lsum_ref, o_ref, s_ref):
    @pl.when(pl.program_id(1) == 0)
    def _():
        s_ref[...] = jnp.zeros_like(s_ref)

    n = HGRN_TOKENS
    qr = q_ref[...]
    q = qr * jax.nn.sigmoid(qr)
    lb = lb_ref[...]
    forget = lb + (1.0 - lb) * jax.nn.sigmoid(f_ref[...])
    logf = jnp.log(forget)
    k = 1.0 - forget
    parts = _split3(logf)
    acum = sum(jnp.dot(lcum_ref[...], p, preferred_element_type=F32) for p in parts)
    atot = sum(jnp.dot(lsum_ref[...], p, preferred_element_type=F32) for p in parts)
    qe = (q * jnp.exp(acum)).astype(BF16)
    ke = (k * jnp.exp(-acum)).astype(BF16)
    kd = (k * jnp.exp(atot - acum)).astype(BF16)
    vb = v_ref[...].astype(BF16)
    ti = lax.broadcasted_iota(jnp.int32, (n, n), 0)
    si = lax.broadcasted_iota(jnp.int32, (n, n), 1)
    intra = (ti // H_CHUNK == si // H_CHUNK) & (ti >= si)
    gate = gate_ref[...]
    gate = gate * jax.nn.sigmoid(gate)
    outs = []
    for h in range(H_HEADS):
        cols = slice(h * H_DK, (h + 1) * H_DK)
        att = jnp.where(intra, _nt_dot(qe[:, cols], ke[:, cols]), 0.0)
        o_h = jnp.dot(att.astype(BF16), vb[:, cols], preferred_element_type=F32)
        decay_t = jnp.exp(atot[:, cols]).T
        state = s_ref[0, cols, :]
        inter = []
        for j in range(n // H_CHUNK):
            rows = slice(j * H_CHUNK, (j + 1) * H_CHUNK)
            inter.append(jnp.dot(qe[rows, cols], state.astype(BF16), preferred_element_type=F32))
            upd = lax.dot_general(kd[rows, cols], vb[rows, cols], (((0,), (0,)), ((), ())),
                                  preferred_element_type=F32)
            state = decay_t[:, j * H_CHUNK:j * H_CHUNK + 1] * state + upd
        s_ref[0, cols, :] = state
        o_h = o_h + jnp.concatenate(inter, axis=0)
        o_h = o_h * lax.rsqrt(jnp.mean(o_h * o_h, axis=-1, keepdims=True) + EPS) * ng_ref[...]
        outs.append(o_h * gate[:, cols])
    o_ref[...] = jnp.concatenate(outs, axis=1).astype(BF16)


def hgrn2_prompt_mixer(x, g, lb, w_in, norm_g, w_out):
    bsz, L, _ = x.shape
    n = HGRN_TOKENS
    assert L % n == 0 and n % H_CHUNK == 0 and H_DK == LANES and H_DV == LANES
    t = bsz * L
    x2 = x.reshape(t, D_MODEL)
    proj = norm_matmul(x2, g, w_in)
    nb = L // n
    r = np.arange(n)
    same = (r[:, None] // H_CHUNK) == (r[None, :] // H_CHUNK)
    lcum = jnp.asarray(same & (r[None, :] <= r[:, None]), BF16)
    lsum = jnp.asarray(same, BF16)
    col = lambda kk: pl.BlockSpec((n, D_MODEL), lambda b, c: (b * nb + c, kk))
    const = lambda a: pl.BlockSpec(a.shape, lambda b, c: (0,) * a.ndim)
    lb2 = lb.reshape(1, D_MODEL)
    ng2 = norm_g.reshape(1, H_DV)
    o, s = pl.pallas_call(
        _hgrn_body,
        name="hgrn_blocks",
        grid=(bsz, nb),
        in_specs=[col(0), col(1), col(2), col(3), const(lb2), const(ng2), const(lcum), const(lsum)],
        out_specs=[pl.BlockSpec((n, D_MODEL), lambda b, c: (b * nb + c, 0)),
                   pl.BlockSpec((1, H_HEADS * H_DK, H_DV), lambda b, c: (b, 0, 0))],
        out_shape=[jax.ShapeDtypeStruct((t, D_MODEL), BF16),
                   jax.ShapeDtypeStruct((bsz, H_HEADS * H_DK, H_DV), F32)],
        compiler_params=pltpu.CompilerParams(
            dimension_semantics=("parallel", "arbitrary"),
            vmem_limit_bytes=VMEM_LIMIT_BYTES),
    )(proj, proj, proj, proj, lb2, ng2, lcum, lsum)
    y = norm_matmul(o, None, w_out, normalize=False, res=x2)
    return y.reshape(bsz, L, D_MODEL), s.reshape(bsz, H_HEADS, H_DK, H_DV)


def hgrn2_mixer(x, g, s0, lb, w_in, norm_g, w_out):
    bsz, L, _ = x.shape
    proj = norm_matmul(x.reshape(bsz * L, D_MODEL), g, w_in).reshape(bsz, L, -1)
    q, fr, iv, gate = jnp.split(proj, 4, axis=-1)
    q = jax.nn.silu(q).reshape(bsz, L, H_HEADS, H_DK)
    forget = lb + (1.0 - lb) * jax.nn.sigmoid(fr)
    logf = jnp.log(forget).reshape(bsz, L, H_HEADS, H_DK)
    k = (1.0 - forget).reshape(bsz, L, H_HEADS, H_DK)
    v = iv.reshape(bsz, L, H_HEADS, H_DV)
    o, s = _gla_chunked(q, k, v, logf, s0)
    o = _rmsnorm(o, norm_g) * jax.nn.silu(gate.reshape(bsz, L, H_HEADS, H_DV))
    y = norm_matmul(o.reshape(bsz * L, D_MODEL), None, w_out, normalize=False)
    return y.reshape(bsz, L, D_MODEL), s


def _gather_pages(cache, page_table):
    pages = cache[page_table]
    b, n, p = pages.shape[0], pages.shape[1], pages.shape[2]
    return pages.reshape(b, n * p, *pages.shape[3:])


def kernel(x_prompt, x_sample, state_ssm, state_conv, cache_kv_cmp, cache_kv_sel, cache_kv_win, state_hgrn, page_table, norm_g, rel_table, m_w_in, m_conv_w, m_conv_b, m_dt_bias, m_a_log, m_d, m_norm_g, m_w_out, n_w_in, n_q_g, n_k_g, n_cmp_pe, n_cmp_w1, n_cmp_w2, n_w_out, h_w_in, h_lb, h_norm_g, h_w_out, moe_w_rg, moe_b_rg, moe_w_re, moe_b_re, moe_w1, moe_w3, moe_w2):
    bp, lp = x_prompt.shape[0], x_prompt.shape[1]
    past_len = page_table.shape[1] * cache_kv_cmp.shape[2]
    dt = x_prompt.dtype
    m_w_in, m_w_out, n_w_in, n_w_out, h_w_in, h_w_out, moe_w1, moe_w3, moe_w2 = (
        w.astype(BF16) for w in (m_w_in, m_w_out, n_w_in, n_w_out, h_w_in, h_w_out, moe_w1, moe_w3, moe_w2))
    lbs = jax.nn.softmax(h_lb.astype(F32), axis=0)
    lbs = jnp.cumsum(lbs, axis=0) - lbs[0]
    xp, xs = x_prompt, x_sample
    ssm_p, conv_p, cmp_p, sel_p, win_p, hg_p = [], [], [], [], [], []
    ssm_s, conv_s, cmp_s, sel_s, win_s, hg_s = [], [], [], [], [], []
    for i in range(DEPTH):
        kind, j = i % N_MIXERS, i // N_MIXERS
        g0 = norm_g[i, 0]
        if kind == 0:
            w = (m_w_in[j], m_conv_w[j], m_conv_b[j], m_dt_bias[j], m_a_log[j], m_d[j], m_norm_g[j], m_w_out[j])
            xp, cbuf, hh = mamba_prompt_mixer(xp, g0, *w)
            ssm_p.append(hh)
            conv_p.append(cbuf)
            ys, cbuf, hh = mamba_mixer(xs, g0, state_conv[j], state_ssm[j], *w)
            xs = xs + ys
            ssm_s.append(hh)
            conv_s.append(cbuf)
        elif kind == 1:
            w = (n_w_in[j], n_q_g[j], n_k_g[j], n_cmp_pe[j], n_cmp_w1[j], n_cmp_w2[j], n_w_out[j], rel_table)
            xp, rc, rs, wb = nsa_prompt_mixer(xp, g0, *w)
            cmp_p.append(rc)
            sel_p.append(rs)
            win_p.append(wb)
            xs, rc, rs, wb = nsa_decode_mixer(xs, g0, past_len, cache_kv_cmp[j], cache_kv_sel[j],
                                              cache_kv_win[j], page_table, *w)
            cmp_s.append(rc)
            sel_s.append(rs)
            win_s.append(wb)
        else:
            w = (lbs[i], h_w_in[j], h_norm_g[j], h_w_out[j])
            xp, st = hgrn2_prompt_mixer(xp, g0, *w)
            hg_p.append(st)
            ys, st = hgrn2_mixer(xs, g0, state_hgrn[j], *w)
            xs = xs + ys
            hg_s.append(st)
        mw = (moe_w_rg[i], moe_b_rg[i], moe_w_re[i], moe_b_re[i], moe_w1[i], moe_w3[i], moe_w2[i])
        xp = hier_moe_residual(xp.reshape(-1, D_MODEL), norm_g[i, 1], *mw).reshape(xp.shape)
        xs = hier_moe_residual(xs.reshape(-1, D_MODEL), norm_g[i, 1], *mw).reshape(xs.shape)
    return (xp, xs,
            jnp.stack(ssm_p), jnp.stack(conv_p), jnp.stack(cmp_p), jnp.stack(sel_p), jnp.stack(win_p), jnp.stack(hg_p),
            jnp.stack(ssm_s), jnp.stack(conv_s), jnp.stack(cmp_s), jnp.stack(sel_s), jnp.stack(win_s), jnp.stack(hg_s))
```

```python
import functools
import math

import jax
import jax.numpy as jnp
import numpy as np
from jax import lax
from jax.experimental import pallas as pl
from jax.experimental.pallas import tpu as pltpu

F32 = jnp.float32
BF16 = jnp.bfloat16
EPS = 1e-6

D_MODEL = 1024
DEPTH = 4
N_MIXERS = 3

M_DINNER = 2 * D_MODEL
M_HEADDIM = 64
M_HEADS = M_DINNER // M_HEADDIM
M_GROUPS = 4
M_DSTATE = 128
M_CONV = 4
M_GN = M_GROUPS * M_DSTATE
M_CONV_DIM = M_DINNER + 2 * M_GN
M_CHUNK = 128

N_HEADS = 16
HEAD_DIM = D_MODEL // N_HEADS
KV_HEADS = 4
HPG = N_HEADS // KV_HEADS
KV_DIM = KV_HEADS * HEAD_DIM
CMP_BLOCK = 32
CMP_STRIDE = 16
SEL_BLOCK = 64
SEL_TOPK = 16
WINDOW = 512
SCALE = HEAD_DIM ** -0.5
REL_BUCKETS = 32
REL_MAX_DIST = 128

H_DK = 128
H_HEADS = D_MODEL // H_DK
H_DV = D_MODEL // H_HEADS
H_CHUNK = 32

MOE_GROUPS = 4
MOE_EPG = 4
MOE_EXPERTS = MOE_GROUPS * MOE_EPG
MOE_FF = 512

VMEM_LIMIT_BYTES = 48 * 1024 * 1024
DECODE_VMEM_LIMIT_BYTES = 56 * 1024 * 1024


def _rmsnorm(x, g):
    xf = x.astype(F32)
    y = xf * lax.rsqrt(jnp.mean(xf * xf, axis=-1, keepdims=True) + EPS)
    return (y * g.astype(F32)).astype(x.dtype)


LANES = 128
MXU_WIDTH = 256


def _norm_matmul_body(*refs, normalize, has_res):
    if has_res:
        x_ref, g_ref, w_ref, res_ref, o_ref, xb_ref = refs
    else:
        x_ref, g_ref, w_ref, o_ref, xb_ref = refs

    @pl.when(pl.program_id(1) == 0)
    def _():
        x = x_ref[...].astype(F32)
        if normalize:
            x = x * lax.rsqrt(jnp.mean(x * x, axis=-1, keepdims=True) + EPS) * g_ref[...]
        xb_ref[...] = x.astype(BF16)

    acc = jnp.dot(xb_ref[...], w_ref[...].astype(BF16), preferred_element_type=F32)
    if has_res:
        acc = acc + res_ref[...]
    o_ref[...] = acc


def _pick_tile(n, pref):
    t = min(n, pref)
    while n % t:
        t //= 2
    return t


def norm_matmul(x, g, w, *, normalize=True, res=None, tm=None, tn=512):
    t, k = x.shape
    n_true = w.shape[1]
    tm = _pick_tile(t, tm or (2048 if k <= 1024 else 1024))
    col_tile = MXU_WIDTH if n_true > MXU_WIDTH else LANES
    if n_true % col_tile:
        assert res is None
        w = jnp.pad(w, ((0, 0), (0, col_tile - n_true % col_tile)))
    n = w.shape[1]
    tn = _pick_tile(n, tn)
    if g is None:
        g = jnp.ones((k,), F32)
    in_specs = [
        pl.BlockSpec((tm, k), lambda i, j: (i, 0)),
        pl.BlockSpec((1, k), lambda i, j: (0, 0)),
        pl.BlockSpec((k, tn), lambda i, j: (0, j)),
    ]
    args = [x, g.reshape(1, k), w]
    if res is not None:
        in_specs.append(pl.BlockSpec((tm, tn), lambda i, j: (i, j)))
        args.append(res)
    out = pl.pallas_call(
        functools.partial(_norm_matmul_body, normalize=normalize, has_res=res is not None),
        name="norm_matmul",
        grid=(t // tm, n // tn),
        in_specs=in_specs,
        out_specs=pl.BlockSpec((tm, tn), lambda i, j: (i, j)),
        out_shape=jax.ShapeDtypeStruct((t, n), F32),
        scratch_shapes=[pltpu.VMEM((tm, k), BF16)],
        compiler_params=pltpu.CompilerParams(
            dimension_semantics=("parallel", "arbitrary"),
            vmem_limit_bytes=VMEM_LIMIT_BYTES),
    )(*args)
    return out if n == n_true else out[:, :n_true]


ROUTE_LANES = 128
MOE_TILE = 1024
MOE_ROWS = 160
MOE_EXPERTS_PER_STEP = 2
NEG = -1e30


def _router_body(x_ref, g_ref, w_ref, b_ref, u_ref, xn_ref, rank_ref, wt_ref):
    x = x_ref[...]
    tm = x.shape[0]
    xn = x * lax.rsqrt(jnp.mean(x * x, axis=-1, keepdims=True) + EPS) * g_ref[...]
    xb = xn.astype(BF16)
    xn_ref[...] = xb
    logits = jnp.dot(xb, w_ref[...].astype(BF16), preferred_element_type=F32) + b_ref[...]
    lane = lax.broadcasted_iota(jnp.int32, (tm, ROUTE_LANES), 1).astype(F32)

    def first_max(mask):
        v = jnp.max(jnp.where(mask, logits, NEG), axis=-1, keepdims=True)
        i = jnp.min(jnp.where(mask & (logits == v), lane, float(ROUTE_LANES)), axis=-1, keepdims=True)
        return v, i

    is_group = lane < MOE_GROUPS
    mg, g_idx = first_max(is_group)
    pg_top = 1.0 / jnp.sum(jnp.where(is_group, jnp.exp(logits - mg), 0.0), axis=-1, keepdims=True)
    lo = MOE_GROUPS + MOE_EPG * g_idx
    in_group = (lane >= lo) & (lane < lo + MOE_EPG)
    v1, i1 = first_max(in_group)
    v2, i2 = first_max(in_group & (lane != i1))
    e2 = jnp.exp(v2 - v1)
    w_a = pg_top / (1.0 + e2)
    w_b = pg_top * e2 / (1.0 + e2)
    info = jnp.where(lane == 0, i1 - MOE_GROUPS, jnp.where(lane == 1, i2 - MOE_GROUPS,
                     jnp.where(lane == 2, w_a, jnp.where(lane == 3, w_b, 0.0))))
    info_t = info.T
    e_a, e_b, w_at, w_bt = info_t[0:1], info_t[1:2], info_t[2:3], info_t[3:4]
    expert = lax.broadcasted_iota(jnp.int32, (MOE_EXPERTS, tm), 0).astype(F32)
    m_a = e_a == expert
    m_b = e_b == expert
    onehot = jnp.concatenate([jnp.where(m_a, 1.0, 0.0), jnp.where(m_b, 1.0, 0.0)], axis=0).astype(BF16)
    cum = jnp.dot(onehot, u_ref[...], preferred_element_type=F32)
    cum_a, cum_b = cum[:MOE_EXPERTS], cum[MOE_EXPERTS:]
    n_a = cum_a[:, tm - 1:tm]
    rank_ref[...] = jnp.where(m_a, cum_a - 1.0, jnp.where(m_b, n_a + cum_b - 1.0, -1.0))
    wt_ref[...] = jnp.where(m_a, w_at, jnp.where(m_b, w_bt, 0.0))


def moe_route(x, g, w_rg, b_rg, w_re, b_re, tm):
    t = x.shape[0]
    pad = ROUTE_LANES - MOE_GROUPS - MOE_EXPERTS
    w = jnp.pad(jnp.concatenate([w_rg, w_re], axis=1), ((0, 0), (0, pad)))
    b = jnp.pad(jnp.concatenate([b_rg, b_re]), (0, pad)).reshape(1, ROUTE_LANES)
    upper = jnp.asarray(np.triu(np.ones((tm, tm), np.float32)), BF16)
    const = lambda a: pl.BlockSpec(a.shape, lambda i: (0,) * a.ndim)
    g2 = g.reshape(1, D_MODEL)
    return pl.pallas_call(
        _router_body,
        name="moe_router",
        grid=(t // tm,),
        in_specs=[pl.BlockSpec((tm, D_MODEL), lambda i: (i, 0)), const(g2), const(w), const(b), const(upper)],
        out_specs=[pl.BlockSpec((tm, D_MODEL), lambda i: (i, 0)),
                   pl.BlockSpec((MOE_EXPERTS, tm), lambda i: (0, i)),
                   pl.BlockSpec((MOE_EXPERTS, tm), lambda i: (0, i))],
        out_shape=[jax.ShapeDtypeStruct((t, D_MODEL), BF16),
                   jax.ShapeDtypeStruct((MOE_EXPERTS, t), F32),
                   jax.ShapeDtypeStruct((MOE_EXPERTS, t), F32)],
        compiler_params=pltpu.CompilerParams(
            dimension_semantics=("parallel",), vmem_limit_bytes=VMEM_LIMIT_BYTES),
    )(x, g2, w, b, upper)


def _moe_expert_body(cnt_ref, xn_ref, rank_ref, wt_ref, w1_ref, w3_ref, w2_ref, res_ref, o_ref, *, rows):
    ti, step = pl.program_id(0), pl.program_id(1)

    @pl.when(step == 0)
    def _():
        o_ref[...] = res_ref[...]

    tm = xn_ref.shape[0]
    experts = [step * MOE_EXPERTS_PER_STEP + k for k in range(MOE_EXPERTS_PER_STEP)]
    ranks = [rank_ref[pl.ds(e, 1), :] for e in experts]
    wts = [wt_ref[pl.ds(e, 1), :] for e in experts]
    n_chunks = functools.reduce(jnp.maximum, [(cnt_ref[e, ti] + rows - 1) // rows for e in experts])

    def chunk(c, carry):
        r = (lax.broadcasted_iota(jnp.int32, (rows, tm), 0) + c * rows).astype(F32)
        total = None
        for k in range(MOE_EXPERTS_PER_STEP):
            sel = jnp.where(ranks[k] == r, 1.0, 0.0)
            selb = sel.astype(BF16)
            xs = jnp.dot(selb, xn_ref[...], preferred_element_type=F32).astype(BF16)
            a = jnp.dot(xs, w1_ref[k].astype(BF16), preferred_element_type=F32)
            b = jnp.dot(xs, w3_ref[k].astype(BF16), preferred_element_type=F32)
            h = (a * jax.nn.sigmoid(a) * b * jnp.sum(sel * wts[k], axis=1, keepdims=True)).astype(BF16)
            y = jnp.dot(h, w2_ref[k].astype(BF16), preferred_element_type=F32)
            y_hi = y.astype(BF16)
            y_lo = (y - y_hi.astype(F32)).astype(BF16)
            tn = (((0,), (0,)), ((), ()))
            back = (lax.dot_general(selb, y_hi, tn, preferred_element_type=F32)
                    + lax.dot_general(selb, y_lo, tn, preferred_element_type=F32))
            total = back if total is None else total + back
        o_ref[...] += total
        return carry

    lax.fori_loop(0, n_chunks, chunk, 0)


def moe_experts(xn, rank, wt, counts, w1, w3, w2, res, tm):
    t = xn.shape[0]
    rows = min(MOE_ROWS, tm)
    grid_spec = pltpu.PrefetchScalarGridSpec(
        num_scalar_prefetch=1,
        grid=(t // tm, MOE_EXPERTS // MOE_EXPERTS_PER_STEP),
        in_specs=[
            pl.BlockSpec((tm, D_MODEL), lambda i, e, c: (i, 0)),
            pl.BlockSpec((MOE_EXPERTS, tm), lambda i, e, c: (0, i)),
            pl.BlockSpec((MOE_EXPERTS, tm), lambda i, e, c: (0, i)),
            pl.BlockSpec((MOE_EXPERTS_PER_STEP, D_MODEL, MOE_FF), lambda i, e, c: (e, 0, 0)),
            pl.BlockSpec((MOE_EXPERTS_PER_STEP, D_MODEL, MOE_FF), lambda i, e, c: (e, 0, 0)),
            pl.BlockSpec((MOE_EXPERTS_PER_STEP, MOE_FF, D_MODEL), lambda i, e, c: (e, 0, 0)),
            pl.BlockSpec((tm, D_MODEL), lambda i, e, c: (i, 0)),
        ],
        out_specs=pl.BlockSpec((tm, D_MODEL), lambda i, e, c: (i, 0)),
    )
    return pl.pallas_call(
        functools.partial(_moe_expert_body, rows=rows),
        name="moe_experts",
        grid_spec=grid_spec,
        out_shape=jax.ShapeDtypeStruct((t, D_MODEL), F32),
        compiler_params=pltpu.CompilerParams(
            dimension_semantics=("parallel", "arbitrary"),
            vmem_limit_bytes=VMEM_LIMIT_BYTES),
    )(counts, xn, rank, wt, w1, w3, w2, res)


def hier_moe_residual(x, g, w_rg, b_rg, w_re, b_re, w1, w3, w2):
    t_true = x.shape[0]
    tm = MOE_TILE if t_true % MOE_TILE == 0 else ROUTE_LANES
    if t_true % tm:
        x = jnp.pad(x, ((0, tm - t_true % tm), (0, 0)))
    t = x.shape[0]
    xn, rank, wt = moe_route(x, g, w_rg, b_rg, w_re, b_re, tm)
    counts = jnp.sum((rank >= 0).reshape(MOE_EXPERTS, t // tm, tm), axis=-1, dtype=jnp.int32)
    return moe_experts(xn, rank, wt, counts, w1, w3, w2, x, tm)[:t_true]


def _causal_dwconv(u, buf, w, b):
    L = u.shape[1]
    ext = jnp.concatenate([buf.astype(u.dtype), u], axis=1)
    out = b + sum(ext[:, k:k + L] * w[k] for k in range(M_CONV))
    return out, ext[:, L:]


def _ssd_scan(x, dt, a, bm, cm, h0):
    bsz, L = x.shape[0], x.shape[1]
    q = M_CHUNK if L % M_CHUNK == 0 else L
    nc = L // q
    hpg = M_HEADS // M_GROUPS
    xdt = (x * dt[..., None]).reshape(bsz, nc, q, M_GROUPS, hpg, M_HEADDIM)
    acum = jnp.cumsum((dt * a).reshape(bsz, nc, q, M_GROUPS, hpg), axis=2)
    bc = bm.reshape(bsz, nc, q, M_GROUPS, M_DSTATE)
    cc = cm.reshape(bsz, nc, q, M_GROUPS, M_DSTATE)
    tri = jnp.tril(jnp.ones((q, q), bool))[:, :, None, None]
    seg = acum[:, :, :, None] - acum[:, :, None]
    decay = jnp.exp(jnp.where(tri, seg, -jnp.inf))
    cb = jnp.einsum('bclgn,bcsgn->bclsg', cc, bc)
    y_diag = jnp.einsum('bclsg,bclsgh,bcsghp->bclghp', cb, decay, xdt)
    decay_end = jnp.exp(acum[:, :, -1:] - acum)
    states = jnp.einsum('bcsgn,bcsgh,bcsghp->bcghpn', bc, decay_end, xdt)
    chunk_decay = jnp.exp(acum[:, :, -1])

    def step(h, inp):
        cd, st = inp
        return cd[..., None, None] * h + st, h

    h_last, h_prev = lax.scan(step, h0.reshape(bsz, M_GROUPS, hpg, M_HEADDIM, M_DSTATE),
                              (chunk_decay.swapaxes(0, 1), states.swapaxes(0, 1)))
    y_off = jnp.einsum('bclgn,bclgh,bcghpn->bclghp', cc, jnp.exp(acum), h_prev.swapaxes(0, 1))
    y = (y_diag + y_off).reshape(bsz, L, M_HEADS, M_HEADDIM)
    return y, h_last.reshape(bsz, M_HEADS, M_HEADDIM, M_DSTATE)


def mamba_mixer(x, g, conv_buf, h0, w_in, conv_w, conv_b, dt_bias, a_log, d_skip, norm_g, w_out):
    bsz, L, _ = x.shape
    proj = norm_matmul(x.reshape(bsz * L, D_MODEL), g, w_in).reshape(bsz, L, -1)
    z = proj[..., :M_DINNER]
    xbc = proj[..., M_DINNER:M_DINNER + M_CONV_DIM]
    dt_raw = proj[..., M_DINNER + M_CONV_DIM:]
    xbc, new_buf = _causal_dwconv(xbc, conv_buf, conv_w, conv_b)
    xbc = jax.nn.silu(xbc)
    xs = xbc[..., :M_DINNER].reshape(bsz, L, M_HEADS, M_HEADDIM)
    bm = xbc[..., M_DINNER:M_DINNER + M_GN].reshape(bsz, L, M_GROUPS, M_DSTATE)
    cm = xbc[..., M_DINNER + M_GN:].reshape(bsz, L, M_GROUPS, M_DSTATE)
    dt = jax.nn.softplus(dt_raw + dt_bias)
    a = -jnp.exp(a_log)
    y, h = _ssd_scan(xs, dt, a, bm, cm, h0)
    y = y + xs * d_skip[:, None]
    y = y.reshape(bsz, L, M_DINNER)
    y = _rmsnorm(y * jax.nn.silu(z), norm_g)
    out = norm_matmul(y.reshape(bsz * L, M_DINNER), None, w_out, normalize=False)
    return out.reshape(bsz, L, D_MODEL), new_buf, h


SSD_HPG = M_HEADS // M_GROUPS
SSD_GROUP_ROWS = SSD_HPG * M_HEADDIM
CONV_PAD = 8


def _transpose_cols(x):
    return jnp.concatenate([x[:, j * LANES:(j + 1) * LANES].T for j in range(x.shape[1] // LANES)], axis=0)


def _transpose_rows(x):
    return jnp.concatenate([x[j * LANES:(j + 1) * LANES, :].T for j in range(x.shape[0] // LANES)], axis=1)


def _ssd_body(z_ref, xbc_ref, dtr_ref, cw_ref, cb_ref, dtb_ref, a_ref, dcol_ref, ng_ref, ltri_ref,
              y_ref, conv_ref, h_ref, xbuf, *, q):
    c = pl.program_id(1)

    @pl.when(c == 0)
    def _():
        h_ref[...] = jnp.zeros_like(h_ref)
        xbuf[0:CONV_PAD, :] = jnp.zeros((CONV_PAD, M_CONV_DIM), F32)

    xbuf[CONV_PAD:CONV_PAD + q, :] = xbc_ref[...]
    conv = cb_ref[...]
    for k in range(M_CONV):
        start = CONV_PAD - (M_CONV - 1) + k
        conv = conv + xbuf[start:start + q, :] * cw_ref[k:k + 1, :]
    tail = xbuf[CONV_PAD + q - (M_CONV - 1):CONV_PAD + q, :]
    xbuf[CONV_PAD - (M_CONV - 1):CONV_PAD, :] = tail
    conv_ref[0] = tail
    xc = conv * jax.nn.sigmoid(conv)
    xs = xc[:, :M_DINNER]
    xs_t = _transpose_cols(xs)

    pre = dtr_ref[...] + dtb_ref[...]
    dt = jnp.maximum(pre, 0.0) + jnp.log1p(jnp.exp(-jnp.abs(pre)))
    da = dt * a_ref[...]
    ltri = ltri_ref[...]
    acum = sum(jnp.dot(ltri, part, preferred_element_type=F32) for part in _split3(da))
    dt_t = dt.T
    acum_t = acum.T
    li = lax.broadcasted_iota(jnp.int32, (q, q), 0)
    si = lax.broadcasted_iota(jnp.int32, (q, q), 1)
    causal = li >= si

    y_t = []
    for g in range(M_GROUPS):
        bm = xc[:, M_DINNER + g * M_DSTATE:M_DINNER + (g + 1) * M_DSTATE].astype(BF16)
        cm = xc[:, M_DINNER + M_GN + g * M_DSTATE:M_DINNER + M_GN + (g + 1) * M_DSTATE].astype(BF16)
        cb = _nt_dot(cm, bm)
        r0 = g * SSD_GROUP_ROWS
        h_prev = h_ref[0, r0:r0 + SSD_GROUP_ROWS, :]
        y_off = _nt_dot(h_prev.astype(BF16), cm)
        x_dec, scale = [], []
        for hh in range(g * SSD_HPG, (g + 1) * SSD_HPG):
            a_row = acum_t[hh:hh + 1, :]
            a_col = acum[:, hh:hh + 1]
            decay = jnp.where(causal, jnp.exp(a_col - a_row), 0.0)
            m = (cb * decay).astype(BF16)
            rows = slice(hh * M_HEADDIM, (hh + 1) * M_HEADDIM)
            xs_h = xs_t[rows]
            xdt = xs_h * dt_t[hh:hh + 1, :]
            y_h = _nt_dot(xdt.astype(BF16), m)
            y_h = y_h + y_off[rows.start - r0:rows.stop - r0] * jnp.exp(a_row) + xs_h * dcol_ref[hh:hh + 1, :]
            y_t.append(y_h)
            a_last = a_row[:, q - 1:q]
            x_dec.append(xdt * jnp.exp(a_last - a_row))
            scale.append(jnp.broadcast_to(jnp.exp(a_last), (M_HEADDIM, 1)))
        upd = jnp.dot(jnp.concatenate(x_dec, axis=0).astype(BF16), bm, preferred_element_type=F32)
        h_ref[0, r0:r0 + SSD_GROUP_ROWS, :] = jnp.concatenate(scale, axis=0) * h_prev + upd

    y = _transpose_rows(jnp.concatenate(y_t, axis=0))
    zz = z_ref[...]
    yg = y * (zz * jax.nn.sigmoid(zz))
    yg = yg * lax.rsqrt(jnp.mean(yg * yg, axis=-1, keepdims=True) + EPS) * ng_ref[...]
    y_ref[...] = yg.astype(BF16)


def ssd_prompt(z, xbc, dtr, conv_w, conv_b, dt_bias, a_log, d_skip, norm_g, bsz, L):
    q = M_CHUNK
    assert L % q == 0 and q == LANES
    nc = L // q
    padl = lambda v: jnp.pad(v, (0, LANES - v.shape[0]))
    dtb = padl(dt_bias).reshape(1, LANES)
    a_row = padl(-jnp.exp(a_log)).reshape(1, LANES)
    dcol = jnp.broadcast_to(padl(d_skip).reshape(LANES, 1), (LANES, LANES))
    ltri = jnp.asarray(np.tril(np.ones((q, q), np.float32)), BF16)
    const = lambda a: pl.BlockSpec(a.shape, lambda b, c: (0,) * a.ndim)
    tok = lambda w: pl.BlockSpec((q, w), lambda b, c: (b * nc + c, 0))
    cb2 = conv_b.reshape(1, M_CONV_DIM)
    ng2 = norm_g.reshape(1, M_DINNER)
    y, conv_tail, h = pl.pallas_call(
        functools.partial(_ssd_body, q=q),
        name="ssd_chunks",
        grid=(bsz, nc),
        in_specs=[tok(M_DINNER), tok(M_CONV_DIM), tok(LANES), const(conv_w), const(cb2), const(dtb),
                  const(a_row), const(dcol), const(ng2), const(ltri)],
        out_specs=[tok(M_DINNER),
                   pl.BlockSpec((1, M_CONV - 1, M_CONV_DIM), lambda b, c: (b, 0, 0)),
                   pl.BlockSpec((1, M_DINNER, M_DSTATE), lambda b, c: (b, 0, 0))],
        out_shape=[jax.ShapeDtypeStruct((bsz * L, M_DINNER), BF16),
                   jax.ShapeDtypeStruct((bsz, M_CONV - 1, M_CONV_DIM), F32),
                   jax.ShapeDtypeStruct((bsz, M_DINNER, M_DSTATE), F32)],
        scratch_shapes=[pltpu.VMEM((CONV_PAD + q, M_CONV_DIM), F32)],
        compiler_params=pltpu.CompilerParams(
            dimension_semantics=("parallel", "arbitrary"),
            vmem_limit_bytes=VMEM_LIMIT_BYTES),
    )(z, xbc, dtr, conv_w, cb2, dtb, a_row, dcol, ng2, ltri)
    return y, conv_tail, h.reshape(bsz, M_HEADS, M_HEADDIM, M_DSTATE)


def mamba_prompt_mixer(x, g, w_in, conv_w, conv_b, dt_bias, a_log, d_skip, norm_g, w_out):
    bsz, L, _ = x.shape
    x2 = x.reshape(bsz * L, D_MODEL)
    z = norm_matmul(x2, g, w_in[:, :M_DINNER])
    xbc = norm_matmul(x2, g, w_in[:, M_DINNER:M_DINNER + M_CONV_DIM])
    dtr = norm_matmul(x2, g, jnp.pad(w_in[:, M_DINNER + M_CONV_DIM:], ((0, 0), (0, LANES - M_HEADS))))
    y, conv_tail, h = ssd_prompt(z, xbc, dtr, conv_w, conv_b, dt_bias, a_log, d_skip, norm_g, bsz, L)
    out = norm_matmul(y, None, w_out, normalize=False, res=x2)
    return out.reshape(bsz, L, D_MODEL), conv_tail, h


def _rel_bucket(dist):
    exact = REL_BUCKETS // 2
    d = jnp.maximum(dist, 0)
    ratio = jnp.log(jnp.maximum(d, 1).astype(F32) / exact) / math.log(REL_MAX_DIST / exact)
    large = jnp.minimum(exact + (ratio * (REL_BUCKETS - exact)).astype(jnp.int32), REL_BUCKETS - 1)
    return jnp.where(d < exact, d, large)


ATT_TQ = 128
ATT_TK = 128
ATT_UNROLL_LOG2 = 2


def _nt_dot(a, b):
    return lax.dot_general(a, b, (((1,), (1,)), ((), ())), preferred_element_type=F32)


def _compress_body(rk_ref, rv_ref, pe_ref, w1_ref, w2_ref, kg_ref, kc_ref, vc_ref, *, nb):
    half = (CMP_BLOCK // 2) * HEAD_DIM
    for kv, (r_ref, o_ref) in enumerate(((rk_ref, kc_ref), (rv_ref, vc_ref))):
        lo = (r_ref[0, 0, 0:nb, :] + pe_ref[kv, 0:1, :]).astype(BF16)
        hi = (r_ref[0, 0, 1:nb + 1, :] + pe_ref[kv, 1:2, :]).astype(BF16)
        h = (jnp.dot(lo, w1_ref[kv, :half, :].astype(BF16), preferred_element_type=F32)
             + jnp.dot(hi, w1_ref[kv, half:, :].astype(BF16), preferred_element_type=F32))
        h = h * jax.nn.sigmoid(h)
        o = jnp.dot(h.astype(BF16), w2_ref[kv].astype(BF16), preferred_element_type=F32)
        if kv == 0:
            o = o * lax.rsqrt(jnp.mean(o * o, axis=-1, keepdims=True) + EPS) * kg_ref[...]
        o_ref[0, 0] = o


def compress_rows(rk, rv, pe, w1, w2, kg):
    bsz, g, nbp, width = rk.shape
    nb = nbp - 8
    strip = pl.BlockSpec((1, 1, nbp, width), lambda b, j: (b, j, 0, 0))
    out = pl.BlockSpec((1, 1, nb, HEAD_DIM), lambda b, j: (b, j, 0, 0))
    full = lambda a: pl.BlockSpec(a.shape, lambda b, j: (0,) * a.ndim)
    pe2 = pe.reshape(2, 2, width)
    kg2 = kg.reshape(1, HEAD_DIM)
    return pl.pallas_call(
        functools.partial(_compress_body, nb=nb),
        name="nsa_compress",
        grid=(bsz, g),
        in_specs=[strip, strip, full(pe2), full(w1), full(w2), full(kg2)],
        out_specs=[out, out],
        out_shape=[jax.ShapeDtypeStruct((bsz, g, nb, HEAD_DIM), F32)] * 2,
        compiler_params=pltpu.CompilerParams(
            dimension_semantics=("parallel", "parallel"),
            vmem_limit_bytes=VMEM_LIMIT_BYTES),
    )(rk, rv, pe2, w1, w2, kg2)


def _split3(x):
    a = x.astype(BF16)
    r = x - a.astype(F32)
    b = r.astype(BF16)
    c = (r - b.astype(F32)).astype(BF16)
    return a, b, c


def _bias_tables(rel_table, n_qtiles):
    tab = rel_table[_rel_bucket(jnp.arange(REL_MAX_DIST + 1))].astype(F32)
    tab = tab.T.reshape(KV_HEADS, HPG, REL_MAX_DIST + 1)

    def skew(v, rows, width, step):
        out = jnp.broadcast_to(v[..., None, :], v.shape[:-1] + (rows, width + step))
        out = out.reshape(v.shape[:-1] + (rows * (width + step),))[..., :rows * width]
        return out.reshape(v.shape[:-1] + (rows, width))

    width = 2 * ATT_TK
    w = np.arange(width + 1)
    s_minus_t = np.where(w <= ATT_TK, w, w - (width + 1))
    d_idx = np.stack([np.clip(delta * ATT_TK - s_minus_t, 0, REL_MAX_DIST) for delta in range(3)])
    btile = skew(tab[:, :, d_idx], ATT_TQ, width, 1)[..., :ATT_TK]
    btile = btile.transpose(0, 2, 1, 3, 4).reshape(KV_HEADS, 3, HPG * ATT_TQ, ATT_TK)

    n_q = n_qtiles * ATT_TQ
    u = np.arange(n_q + CMP_STRIDE)
    v_cmp = tab[:, :, np.clip(u - (CMP_BLOCK - 1), 0, REL_MAX_DIST)]
    bcmp = skew(v_cmp, ATT_TK, n_q, CMP_STRIDE)
    bcmp = bcmp.reshape(KV_HEADS, HPG, ATT_TK, n_qtiles, ATT_TQ).transpose(3, 0, 1, 4, 2)
    return btile, bcmp.reshape(n_qtiles, KV_HEADS, HPG * ATT_TQ, ATT_TK)


def _nsa_attn_t_body(qt_ref, kc_ref, vct_ref, ks_ref, vst_ref, kw_ref, vwt_ref, gate_ref, bcmp_ref, btile_ref,
                     ovl_ref, o_ref, s_scr, *, n_cmp, n_sel, topk):
    i = pl.program_id(2)
    q0 = i * ATT_TQ
    cols = HPG * ATT_TQ
    qs = (qt_ref[0, 0, 0] * SCALE).astype(BF16)
    key = lax.broadcasted_iota(jnp.int32, (ATT_TK, cols), 0)
    qpos = q0 + (lax.broadcasted_iota(jnp.int32, (ATT_TK, cols), 1) & (ATT_TQ - 1))

    def fold(x, op):
        return op(x.reshape(ATT_TK // 8, 8, cols), axis=0)

    s = jnp.dot(kc_ref[0, 0].astype(BF16), qs, preferred_element_type=F32) + bcmp_ref[0, 0]
    valid = (qpos >= key * CMP_STRIDE + (CMP_BLOCK - 1)) & (key < n_cmp)
    s = jnp.where(valid, s, NEG)
    e = jnp.where(valid, jnp.exp(s - jnp.max(s, axis=0, keepdims=True)), 0.0)
    p = e / jnp.maximum(jnp.sum(e, axis=0, keepdims=True), 1e-30)
    o_cmp = jnp.dot(vct_ref[0, 0].astype(BF16), p.astype(BF16), preferred_element_type=F32)

    psum = p[:, 0:ATT_TQ]
    for h in range(1, HPG):
        psum = psum + p[:, h * ATT_TQ:(h + 1) * ATT_TQ]
    imp_t = jnp.dot(ovl_ref[...], psum.astype(BF16), preferred_element_type=F32)
    nblk = imp_t.shape[0]
    nrank = -(-n_sel // 8) * 8
    j = lax.broadcasted_iota(jnp.int32, (nrank, ATT_TQ), 0)
    qblk = (q0 + lax.broadcasted_iota(jnp.int32, (nrank, ATT_TQ), 1)) // SEL_BLOCK
    forced = (j == qblk) | (j == 0)
    score = jnp.where(forced, 1e9, jnp.where(j <= qblk, imp_t[:nrank], NEG))
    score = jnp.where(j < n_sel, score, -3e38)
    rank = jnp.zeros((nrank, ATT_TQ), F32)
    for jp in range(n_sel):
        other = score[jp:jp + 1, :]
        beats = (other > score) | ((other == score) & (jp < j))
        rank = rank + jnp.where(beats, 1.0, 0.0)
    sel_t = jnp.where((rank < topk) & (j < n_sel), 1.0, 0.0)
    if nrank < nblk:
        sel_t = jnp.concatenate([sel_t, jnp.zeros((nblk - nrank, ATT_TQ), F32)], axis=0)
    sel_t = sel_t.astype(BF16)

    kk = lax.broadcasted_iota(jnp.int32, (ATT_TK, nblk), 0) // SEL_BLOCK
    jj = lax.broadcasted_iota(jnp.int32, (ATT_TK, nblk), 1)
    blocks_per_step = ATT_TK // SEL_BLOCK

    def attend(k_ref, vt_ref, lo, hi, penalty_fn, whole_groups=False):
        def tile(kc):
            return pl.ds(pl.multiple_of(kc * ATT_TK, ATT_TK), ATT_TK)

        def sweep(fn, init):
            extra = (1 << ATT_UNROLL_LOG2) - 1 if whole_groups else 0
            n_groups = lax.shift_right_logical(hi - lo + extra, ATT_UNROLL_LOG2)

            def group(gi, carry):
                kc = lo + gi * (1 << ATT_UNROLL_LOG2)
                for u in range(1 << ATT_UNROLL_LOG2):
                    carry = fn(kc + u, carry)
                return carry

            carry = lax.fori_loop(0, n_groups, group, init)
            return lax.fori_loop(lo + n_groups * (1 << ATT_UNROLL_LOG2), hi, fn, carry)

        def scores(kc, m_run):
            kblk = k_ref[0, 0, tile(kc), :].astype(BF16)
            sc = (jnp.dot(kblk, qs, preferred_element_type=F32) + btile_ref[0, jnp.clip(i - kc, 0, 2)]
                  + penalty_fn(kc))
            s_scr[tile(kc), :] = sc
            return jnp.maximum(m_run, fold(sc, jnp.max))

        m = jnp.max(sweep(scores, jnp.full((8, cols), NEG, F32)), axis=0, keepdims=True)

        def exps(kc, l_run):
            ex = jnp.exp(s_scr[tile(kc), :] - m)
            s_scr[tile(kc), :] = ex
            return l_run + fold(ex, jnp.sum)

        den = jnp.sum(sweep(exps, jnp.zeros((8, cols), F32)), axis=0, keepdims=True)
        inv = 1.0 / jnp.maximum(den, 1e-30)

        def weighted(kc, acc):
            pr = (s_scr[tile(kc), :] * inv).astype(BF16)
            return acc + jnp.dot(vt_ref[0, 0, :, tile(kc)].astype(BF16), pr, preferred_element_type=F32)

        return sweep(weighted, jnp.zeros((HEAD_DIM, cols), F32))

    def sel_penalty(kc):
        expand = jnp.where(jj == kc * blocks_per_step + kk, 1.0, 0.0).astype(BF16)
        chosen = jnp.dot(expand, sel_t, preferred_element_type=F32)
        pen = jnp.concatenate([(chosen - 1.0) * (-NEG)] * HPG, axis=1)
        return jnp.where(qpos >= kc * ATT_TK + key, pen, NEG)

    def win_penalty(kc):
        dist = qpos - (kc * ATT_TK + key)
        return jnp.where((dist >= 0) & (dist <= WINDOW), 0.0, NEG)

    o_sel = attend(ks_ref, vst_ref, 0, i + 1, sel_penalty, whole_groups=True)
    o_win = attend(kw_ref, vwt_ref, jnp.maximum(i - WINDOW // ATT_TK, 0), i + 1, win_penalty)
    gate = jax.nn.sigmoid(gate_ref[0, 0, 0])
    o_ref[0, 0, 0] = gate[0:1] * o_cmp + gate[1:2] * o_sel + gate[2:3] * o_win


def nsa_prompt_attention_t(q, kcmp, vcmp, ks, vs_t, kw, vw_t, gl, rel_table):
    bsz, L = q.shape[0], q.shape[1]
    assert L % ATT_TQ == 0 and kcmp.shape[2] == ATT_TK
    assert math.frexp(SCALE)[0] == 0.5, "the kernel folds SCALE into q, exact only for powers of two"
    nq = L // ATT_TQ
    n_cmp = (L - CMP_BLOCK) // CMP_STRIDE + 1
    n_sel = L // SEL_BLOCK
    assert n_sel <= ATT_TK and nq % (1 << ATT_UNROLL_LOG2) == 0
    cols = HPG * ATT_TQ
    btile, bcmp = _bias_tables(rel_table, nq)
    btile = btile.transpose(0, 1, 3, 2)
    bcmp = bcmp.transpose(0, 1, 3, 2)
    c = np.arange(ATT_TK)[None, :] * CMP_STRIDE
    sb = np.arange(ATT_TK)[:, None] * SEL_BLOCK
    ovl = ((c < sb + SEL_BLOCK) & (c + CMP_BLOCK > sb) & (np.arange(ATT_TK)[None, :] < n_cmp)
           & (np.arange(ATT_TK)[:, None] < n_sel))
    ovl = jnp.asarray(ovl, BF16)
    q_t = q.reshape(bsz, nq, ATT_TQ, KV_HEADS, HPG, HEAD_DIM).transpose(0, 3, 1, 5, 4, 2)
    q_t = q_t.reshape(bsz, KV_HEADS, nq, HEAD_DIM, cols)
    gate_t = gl.reshape(bsz, nq, ATT_TQ, KV_HEADS, HPG, 3).transpose(0, 3, 1, 5, 4, 2)
    gate_t = gate_t.reshape(bsz, KV_HEADS, nq, 3, cols)
    vc_t = vcmp.transpose(0, 1, 3, 2)
    rows_spec = lambda n: pl.BlockSpec((1, 1, n, HEAD_DIM), lambda b, g, i: (b, g, 0, 0))
    cols_spec = lambda n: pl.BlockSpec((1, 1, HEAD_DIM, n), lambda b, g, i: (b, g, 0, 0))
    tile_spec = lambda r: pl.BlockSpec((1, 1, 1, r, cols), lambda b, g, i: (b, g, i, 0, 0))
    o_t = pl.pallas_call(
        functools.partial(_nsa_attn_t_body, n_cmp=n_cmp, n_sel=n_sel, topk=min(SEL_TOPK, n_sel)),
        name="nsa_attention",
        grid=(bsz, KV_HEADS, nq),
        in_specs=[
            tile_spec(HEAD_DIM),
            rows_spec(ATT_TK), cols_spec(ATT_TK), rows_spec(L), cols_spec(L), rows_spec(L), cols_spec(L),
            tile_spec(3),
            pl.BlockSpec((1, 1, ATT_TK, cols), lambda b, g, i: (i, g, 0, 0)),
            pl.BlockSpec((1, 3, ATT_TK, cols), lambda b, g, i: (g, 0, 0, 0)),
            pl.BlockSpec((ATT_TK, ATT_TK), lambda b, g, i: (0, 0)),
        ],
        out_specs=tile_spec(HEAD_DIM),
        out_shape=jax.ShapeDtypeStruct((bsz, KV_HEADS, nq, HEAD_DIM, cols), F32),
        scratch_shapes=[pltpu.VMEM((L, cols), F32)],
        compiler_params=pltpu.CompilerParams(
            dimension_semantics=("parallel", "parallel", "arbitrary"),
            vmem_limit_bytes=VMEM_LIMIT_BYTES),
    )(q_t, kcmp, vc_t, ks, vs_t, kw, vw_t, gate_t, bcmp, btile, ovl)
    o = o_t.reshape(bsz, KV_HEADS, nq, HEAD_DIM, HPG, ATT_TQ).transpose(0, 2, 5, 1, 4, 3)
    return o.reshape(bsz, L, N_HEADS * HEAD_DIM)


def nsa_prompt_mixer(x, g, w_in, q_g, k_g, cmp_pe, cmp_w1, cmp_w2, w_out, rel_table):
    bsz, L, _ = x.shape
    t = bsz * L
    proj = norm_matmul(x.reshape(t, D_MODEL), g, w_in)
    c0 = N_HEADS * HEAD_DIM
    q = _rmsnorm(proj[:, :c0].reshape(bsz, L, N_HEADS, HEAD_DIM), q_g)

    def kvpair(k, normed_g):
        kcols = proj[:, c0 + 2 * k * KV_DIM:c0 + (2 * k + 1) * KV_DIM]
        vcols = proj[:, c0 + (2 * k + 1) * KV_DIM:c0 + (2 * k + 2) * KV_DIM]
        if normed_g is not None:
            kcols = _rmsnorm(kcols.reshape(t, KV_HEADS, HEAD_DIM), normed_g).reshape(t, KV_DIM)
        rows = jnp.stack([kcols, vcols], axis=1).reshape(bsz, L, 2, KV_HEADS, HEAD_DIM)
        k4 = kcols.reshape(bsz, L, KV_HEADS, HEAD_DIM).transpose(0, 2, 1, 3)
        v4 = vcols.reshape(bsz, L, KV_HEADS, HEAD_DIM)
        return rows, k4, v4

    new_cmp, kc_rows, vc_rows = kvpair(0, None)
    new_sel, ks, vs = kvpair(1, k_g[1])
    new_win, kw, vw = kvpair(2, k_g[2])
    gl = proj[:, c0 + 6 * KV_DIM:].reshape(bsz, L, KV_HEADS, HPG, 3)
    vc_rows = vc_rows.transpose(0, 2, 1, 3)
    vs_t, vw_t = vs.transpose(0, 2, 3, 1), vw.transpose(0, 2, 3, 1)

    nb = L // CMP_STRIDE

    def strips(r):
        r = r.reshape(bsz, KV_HEADS, nb, CMP_STRIDE * HEAD_DIM)
        return jnp.pad(r, ((0, 0), (0, 0), (0, ATT_TK + 8 - nb), (0, 0)))

    kcmp, vcmp = compress_rows(strips(kc_rows), strips(vc_rows), cmp_pe, cmp_w1, cmp_w2, k_g[0])
    o = nsa_prompt_attention_t(q, kcmp, vcmp, ks, vs_t, kw, vw_t, gl, rel_table)
    y = norm_matmul(o.reshape(t, c0), None, w_out, normalize=False, res=x.reshape(t, D_MODEL))
    return y.reshape(bsz, L, D_MODEL), new_cmp, new_sel, new_win[:, -min(WINDOW, L):]


DEC_PAGES_PER_STEP = 8
STRIP = CMP_STRIDE
ROW_LANES = 2 * KV_DIM
SEL_LANES = 256


def _group_rmsnorm(x, gain_row):
    lane = lax.broadcasted_iota(jnp.int32, x.shape, 1) // HEAD_DIM
    sq = x * x
    ms = jnp.zeros_like(x)
    for grp in range(KV_HEADS):
        tot = jnp.sum(jnp.where(lane == grp, sq, 0.0), axis=-1, keepdims=True)
        ms = jnp.where(lane == grp, tot, ms)
    return x * lax.rsqrt(ms / HEAD_DIM + EPS) * gain_row


def _decode_cmp_body(pt_ref, *refs, n_strips, n_sel, qblk, topk):
    pages = refs[:DEC_PAGES_PER_STEP]
    (bd_ref, peterm_ref, w2bd_ref, kg_ref, qbd_ref, bias_ref, ovl_ref, upper_ref,
     ocmp_ref, idx_ref, seq, tbuf, hibuf) = refs[DEC_PAGES_PER_STEP:]
    s = pl.program_id(1)
    strips_per_page = pages[0].shape[2] // STRIP
    for r in range(DEC_PAGES_PER_STEP):
        row0 = pl.multiple_of((s * DEC_PAGES_PER_STEP + r) * strips_per_page, strips_per_page)
        for c in range(ROW_LANES // LANES):
            lanes = slice(c * LANES, (c + 1) * LANES)
            tbuf[c] = pages[r][0, lanes, :].T
            for l in range(STRIP):
                seq[l, pl.ds(row0, strips_per_page), lanes] = tbuf[c, pl.ds(l, strips_per_page, stride=STRIP), :]

    @pl.when(s == pl.num_programs(1) - 1)
    def _():
        n_cmp = n_strips - 1
        summaries = []
        for kv in range(2):
            halves = []
            for half in range(2):
                acc = jnp.zeros((n_strips, KV_DIM), F32)
                for l in range(STRIP):
                    w_idx = (kv * 2 + half) * STRIP + l
                    xl = (seq[l, :, kv * KV_DIM:(kv + 1) * KV_DIM] + peterm_ref[w_idx:w_idx + 1, :]).astype(BF16)
                    acc = acc + jnp.dot(xl, bd_ref[w_idx], preferred_element_type=F32)
                halves.append(acc)
            hibuf[0:n_strips, :] = halves[1]
            hibuf[n_strips:n_strips + 8, :] = jnp.zeros((8, KV_DIM), F32)
            h = halves[0] + hibuf[1:n_strips + 1, :]
            h = (h * jax.nn.sigmoid(h)).astype(BF16)
            o = jnp.dot(h, w2bd_ref[kv], preferred_element_type=F32)
            if kv == 0:
                o = _group_rmsnorm(o, kg_ref[...])
            summaries.append(o.astype(BF16))
        kcmp, vcmp = summaries

        qbd = qbd_ref[0].astype(BF16)
        sc = _nt_dot(qbd, kcmp) * SCALE + bias_ref[...]
        col = lax.broadcasted_iota(jnp.int32, sc.shape, 1)
        valid = col < n_cmp
        sc = jnp.where(valid, sc, NEG)
        e = jnp.where(valid, jnp.exp(sc - jnp.max(sc, axis=-1, keepdims=True)), 0.0)
        p = e / jnp.maximum(jnp.sum(e, axis=-1, keepdims=True), 1e-30)
        pb = p.astype(BF16)
        o_cmp = jnp.dot(pb, vcmp, preferred_element_type=F32)
        head_grp = lax.broadcasted_iota(jnp.int32, o_cmp.shape, 0) // HPG
        lane_grp = lax.broadcasted_iota(jnp.int32, o_cmp.shape, 1) // HEAD_DIM
        ocmp_ref[0] = jnp.where(head_grp == lane_grp, o_cmp, 0.0)

        psum = jnp.concatenate(
            [jnp.sum(p[grp * HPG:(grp + 1) * HPG], axis=0, keepdims=True) for grp in range(KV_HEADS)]
            + [jnp.zeros((8 - KV_HEADS, n_strips), F32)], axis=0)
        imp = jnp.dot(psum.astype(BF16), ovl_ref[...], preferred_element_type=F32)
        j = lax.broadcasted_iota(jnp.int32, imp.shape, 1)
        forced = (j == qblk) | (j == 0)
        score = jnp.where(forced, 1e9, jnp.where(j <= qblk, imp, NEG))
        score = jnp.where(j < n_sel, score, -3e38)
        rank = jnp.zeros(imp.shape, F32)
        for jp in range(n_sel):
            other = score[:, jp:jp + 1]
            beats = (other > score) | ((other == score) & (jp < j))
            rank = rank + jnp.where(beats, 1.0, 0.0)
        chosen = (rank < topk) & (j < n_sel)
        cum = jnp.dot(jnp.where(chosen, 1.0, 0.0).astype(BF16), upper_ref[...], preferred_element_type=F32)
        jf = j.astype(F32)
        out_lane = lax.broadcasted_iota(jnp.int32, (8, LANES), 1)
        out = jnp.zeros((8, LANES), F32)
        for k in range(topk):
            pick = jnp.sum(jnp.where(chosen & (cum == k + 1.0), jf, 0.0), axis=-1, keepdims=True)
            out = jnp.where(out_lane == k, pick, out)
        idx_ref[0] = out


def _decode_attn_body(tbl_ref, idx_ref, *refs, n_blk, qblk, per_page):
    pages = refs[:n_blk]
    (win_ref, q_ref, knew_ref, vnew_ref, wknew_ref, wvnew_ref, bsel_ref, bwin_ref, ocmp_ref, gate_ref,
     o_ref) = refs[n_blk:]
    b, g = pl.program_id(0), pl.program_id(1)
    qb = q_ref[0, 0].astype(BF16)
    qf = qb.astype(F32)
    k_rows = pl.ds(pl.multiple_of(g * HEAD_DIM, HEAD_DIM), HEAD_DIM)
    v_rows = pl.ds(pl.multiple_of(KV_DIM + g * HEAD_DIM, HEAD_DIM), HEAD_DIM)

    def attend(parts, k_new, v_new, bias_new):
        s_new = jnp.sum(qf * k_new.astype(BF16).astype(F32), axis=-1, keepdims=True) * SCALE + bias_new
        m = s_new
        for sc, ok, _ in parts:
            m = jnp.maximum(m, jnp.max(jnp.where(ok, sc, NEG), axis=-1, keepdims=True))
        e_new = jnp.exp(s_new - m)
        es = [jnp.where(ok, jnp.exp(sc - m), 0.0) for sc, ok, _ in parts]
        den = e_new
        for e in es:
            den = den + jnp.sum(e, axis=-1, keepdims=True)
        den = jnp.maximum(den, 1e-30)
        acc = (e_new / den).astype(BF16).astype(F32) * v_new.astype(BF16).astype(F32)
        for e, (_, _, v_t) in zip(es, parts):
            acc = acc + _nt_dot((e / den).astype(BF16), v_t)
        return acc

    sel_parts = []
    for k in range(n_blk):
        bidx = idx_ref[b, g, k]
        k_t = pages[k][0, k_rows, :].astype(BF16)
        v_t = pages[k][0, v_rows, :].astype(BF16)
        pg = jnp.minimum(bidx, qblk) // per_page
        sc = jnp.dot(qb, k_t, preferred_element_type=F32) * SCALE + bsel_ref[pg, 0]
        lane_blk = lax.broadcasted_iota(jnp.int32, sc.shape, 1) // SEL_BLOCK
        ok = (lane_blk == bidx % per_page) & (bidx < qblk)
        sel_parts.append((sc, ok, v_t))
    bias0 = bsel_ref[qblk // per_page, 0][:, 0:1]
    o_sel = attend(sel_parts, knew_ref[0, 0], vnew_ref[0, 0], bias0)

    sc = jnp.dot(qb, win_ref[0, k_rows, :].astype(BF16), preferred_element_type=F32) * SCALE + bwin_ref[0]
    o_win = attend([(sc, jnp.full(sc.shape, True), win_ref[0, v_rows, :].astype(BF16))],
                   wknew_ref[0, 0], wvnew_ref[0, 0], bias0)

    gate = jax.nn.sigmoid(gate_ref[0, 0])
    o_ref[0, 0] = gate[:, 0:1] * ocmp_ref[0, 0] + gate[:, 1:2] * o_sel + gate[:, 2:3] * o_win


def nsa_decode_mixer(x, g, past_len, cache_cmp, cache_sel, win_past, page_table,
                     w_in, q_g, k_g, cmp_pe, cmp_w1, cmp_w2, w_out, rel_table):
    bsz = x.shape[0]
    n_pool, page = cache_cmp.shape[0], cache_cmp.shape[1]
    n_pages = page_table.shape[1]
    assert past_len == n_pages * page and page % STRIP == 0 and n_pages % DEC_PAGES_PER_STEP == 0
    assert win_past.shape[1] == WINDOW and past_len >= WINDOW and page % SEL_BLOCK == 0
    x2 = x.reshape(bsz, D_MODEL)
    proj = norm_matmul(x2, g, w_in)
    c0 = N_HEADS * HEAD_DIM
    q = _rmsnorm(proj[:, :c0].reshape(bsz, N_HEADS, HEAD_DIM), q_g)

    def rows(k, gain):
        kcols = proj[:, c0 + 2 * k * KV_DIM:c0 + (2 * k + 1) * KV_DIM]
        vcols = proj[:, c0 + (2 * k + 1) * KV_DIM:c0 + (2 * k + 2) * KV_DIM]
        if gain is not None:
            kcols = _rmsnorm(kcols.reshape(bsz, KV_HEADS, HEAD_DIM), gain).reshape(bsz, KV_DIM)
        return kcols, vcols

    kc_new, vc_new = rows(0, None)
    ks_new, vs_new = rows(1, k_g[1])
    kw_new, vw_new = rows(2, k_g[2])
    gl = proj[:, c0 + 6 * KV_DIM:].reshape(bsz, KV_HEADS, HPG, 3)
    as_row = lambda kk, vv: jnp.stack([kk, vv], axis=1).reshape(bsz, 1, 2, KV_HEADS, HEAD_DIM)
    new_cmp, new_sel, new_win = as_row(kc_new, vc_new), as_row(ks_new, vs_new), as_row(kw_new, vw_new)

    eye = jnp.eye(KV_HEADS, dtype=F32)
    qbd = jnp.einsum('bghd,gk->bghkd', q.reshape(bsz, KV_HEADS, HPG, HEAD_DIM), eye).reshape(bsz, N_HEADS, KV_DIM)
    w1r = cmp_w1.reshape(2, 2, STRIP, HEAD_DIM, HEAD_DIM)
    bd = jnp.einsum('khlio,gj->khlgijo', w1r, eye).reshape(2 * 2 * STRIP, KV_DIM, KV_DIM).astype(BF16)
    w2bd = jnp.einsum('kio,gj->kgijo', cmp_w2, eye).reshape(2, KV_DIM, KV_DIM).astype(BF16)
    peterm = jnp.tile(cmp_pe.reshape(2 * 2 * STRIP, HEAD_DIM), (1, KV_HEADS))
    kg_row = jnp.tile(k_g[0], KV_HEADS).reshape(1, KV_DIM)

    n_strips = past_len // STRIP
    n_cmp = n_strips - 1
    qblk = past_len // SEL_BLOCK
    n_sel = qblk + 1
    topk = min(SEL_TOPK, n_sel)
    assert n_sel <= SEL_LANES and n_strips % 8 == 0
    tab = rel_table[_rel_bucket(jnp.arange(REL_MAX_DIST + 1))].astype(F32).T
    ends = np.arange(n_strips) * CMP_STRIDE + CMP_BLOCK - 1
    bias_cmp = tab[:, np.clip(past_len - ends, 0, REL_MAX_DIST)]
    cs = np.arange(n_strips)[:, None] * CMP_STRIDE
    ss = np.arange(SEL_LANES)[None, :] * SEL_BLOCK
    ovl = (cs < ss + SEL_BLOCK) & (cs + CMP_BLOCK > ss) & (np.arange(n_strips)[:, None] < n_cmp) \
        & (np.arange(SEL_LANES)[None, :] < n_sel)
    ovl = jnp.asarray(ovl, BF16)
    upper = jnp.asarray(np.triu(np.ones((SEL_LANES, SEL_LANES), np.float32)), BF16)

    strips_per_page = page // STRIP
    cmp_view = cache_cmp.reshape(n_pool, page, ROW_LANES).transpose(0, 2, 1)
    steps = n_pages // DEC_PAGES_PER_STEP
    const = lambda a: pl.BlockSpec(a.shape, lambda b, s, pt: (0,) * a.ndim)
    page_spec = lambda r: pl.BlockSpec((1, ROW_LANES, page),
                                       lambda b, s, pt: (pt[b, s * DEC_PAGES_PER_STEP + r], 0, 0))
    o_cmp, idx = pl.pallas_call(
        functools.partial(_decode_cmp_body, n_strips=n_strips, n_sel=n_sel, qblk=qblk, topk=topk),
        name="nsa_decode_cmp",
        grid_spec=pltpu.PrefetchScalarGridSpec(
            num_scalar_prefetch=1,
            grid=(bsz, steps),
            in_specs=[page_spec(r) for r in range(DEC_PAGES_PER_STEP)]
            + [const(bd), const(peterm), const(w2bd), const(kg_row),
               pl.BlockSpec((1, N_HEADS, KV_DIM), lambda b, s, pt: (b, 0, 0)),
               const(bias_cmp), const(ovl), const(upper)],
            out_specs=[pl.BlockSpec((1, N_HEADS, KV_DIM), lambda b, s, pt: (b, 0, 0)),
                       pl.BlockSpec((1, 8, LANES), lambda b, s, pt: (b, 0, 0))],
            scratch_shapes=[pltpu.VMEM((STRIP, n_strips, ROW_LANES), F32),
                            pltpu.VMEM((ROW_LANES // LANES, page, LANES), F32),
                            pltpu.VMEM((n_strips + 8, KV_DIM), F32)],
        ),
        out_shape=[jax.ShapeDtypeStruct((bsz, N_HEADS, KV_DIM), F32),
                   jax.ShapeDtypeStruct((bsz, 8, LANES), F32)],
        compiler_params=pltpu.CompilerParams(
            dimension_semantics=("parallel", "arbitrary"),
            vmem_limit_bytes=DECODE_VMEM_LIMIT_BYTES),
    )(page_table, *([cmp_view] * DEC_PAGES_PER_STEP), bd, peterm, w2bd, kg_row, qbd, bias_cmp, ovl, upper)

    blk_idx = idx[:, :KV_HEADS, :topk].astype(jnp.int32)
    per_page = page // SEL_BLOCK
    safe = jnp.minimum(blk_idx, qblk - 1)
    page_of = jnp.take_along_axis(page_table, (safe // per_page).reshape(bsz, -1), axis=1).reshape(safe.shape)
    sel_view = cache_sel.reshape(n_pool, page, ROW_LANES).transpose(0, 2, 1)
    win_view = win_past.reshape(bsz, WINDOW, ROW_LANES).transpose(0, 2, 1)

    n_pg = n_pages + 1
    dist_sel = past_len - (np.arange(n_pg)[:, None] * page + np.arange(page)[None, :])
    bsel = tab[:, np.clip(dist_sel, 0, REL_MAX_DIST)]
    bsel = bsel.reshape(KV_HEADS, HPG, n_pg, page).transpose(2, 0, 1, 3)
    bwin = tab[:, np.clip(past_len - (past_len - WINDOW + np.arange(WINDOW)), 0, REL_MAX_DIST)]
    bwin = bwin.reshape(KV_HEADS, HPG, WINDOW)

    per_group = lambda a: a.reshape(bsz, KV_HEADS, 1, HEAD_DIM)
    q4 = q.reshape(bsz, KV_HEADS, HPG, HEAD_DIM)
    ocmp4 = o_cmp.reshape(bsz, KV_HEADS, HPG, KV_HEADS, HEAD_DIM).sum(axis=3)
    page_spec2 = lambda k: pl.BlockSpec((1, ROW_LANES, page), lambda b, g, t, i: (t[b, g, k], 0, 0))
    per_bg = lambda *shape: pl.BlockSpec((1, 1) + shape, lambda b, g, t, i: (b, g) + (0,) * len(shape))
    o4 = pl.pallas_call(
        functools.partial(_decode_attn_body, n_blk=topk, qblk=qblk, per_page=per_page),
        name="nsa_decode_attn",
        grid_spec=pltpu.PrefetchScalarGridSpec(
            num_scalar_prefetch=2,
            grid=(bsz, KV_HEADS),
            in_specs=[page_spec2(k) for k in range(topk)]
            + [pl.BlockSpec((1, ROW_LANES, WINDOW), lambda b, g, t, i: (b, 0, 0)),
               per_bg(HPG, HEAD_DIM), per_bg(1, HEAD_DIM), per_bg(1, HEAD_DIM), per_bg(1, HEAD_DIM),
               per_bg(1, HEAD_DIM),
               pl.BlockSpec((n_pg, 1, HPG, page), lambda b, g, t, i: (0, g, 0, 0)),
               pl.BlockSpec((1, HPG, WINDOW), lambda b, g, t, i: (g, 0, 0)),
               per_bg(HPG, HEAD_DIM), per_bg(HPG, 3)],
            out_specs=per_bg(HPG, HEAD_DIM),
        ),
        out_shape=jax.ShapeDtypeStruct((bsz, KV_HEADS, HPG, HEAD_DIM), F32),
        compiler_params=pltpu.CompilerParams(
            dimension_semantics=("parallel", "arbitrary"),
            vmem_limit_bytes=VMEM_LIMIT_BYTES),
    )(page_of, blk_idx, *([sel_view] * topk), win_view, q4, per_group(ks_new), per_group(vs_new),
      per_group(kw_new), per_group(vw_new), bsel, bwin, ocmp4, gl)
    y = norm_matmul(o4.reshape(bsz, c0), None, w_out, normalize=False, res=x2)
    new_win_buf = jnp.concatenate([win_past[:, 1:], new_win], axis=1)
    return y.reshape(x.shape), new_cmp, new_sel, new_win_buf


def _gla_chunked(q, k, v, logf, s0):
    bsz, L = q.shape[0], q.shape[1]
    nc = -(-L // H_CHUNK)
    pad = nc * H_CHUNK - L

    def prep(t):
        t = jnp.pad(t, ((0, 0), (0, pad), (0, 0), (0, 0)))
        return t.reshape(bsz, nc, H_CHUNK, H_HEADS, t.shape[-1])

    q, k, v, logf = prep(q), prep(k), prep(v), prep(logf)
    acum = jnp.cumsum(logf, axis=2)
    alast = acum[:, :, -1:]
    qe = q * jnp.exp(acum)
    ke = k * jnp.exp(-acum)
    kd = k * jnp.exp(alast - acum)
    tri = jnp.tril(jnp.ones((H_CHUNK, H_CHUNK), bool))
    att = jnp.where(tri, jnp.einsum('bcthk,bcshk->bchts', qe, ke), 0.0)
    o_intra = jnp.einsum('bchts,bcshv->bcthv', att, v)
    upd = jnp.einsum('bcshk,bcshv->bchkv', kd, v)

    def step(s, inp):
        dec, up = inp
        return dec[..., None] * s + up, s

    s_last, s_prev = lax.scan(step, s0, (jnp.exp(alast[:, :, 0]).swapaxes(0, 1), upd.swapaxes(0, 1)))
    o_inter = jnp.einsum('bcthk,bchkv->bcthv', qe, s_prev.swapaxes(0, 1))
    o = (o_intra + o_inter).reshape(bsz, nc * H_CHUNK, H_HEADS, H_DV)[:, :L]
    return o, s_last


HGRN_TOKENS = 128


def _hgrn_body(q_ref, f_ref, v_ref, gate_ref, lb_ref, ng_ref, lcum_ref, lsum_ref, o_ref, s_ref):
    @pl.when(pl.program_id(1) == 0)
    def _():
        s_ref[...] = jnp.zeros_like(s_ref)

    n = HGRN_TOKENS
    qr = q_ref[...]
    q = qr * jax.nn.sigmoid(qr)
    lb = lb_ref[...]
    forget = lb + (1.0 - lb) * jax.nn.sigmoid(f_ref[...])
    logf = jnp.log(forget)
    k = 1.0 - forget
    parts = _split3(logf)
    acum = sum(jnp.dot(lcum_ref[...], p, preferred_element_type=F32) for p in parts)
    atot = sum(jnp.dot(lsum_ref[...], p, preferred_element_type=F32) for p in parts)
    qe = (q * jnp.exp(acum)).astype(BF16)
    ke = (k * jnp.exp(-acum)).astype(BF16)
    kd = (k * jnp.exp(atot - acum)).astype(BF16)
    vb = v_ref[...].astype(BF16)
    ti = lax.broadcasted_iota(jnp.int32, (n, n), 0)
    si = lax.broadcasted_iota(jnp.int32, (n, n), 1)
    intra = (ti // H_CHUNK == si // H_CHUNK) & (ti >= si)
    gate = gate_ref[...]
    gate = gate * jax.nn.sigmoid(gate)
    outs = []
    for h in range(H_HEADS):
        cols = slice(h * H_DK, (h + 1) * H_DK)
        att = jnp.where(intra, _nt_dot(qe[:, cols], ke[:, cols]), 0.0)
        o_h = jnp.dot(att.astype(BF16), vb[:, cols], preferred_element_type=F32)
        decay_t = jnp.exp(atot[:, cols]).T
        state = s_ref[0, cols, :]
        inter = []
        for j in range(n // H_CHUNK):
            rows = slice(j * H_CHUNK, (j + 1) * H_CHUNK)
            inter.append(jnp.dot(qe[rows, cols], state.astype(BF16), preferred_element_type=F32))
            upd = lax.dot_general(kd[rows, cols], vb[rows, cols], (((0,), (0,)), ((), ())),
                                  preferred_element_type=F32)
            state = decay_t[:, j * H_CHUNK:j * H_CHUNK + 1] * state + upd
        s_ref[0, cols, :] = state
        o_h = o_h + jnp.concatenate(inter, axis=0)
        o_h = o_h * lax.rsqrt(jnp.mean(o_h * o_h, axis=-1, keepdims=True) + EPS) * ng_ref[...]
        outs.append(o_h * gate[:, cols])
    o_ref[...] = jnp.concatenate(outs, axis=1).astype(BF16)


def hgrn2_prompt_mixer(x, g, lb, w_in, norm_g, w_out):
    bsz, L, _ = x.shape
    n = HGRN_TOKENS
    assert L % n == 0 and n % H_CHUNK == 0 and H_DK == LANES and H_DV == LANES
    t = bsz * L
    x2 = x.reshape(t, D_MODEL)
    proj = norm_matmul(x2, g, w_in)
    nb = L // n
    r = np.arange(n)
    same = (r[:, None] // H_CHUNK) == (r[None, :] // H_CHUNK)
    lcum = jnp.asarray(same & (r[None, :] <= r[:, None]), BF16)
    lsum = jnp.asarray(same, BF16)
    col = lambda kk: pl.BlockSpec((n, D_MODEL), lambda b, c: (b * nb + c, kk))
    const = lambda a: pl.BlockSpec(a.shape, lambda b, c: (0,) * a.ndim)
    lb2 = lb.reshape(1, D_MODEL)
    ng2 = norm_g.reshape(1, H_DV)
    o, s = pl.pallas_call(
        _hgrn_body,
        name="hgrn_blocks",
        grid=(bsz, nb),
        in_specs=[col(0), col(1), col(2), col(3), const(lb2), const(ng2), const(lcum), const(lsum)],
        out_specs=[pl.BlockSpec((n, D_MODEL), lambda b, c: (b * nb + c, 0)),
                   pl.BlockSpec((1, H_HEADS * H_DK, H_DV), lambda b, c: (b, 0, 0))],
        out_shape=[jax.ShapeDtypeStruct((t, D_MODEL), BF16),
                   jax.ShapeDtypeStruct((bsz, H_HEADS * H_DK, H_DV), F32)],
        compiler_params=pltpu.CompilerParams(
            dimension_semantics=("parallel", "arbitrary"),
            vmem_limit_bytes=VMEM_LIMIT_BYTES),
    )(proj, proj, proj, proj, lb2, ng2, lcum, lsum)
    y = norm_matmul(o, None, w_out, normalize=False, res=x2)
    return y.reshape(bsz, L, D_MODEL), s.reshape(bsz, H_HEADS, H_DK, H_DV)


def hgrn2_mixer(x, g, s0, lb, w_in, norm_g, w_out):
    bsz, L, _ = x.shape
    proj = norm_matmul(x.reshape(bsz * L, D_MODEL), g, w_in).reshape(bsz, L, -1)
    q, fr, iv, gate = jnp.split(proj, 4, axis=-1)
    q = jax.nn.silu(q).reshape(bsz, L, H_HEADS, H_DK)
    forget = lb + (1.0 - lb) * jax.nn.sigmoid(fr)
    logf = jnp.log(forget).reshape(bsz, L, H_HEADS, H_DK)
    k = (1.0 - forget).reshape(bsz, L, H_HEADS, H_DK)
    v = iv.reshape(bsz, L, H_HEADS, H_DV)
    o, s = _gla_chunked(q, k, v, logf, s0)
    o = _rmsnorm(o, norm_g) * jax.nn.silu(gate.reshape(bsz, L, H_HEADS, H_DV))
    y = norm_matmul(o.reshape(bsz * L, D_MODEL), None, w_out, normalize=False)
    return y.reshape(bsz, L, D_MODEL), s


def kernel(x_prompt, x_sample, state_ssm, state_conv, cache_kv_cmp, cache_kv_sel, cache_kv_win, state_hgrn, page_table, norm_g, rel_table, m_w_in, m_conv_w, m_conv_b, m_dt_bias, m_a_log, m_d, m_norm_g, m_w_out, n_w_in, n_q_g, n_k_g, n_cmp_pe, n_cmp_w1, n_cmp_w2, n_w_out, h_w_in, h_lb, h_norm_g, h_w_out, moe_w_rg, moe_b_rg, moe_w_re, moe_b_re, moe_w1, moe_w3, moe_w2):
    bp, lp = x_prompt.shape[0], x_prompt.shape[1]
    past_len = page_table.shape[1] * cache_kv_cmp.shape[2]
    dt = x_prompt.dtype
    m_w_in, m_w_out, n_w_in, n_w_out, h_w_in, h_w_out, moe_w1, moe_w3, moe_w2 = (
        w.astype(BF16) for w in (m_w_in, m_w_out, n_w_in, n_w_out, h_w_in, h_w_out, moe_w1, moe_w3, moe_w2))
    lbs = jax.nn.softmax(h_lb.astype(F32), axis=0)
    lbs = jnp.cumsum(lbs, axis=0) - lbs[0]
    xp, xs = x_prompt, x_sample
    ssm_p, conv_p, cmp_p, sel_p, win_p, hg_p = [], [], [], [], [], []
    ssm_s, conv_s, cmp_s, sel_s, win_s, hg_s = [], [], [], [], [], []
    for i in range(DEPTH):
        kind, j = i % N_MIXERS, i // N_MIXERS
        g0 = norm_g[i, 0]
        if kind == 0:
            w = (m_w_in[j], m_conv_w[j], m_conv_b[j], m_dt_bias[j], m_a_log[j], m_d[j], m_norm_g[j], m_w_out[j])
            xp, cbuf, hh = mamba_prompt_mixer(xp, g0, *w)
            ssm_p.append(hh)
            conv_p.append(cbuf)
            ys, cbuf, hh = mamba_mixer(xs, g0, state_conv[j], state_ssm[j], *w)
            xs = xs + ys
            ssm_s.append(hh)
            conv_s.append(cbuf)
        elif kind == 1:
            w = (n_w_in[j], n_q_g[j], n_k_g[j], n_cmp_pe[j], n_cmp_w1[j], n_cmp_w2[j], n_w_out[j], rel_table)
            xp, rc, rs, wb = nsa_prompt_mixer(xp, g0, *w)
            cmp_p.append(rc)
            sel_p.append(rs)
            win_p.append(wb)
            xs, rc, rs, wb = nsa_decode_mixer(xs, g0, past_len, cache_kv_cmp[j], cache_kv_sel[j],
                                              cache_kv_win[j], page_table, *w)
            cmp_s.append(rc)
            sel_s.append(rs)
            win_s.append(wb)
        else:
            w = (lbs[i], h_w_in[j], h_norm_g[j], h_w_out[j])
            xp, st = hgrn2_prompt_mixer(xp, g0, *w)
            hg_p.append(st)
            ys, st = hgrn2_mixer(xs, g0, state_hgrn[j], *w)
            xs = xs + ys
            hg_s.append(st)
        mw = (moe_w_rg[i], moe_b_rg[i], moe_w_re[i], moe_b_re[i], moe_w1[i], moe_w3[i], moe_w2[i])
        xp = hier_moe_residual(xp.reshape(-1, D_MODEL), norm_g[i, 1], *mw).reshape(xp.shape)
        xs = hier_moe_residual(xs.reshape(-1, D_MODEL), norm_g[i, 1], *mw).reshape(xs.shape)
    return (xp, xs,
            jnp.stack(ssm_p), jnp.stack(conv_p), jnp.stack(cmp_p), jnp.stack(sel_p), jnp.stack(win_p), jnp.stack(hg_p),
            jnp.stack(ssm_s), jnp.stack(conv_s), jnp.stack(cmp_s), jnp.stack(sel_s), jnp.stack(win_s), jnp.stack(hg_s))
```

```python
import functools
import math

import jax
import jax.numpy as jnp
import numpy as np
from jax import lax
from jax.experimental import pallas as pl
from jax.experimental.pallas import tpu as pltpu

F32 = jnp.float32
BF16 = jnp.bfloat16
EPS = 1e-6

D_MODEL = 1024
DEPTH = 4
N_MIXERS = 3

M_DINNER = 2 * D_MODEL
M_HEADDIM = 64
M_HEADS = M_DINNER // M_HEADDIM
M_GROUPS = 4
M_DSTATE = 128
M_CONV = 4
M_GN = M_GROUPS * M_DSTATE
M_CONV_DIM = M_DINNER + 2 * M_GN
M_CHUNK = 128

N_HEADS = 16
HEAD_DIM = D_MODEL // N_HEADS
KV_HEADS = 4
HPG = N_HEADS // KV_HEADS
KV_DIM = KV_HEADS * HEAD_DIM
CMP_BLOCK = 32
CMP_STRIDE = 16
SEL_BLOCK = 64
SEL_TOPK = 16
WINDOW = 512
SCALE = HEAD_DIM ** -0.5
REL_BUCKETS = 32
REL_MAX_DIST = 128

H_DK = 128
H_HEADS = D_MODEL // H_DK
H_DV = D_MODEL // H_HEADS
H_CHUNK = 32

MOE_GROUPS = 4
MOE_EPG = 4
MOE_EXPERTS = MOE_GROUPS * MOE_EPG
MOE_FF = 512

VMEM_LIMIT_BYTES = 48 * 1024 * 1024
DECODE_VMEM_LIMIT_BYTES = 56 * 1024 * 1024


def _rmsnorm(x, g):
    xf = x.astype(F32)
    y = xf * lax.rsqrt(jnp.mean(xf * xf, axis=-1, keepdims=True) + EPS)
    return (y * g.astype(F32)).astype(x.dtype)


LANES = 128
MXU_WIDTH = 256


def _norm_matmul_body(*refs, normalize, has_res):
    if has_res:
        x_ref, g_ref, w_ref, res_ref, o_ref, xb_ref = refs
    else:
        x_ref, g_ref, w_ref, o_ref, xb_ref = refs

    @pl.when(pl.program_id(1) == 0)
    def _():
        x = x_ref[...].astype(F32)
        if normalize:
            x = x * lax.rsqrt(jnp.mean(x * x, axis=-1, keepdims=True) + EPS) * g_ref[...]
        xb_ref[...] = x.astype(BF16)

    acc = jnp.dot(xb_ref[...], w_ref[...].astype(BF16), preferred_element_type=F32)
    if has_res:
        acc = acc + res_ref[...]
    o_ref[...] = acc


def _pick_tile(n, pref):
    t = min(n, pref)
    while n % t:
        t //= 2
    return t


def norm_matmul(x, g, w, *, normalize=True, res=None, tm=None, tn=512):
    t, k = x.shape
    n_true = w.shape[1]
    tm = _pick_tile(t, tm or (2048 if k <= 1024 else 1024))
    col_tile = MXU_WIDTH if n_true > MXU_WIDTH else LANES
    if n_true % col_tile:
        assert res is None
        w = jnp.pad(w, ((0, 0), (0, col_tile - n_true % col_tile)))
    n = w.shape[1]
    tn = _pick_tile(n, tn)
    if g is None:
        g = jnp.ones((k,), F32)
    in_specs = [
        pl.BlockSpec((tm, k), lambda i, j: (i, 0)),
        pl.BlockSpec((1, k), lambda i, j: (0, 0)),
        pl.BlockSpec((k, tn), lambda i, j: (0, j)),
    ]
    args = [x, g.reshape(1, k), w]
    if res is not None:
        in_specs.append(pl.BlockSpec((tm, tn), lambda i, j: (i, j)))
        args.append(res)
    out = pl.pallas_call(
        functools.partial(_norm_matmul_body, normalize=normalize, has_res=res is not None),
        name="norm_matmul",
        grid=(t // tm, n // tn),
        in_specs=in_specs,
        out_specs=pl.BlockSpec((tm, tn), lambda i, j: (i, j)),
        out_shape=jax.ShapeDtypeStruct((t, n), F32),
        scratch_shapes=[pltpu.VMEM((tm, k), BF16)],
        compiler_params=pltpu.CompilerParams(
            dimension_semantics=("parallel", "arbitrary"),
            vmem_limit_bytes=VMEM_LIMIT_BYTES),
    )(*args)
    return out if n == n_true else out[:, :n_true]


ROUTE_LANES = 128
MOE_TILE = 1024
MOE_ROWS = 160
MOE_EXPERTS_PER_STEP = 2
NEG = -1e30


def _router_body(x_ref, g_ref, w_ref, b_ref, u_ref, xn_ref, rank_ref, wt_ref):
    x = x_ref[...]
    tm = x.shape[0]
    xn = x * lax.rsqrt(jnp.mean(x * x, axis=-1, keepdims=True) + EPS) * g_ref[...]
    xb = xn.astype(BF16)
    xn_ref[...] = xb
    logits = jnp.dot(xb, w_ref[...].astype(BF16), preferred_element_type=F32) + b_ref[...]
    lane = lax.broadcasted_iota(jnp.int32, (tm, ROUTE_LANES), 1).astype(F32)

    def first_max(mask):
        v = jnp.max(jnp.where(mask, logits, NEG), axis=-1, keepdims=True)
        i = jnp.min(jnp.where(mask & (logits == v), lane, float(ROUTE_LANES)), axis=-1, keepdims=True)
        return v, i

    is_group = lane < MOE_GROUPS
    mg, g_idx = first_max(is_group)
    pg_top = 1.0 / jnp.sum(jnp.where(is_group, jnp.exp(logits - mg), 0.0), axis=-1, keepdims=True)
    lo = MOE_GROUPS + MOE_EPG * g_idx
    in_group = (lane >= lo) & (lane < lo + MOE_EPG)
    v1, i1 = first_max(in_group)
    v2, i2 = first_max(in_group & (lane != i1))
    e2 = jnp.exp(v2 - v1)
    w_a = pg_top / (1.0 + e2)
    w_b = pg_top * e2 / (1.0 + e2)
    info = jnp.where(lane == 0, i1 - MOE_GROUPS, jnp.where(lane == 1, i2 - MOE_GROUPS,
                     jnp.where(lane == 2, w_a, jnp.where(lane == 3, w_b, 0.0))))
    info_t = info.T
    e_a, e_b, w_at, w_bt = info_t[0:1], info_t[1:2], info_t[2:3], info_t[3:4]
    expert = lax.broadcasted_iota(jnp.int32, (MOE_EXPERTS, tm), 0).astype(F32)
    m_a = e_a == expert
    m_b = e_b == expert
    onehot = jnp.concatenate([jnp.where(m_a, 1.0, 0.0), jnp.where(m_b, 1.0, 0.0)], axis=0).astype(BF16)
    cum = jnp.dot(onehot, u_ref[...], preferred_element_type=F32)
    cum_a, cum_b = cum[:MOE_EXPERTS], cum[MOE_EXPERTS:]
    n_a = cum_a[:, tm - 1:tm]
    rank_ref[...] = jnp.where(m_a, cum_a - 1.0, jnp.where(m_b, n_a + cum_b - 1.0, -1.0))
    wt_ref[...] = jnp.where(m_a, w_at, jnp.where(m_b, w_bt, 0.0))


def moe_route(x, g, w_rg, b_rg, w_re, b_re, tm):
    t = x.shape[0]
    pad = ROUTE_LANES - MOE_GROUPS - MOE_EXPERTS
    w = jnp.pad(jnp.concatenate([w_rg, w_re], axis=1), ((0, 0), (0, pad)))
    b = jnp.pad(jnp.concatenate([b_rg, b_re]), (0, pad)).reshape(1, ROUTE_LANES)
    upper = jnp.asarray(np.triu(np.ones((tm, tm), np.float32)), BF16)
    const = lambda a: pl.BlockSpec(a.shape, lambda i: (0,) * a.ndim)
    g2 = g.reshape(1, D_MODEL)
    return pl.pallas_call(
        _router_body,
        name="moe_router",
        grid=(t // tm,),
        in_specs=[pl.BlockSpec((tm, D_MODEL), lambda i: (i, 0)), const(g2), const(w), const(b), const(upper)],
        out_specs=[pl.BlockSpec((tm, D_MODEL), lambda i: (i, 0)),
                   pl.BlockSpec((MOE_EXPERTS, tm), lambda i: (0, i)),
                   pl.BlockSpec((MOE_EXPERTS, tm), lambda i: (0, i))],
        out_shape=[jax.ShapeDtypeStruct((t, D_MODEL), BF16),
                   jax.ShapeDtypeStruct((MOE_EXPERTS, t), F32),
                   jax.ShapeDtypeStruct((MOE_EXPERTS, t), F32)],
        compiler_params=pltpu.CompilerParams(
            dimension_semantics=("parallel",), vmem_limit_bytes=VMEM_LIMIT_BYTES),
    )(x, g2, w, b, upper)


def _moe_expert_body(cnt_ref, xn_ref, rank_ref, wt_ref, w1_ref, w3_ref, w2_ref, res_ref, o_ref, *, rows):
    ti, step = pl.program_id(0), pl.program_id(1)

    @pl.when(step == 0)
    def _():
        o_ref[...] = res_ref[...]

    tm = xn_ref.shape[0]
    experts = [step * MOE_EXPERTS_PER_STEP + k for k in range(MOE_EXPERTS_PER_STEP)]
    ranks = [rank_ref[pl.ds(e, 1), :] for e in experts]
    wts = [wt_ref[pl.ds(e, 1), :] for e in experts]
    n_chunks = functools.reduce(jnp.maximum, [(cnt_ref[e, ti] + rows - 1) // rows for e in experts])

    def chunk(c, carry):
        r = (lax.broadcasted_iota(jnp.int32, (rows, tm), 0) + c * rows).astype(F32)
        total = None
        for k in range(MOE_EXPERTS_PER_STEP):
            sel = jnp.where(ranks[k] == r, 1.0, 0.0)
            selb = sel.astype(BF16)
            xs = jnp.dot(selb, xn_ref[...], preferred_element_type=F32).astype(BF16)
            a = jnp.dot(xs, w1_ref[k].astype(BF16), preferred_element_type=F32)
            b = jnp.dot(xs, w3_ref[k].astype(BF16), preferred_element_type=F32)
            h = (a * jax.nn.sigmoid(a) * b * jnp.sum(sel * wts[k], axis=1, keepdims=True)).astype(BF16)
            y = jnp.dot(h, w2_ref[k].astype(BF16), preferred_element_type=F32)
            y_hi = y.astype(BF16)
            y_lo = (y - y_hi.astype(F32)).astype(BF16)
            tn = (((0,), (0,)), ((), ()))
            back = (lax.dot_general(selb, y_hi, tn, preferred_element_type=F32)
                    + lax.dot_general(selb, y_lo, tn, preferred_element_type=F32))
            total = back if total is None else total + back
        o_ref[...] += total
        return carry

    lax.fori_loop(0, n_chunks, chunk, 0)


def moe_experts(xn, rank, wt, counts, w1, w3, w2, res, tm):
    t = xn.shape[0]
    rows = min(MOE_ROWS, tm)
    grid_spec = pltpu.PrefetchScalarGridSpec(
        num_scalar_prefetch=1,
        grid=(t // tm, MOE_EXPERTS // MOE_EXPERTS_PER_STEP),
        in_specs=[
            pl.BlockSpec((tm, D_MODEL), lambda i, e, c: (i, 0)),
            pl.BlockSpec((MOE_EXPERTS, tm), lambda i, e, c: (0, i)),
            pl.BlockSpec((MOE_EXPERTS, tm), lambda i, e, c: (0, i)),
            pl.BlockSpec((MOE_EXPERTS_PER_STEP, D_MODEL, MOE_FF), lambda i, e, c: (e, 0, 0)),
            pl.BlockSpec((MOE_EXPERTS_PER_STEP, D_MODEL, MOE_FF), lambda i, e, c: (e, 0, 0)),
            pl.BlockSpec((MOE_EXPERTS_PER_STEP, MOE_FF, D_MODEL), lambda i, e, c: (e, 0, 0)),
            pl.BlockSpec((tm, D_MODEL), lambda i, e, c: (i, 0)),
        ],
        out_specs=pl.BlockSpec((tm, D_MODEL), lambda i, e, c: (i, 0)),
    )
    return pl.pallas_call(
        functools.partial(_moe_expert_body, rows=rows),
        name="moe_experts",
        grid_spec=grid_spec,
        out_shape=jax.ShapeDtypeStruct((t, D_MODEL), F32),
        compiler_params=pltpu.CompilerParams(
            dimension_semantics=("parallel", "arbitrary"),
            vmem_limit_bytes=VMEM_LIMIT_BYTES),
    )(counts, xn, rank, wt, w1, w3, w2, res)


def hier_moe_residual(x, g, w_rg, b_rg, w_re, b_re, w1, w3, w2):
    t_true = x.shape[0]
    tm = MOE_TILE if t_true % MOE_TILE == 0 else ROUTE_LANES
    if t_true % tm:
        x = jnp.pad(x, ((0, tm - t_true % tm), (0, 0)))
    t = x.shape[0]
    xn, rank, wt = moe_route(x, g, w_rg, b_rg, w_re, b_re, tm)
    counts = jnp.sum((rank >= 0).reshape(MOE_EXPERTS, t // tm, tm), axis=-1, dtype=jnp.int32)
    return moe_experts(xn, rank, wt, counts, w1, w3, w2, x, tm)[:t_true]


def _causal_dwconv(u, buf, w, b):
    L = u.shape[1]
    ext = jnp.concatenate([buf.astype(u.dtype), u], axis=1)
    out = b + sum(ext[:, k:k + L] * w[k] for k in range(M_CONV))
    return out, ext[:, L:]


def _ssd_scan(x, dt, a, bm, cm, h0):
    bsz, L = x.shape[0], x.shape[1]
    q = M_CHUNK if L % M_CHUNK == 0 else L
    nc = L // q
    hpg = M_HEADS // M_GROUPS
    xdt = (x * dt[..., None]).reshape(bsz, nc, q, M_GROUPS, hpg, M_HEADDIM)
    acum = jnp.cumsum((dt * a).reshape(bsz, nc, q, M_GROUPS, hpg), axis=2)
    bc = bm.reshape(bsz, nc, q, M_GROUPS, M_DSTATE)
    cc = cm.reshape(bsz, nc, q, M_GROUPS, M_DSTATE)
    tri = jnp.tril(jnp.ones((q, q), bool))[:, :, None, None]
    seg = acum[:, :, :, None] - acum[:, :, None]
    decay = jnp.exp(jnp.where(tri, seg, -jnp.inf))
    cb = jnp.einsum('bclgn,bcsgn->bclsg', cc, bc)
    y_diag = jnp.einsum('bclsg,bclsgh,bcsghp->bclghp', cb, decay, xdt)
    decay_end = jnp.exp(acum[:, :, -1:] - acum)
    states = jnp.einsum('bcsgn,bcsgh,bcsghp->bcghpn', bc, decay_end, xdt)
    chunk_decay = jnp.exp(acum[:, :, -1])

    def step(h, inp):
        cd, st = inp
        return cd[..., None, None] * h + st, h

    h_last, h_prev = lax.scan(step, h0.reshape(bsz, M_GROUPS, hpg, M_HEADDIM, M_DSTATE),
                              (chunk_decay.swapaxes(0, 1), states.swapaxes(0, 1)))
    y_off = jnp.einsum('bclgn,bclgh,bcghpn->bclghp', cc, jnp.exp(acum), h_prev.swapaxes(0, 1))
    y = (y_diag + y_off).reshape(bsz, L, M_HEADS, M_HEADDIM)
    return y, h_last.reshape(bsz, M_HEADS, M_HEADDIM, M_DSTATE)


def mamba_mixer(x, g, conv_buf, h0, w_in, conv_w, conv_b, dt_bias, a_log, d_skip, norm_g, w_out):
    bsz, L, _ = x.shape
    proj = norm_matmul(x.reshape(bsz * L, D_MODEL), g, w_in).reshape(bsz, L, -1)
    z = proj[..., :M_DINNER]
    xbc = proj[..., M_DINNER:M_DINNER + M_CONV_DIM]
    dt_raw = proj[..., M_DINNER + M_CONV_DIM:]
    xbc, new_buf = _causal_dwconv(xbc, conv_buf, conv_w, conv_b)
    xbc = jax.nn.silu(xbc)
    xs = xbc[..., :M_DINNER].reshape(bsz, L, M_HEADS, M_HEADDIM)
    bm = xbc[..., M_DINNER:M_DINNER + M_GN].reshape(bsz, L, M_GROUPS, M_DSTATE)
    cm = xbc[..., M_DINNER + M_GN:].reshape(bsz, L, M_GROUPS, M_DSTATE)
    dt = jax.nn.softplus(dt_raw + dt_bias)
    a = -jnp.exp(a_log)
    y, h = _ssd_scan(xs, dt, a, bm, cm, h0)
    y = y + xs * d_skip[:, None]
    y = y.reshape(bsz, L, M_DINNER)
    y = _rmsnorm(y * jax.nn.silu(z), norm_g)
    out = norm_matmul(y.reshape(bsz * L, M_DINNER), None, w_out, normalize=False)
    return out.reshape(bsz, L, D_MODEL), new_buf, h


SSD_HPG = M_HEADS // M_GROUPS
SSD_GROUP_ROWS = SSD_HPG * M_HEADDIM
CONV_PAD = 8


def _transpose_cols(x):
    return jnp.concatenate([x[:, j * LANES:(j + 1) * LANES].T for j in range(x.shape[1] // LANES)], axis=0)


def _transpose_rows(x):
    return jnp.concatenate([x[j * LANES:(j + 1) * LANES, :].T for j in range(x.shape[0] // LANES)], axis=1)


def _ssd_body(z_ref, xbc_ref, dtr_ref, cw_ref, cb_ref, dtb_ref, a_ref, dcol_ref, ng_ref, ltri_ref,
              y_ref, conv_ref, h_ref, xbuf, *, q):
    c = pl.program_id(1)

    @pl.when(c == 0)
    def _():
        h_ref[...] = jnp.zeros_like(h_ref)
        xbuf[0:CONV_PAD, :] = jnp.zeros((CONV_PAD, M_CONV_DIM), F32)

    xbuf[CONV_PAD:CONV_PAD + q, :] = xbc_ref[...]
    conv = cb_ref[...]
    for k in range(M_CONV):
        start = CONV_PAD - (M_CONV - 1) + k
        conv = conv + xbuf[start:start + q, :] * cw_ref[k:k + 1, :]
    tail = xbuf[CONV_PAD + q - (M_CONV - 1):CONV_PAD + q, :]
    xbuf[CONV_PAD - (M_CONV - 1):CONV_PAD, :] = tail
    conv_ref[0] = tail
    xc = conv * jax.nn.sigmoid(conv)
    xs = xc[:, :M_DINNER]
    xs_t = _transpose_cols(xs)

    pre = dtr_ref[...] + dtb_ref[...]
    dt = jnp.maximum(pre, 0.0) + jnp.log1p(jnp.exp(-jnp.abs(pre)))
    da = dt * a_ref[...]
    ltri = ltri_ref[...]
    acum = sum(jnp.dot(ltri, part, preferred_element_type=F32) for part in _split3(da))
    dt_t = dt.T
    acum_t = acum.T
    li = lax.broadcasted_iota(jnp.int32, (q, q), 0)
    si = lax.broadcasted_iota(jnp.int32, (q, q), 1)
    causal = li >= si

    y_t = []
    for g in range(M_GROUPS):
        bm = xc[:, M_DINNER + g * M_DSTATE:M_DINNER + (g + 1) * M_DSTATE].astype(BF16)
        cm = xc[:, M_DINNER + M_GN + g * M_DSTATE:M_DINNER + M_GN + (g + 1) * M_DSTATE].astype(BF16)
        cb = _nt_dot(cm, bm)
        r0 = g * SSD_GROUP_ROWS
        h_prev = h_ref[0, r0:r0 + SSD_GROUP_ROWS, :]
        y_off = _nt_dot(h_prev.astype(BF16), cm)
        x_dec, scale = [], []
        for hh in range(g * SSD_HPG, (g + 1) * SSD_HPG):
            a_row = acum_t[hh:hh + 1, :]
            a_col = acum[:, hh:hh + 1]
            decay = jnp.where(causal, jnp.exp(a_col - a_row), 0.0)
            m = (cb * decay).astype(BF16)
            rows = slice(hh * M_HEADDIM, (hh + 1) * M_HEADDIM)
            xs_h = xs_t[rows]
            xdt = xs_h * dt_t[hh:hh + 1, :]
            y_h = _nt_dot(xdt.astype(BF16), m)
            y_h = y_h + y_off[rows.start - r0:rows.stop - r0] * jnp.exp(a_row) + xs_h * dcol_ref[hh:hh + 1, :]
            y_t.append(y_h)
            a_last = a_row[:, q - 1:q]
            x_dec.append(xdt * jnp.exp(a_last - a_row))
            scale.append(jnp.broadcast_to(jnp.exp(a_last), (M_HEADDIM, 1)))
        upd = jnp.dot(jnp.concatenate(x_dec, axis=0).astype(BF16), bm, preferred_element_type=F32)
        h_ref[0, r0:r0 + SSD_GROUP_ROWS, :] = jnp.concatenate(scale, axis=0) * h_prev + upd

    y = _transpose_rows(jnp.concatenate(y_t, axis=0))
    zz = z_ref[...]
    yg = y * (zz * jax.nn.sigmoid(zz))
    yg = yg * lax.rsqrt(jnp.mean(yg * yg, axis=-1, keepdims=True) + EPS) * ng_ref[...]
    y_ref[...] = yg.astype(BF16)


def ssd_prompt(z, xbc, dtr, conv_w, conv_b, dt_bias, a_log, d_skip, norm_g, bsz, L):
    q = M_CHUNK
    assert L % q == 0 and q == LANES
    nc = L // q
    padl = lambda v: jnp.pad(v, (0, LANES - v.shape[0]))
    dtb = padl(dt_bias).reshape(1, LANES)
    a_row = padl(-jnp.exp(a_log)).reshape(1, LANES)
    dcol = jnp.broadcast_to(padl(d_skip).reshape(LANES, 1), (LANES, LANES))
    ltri = jnp.asarray(np.tril(np.ones((q, q), np.float32)), BF16)
    const = lambda a: pl.BlockSpec(a.shape, lambda b, c: (0,) * a.ndim)
    tok = lambda w: pl.BlockSpec((q, w), lambda b, c: (b * nc + c, 0))
    cb2 = conv_b.reshape(1, M_CONV_DIM)
    ng2 = norm_g.reshape(1, M_DINNER)
    y, conv_tail, h = pl.pallas_call(
        functools.partial(_ssd_body, q=q),
        name="ssd_chunks",
        grid=(bsz, nc),
        in_specs=[tok(M_DINNER), tok(M_CONV_DIM), tok(LANES), const(conv_w), const(cb2), const(dtb),
                  const(a_row), const(dcol), const(ng2), const(ltri)],
        out_specs=[tok(M_DINNER),
                   pl.BlockSpec((1, M_CONV - 1, M_CONV_DIM), lambda b, c: (b, 0, 0)),
                   pl.BlockSpec((1, M_DINNER, M_DSTATE), lambda b, c: (b, 0, 0))],
        out_shape=[jax.ShapeDtypeStruct((bsz * L, M_DINNER), BF16),
                   jax.ShapeDtypeStruct((bsz, M_CONV - 1, M_CONV_DIM), F32),
                   jax.ShapeDtypeStruct((bsz, M_DINNER, M_DSTATE), F32)],
        scratch_shapes=[pltpu.VMEM((CONV_PAD + q, M_CONV_DIM), F32)],
        compiler_params=pltpu.CompilerParams(
            dimension_semantics=("parallel", "arbitrary"),
            vmem_limit_bytes=VMEM_LIMIT_BYTES),
    )(z, xbc, dtr, conv_w, cb2, dtb, a_row, dcol, ng2, ltri)
    return y, conv_tail, h.reshape(bsz, M_HEADS, M_HEADDIM, M_DSTATE)


def mamba_prompt_mixer(x, g, w_in, conv_w, conv_b, dt_bias, a_log, d_skip, norm_g, w_out):
    bsz, L, _ = x.shape
    x2 = x.reshape(bsz * L, D_MODEL)
    z = norm_matmul(x2, g, w_in[:, :M_DINNER])
    xbc = norm_matmul(x2, g, w_in[:, M_DINNER:M_DINNER + M_CONV_DIM])
    dtr = norm_matmul(x2, g, jnp.pad(w_in[:, M_DINNER + M_CONV_DIM:], ((0, 0), (0, LANES - M_HEADS))))
    y, conv_tail, h = ssd_prompt(z, xbc, dtr, conv_w, conv_b, dt_bias, a_log, d_skip, norm_g, bsz, L)
    out = norm_matmul(y, None, w_out, normalize=False, res=x2)
    return out.reshape(bsz, L, D_MODEL), conv_tail, h


def _rel_bucket(dist):
    exact = REL_BUCKETS // 2
    d = jnp.maximum(dist, 0)
    ratio = jnp.log(jnp.maximum(d, 1).astype(F32) / exact) / math.log(REL_MAX_DIST / exact)
    large = jnp.minimum(exact + (ratio * (REL_BUCKETS - exact)).astype(jnp.int32), REL_BUCKETS - 1)
    return jnp.where(d < exact, d, large)


ATT_TQ = 128
ATT_TK = 128
ATT_UNROLL_LOG2 = 2


def _nt_dot(a, b):
    return lax.dot_general(a, b, (((1,), (1,)), ((), ())), preferred_element_type=F32)


def _compress_body(rk_ref, rv_ref, pe_ref, w1_ref, w2_ref, kg_ref, kc_ref, vc_ref, *, nb):
    half = (CMP_BLOCK // 2) * HEAD_DIM
    for kv, (r_ref, o_ref) in enumerate(((rk_ref, kc_ref), (rv_ref, vc_ref))):
        lo = (r_ref[0, 0, 0:nb, :] + pe_ref[kv, 0:1, :]).astype(BF16)
        hi = (r_ref[0, 0, 1:nb + 1, :] + pe_ref[kv, 1:2, :]).astype(BF16)
        h = (jnp.dot(lo, w1_ref[kv, :half, :].astype(BF16), preferred_element_type=F32)
             + jnp.dot(hi, w1_ref[kv, half:, :].astype(BF16), preferred_element_type=F32))
        h = h * jax.nn.sigmoid(h)
        o = jnp.dot(h.astype(BF16), w2_ref[kv].astype(BF16), preferred_element_type=F32)
        if kv == 0:
            o = o * lax.rsqrt(jnp.mean(o * o, axis=-1, keepdims=True) + EPS) * kg_ref[...]
        o_ref[0, 0] = o


def compress_rows(rk, rv, pe, w1, w2, kg):
    bsz, g, nbp, width = rk.shape
    nb = nbp - 8
    strip = pl.BlockSpec((1, 1, nbp, width), lambda b, j: (b, j, 0, 0))
    out = pl.BlockSpec((1, 1, nb, HEAD_DIM), lambda b, j: (b, j, 0, 0))
    full = lambda a: pl.BlockSpec(a.shape, lambda b, j: (0,) * a.ndim)
    pe2 = pe.reshape(2, 2, width)
    kg2 = kg.reshape(1, HEAD_DIM)
    return pl.pallas_call(
        functools.partial(_compress_body, nb=nb),
        name="nsa_compress",
        grid=(bsz, g),
        in_specs=[strip, strip, full(pe2), full(w1), full(w2), full(kg2)],
        out_specs=[out, out],
        out_shape=[jax.ShapeDtypeStruct((bsz, g, nb, HEAD_DIM), F32)] * 2,
        compiler_params=pltpu.CompilerParams(
            dimension_semantics=("parallel", "parallel"),
            vmem_limit_bytes=VMEM_LIMIT_BYTES),
    )(rk, rv, pe2, w1, w2, kg2)


def _split3(x):
    a = x.astype(BF16)
    r = x - a.astype(F32)
    b = r.astype(BF16)
    c = (r - b.astype(F32)).astype(BF16)
    return a, b, c


def _bias_tables(rel_table, n_qtiles):
    tab = rel_table[_rel_bucket(jnp.arange(REL_MAX_DIST + 1))].astype(F32)
    tab = tab.T.reshape(KV_HEADS, HPG, REL_MAX_DIST + 1)

    def skew(v, rows, width, step):
        out = jnp.broadcast_to(v[..., None, :], v.shape[:-1] + (rows, width + step))
        out = out.reshape(v.shape[:-1] + (rows * (width + step),))[..., :rows * width]
        return out.reshape(v.shape[:-1] + (rows, width))

    width = 2 * ATT_TK
    w = np.arange(width + 1)
    s_minus_t = np.where(w <= ATT_TK, w, w - (width + 1))
    d_idx = np.stack([np.clip(delta * ATT_TK - s_minus_t, 0, REL_MAX_DIST) for delta in range(3)])
    btile = skew(tab[:, :, d_idx], ATT_TQ, width, 1)[..., :ATT_TK]
    btile = btile.transpose(0, 2, 1, 3, 4).reshape(KV_HEADS, 3, HPG * ATT_TQ, ATT_TK)

    n_q = n_qtiles * ATT_TQ
    u = np.arange(n_q + CMP_STRIDE)
    v_cmp = tab[:, :, np.clip(u - (CMP_BLOCK - 1), 0, REL_MAX_DIST)]
    bcmp = skew(v_cmp, ATT_TK, n_q, CMP_STRIDE)
    bcmp = bcmp.reshape(KV_HEADS, HPG, ATT_TK, n_qtiles, ATT_TQ).transpose(3, 0, 1, 4, 2)
    return btile, bcmp.reshape(n_qtiles, KV_HEADS, HPG * ATT_TQ, ATT_TK)


def _nsa_attn_t_body(qt_ref, kc_ref, vct_ref, ks_ref, vst_ref, kw_ref, vwt_ref, gate_ref, bcmp_ref, btile_ref,
                     ovl_ref, o_ref, s_scr, *, n_cmp, n_sel, topk):
    i = pl.program_id(2)
    q0 = i * ATT_TQ
    cols = HPG * ATT_TQ
    qs = qt_ref[0, 0, 0]
    key = lax.broadcasted_iota(jnp.int32, (ATT_TK, cols), 0)
    qpos = q0 + (lax.broadcasted_iota(jnp.int32, (ATT_TK, cols), 1) & (ATT_TQ - 1))

    def fold(x, op):
        return op(x.reshape(ATT_TK // 8, 8, cols), axis=0)

    s = jnp.dot(kc_ref[0, 0].astype(BF16), qs, preferred_element_type=F32) + bcmp_ref[0, 0]
    valid = (qpos >= key * CMP_STRIDE + (CMP_BLOCK - 1)) & (key < n_cmp)
    s = jnp.where(valid, s, NEG)
    e = jnp.where(valid, jnp.exp(s - jnp.max(s, axis=0, keepdims=True)), 0.0)
    p = e / jnp.maximum(jnp.sum(e, axis=0, keepdims=True), 1e-30)
    o_cmp = jnp.dot(vct_ref[0, 0].astype(BF16), p.astype(BF16), preferred_element_type=F32)

    psum = p[:, 0:ATT_TQ]
    for h in range(1, HPG):
        psum = psum + p[:, h * ATT_TQ:(h + 1) * ATT_TQ]
    imp_t = jnp.dot(ovl_ref[...], psum.astype(BF16), preferred_element_type=F32)
    nblk = imp_t.shape[0]
    nrank = -(-n_sel // 8) * 8
    j = lax.broadcasted_iota(jnp.int32, (nrank, ATT_TQ), 0)
    qblk = (q0 + lax.broadcasted_iota(jnp.int32, (nrank, ATT_TQ), 1)) // SEL_BLOCK
    forced = (j == qblk) | (j == 0)
    score = jnp.where(forced, 1e9, jnp.where(j <= qblk, imp_t[:nrank], NEG))
    score = jnp.where(j < n_sel, score, -3e38)
    rank = jnp.zeros((nrank, ATT_TQ), F32)
    for jp in range(n_sel):
        other = score[jp:jp + 1, :]
        beats = (other > score) | ((other == score) & (jp < j))
        rank = rank + jnp.where(beats, 1.0, 0.0)
    sel_t = jnp.where((rank < topk) & (j < n_sel), 1.0, 0.0)
    if nrank < nblk:
        sel_t = jnp.concatenate([sel_t, jnp.zeros((nblk - nrank, ATT_TQ), F32)], axis=0)
    sel_t = sel_t.astype(BF16)

    kk = lax.broadcasted_iota(jnp.int32, (ATT_TK, nblk), 0) // SEL_BLOCK
    jj = lax.broadcasted_iota(jnp.int32, (ATT_TK, nblk), 1)
    blocks_per_step = ATT_TK // SEL_BLOCK

    def attend(k_ref, vt_ref, lo, hi, penalty_fn, whole_groups=False):
        def tile(kc):
            return pl.ds(pl.multiple_of(kc * ATT_TK, ATT_TK), ATT_TK)

        def sweep(fn, init):
            extra = (1 << ATT_UNROLL_LOG2) - 1 if whole_groups else 0
            n_groups = lax.shift_right_logical(hi - lo + extra, ATT_UNROLL_LOG2)

            def group(gi, carry):
                kc = lo + gi * (1 << ATT_UNROLL_LOG2)
                for u in range(1 << ATT_UNROLL_LOG2):
                    carry = fn(kc + u, carry)
                return carry

            carry = lax.fori_loop(0, n_groups, group, init)
            return lax.fori_loop(lo + n_groups * (1 << ATT_UNROLL_LOG2), hi, fn, carry)

        def scores(kc, m_run):
            kblk = k_ref[0, 0, tile(kc), :].astype(BF16)
            sc = (jnp.dot(kblk, qs, preferred_element_type=F32) + btile_ref[0, jnp.clip(i - kc, 0, 2)]
                  + penalty_fn(kc))
            s_scr[tile(kc), :] = sc
            return jnp.maximum(m_run, fold(sc, jnp.max))

        m = jnp.max(sweep(scores, jnp.full((8, cols), NEG, F32)), axis=0, keepdims=True)

        def exps(kc, l_run):
            ex = jnp.exp(s_scr[tile(kc), :] - m)
            s_scr[tile(kc), :] = ex
            return l_run + fold(ex, jnp.sum)

        den = jnp.sum(sweep(exps, jnp.zeros((8, cols), F32)), axis=0, keepdims=True)
        inv = 1.0 / jnp.maximum(den, 1e-30)

        def weighted(kc, acc):
            pr = (s_scr[tile(kc), :] * inv).astype(BF16)
            return acc + jnp.dot(vt_ref[0, 0, :, tile(kc)].astype(BF16), pr, preferred_element_type=F32)

        return sweep(weighted, jnp.zeros((HEAD_DIM, cols), F32))

    def sel_penalty(kc):
        expand = jnp.where(jj == kc * blocks_per_step + kk, 1.0, 0.0).astype(BF16)
        chosen = jnp.dot(expand, sel_t, preferred_element_type=F32)
        pen = jnp.concatenate([(chosen - 1.0) * (-NEG)] * HPG, axis=1)
        return jnp.where(qpos >= kc * ATT_TK + key, pen, NEG)

    def win_penalty(kc):
        dist = qpos - (kc * ATT_TK + key)
        return jnp.where((dist >= 0) & (dist <= WINDOW), 0.0, NEG)

    o_sel = attend(ks_ref, vst_ref, 0, i + 1, sel_penalty, whole_groups=True)
    o_win = attend(kw_ref, vwt_ref, jnp.maximum(i - WINDOW // ATT_TK, 0), i + 1, win_penalty)
    gate = jax.nn.sigmoid(gate_ref[0, 0, 0])
    o_ref[0, 0, 0] = gate[0:1] * o_cmp + gate[1:2] * o_sel + gate[2:3] * o_win


def nsa_prompt_attention_t(q, kcmp, vcmp, ks, vs_t, kw, vw_t, gl, rel_table):
    bsz, L = q.shape[0], q.shape[1]
    assert L % ATT_TQ == 0 and kcmp.shape[2] == ATT_TK
    assert math.frexp(SCALE)[0] == 0.5, "the kernel folds SCALE into q, exact only for powers of two"
    nq = L // ATT_TQ
    n_cmp = (L - CMP_BLOCK) // CMP_STRIDE + 1
    n_sel = L // SEL_BLOCK
    assert n_sel <= ATT_TK and nq % (1 << ATT_UNROLL_LOG2) == 0
    cols = HPG * ATT_TQ
    btile, bcmp = _bias_tables(rel_table, nq)
    btile = btile.transpose(0, 1, 3, 2)
    bcmp = bcmp.transpose(0, 1, 3, 2)
    c = np.arange(ATT_TK)[None, :] * CMP_STRIDE
    sb = np.arange(ATT_TK)[:, None] * SEL_BLOCK
    ovl = ((c < sb + SEL_BLOCK) & (c + CMP_BLOCK > sb) & (np.arange(ATT_TK)[None, :] < n_cmp)
           & (np.arange(ATT_TK)[:, None] < n_sel))
    ovl = jnp.asarray(ovl, BF16)
    q_t = (q * SCALE).astype(BF16).reshape(bsz, nq, ATT_TQ, KV_HEADS, HPG, HEAD_DIM).transpose(0, 3, 1, 5, 4, 2)
    q_t = q_t.reshape(bsz, KV_HEADS, nq, HEAD_DIM, cols)
    kcmp, ks, kw, vs_t, vw_t = (a.astype(BF16) for a in (kcmp, ks, kw, vs_t, vw_t))
    gate_t = gl.reshape(bsz, nq, ATT_TQ, KV_HEADS, HPG, 3).transpose(0, 3, 1, 5, 4, 2)
    gate_t = gate_t.reshape(bsz, KV_HEADS, nq, 3, cols)
    vc_t = vcmp.transpose(0, 1, 3, 2).astype(BF16)
    rows_spec = lambda n: pl.BlockSpec((1, 1, n, HEAD_DIM), lambda b, g, i: (b, g, 0, 0))
    cols_spec = lambda n: pl.BlockSpec((1, 1, HEAD_DIM, n), lambda b, g, i: (b, g, 0, 0))
    tile_spec = lambda r: pl.BlockSpec((1, 1, 1, r, cols), lambda b, g, i: (b, g, i, 0, 0))
    o_t = pl.pallas_call(
        functools.partial(_nsa_attn_t_body, n_cmp=n_cmp, n_sel=n_sel, topk=min(SEL_TOPK, n_sel)),
        name="nsa_attention",
        grid=(bsz, KV_HEADS, nq),
        in_specs=[
            tile_spec(HEAD_DIM),
            rows_spec(ATT_TK), cols_spec(ATT_TK), rows_spec(L), cols_spec(L), rows_spec(L), cols_spec(L),
            tile_spec(3),
            pl.BlockSpec((1, 1, ATT_TK, cols), lambda b, g, i: (i, g, 0, 0)),
            pl.BlockSpec((1, 3, ATT_TK, cols), lambda b, g, i: (g, 0, 0, 0)),
            pl.BlockSpec((ATT_TK, ATT_TK), lambda b, g, i: (0, 0)),
        ],
        out_specs=tile_spec(HEAD_DIM),
        out_shape=jax.ShapeDtypeStruct((bsz, KV_HEADS, nq, HEAD_DIM, cols), F32),
        scratch_shapes=[pltpu.VMEM((L, cols), F32)],
        compiler_params=pltpu.CompilerParams(
            dimension_semantics=("parallel", "parallel", "arbitrary"),
            vmem_limit_bytes=VMEM_LIMIT_BYTES),
    )(q_t, kcmp, vc_t, ks, vs_t, kw, vw_t, gate_t, bcmp, btile, ovl)
    o = o_t.reshape(bsz, KV_HEADS, nq, HEAD_DIM, HPG, ATT_TQ).transpose(0, 2, 5, 1, 4, 3)
    return o.reshape(bsz, L, N_HEADS * HEAD_DIM)


def nsa_prompt_mixer(x, g, w_in, q_g, k_g, cmp_pe, cmp_w1, cmp_w2, w_out, rel_table):
    bsz, L, _ = x.shape
    t = bsz * L
    proj = norm_matmul(x.reshape(t, D_MODEL), g, w_in)
    c0 = N_HEADS * HEAD_DIM
    q = _rmsnorm(proj[:, :c0].reshape(bsz, L, N_HEADS, HEAD_DIM), q_g)

    def kvpair(k, normed_g):
        kcols = proj[:, c0 + 2 * k * KV_DIM:c0 + (2 * k + 1) * KV_DIM]
        vcols = proj[:, c0 + (2 * k + 1) * KV_DIM:c0 + (2 * k + 2) * KV_DIM]
        if normed_g is not None:
            kcols = _rmsnorm(kcols.reshape(t, KV_HEADS, HEAD_DIM), normed_g).reshape(t, KV_DIM)
        rows = jnp.stack([kcols, vcols], axis=1).reshape(bsz, L, 2, KV_HEADS, HEAD_DIM)
        k4 = kcols.reshape(bsz, L, KV_HEADS, HEAD_DIM).transpose(0, 2, 1, 3)
        v4 = vcols.reshape(bsz, L, KV_HEADS, HEAD_DIM)
        return rows, k4, v4

    new_cmp, kc_rows, vc_rows = kvpair(0, None)
    new_sel, ks, vs = kvpair(1, k_g[1])
    new_win, kw, vw = kvpair(2, k_g[2])
    gl = proj[:, c0 + 6 * KV_DIM:].reshape(bsz, L, KV_HEADS, HPG, 3)
    vc_rows = vc_rows.transpose(0, 2, 1, 3)
    vs_t, vw_t = vs.transpose(0, 2, 3, 1), vw.transpose(0, 2, 3, 1)

    nb = L // CMP_STRIDE

    def strips(r):
        r = r.reshape(bsz, KV_HEADS, nb, CMP_STRIDE * HEAD_DIM)
        return jnp.pad(r, ((0, 0), (0, 0), (0, ATT_TK + 8 - nb), (0, 0)))

    kcmp, vcmp = compress_rows(strips(kc_rows), strips(vc_rows), cmp_pe, cmp_w1, cmp_w2, k_g[0])
    o = nsa_prompt_attention_t(q, kcmp, vcmp, ks, vs_t, kw, vw_t, gl, rel_table)
    y = norm_matmul(o.reshape(t, c0), None, w_out, normalize=False, res=x.reshape(t, D_MODEL))
    return y.reshape(bsz, L, D_MODEL), new_cmp, new_sel, new_win[:, -min(WINDOW, L):]


DEC_PAGES_PER_STEP = 8
STRIP = CMP_STRIDE
ROW_LANES = 2 * KV_DIM
SEL_LANES = 256


def _group_rmsnorm(x, gain_row):
    lane = lax.broadcasted_iota(jnp.int32, x.shape, 1) // HEAD_DIM
    sq = x * x
    ms = jnp.zeros_like(x)
    for grp in range(KV_HEADS):
        tot = jnp.sum(jnp.where(lane == grp, sq, 0.0), axis=-1, keepdims=True)
        ms = jnp.where(lane == grp, tot, ms)
    return x * lax.rsqrt(ms / HEAD_DIM + EPS) * gain_row


def _decode_cmp_body(pt_ref, *refs, n_strips, n_sel, qblk, topk):
    pages = refs[:DEC_PAGES_PER_STEP]
    (bd_ref, peterm_ref, w2bd_ref, kg_ref, qbd_ref, bias_ref, ovl_ref, upper_ref,
     ocmp_ref, idx_ref, seq, tbuf, hibuf) = refs[DEC_PAGES_PER_STEP:]
    s = pl.program_id(1)
    strips_per_page = pages[0].shape[2] // STRIP
    for r in range(DEC_PAGES_PER_STEP):
        row0 = pl.multiple_of((s * DEC_PAGES_PER_STEP + r) * strips_per_page, strips_per_page)
        for c in range(ROW_LANES // LANES):
            lanes = slice(c * LANES, (c + 1) * LANES)
            tbuf[c] = pages[r][0, lanes, :].T
            for l in range(STRIP):
                seq[l, pl.ds(row0, strips_per_page), lanes] = tbuf[c, pl.ds(l, strips_per_page, stride=STRIP), :]

    @pl.when(s == pl.num_programs(1) - 1)
    def _():
        n_cmp = n_strips - 1
        summaries = []
        for kv in range(2):
            halves = []
            for half in range(2):
                acc = jnp.zeros((n_strips, KV_DIM), F32)
                for l in range(STRIP):
                    w_idx = (kv * 2 + half) * STRIP + l
                    xl = (seq[l, :, kv * KV_DIM:(kv + 1) * KV_DIM] + peterm_ref[w_idx:w_idx + 1, :]).astype(BF16)
                    acc = acc + jnp.dot(xl, bd_ref[w_idx], preferred_element_type=F32)
                halves.append(acc)
            hibuf[0:n_strips, :] = halves[1]
            hibuf[n_strips:n_strips + 8, :] = jnp.zeros((8, KV_DIM), F32)
            h = halves[0] + hibuf[1:n_strips + 1, :]
            h = (h * jax.nn.sigmoid(h)).astype(BF16)
            o = jnp.dot(h, w2bd_ref[kv], preferred_element_type=F32)
            if kv == 0:
                o = _group_rmsnorm(o, kg_ref[...])
            summaries.append(o.astype(BF16))
        kcmp, vcmp = summaries

        qbd = qbd_ref[0].astype(BF16)
        sc = _nt_dot(qbd, kcmp) * SCALE + bias_ref[...]
        col = lax.broadcasted_iota(jnp.int32, sc.shape, 1)
        valid = col < n_cmp
        sc = jnp.where(valid, sc, NEG)
        e = jnp.where(valid, jnp.exp(sc - jnp.max(sc, axis=-1, keepdims=True)), 0.0)
        p = e / jnp.maximum(jnp.sum(e, axis=-1, keepdims=True), 1e-30)
        pb = p.astype(BF16)
        o_cmp = jnp.dot(pb, vcmp, preferred_element_type=F32)
        head_grp = lax.broadcasted_iota(jnp.int32, o_cmp.shape, 0) // HPG
        lane_grp = lax.broadcasted_iota(jnp.int32, o_cmp.shape, 1) // HEAD_DIM
        ocmp_ref[0] = jnp.where(head_grp == lane_grp, o_cmp, 0.0)

        psum = jnp.concatenate(
            [jnp.sum(p[grp * HPG:(grp + 1) * HPG], axis=0, keepdims=True) for grp in range(KV_HEADS)]
            + [jnp.zeros((8 - KV_HEADS, n_strips), F32)], axis=0)
        imp = jnp.dot(psum.astype(BF16), ovl_ref[...], preferred_element_type=F32)
        j = lax.broadcasted_iota(jnp.int32, imp.shape, 1)
        forced = (j == qblk) | (j == 0)
        score = jnp.where(forced, 1e9, jnp.where(j <= qblk, imp, NEG))
        score = jnp.where(j < n_sel, score, -3e38)
        rank = jnp.zeros(imp.shape, F32)
        for jp in range(n_sel):
            other = score[:, jp:jp + 1]
            beats = (other > score) | ((other == score) & (jp < j))
            rank = rank + jnp.where(beats, 1.0, 0.0)
        chosen = (rank < topk) & (j < n_sel)
        cum = jnp.dot(jnp.where(chosen, 1.0, 0.0).astype(BF16), upper_ref[...], preferred_element_type=F32)
        jf = j.astype(F32)
        out_lane = lax.broadcasted_iota(jnp.int32, (8, LANES), 1)
        out = jnp.zeros((8, LANES), F32)
        for k in range(topk):
            pick = jnp.sum(jnp.where(chosen & (cum == k + 1.0), jf, 0.0), axis=-1, keepdims=True)
            out = jnp.where(out_lane == k, pick, out)
        idx_ref[0] = out


def _decode_attn_body(tbl_ref, idx_ref, *refs, n_blk, qblk, per_page):
    pages = refs[:n_blk]
    (win_ref, q_ref, knew_ref, vnew_ref, wknew_ref, wvnew_ref, bsel_ref, bwin_ref, ocmp_ref, gate_ref,
     o_ref) = refs[n_blk:]
    b, g = pl.program_id(0), pl.program_id(1)
    qb = q_ref[0, 0].astype(BF16)
    qf = qb.astype(F32)
    k_rows = pl.ds(pl.multiple_of(g * HEAD_DIM, HEAD_DIM), HEAD_DIM)
    v_rows = pl.ds(pl.multiple_of(KV_DIM + g * HEAD_DIM, HEAD_DIM), HEAD_DIM)

    def attend(parts, k_new, v_new, bias_new):
        s_new = jnp.sum(qf * k_new.astype(BF16).astype(F32), axis=-1, keepdims=True) * SCALE + bias_new
        m = s_new
        for sc, ok, _ in parts:
            m = jnp.maximum(m, jnp.max(jnp.where(ok, sc, NEG), axis=-1, keepdims=True))
        e_new = jnp.exp(s_new - m)
        es = [jnp.where(ok, jnp.exp(sc - m), 0.0) for sc, ok, _ in parts]
        den = e_new
        for e in es:
            den = den + jnp.sum(e, axis=-1, keepdims=True)
        den = jnp.maximum(den, 1e-30)
        acc = (e_new / den).astype(BF16).astype(F32) * v_new.astype(BF16).astype(F32)
        for e, (_, _, v_t) in zip(es, parts):
            acc = acc + _nt_dot((e / den).astype(BF16), v_t)
        return acc

    sel_parts = []
    for k in range(n_blk):
        bidx = idx_ref[b, g, k]
        k_t = pages[k][0, k_rows, :].astype(BF16)
        v_t = pages[k][0, v_rows, :].astype(BF16)
        pg = jnp.minimum(bidx, qblk) // per_page
        sc = jnp.dot(qb, k_t, preferred_element_type=F32) * SCALE + bsel_ref[pg, 0]
        lane_blk = lax.broadcasted_iota(jnp.int32, sc.shape, 1) // SEL_BLOCK
        ok = (lane_blk == bidx % per_page) & (bidx < qblk)
        sel_parts.append((sc, ok, v_t))
    bias0 = bsel_ref[qblk // per_page, 0][:, 0:1]
    o_sel = attend(sel_parts, knew_ref[0, 0], vnew_ref[0, 0], bias0)

    sc = jnp.dot(qb, win_ref[0, k_rows, :].astype(BF16), preferred_element_type=F32) * SCALE + bwin_ref[0]
    o_win = attend([(sc, jnp.full(sc.shape, True), win_ref[0, v_rows, :].astype(BF16))],
                   wknew_ref[0, 0], wvnew_ref[0, 0], bias0)

    gate = jax.nn.sigmoid(gate_ref[0, 0])
    o_ref[0, 0] = gate[:, 0:1] * ocmp_ref[0, 0] + gate[:, 1:2] * o_sel + gate[:, 2:3] * o_win


def nsa_decode_mixer(x, g, past_len, cache_cmp, cache_sel, win_past, page_table,
                     w_in, q_g, k_g, cmp_pe, cmp_w1, cmp_w2, w_out, rel_table):
    bsz = x.shape[0]
    n_pool, page = cache_cmp.shape[0], cache_cmp.shape[1]
    n_pages = page_table.shape[1]
    assert past_len == n_pages * page and page % STRIP == 0 and n_pages % DEC_PAGES_PER_STEP == 0
    assert win_past.shape[1] == WINDOW and past_len >= WINDOW and page % SEL_BLOCK == 0
    x2 = x.reshape(bsz, D_MODEL)
    proj = norm_matmul(x2, g, w_in)
    c0 = N_HEADS * HEAD_DIM
    q = _rmsnorm(proj[:, :c0].reshape(bsz, N_HEADS, HEAD_DIM), q_g)

    def rows(k, gain):
        kcols = proj[:, c0 + 2 * k * KV_DIM:c0 + (2 * k + 1) * KV_DIM]
        vcols = proj[:, c0 + (2 * k + 1) * KV_DIM:c0 + (2 * k + 2) * KV_DIM]
        if gain is not None:
            kcols = _rmsnorm(kcols.reshape(bsz, KV_HEADS, HEAD_DIM), gain).reshape(bsz, KV_DIM)
        return kcols, vcols

    kc_new, vc_new = rows(0, None)
    ks_new, vs_new = rows(1, k_g[1])
    kw_new, vw_new = rows(2, k_g[2])
    gl = proj[:, c0 + 6 * KV_DIM:].reshape(bsz, KV_HEADS, HPG, 3)
    as_row = lambda kk, vv: jnp.stack([kk, vv], axis=1).reshape(bsz, 1, 2, KV_HEADS, HEAD_DIM)
    new_cmp, new_sel, new_win = as_row(kc_new, vc_new), as_row(ks_new, vs_new), as_row(kw_new, vw_new)

    eye = jnp.eye(KV_HEADS, dtype=F32)
    qbd = jnp.einsum('bghd,gk->bghkd', q.reshape(bsz, KV_HEADS, HPG, HEAD_DIM), eye).reshape(bsz, N_HEADS, KV_DIM)
    w1r = cmp_w1.reshape(2, 2, STRIP, HEAD_DIM, HEAD_DIM)
    bd = jnp.einsum('khlio,gj->khlgijo', w1r, eye).reshape(2 * 2 * STRIP, KV_DIM, KV_DIM).astype(BF16)
    w2bd = jnp.einsum('kio,gj->kgijo', cmp_w2, eye).reshape(2, KV_DIM, KV_DIM).astype(BF16)
    peterm = jnp.tile(cmp_pe.reshape(2 * 2 * STRIP, HEAD_DIM), (1, KV_HEADS))
    kg_row = jnp.tile(k_g[0], KV_HEADS).reshape(1, KV_DIM)

    n_strips = past_len // STRIP
    n_cmp = n_strips - 1
    qblk = past_len // SEL_BLOCK
    n_sel = qblk + 1
    topk = min(SEL_TOPK, n_sel)
    assert n_sel <= SEL_LANES and n_strips % 8 == 0
    tab = rel_table[_rel_bucket(jnp.arange(REL_MAX_DIST + 1))].astype(F32).T
    ends = np.arange(n_strips) * CMP_STRIDE + CMP_BLOCK - 1
    bias_cmp = tab[:, np.clip(past_len - ends, 0, REL_MAX_DIST)]
    cs = np.arange(n_strips)[:, None] * CMP_STRIDE
    ss = np.arange(SEL_LANES)[None, :] * SEL_BLOCK
    ovl = (cs < ss + SEL_BLOCK) & (cs + CMP_BLOCK > ss) & (np.arange(n_strips)[:, None] < n_cmp) \
        & (np.arange(SEL_LANES)[None, :] < n_sel)
    ovl = jnp.asarray(ovl, BF16)
    upper = jnp.asarray(np.triu(np.ones((SEL_LANES, SEL_LANES), np.float32)), BF16)

    strips_per_page = page // STRIP
    cmp_view = cache_cmp.reshape(n_pool, page, ROW_LANES).transpose(0, 2, 1)
    steps = n_pages // DEC_PAGES_PER_STEP
    const = lambda a: pl.BlockSpec(a.shape, lambda b, s, pt: (0,) * a.ndim)
    page_spec = lambda r: pl.BlockSpec((1, ROW_LANES, page),
                                       lambda b, s, pt: (pt[b, s * DEC_PAGES_PER_STEP + r], 0, 0))
    o_cmp, idx = pl.pallas_call(
        functools.partial(_decode_cmp_body, n_strips=n_strips, n_sel=n_sel, qblk=qblk, topk=topk),
        name="nsa_decode_cmp",
        grid_spec=pltpu.PrefetchScalarGridSpec(
            num_scalar_prefetch=1,
            grid=(bsz, steps),
            in_specs=[page_spec(r) for r in range(DEC_PAGES_PER_STEP)]
            + [const(bd), const(peterm), const(w2bd), const(kg_row),
               pl.BlockSpec((1, N_HEADS, KV_DIM), lambda b, s, pt: (b, 0, 0)),
               const(bias_cmp), const(ovl), const(upper)],
            out_specs=[pl.BlockSpec((1, N_HEADS, KV_DIM), lambda b, s, pt: (b, 0, 0)),
                       pl.BlockSpec((1, 8, LANES), lambda b, s, pt: (b, 0, 0))],
            scratch_shapes=[pltpu.VMEM((STRIP, n_strips, ROW_LANES), F32),
                            pltpu.VMEM((ROW_LANES // LANES, page, LANES), F32),
                            pltpu.VMEM((n_strips + 8, KV_DIM), F32)],
        ),
        out_shape=[jax.ShapeDtypeStruct((bsz, N_HEADS, KV_DIM), F32),
                   jax.ShapeDtypeStruct((bsz, 8, LANES), F32)],
        compiler_params=pltpu.CompilerParams(
            dimension_semantics=("parallel", "arbitrary"),
            vmem_limit_bytes=DECODE_VMEM_LIMIT_BYTES),
    )(page_table, *([cmp_view] * DEC_PAGES_PER_STEP), bd, peterm, w2bd, kg_row, qbd, bias_cmp, ovl, upper)

    blk_idx = idx[:, :KV_HEADS, :topk].astype(jnp.int32)
    per_page = page // SEL_BLOCK
    safe = jnp.minimum(blk_idx, qblk - 1)
    page_of = jnp.take_along_axis(page_table, (safe // per_page).reshape(bsz, -1), axis=1).reshape(safe.shape)
    sel_view = cache_sel.reshape(n_pool, page, ROW_LANES).transpose(0, 2, 1)
    win_view = win_past.reshape(bsz, WINDOW, ROW_LANES).transpose(0, 2, 1)

    n_pg = n_pages + 1
    dist_sel = past_len - (np.arange(n_pg)[:, None] * page + np.arange(page)[None, :])
    bsel = tab[:, np.clip(dist_sel, 0, REL_MAX_DIST)]
    bsel = bsel.reshape(KV_HEADS, HPG, n_pg, page).transpose(2, 0, 1, 3)
    bwin = tab[:, np.clip(past_len - (past_len - WINDOW + np.arange(WINDOW)), 0, REL_MAX_DIST)]
    bwin = bwin.reshape(KV_HEADS, HPG, WINDOW)

    per_group = lambda a: a.reshape(bsz, KV_HEADS, 1, HEAD_DIM)
    q4 = q.reshape(bsz, KV_HEADS, HPG, HEAD_DIM)
    ocmp4 = o_cmp.reshape(bsz, KV_HEADS, HPG, KV_HEADS, HEAD_DIM).sum(axis=3)
    page_spec2 = lambda k: pl.BlockSpec((1, ROW_LANES, page), lambda b, g, t, i: (t[b, g, k], 0, 0))
    per_bg = lambda *shape: pl.BlockSpec((1, 1) + shape, lambda b, g, t, i: (b, g) + (0,) * len(shape))
    o4 = pl.pallas_call(
        functools.partial(_decode_attn_body, n_blk=topk, qblk=qblk, per_page=per_page),
        name="nsa_decode_attn",
        grid_spec=pltpu.PrefetchScalarGridSpec(
            num_scalar_prefetch=2,
            grid=(bsz, KV_HEADS),
            in_specs=[page_spec2(k) for k in range(topk)]
            + [pl.BlockSpec((1, ROW_LANES, WINDOW), lambda b, g, t, i: (b, 0, 0)),
               per_bg(HPG, HEAD_DIM), per_bg(1, HEAD_DIM), per_bg(1, HEAD_DIM), per_bg(1, HEAD_DIM),
               per_bg(1, HEAD_DIM),
               pl.BlockSpec((n_pg, 1, HPG, page), lambda b, g, t, i: (0, g, 0, 0)),
               pl.BlockSpec((1, HPG, WINDOW), lambda b, g, t, i: (g, 0, 0)),
               per_bg(HPG, HEAD_DIM), per_bg(HPG, 3)],
            out_specs=per_bg(HPG, HEAD_DIM),
        ),
        out_shape=jax.ShapeDtypeStruct((bsz, KV_HEADS, HPG, HEAD_DIM), F32),
        compiler_params=pltpu.CompilerParams(
            dimension_semantics=("parallel", "arbitrary"),
            vmem_limit_bytes=VMEM_LIMIT_BYTES),
    )(page_of, blk_idx, *([sel_view] * topk), win_view, q4, per_group(ks_new), per_group(vs_new),
      per_group(kw_new), per_group(vw_new), bsel, bwin, ocmp4, gl)
    y = norm_matmul(o4.reshape(bsz, c0), None, w_out, normalize=False, res=x2)
    new_win_buf = jnp.concatenate([win_past[:, 1:], new_win], axis=1)
    return y.reshape(x.shape), new_cmp, new_sel, new_win_buf


def _gla_chunked(q, k, v, logf, s0):
    bsz, L = q.shape[0], q.shape[1]
    nc = -(-L // H_CHUNK)
    pad = nc * H_CHUNK - L

    def prep(t):
        t = jnp.pad(t, ((0, 0), (0, pad), (0, 0), (0, 0)))
        return t.reshape(bsz, nc, H_CHUNK, H_HEADS, t.shape[-1])

    q, k, v, logf = prep(q), prep(k), prep(v), prep(logf)
    acum = jnp.cumsum(logf, axis=2)
    alast = acum[:, :, -1:]
    qe = q * jnp.exp(acum)
    ke = k * jnp.exp(-acum)
    kd = k * jnp.exp(alast - acum)
    tri = jnp.tril(jnp.ones((H_CHUNK, H_CHUNK), bool))
    att = jnp.where(tri, jnp.einsum('bcthk,bcshk->bchts', qe, ke), 0.0)
    o_intra = jnp.einsum('bchts,bcshv->bcthv', att, v)
    upd = jnp.einsum('bcshk,bcshv->bchkv', kd, v)

    def step(s, inp):
        dec, up = inp
        return dec[..., None] * s + up, s

    s_last, s_prev = lax.scan(step, s0, (jnp.exp(alast[:, :, 0]).swapaxes(0, 1), upd.swapaxes(0, 1)))
    o_inter = jnp.einsum('bcthk,bchkv->bcthv', qe, s_prev.swapaxes(0, 1))
    o = (o_intra + o_inter).reshape(bsz, nc * H_CHUNK, H_HEADS, H_DV)[:, :L]
    return o, s_last


HGRN_TOKENS = 128


def _hgrn_body(q_ref, f_ref, v_ref, gate_ref, lb_ref, ng_ref, lcum_ref, lsum_ref, o_ref, s_ref):
    @pl.when(pl.program_id(1) == 0)
    def _():
        s_ref[...] = jnp.zeros_like(s_ref)

    n = HGRN_TOKENS
    qr = q_ref[...]
    q = qr * jax.nn.sigmoid(qr)
    lb = lb_ref[...]
    forget = lb + (1.0 - lb) * jax.nn.sigmoid(f_ref[...])
    logf = jnp.log(forget)
    k = 1.0 - forget
    parts = _split3(logf)
    acum = sum(jnp.dot(lcum_ref[...], p, preferred_element_type=F32) for p in parts)
    atot = sum(jnp.dot(lsum_ref[...], p, preferred_element_type=F32) for p in parts)
    qe = (q * jnp.exp(acum)).astype(BF16)
    ke = (k * jnp.exp(-acum)).astype(BF16)
    kd = (k * jnp.exp(atot - acum)).astype(BF16)
    vb = v_ref[...].astype(BF16)
    ti = lax.broadcasted_iota(jnp.int32, (n, n), 0)
    si = lax.broadcasted_iota(jnp.int32, (n, n), 1)
    intra = (ti // H_CHUNK == si // H_CHUNK) & (ti >= si)
    gate = gate_ref[...]
    gate = gate * jax.nn.sigmoid(gate)
    outs = []
    for h in range(H_HEADS):
        cols = slice(h * H_DK, (h + 1) * H_DK)
        att = jnp.where(intra, _nt_dot(qe[:, cols], ke[:, cols]), 0.0)
        o_h = jnp.dot(att.astype(BF16), vb[:, cols], preferred_element_type=F32)
        decay_t = jnp.exp(atot[:, cols]).T
        state = s_ref[0, cols, :]
        inter = []
        for j in range(n // H_CHUNK):
            rows = slice(j * H_CHUNK, (j + 1) * H_CHUNK)
            inter.append(jnp.dot(qe[rows, cols], state.astype(BF16), preferred_element_type=F32))
            upd = lax.dot_general(kd[rows, cols], vb[rows, cols], (((0,), (0,)), ((), ())),
                                  preferred_element_type=F32)
            state = decay_t[:, j * H_CHUNK:j * H_CHUNK + 1] * state + upd
        s_ref[0, cols, :] = state
        o_h = o_h + jnp.concatenate(inter, axis=0)
        o_h = o_h * lax.rsqrt(jnp.mean(o_h * o_h, axis=-1, keepdims=True) + EPS) * ng_ref[...]
        outs.append(o_h * gate[:, cols])
    o_ref[...] = jnp.concatenate(outs, axis=1).astype(BF16)


def hgrn2_prompt_mixer(x, g, lb, w_in, norm_g, w_out):
    bsz, L, _ = x.shape
    n = HGRN_TOKENS
    assert L % n == 0 and n % H_CHUNK == 0 and H_DK == LANES and H_DV == LANES
    t = bsz * L
    x2 = x.reshape(t, D_MODEL)
    proj = norm_matmul(x2, g, w_in)
    nb = L // n
    r = np.arange(n)
    same = (r[:, None] // H_CHUNK) == (r[None, :] // H_CHUNK)
    lcum = jnp.asarray(same & (r[None, :] <= r[:, None]), BF16)
    lsum = jnp.asarray(same, BF16)
    col = lambda kk: pl.BlockSpec((n, D_MODEL), lambda b, c: (b * nb + c, kk))
    const = lambda a: pl.BlockSpec(a.shape, lambda b, c: (0,) * a.ndim)
    lb2 = lb.reshape(1, D_MODEL)
    ng2 = norm_g.reshape(1, H_DV)
    o, s = pl.pallas_call(
        _hgrn_body,
        name="hgrn_blocks",
        grid=(bsz, nb),
        in_specs=[col(0), col(1), col(2), col(3), const(lb2), const(ng2), const(lcum), const(lsum)],
        out_specs=[pl.BlockSpec((n, D_MODEL), lambda b, c: (b * nb + c, 0)),
                   pl.BlockSpec((1, H_HEADS * H_DK, H_DV), lambda b, c: (b, 0, 0))],
        out_shape=[jax.ShapeDtypeStruct((t, D_MODEL), BF16),
                   jax.ShapeDtypeStruct((bsz, H_HEADS * H_DK, H_DV), F32)],
        compiler_params=pltpu.CompilerParams(
            dimension_semantics=("parallel", "arbitrary"),
            vmem_limit_bytes=VMEM_LIMIT_BYTES),
    )(proj, proj, proj, proj, lb2, ng2, lcum, lsum)
    y = norm_matmul(o, None, w_out, normalize=False, res=x2)
    return y.reshape(bsz, L, D_MODEL), s.reshape(bsz, H_HEADS, H_DK, H_DV)


def hgrn2_mixer(x, g, s0, lb, w_in, norm_g, w_out):
    bsz, L, _ = x.shape
    proj = norm_matmul(x.reshape(bsz * L, D_MODEL), g, w_in).reshape(bsz, L, -1)
    q, fr, iv, gate = jnp.split(proj, 4, axis=-1)
    q = jax.nn.silu(q).reshape(bsz, L, H_HEADS, H_DK)
    forget = lb + (1.0 - lb) * jax.nn.sigmoid(fr)
    logf = jnp.log(forget).reshape(bsz, L, H_HEADS, H_DK)
    k = (1.0 - forget).reshape(bsz, L, H_HEADS, H_DK)
    v = iv.reshape(bsz, L, H_HEADS, H_DV)
    o, s = _gla_chunked(q, k, v, logf, s0)
    o = _rmsnorm(o, norm_g) * jax.nn.silu(gate.reshape(bsz, L, H_HEADS, H_DV))
    y = norm_matmul(o.reshape(bsz * L, D_MODEL), None, w_out, normalize=False)
    return y.reshape(bsz, L, D_MODEL), s


def kernel(x_prompt, x_sample, state_ssm, state_conv, cache_kv_cmp, cache_kv_sel, cache_kv_win, state_hgrn, page_table, norm_g, rel_table, m_w_in, m_conv_w, m_conv_b, m_dt_bias, m_a_log, m_d, m_norm_g, m_w_out, n_w_in, n_q_g, n_k_g, n_cmp_pe, n_cmp_w1, n_cmp_w2, n_w_out, h_w_in, h_lb, h_norm_g, h_w_out, moe_w_rg, moe_b_rg, moe_w_re, moe_b_re, moe_w1, moe_w3, moe_w2):
    bp, lp = x_prompt.shape[0], x_prompt.shape[1]
    past_len = page_table.shape[1] * cache_kv_cmp.shape[2]
    dt = x_prompt.dtype
    m_w_in, m_w_out, n_w_in, n_w_out, h_w_in, h_w_out, moe_w1, moe_w3, moe_w2 = (
        w.astype(BF16) for w in (m_w_in, m_w_out, n_w_in, n_w_out, h_w_in, h_w_out, moe_w1, moe_w3, moe_w2))
    lbs = jax.nn.softmax(h_lb.astype(F32), axis=0)
    lbs = jnp.cumsum(lbs, axis=0) - lbs[0]
    xp, xs = x_prompt, x_sample
    ssm_p, conv_p, cmp_p, sel_p, win_p, hg_p = [], [], [], [], [], []
    ssm_s, conv_s, cmp_s, sel_s, win_s, hg_s = [], [], [], [], [], []
    for i in range(DEPTH):
        kind, j = i % N_MIXERS, i // N_MIXERS
        g0 = norm_g[i, 0]
        if kind == 0:
            w = (m_w_in[j], m_conv_w[j], m_conv_b[j], m_dt_bias[j], m_a_log[j], m_d[j], m_norm_g[j], m_w_out[j])
            xp, cbuf, hh = mamba_prompt_mixer(xp, g0, *w)
            ssm_p.append(hh)
            conv_p.append(cbuf)
            ys, cbuf, hh = mamba_mixer(xs, g0, state_conv[j], state_ssm[j], *w)
            xs = xs + ys
            ssm_s.append(hh)
            conv_s.append(cbuf)
        elif kind == 1:
            w = (n_w_in[j], n_q_g[j], n_k_g[j], n_cmp_pe[j], n_cmp_w1[j], n_cmp_w2[j], n_w_out[j], rel_table)
            xp, rc, rs, wb = nsa_prompt_mixer(xp, g0, *w)
            cmp_p.append(rc)
            sel_p.append(rs)
            win_p.append(wb)
            xs, rc, rs, wb = nsa_decode_mixer(xs, g0, past_len, cache_kv_cmp[j], cache_kv_sel[j],
                                              cache_kv_win[j], page_table, *w)
            cmp_s.append(rc)
            sel_s.append(rs)
            win_s.append(wb)
        else:
            w = (lbs[i], h_w_in[j], h_norm_g[j], h_w_out[j])
            xp, st = hgrn2_prompt_mixer(xp, g0, *w)
            hg_p.append(st)
            ys, st = hgrn2_mixer(xs, g0, state_hgrn[j], *w)
            xs = xs + ys
            hg_s.append(st)
        mw = (moe_w_rg[i], moe_b_rg[i], moe_w_re[i], moe_b_re[i], moe_w1[i], moe_w3[i], moe_w2[i])
        xp = hier_moe_residual(xp.reshape(-1, D_MODEL), norm_g[i, 1], *mw).reshape(xp.shape)
        xs = hier_moe_residual(xs.reshape(-1, D_MODEL), norm_g[i, 1], *mw).reshape(xs.shape)
    return (xp, xs,
            jnp.stack(ssm_p), jnp.stack(conv_p), jnp.stack(cmp_p), jnp.stack(sel_p), jnp.stack(win_p), jnp.stack(hg_p),
            jnp.stack(ssm_s), jnp.stack(conv_s), jnp.stack(cmp_s), jnp.stack(sel_s), jnp.stack(win_s), jnp.stack(hg_s))
```

```python
import functools
import math

import jax
import jax.numpy as jnp
import numpy as np
from jax import lax
from jax.experimental import pallas as pl
from jax.experimental.pallas import tpu as pltpu

F32 = jnp.float32
BF16 = jnp.bfloat16
EPS = 1e-6

D_MODEL = 1024
DEPTH = 4
N_MIXERS = 3

M_DINNER = 2 * D_MODEL
M_HEADDIM = 64
M_HEADS = M_DINNER // M_HEADDIM
M_GROUPS = 4
M_DSTATE = 128
M_CONV = 4
M_GN = M_GROUPS * M_DSTATE
M_CONV_DIM = M_DINNER + 2 * M_GN
M_CHUNK = 128

N_HEADS = 16
HEAD_DIM = D_MODEL // N_HEADS
KV_HEADS = 4
HPG = N_HEADS // KV_HEADS
KV_DIM = KV_HEADS * HEAD_DIM
CMP_BLOCK = 32
CMP_STRIDE = 16
SEL_BLOCK = 64
SEL_TOPK = 16
WINDOW = 512
SCALE = HEAD_DIM ** -0.5
REL_BUCKETS = 32
REL_MAX_DIST = 128

H_DK = 128
H_HEADS = D_MODEL // H_DK
H_DV = D_MODEL // H_HEADS
H_CHUNK = 32

MOE_GROUPS = 4
MOE_EPG = 4
MOE_EXPERTS = MOE_GROUPS * MOE_EPG
MOE_FF = 512

VMEM_LIMIT_BYTES = 48 * 1024 * 1024
DECODE_VMEM_LIMIT_BYTES = 56 * 1024 * 1024


def _rmsnorm(x, g):
    xf = x.astype(F32)
    y = xf * lax.rsqrt(jnp.mean(xf * xf, axis=-1, keepdims=True) + EPS)
    return (y * g.astype(F32)).astype(x.dtype)


LANES = 128
MXU_WIDTH = 256


def _norm_matmul_body(*refs, normalize, has_res):
    if has_res:
        x_ref, g_ref, w_ref, res_ref, o_ref, xb_ref = refs
    else:
        x_ref, g_ref, w_ref, o_ref, xb_ref = refs

    @pl.when(pl.program_id(1) == 0)
    def _():
        x = x_ref[...].astype(F32)
        if normalize:
            x = x * lax.rsqrt(jnp.mean(x * x, axis=-1, keepdims=True) + EPS) * g_ref[...]
        xb_ref[...] = x.astype(BF16)

    acc = jnp.dot(xb_ref[...], w_ref[...].astype(BF16), preferred_element_type=F32)
    if has_res:
        acc = acc + res_ref[...]
    o_ref[...] = acc


def _pick_tile(n, pref):
    t = min(n, pref)
    while n % t:
        t //= 2
    return t


def norm_matmul(x, g, w, *, normalize=True, res=None, tm=None, tn=512):
    t, k = x.shape
    n_true = w.shape[1]
    tm = _pick_tile(t, tm or (2048 if k <= 1024 else 1024))
    col_tile = MXU_WIDTH if n_true > MXU_WIDTH else LANES
    if n_true % col_tile:
        assert res is None
        w = jnp.pad(w, ((0, 0), (0, col_tile - n_true % col_tile)))
    n = w.shape[1]
    tn = _pick_tile(n, tn)
    if g is None:
        g = jnp.ones((k,), F32)
    in_specs = [
        pl.BlockSpec((tm, k), lambda i, j: (i, 0)),
        pl.BlockSpec((1, k), lambda i, j: (0, 0)),
        pl.BlockSpec((k, tn), lambda i, j: (0, j)),
    ]
    args = [x, g.reshape(1, k), w]
    if res is not None:
        in_specs.append(pl.BlockSpec((tm, tn), lambda i, j: (i, j)))
        args.append(res)
    out = pl.pallas_call(
        functools.partial(_norm_matmul_body, normalize=normalize, has_res=res is not None),
        name="norm_matmul",
        grid=(t // tm, n // tn),
        in_specs=in_specs,
        out_specs=pl.BlockSpec((tm, tn), lambda i, j: (i, j)),
        out_shape=jax.ShapeDtypeStruct((t, n), F32),
        scratch_shapes=[pltpu.VMEM((tm, k), BF16)],
        compiler_params=pltpu.CompilerParams(
            dimension_semantics=("parallel", "arbitrary"),
            vmem_limit_bytes=VMEM_LIMIT_BYTES),
    )(*args)
    return out if n == n_true else out[:, :n_true]


ROUTE_LANES = 128
MOE_TILE = 1024
MOE_ROWS = 160
MOE_EXPERTS_PER_STEP = 2
NEG = -1e30


def _router_body(x_ref, g_ref, w_ref, b_ref, u_ref, xn_ref, rank_ref, wt_ref):
    x = x_ref[...]
    tm = x.shape[0]
    xn = x * lax.rsqrt(jnp.mean(x * x, axis=-1, keepdims=True) + EPS) * g_ref[...]
    xb = xn.astype(BF16)
    xn_ref[...] = xb
    logits = jnp.dot(xb, w_ref[...].astype(BF16), preferred_element_type=F32) + b_ref[...]
    lane = lax.broadcasted_iota(jnp.int32, (tm, ROUTE_LANES), 1).astype(F32)

    def first_max(mask):
        v = jnp.max(jnp.where(mask, logits, NEG), axis=-1, keepdims=True)
        i = jnp.min(jnp.where(mask & (logits == v), lane, float(ROUTE_LANES)), axis=-1, keepdims=True)
        return v, i

    is_group = lane < MOE_GROUPS
    mg, g_idx = first_max(is_group)
    pg_top = 1.0 / jnp.sum(jnp.where(is_group, jnp.exp(logits - mg), 0.0), axis=-1, keepdims=True)
    lo = MOE_GROUPS + MOE_EPG * g_idx
    in_group = (lane >= lo) & (lane < lo + MOE_EPG)
    v1, i1 = first_max(in_group)
    v2, i2 = first_max(in_group & (lane != i1))
    e2 = jnp.exp(v2 - v1)
    w_a = pg_top / (1.0 + e2)
    w_b = pg_top * e2 / (1.0 + e2)
    info = jnp.where(lane == 0, i1 - MOE_GROUPS, jnp.where(lane == 1, i2 - MOE_GROUPS,
                     jnp.where(lane == 2, w_a, jnp.where(lane == 3, w_b, 0.0))))
    info_t = info.T
    e_a, e_b, w_at, w_bt = info_t[0:1], info_t[1:2], info_t[2:3], info_t[3:4]
    expert = lax.broadcasted_iota(jnp.int32, (MOE_EXPERTS, tm), 0).astype(F32)
    m_a = e_a == expert
    m_b = e_b == expert
    onehot = jnp.concatenate([jnp.where(m_a, 1.0, 0.0), jnp.where(m_b, 1.0, 0.0)], axis=0).astype(BF16)
    cum = jnp.dot(onehot, u_ref[...], preferred_element_type=F32)
    cum_a, cum_b = cum[:MOE_EXPERTS], cum[MOE_EXPERTS:]
    n_a = cum_a[:, tm - 1:tm]
    rank_ref[...] = jnp.where(m_a, cum_a - 1.0, jnp.where(m_b, n_a + cum_b - 1.0, -1.0))
    wt_ref[...] = jnp.where(m_a, w_at, jnp.where(m_b, w_bt, 0.0))


def moe_route(x, g, w_rg, b_rg, w_re, b_re, tm):
    t = x.shape[0]
    pad = ROUTE_LANES - MOE_GROUPS - MOE_EXPERTS
    w = jnp.pad(jnp.concatenate([w_rg, w_re], axis=1), ((0, 0), (0, pad)))
    b = jnp.pad(jnp.concatenate([b_rg, b_re]), (0, pad)).reshape(1, ROUTE_LANES)
    upper = jnp.asarray(np.triu(np.ones((tm, tm), np.float32)), BF16)
    const = lambda a: pl.BlockSpec(a.shape, lambda i: (0,) * a.ndim)
    g2 = g.reshape(1, D_MODEL)
    return pl.pallas_call(
        _router_body,
        name="moe_router",
        grid=(t // tm,),
        in_specs=[pl.BlockSpec((tm, D_MODEL), lambda i: (i, 0)), const(g2), const(w), const(b), const(upper)],
        out_specs=[pl.BlockSpec((tm, D_MODEL), lambda i: (i, 0)),
                   pl.BlockSpec((MOE_EXPERTS, tm), lambda i: (0, i)),
                   pl.BlockSpec((MOE_EXPERTS, tm), lambda i: (0, i))],
        out_shape=[jax.ShapeDtypeStruct((t, D_MODEL), BF16),
                   jax.ShapeDtypeStruct((MOE_EXPERTS, t), F32),
                   jax.ShapeDtypeStruct((MOE_EXPERTS, t), F32)],
        compiler_params=pltpu.CompilerParams(
            dimension_semantics=("parallel",), vmem_limit_bytes=VMEM_LIMIT_BYTES),
    )(x, g2, w, b, upper)


def _moe_expert_body(cnt_ref, xn_ref, rank_ref, wt_ref, w1_ref, w3_ref, w2_ref, res_ref, o_ref, *, rows):
    ti, step = pl.program_id(0), pl.program_id(1)

    @pl.when(step == 0)
    def _():
        o_ref[...] = res_ref[...]

    tm = xn_ref.shape[0]
    experts = [step * MOE_EXPERTS_PER_STEP + k for k in range(MOE_EXPERTS_PER_STEP)]
    ranks = [rank_ref[pl.ds(e, 1), :] for e in experts]
    wts = [wt_ref[pl.ds(e, 1), :] for e in experts]
    n_chunks = functools.reduce(jnp.maximum, [(cnt_ref[e, ti] + rows - 1) // rows for e in experts])

    def chunk(c, carry):
        r = (lax.broadcasted_iota(jnp.int32, (rows, tm), 0) + c * rows).astype(F32)
        total = None
        for k in range(MOE_EXPERTS_PER_STEP):
            sel = jnp.where(ranks[k] == r, 1.0, 0.0)
            selb = sel.astype(BF16)
            xs = jnp.dot(selb, xn_ref[...], preferred_element_type=F32).astype(BF16)
            a = jnp.dot(xs, w1_ref[k].astype(BF16), preferred_element_type=F32)
            b = jnp.dot(xs, w3_ref[k].astype(BF16), preferred_element_type=F32)
            h = (a * jax.nn.sigmoid(a) * b * jnp.sum(sel * wts[k], axis=1, keepdims=True)).astype(BF16)
            y = jnp.dot(h, w2_ref[k].astype(BF16), preferred_element_type=F32)
            y_hi = y.astype(BF16)
            y_lo = (y - y_hi.astype(F32)).astype(BF16)
            tn = (((0,), (0,)), ((), ()))
            back = (lax.dot_general(selb, y_hi, tn, preferred_element_type=F32)
                    + lax.dot_general(selb, y_lo, tn, preferred_element_type=F32))
            total = back if total is None else total + back
        o_ref[...] += total
        return carry

    lax.fori_loop(0, n_chunks, chunk, 0)


def moe_experts(xn, rank, wt, counts, w1, w3, w2, res, tm):
    t = xn.shape[0]
    rows = min(MOE_ROWS, tm)
    grid_spec = pltpu.PrefetchScalarGridSpec(
        num_scalar_prefetch=1,
        grid=(t // tm, MOE_EXPERTS // MOE_EXPERTS_PER_STEP),
        in_specs=[
            pl.BlockSpec((tm, D_MODEL), lambda i, e, c: (i, 0)),
            pl.BlockSpec((MOE_EXPERTS, tm), lambda i, e, c: (0, i)),
            pl.BlockSpec((MOE_EXPERTS, tm), lambda i, e, c: (0, i)),
            pl.BlockSpec((MOE_EXPERTS_PER_STEP, D_MODEL, MOE_FF), lambda i, e, c: (e, 0, 0)),
            pl.BlockSpec((MOE_EXPERTS_PER_STEP, D_MODEL, MOE_FF), lambda i, e, c: (e, 0, 0)),
            pl.BlockSpec((MOE_EXPERTS_PER_STEP, MOE_FF, D_MODEL), lambda i, e, c: (e, 0, 0)),
            pl.BlockSpec((tm, D_MODEL), lambda i, e, c: (i, 0)),
        ],
        out_specs=pl.BlockSpec((tm, D_MODEL), lambda i, e, c: (i, 0)),
    )
    return pl.pallas_call(
        functools.partial(_moe_expert_body, rows=rows),
        name="moe_experts",
        grid_spec=grid_spec,
        out_shape=jax.ShapeDtypeStruct((t, D_MODEL), F32),
        compiler_params=pltpu.CompilerParams(
            dimension_semantics=("parallel", "arbitrary"),
            vmem_limit_bytes=VMEM_LIMIT_BYTES),
    )(counts, xn, rank, wt, w1, w3, w2, res)


def hier_moe_residual(x, g, w_rg, b_rg, w_re, b_re, w1, w3, w2):
    t_true = x.shape[0]
    tm = MOE_TILE if t_true % MOE_TILE == 0 else ROUTE_LANES
    if t_true % tm:
        x = jnp.pad(x, ((0, tm - t_true % tm), (0, 0)))
    t = x.shape[0]
    xn, rank, wt = moe_route(x, g, w_rg, b_rg, w_re, b_re, tm)
    counts = jnp.sum((rank >= 0).reshape(MOE_EXPERTS, t // tm, tm), axis=-1, dtype=jnp.int32)
    return moe_experts(xn, rank, wt, counts, w1, w3, w2, x, tm)[:t_true]


def _causal_dwconv(u, buf, w, b):
    L = u.shape[1]
    ext = jnp.concatenate([buf.astype(u.dtype), u], axis=1)
    out = b + sum(ext[:, k:k + L] * w[k] for k in range(M_CONV))
    return out, ext[:, L:]


def _ssd_scan(x, dt, a, bm, cm, h0):
    bsz, L = x.shape[0], x.shape[1]
    q = M_CHUNK if L % M_CHUNK == 0 else L
    nc = L // q
    hpg = M_HEADS // M_GROUPS
    xdt = (x * dt[..., None]).reshape(bsz, nc, q, M_GROUPS, hpg, M_HEADDIM)
    acum = jnp.cumsum((dt * a).reshape(bsz, nc, q, M_GROUPS, hpg), axis=2)
    bc = bm.reshape(bsz, nc, q, M_GROUPS, M_DSTATE)
    cc = cm.reshape(bsz, nc, q, M_GROUPS, M_DSTATE)
    tri = jnp.tril(jnp.ones((q, q), bool))[:, :, None, None]
    seg = acum[:, :, :, None] - acum[:, :, None]
    decay = jnp.exp(jnp.where(tri, seg, -jnp.inf))
    cb = jnp.einsum('bclgn,bcsgn->bclsg', cc, bc)
    y_diag = jnp.einsum('bclsg,bclsgh,bcsghp->bclghp', cb, decay, xdt)
    decay_end = jnp.exp(acum[:, :, -1:] - acum)
    states = jnp.einsum('bcsgn,bcsgh,bcsghp->bcghpn', bc, decay_end, xdt)
    chunk_decay = jnp.exp(acum[:, :, -1])

    def step(h, inp):
        cd, st = inp
        return cd[..., None, None] * h + st, h

    h_last, h_prev = lax.scan(step, h0.reshape(bsz, M_GROUPS, hpg, M_HEADDIM, M_DSTATE),
                              (chunk_decay.swapaxes(0, 1), states.swapaxes(0, 1)))
    y_off = jnp.einsum('bclgn,bclgh,bcghpn->bclghp', cc, jnp.exp(acum), h_prev.swapaxes(0, 1))
    y = (y_diag + y_off).reshape(bsz, L, M_HEADS, M_HEADDIM)
    return y, h_last.reshape(bsz, M_HEADS, M_HEADDIM, M_DSTATE)


def mamba_mixer(x, g, conv_buf, h0, w_in, conv_w, conv_b, dt_bias, a_log, d_skip, norm_g, w_out):
    bsz, L, _ = x.shape
    proj = norm_matmul(x.reshape(bsz * L, D_MODEL), g, w_in).reshape(bsz, L, -1)
    z = proj[..., :M_DINNER]
    xbc = proj[..., M_DINNER:M_DINNER + M_CONV_DIM]
    dt_raw = proj[..., M_DINNER + M_CONV_DIM:]
    xbc, new_buf = _causal_dwconv(xbc, conv_buf, conv_w, conv_b)
    xbc = jax.nn.silu(xbc)
    xs = xbc[..., :M_DINNER].reshape(bsz, L, M_HEADS, M_HEADDIM)
    bm = xbc[..., M_DINNER:M_DINNER + M_GN].reshape(bsz, L, M_GROUPS, M_DSTATE)
    cm = xbc[..., M_DINNER + M_GN:].reshape(bsz, L, M_GROUPS, M_DSTATE)
    dt = jax.nn.softplus(dt_raw + dt_bias)
    a = -jnp.exp(a_log)
    y, h = _ssd_scan(xs, dt, a, bm, cm, h0)
    y = y + xs * d_skip[:, None]
    y = y.reshape(bsz, L, M_DINNER)
    y = _rmsnorm(y * jax.nn.silu(z), norm_g)
    out = norm_matmul(y.reshape(bsz * L, M_DINNER), None, w_out, normalize=False)
    return out.reshape(bsz, L, D_MODEL), new_buf, h


SSD_HPG = M_HEADS // M_GROUPS
SSD_GROUP_ROWS = SSD_HPG * M_HEADDIM
CONV_PAD = 8


def _transpose_cols(x):
    return jnp.concatenate([x[:, j * LANES:(j + 1) * LANES].T for j in range(x.shape[1] // LANES)], axis=0)


def _transpose_rows(x):
    return jnp.concatenate([x[j * LANES:(j + 1) * LANES, :].T for j in range(x.shape[0] // LANES)], axis=1)


def _ssd_body(z_ref, xbc_ref, dtr_ref, cw_ref, cb_ref, dtb_ref, a_ref, dcol_ref, ng_ref, ltri_ref,
              y_ref, conv_ref, h_ref, xbuf, *, q):
    c = pl.program_id(1)

    @pl.when(c == 0)
    def _():
        h_ref[...] = jnp.zeros_like(h_ref)
        xbuf[0:CONV_PAD, :] = jnp.zeros((CONV_PAD, M_CONV_DIM), F32)

    xbuf[CONV_PAD:CONV_PAD + q, :] = xbc_ref[...]
    conv = cb_ref[...]
    for k in range(M_CONV):
        start = CONV_PAD - (M_CONV - 1) + k
        conv = conv + xbuf[start:start + q, :] * cw_ref[k:k + 1, :]
    tail = xbuf[CONV_PAD + q - (M_CONV - 1):CONV_PAD + q, :]
    xbuf[CONV_PAD - (M_CONV - 1):CONV_PAD, :] = tail
    conv_ref[0] = tail
    xc = conv * jax.nn.sigmoid(conv)
    xs = xc[:, :M_DINNER]
    xs_t = _transpose_cols(xs)

    pre = dtr_ref[...] + dtb_ref[...]
    dt = jnp.maximum(pre, 0.0) + jnp.log1p(jnp.exp(-jnp.abs(pre)))
    da = dt * a_ref[...]
    ltri = ltri_ref[...]
    acum = sum(jnp.dot(ltri, part, preferred_element_type=F32) for part in _split3(da))
    dt_t = dt.T
    acum_t = acum.T
    li = lax.broadcasted_iota(jnp.int32, (q, q), 0)
    si = lax.broadcasted_iota(jnp.int32, (q, q), 1)
    causal = li >= si

    y_t = []
    for g in range(M_GROUPS):
        bm = xc[:, M_DINNER + g * M_DSTATE:M_DINNER + (g + 1) * M_DSTATE].astype(BF16)
        cm = xc[:, M_DINNER + M_GN + g * M_DSTATE:M_DINNER + M_GN + (g + 1) * M_DSTATE].astype(BF16)
        cb = _nt_dot(cm, bm)
        r0 = g * SSD_GROUP_ROWS
        h_prev = h_ref[0, r0:r0 + SSD_GROUP_ROWS, :]
        y_off = _nt_dot(h_prev.astype(BF16), cm)
        x_dec, scale = [], []
        for hh in range(g * SSD_HPG, (g + 1) * SSD_HPG):
            a_row = acum_t[hh:hh + 1, :]
            a_col = acum[:, hh:hh + 1]
            decay = jnp.where(causal, jnp.exp(a_col - a_row), 0.0)
            m = (cb * decay).astype(BF16)
            rows = slice(hh * M_HEADDIM, (hh + 1) * M_HEADDIM)
            xs_h = xs_t[rows]
            xdt = xs_h * dt_t[hh:hh + 1, :]
            y_h = _nt_dot(xdt.astype(BF16), m)
            y_h = y_h + y_off[rows.start - r0:rows.stop - r0] * jnp.exp(a_row) + xs_h * dcol_ref[hh:hh + 1, :]
            y_t.append(y_h)
            a_last = a_row[:, q - 1:q]
            x_dec.append(xdt * jnp.exp(a_last - a_row))
            scale.append(jnp.broadcast_to(jnp.exp(a_last), (M_HEADDIM, 1)))
        upd = jnp.dot(jnp.concatenate(x_dec, axis=0).astype(BF16), bm, preferred_element_type=F32)
        h_ref[0, r0:r0 + SSD_GROUP_ROWS, :] = jnp.concatenate(scale, axis=0) * h_prev + upd

    y = _transpose_rows(jnp.concatenate(y_t, axis=0))
    zz = z_ref[...]
    yg = y * (zz * jax.nn.sigmoid(zz))
    yg = yg * lax.rsqrt(jnp.mean(yg * yg, axis=-1, keepdims=True) + EPS) * ng_ref[...]
    y_ref[...] = yg.astype(BF16)


def ssd_prompt(z, xbc, dtr, conv_w, conv_b, dt_bias, a_log, d_skip, norm_g, bsz, L):
    q = M_CHUNK
    assert L % q == 0 and q == LANES
    nc = L // q
    padl = lambda v: jnp.pad(v, (0, LANES - v.shape[0]))
    dtb = padl(dt_bias).reshape(1, LANES)
    a_row = padl(-jnp.exp(a_log)).reshape(1, LANES)
    dcol = jnp.broadcast_to(padl(d_skip).reshape(LANES, 1), (LANES, LANES))
    ltri = jnp.asarray(np.tril(np.ones((q, q), np.float32)), BF16)
    const = lambda a: pl.BlockSpec(a.shape, lambda b, c: (0,) * a.ndim)
    tok = lambda w: pl.BlockSpec((q, w), lambda b, c: (b * nc + c, 0))
    cb2 = conv_b.reshape(1, M_CONV_DIM)
    ng2 = norm_g.reshape(1, M_DINNER)
    y, conv_tail, h = pl.pallas_call(
        functools.partial(_ssd_body, q=q),
        name="ssd_chunks",
        grid=(bsz, nc),
        in_specs=[tok(M_DINNER), tok(M_CONV_DIM), tok(LANES), const(conv_w), const(cb2), const(dtb),
                  const(a_row), const(dcol), const(ng2), const(ltri)],
        out_specs=[tok(M_DINNER),
                   pl.BlockSpec((1, M_CONV - 1, M_CONV_DIM), lambda b, c: (b, 0, 0)),
                   pl.BlockSpec((1, M_DINNER, M_DSTATE), lambda b, c: (b, 0, 0))],
        out_shape=[jax.ShapeDtypeStruct((bsz * L, M_DINNER), BF16),
                   jax.ShapeDtypeStruct((bsz, M_CONV - 1, M_CONV_DIM), F32),
                   jax.ShapeDtypeStruct((bsz, M_DINNER, M_DSTATE), F32)],
        scratch_shapes=[pltpu.VMEM((CONV_PAD + q, M_CONV_DIM), F32)],
        compiler_params=pltpu.CompilerParams(
            dimension_semantics=("parallel", "arbitrary"),
            vmem_limit_bytes=VMEM_LIMIT_BYTES),
    )(z, xbc, dtr, conv_w, cb2, dtb, a_row, dcol, ng2, ltri)
    return y, conv_tail, h.reshape(bsz, M_HEADS, M_HEADDIM, M_DSTATE)


def mamba_prompt_mixer(x, g, w_in, conv_w, conv_b, dt_bias, a_log, d_skip, norm_g, w_out):
    bsz, L, _ = x.shape
    x2 = x.reshape(bsz * L, D_MODEL)
    z = norm_matmul(x2, g, w_in[:, :M_DINNER])
    xbc = norm_matmul(x2, g, w_in[:, M_DINNER:M_DINNER + M_CONV_DIM])
    dtr = norm_matmul(x2, g, jnp.pad(w_in[:, M_DINNER + M_CONV_DIM:], ((0, 0), (0, LANES - M_HEADS))))
    y, conv_tail, h = ssd_prompt(z, xbc, dtr, conv_w, conv_b, dt_bias, a_log, d_skip, norm_g, bsz, L)
    out = norm_matmul(y, None, w_out, normalize=False, res=x2)
    return out.reshape(bsz, L, D_MODEL), conv_tail, h


def _rel_bucket(dist):
    exact = REL_BUCKETS // 2
    d = jnp.maximum(dist, 0)
    ratio = jnp.log(jnp.maximum(d, 1).astype(F32) / exact) / math.log(REL_MAX_DIST / exact)
    large = jnp.minimum(exact + (ratio * (REL_BUCKETS - exact)).astype(jnp.int32), REL_BUCKETS - 1)
    return jnp.where(d < exact, d, large)


ATT_TQ = 128
ATT_TK = 128
ATT_UNROLL_LOG2 = 2


def _nt_dot(a, b):
    return lax.dot_general(a, b, (((1,), (1,)), ((), ())), preferred_element_type=F32)


def _compress_body(rk_ref, rv_ref, pe_ref, w1_ref, w2_ref, kg_ref, kc_ref, vc_ref, *, nb):
    half = (CMP_BLOCK // 2) * HEAD_DIM
    for kv, (r_ref, o_ref) in enumerate(((rk_ref, kc_ref), (rv_ref, vc_ref))):
        lo = (r_ref[0, 0, 0:nb, :] + pe_ref[kv, 0:1, :]).astype(BF16)
        hi = (r_ref[0, 0, 1:nb + 1, :] + pe_ref[kv, 1:2, :]).astype(BF16)
        h = (jnp.dot(lo, w1_ref[kv, :half, :].astype(BF16), preferred_element_type=F32)
             + jnp.dot(hi, w1_ref[kv, half:, :].astype(BF16), preferred_element_type=F32))
        h = h * jax.nn.sigmoid(h)
        o = jnp.dot(h.astype(BF16), w2_ref[kv].astype(BF16), preferred_element_type=F32)
        if kv == 0:
            o = o * lax.rsqrt(jnp.mean(o * o, axis=-1, keepdims=True) + EPS) * kg_ref[...]
        o_ref[0, 0] = o


def compress_rows(rk, rv, pe, w1, w2, kg):
    bsz, g, nbp, width = rk.shape
    nb = nbp - 8
    strip = pl.BlockSpec((1, 1, nbp, width), lambda b, j: (b, j, 0, 0))
    out = pl.BlockSpec((1, 1, nb, HEAD_DIM), lambda b, j: (b, j, 0, 0))
    full = lambda a: pl.BlockSpec(a.shape, lambda b, j: (0,) * a.ndim)
    pe2 = pe.reshape(2, 2, width)
    kg2 = kg.reshape(1, HEAD_DIM)
    return pl.pallas_call(
        functools.partial(_compress_body, nb=nb),
        name="nsa_compress",
        grid=(bsz, g),
        in_specs=[strip, strip, full(pe2), full(w1), full(w2), full(kg2)],
        out_specs=[out, out],
        out_shape=[jax.ShapeDtypeStruct((bsz, g, nb, HEAD_DIM), F32)] * 2,
        compiler_params=pltpu.CompilerParams(
            dimension_semantics=("parallel", "parallel"),
            vmem_limit_bytes=VMEM_LIMIT_BYTES),
    )(rk, rv, pe2, w1, w2, kg2)


def _split3(x):
    a = x.astype(BF16)
    r = x - a.astype(F32)
    b = r.astype(BF16)
    c = (r - b.astype(F32)).astype(BF16)
    return a, b, c


def _bias_tables(rel_table, n_qtiles):
    tab = rel_table[_rel_bucket(jnp.arange(REL_MAX_DIST + 1))].astype(F32)
    tab = tab.T.reshape(KV_HEADS, HPG, REL_MAX_DIST + 1)

    def skew(v, rows, width, step):
        out = jnp.broadcast_to(v[..., None, :], v.shape[:-1] + (rows, width + step))
        out = out.reshape(v.shape[:-1] + (rows * (width + step),))[..., :rows * width]
        return out.reshape(v.shape[:-1] + (rows, width))

    width = 2 * ATT_TK
    w = np.arange(width + 1)
    s_minus_t = np.where(w <= ATT_TK, w, w - (width + 1))
    d_idx = np.stack([np.clip(delta * ATT_TK - s_minus_t, 0, REL_MAX_DIST) for delta in range(3)])
    btile = skew(tab[:, :, d_idx], ATT_TQ, width, 1)[..., :ATT_TK]
    btile = btile.transpose(0, 2, 1, 3, 4).reshape(KV_HEADS, 3, HPG * ATT_TQ, ATT_TK)

    n_q = n_qtiles * ATT_TQ
    u = np.arange(n_q + CMP_STRIDE)
    v_cmp = tab[:, :, np.clip(u - (CMP_BLOCK - 1), 0, REL_MAX_DIST)]
    bcmp = skew(v_cmp, ATT_TK, n_q, CMP_STRIDE)
    bcmp = bcmp.reshape(KV_HEADS, HPG, ATT_TK, n_qtiles, ATT_TQ).transpose(3, 0, 1, 4, 2)
    return btile, bcmp.reshape(n_qtiles, KV_HEADS, HPG * ATT_TQ, ATT_TK)


def _nsa_attn_t_body(qt_ref, kc_ref, vct_ref, ks_ref, vst_ref, kw_ref, vwt_ref, gate_ref, bcmp_ref, btile_ref,
                     ovl_ref, o_ref, s_scr, *, n_cmp, n_sel, topk):
    i = pl.program_id(2)
    q0 = i * ATT_TQ
    cols = HPG * ATT_TQ
    qs = qt_ref[0, 0, 0]
    key = lax.broadcasted_iota(jnp.int32, (ATT_TK, cols), 0)
    qpos = q0 + (lax.broadcasted_iota(jnp.int32, (ATT_TK, cols), 1) & (ATT_TQ - 1))

    def fold(x, op):
        return op(x.reshape(ATT_TK // 8, 8, cols), axis=0)

    s = jnp.dot(kc_ref[0, 0].astype(BF16), qs, preferred_element_type=F32) + bcmp_ref[0, 0]
    valid = (qpos >= key * CMP_STRIDE + (CMP_BLOCK - 1)) & (key < n_cmp)
    s = jnp.where(valid, s, NEG)
    e = jnp.where(valid, jnp.exp(s - jnp.max(s, axis=0, keepdims=True)), 0.0)
    p = e / jnp.maximum(jnp.sum(e, axis=0, keepdims=True), 1e-30)
    o_cmp = jnp.dot(vct_ref[0, 0].astype(BF16), p.astype(BF16), preferred_element_type=F32)

    psum = p[:, 0:ATT_TQ]
    for h in range(1, HPG):
        psum = psum + p[:, h * ATT_TQ:(h + 1) * ATT_TQ]
    imp_t = jnp.dot(ovl_ref[...], psum.astype(BF16), preferred_element_type=F32)
    nblk = imp_t.shape[0]
    nrank = -(-n_sel // 8) * 8
    j = lax.broadcasted_iota(jnp.int32, (nrank, ATT_TQ), 0)
    qblk = (q0 + lax.broadcasted_iota(jnp.int32, (nrank, ATT_TQ), 1)) // SEL_BLOCK
    forced = (j == qblk) | (j == 0)
    score = jnp.where(forced, 1e9, jnp.where(j <= qblk, imp_t[:nrank], NEG))
    score = jnp.where(j < n_sel, score, -3e38)
    rank = jnp.zeros((nrank, ATT_TQ), F32)
    for jp in range(n_sel):
        other = score[jp:jp + 1, :]
        beats = (other > score) | ((other == score) & (jp < j))
        rank = rank + jnp.where(beats, 1.0, 0.0)
    sel_t = jnp.where((rank < topk) & (j < n_sel), 1.0, 0.0)
    if nrank < nblk:
        sel_t = jnp.concatenate([sel_t, jnp.zeros((nblk - nrank, ATT_TQ), F32)], axis=0)
    sel_t = sel_t.astype(BF16)

    kk = lax.broadcasted_iota(jnp.int32, (ATT_TK, nblk), 0) // SEL_BLOCK
    jj = lax.broadcasted_iota(jnp.int32, (ATT_TK, nblk), 1)
    blocks_per_step = ATT_TK // SEL_BLOCK

    def attend(k_ref, vt_ref, lo, hi, penalty_fn, whole_groups=False):
        def tile(kc):
            return pl.ds(pl.multiple_of(kc * ATT_TK, ATT_TK), ATT_TK)

        def sweep(fn, init):
            extra = (1 << ATT_UNROLL_LOG2) - 1 if whole_groups else 0
            n_groups = lax.shift_right_logical(hi - lo + extra, ATT_UNROLL_LOG2)

            def group(gi, carry):
                kc = lo + gi * (1 << ATT_UNROLL_LOG2)
                for u in range(1 << ATT_UNROLL_LOG2):
                    carry = fn(kc + u, carry)
                return carry

            carry = lax.fori_loop(0, n_groups, group, init)
            return lax.fori_loop(lo + n_groups * (1 << ATT_UNROLL_LOG2), hi, fn, carry)

        def scores(kc, m_run):
            kblk = k_ref[0, 0, tile(kc), :].astype(BF16)
            sc = (jnp.dot(kblk, qs, preferred_element_type=F32) + btile_ref[0, jnp.clip(i - kc, 0, 2)]
                  + penalty_fn(kc))
            s_scr[tile(kc), :] = sc
            return jnp.maximum(m_run, fold(sc, jnp.max))

        m = jnp.max(sweep(scores, jnp.full((8, cols), NEG, F32)), axis=0, keepdims=True)

        def exps(kc, l_run):
            ex = jnp.exp(s_scr[tile(kc), :] - m)
            s_scr[tile(kc), :] = ex
            return l_run + fold(ex, jnp.sum)

        den = jnp.sum(sweep(exps, jnp.zeros((8, cols), F32)), axis=0, keepdims=True)
        inv = 1.0 / jnp.maximum(den, 1e-30)

        def weighted(kc, acc):
            pr = (s_scr[tile(kc), :] * inv).astype(BF16)
            return acc + jnp.dot(vt_ref[0, 0, :, tile(kc)].astype(BF16), pr, preferred_element_type=F32)

        return sweep(weighted, jnp.zeros((HEAD_DIM, cols), F32))

    def sel_penalty(kc):
        expand = jnp.where(jj == kc * blocks_per_step + kk, 1.0, 0.0).astype(BF16)
        chosen = jnp.dot(expand, sel_t, preferred_element_type=F32)
        pen = jnp.concatenate([(chosen - 1.0) * (-NEG)] * HPG, axis=1)
        return jnp.where(qpos >= kc * ATT_TK + key, pen, NEG)

    def win_penalty(kc):
        dist = qpos - (kc * ATT_TK + key)
        return jnp.where((dist >= 0) & (dist <= WINDOW), 0.0, NEG)

    o_sel = attend(ks_ref, vst_ref, 0, i + 1, sel_penalty, whole_groups=True)
    o_win = attend(kw_ref, vwt_ref, jnp.maximum(i - WINDOW // ATT_TK, 0), i + 1, win_penalty)
    gate = jax.nn.sigmoid(gate_ref[0, 0, 0])
    o_ref[0, 0, 0] = (gate[0:1] * o_cmp + gate[1:2] * o_sel + gate[2:3] * o_win).astype(BF16)


def nsa_prompt_attention_t(q, kcmp, vcmp, ks, vs_t, kw, vw_t, gl, rel_table):
    bsz, L = q.shape[0], q.shape[1]
    assert L % ATT_TQ == 0 and kcmp.shape[2] == ATT_TK
    assert math.frexp(SCALE)[0] == 0.5, "the kernel folds SCALE into q, exact only for powers of two"
    nq = L // ATT_TQ
    n_cmp = (L - CMP_BLOCK) // CMP_STRIDE + 1
    n_sel = L // SEL_BLOCK
    assert n_sel <= ATT_TK and nq % (1 << ATT_UNROLL_LOG2) == 0
    cols = HPG * ATT_TQ
    btile, bcmp = _bias_tables(rel_table, nq)
    btile = btile.transpose(0, 1, 3, 2)
    bcmp = bcmp.transpose(0, 1, 3, 2)
    c = np.arange(ATT_TK)[None, :] * CMP_STRIDE
    sb = np.arange(ATT_TK)[:, None] * SEL_BLOCK
    ovl = ((c < sb + SEL_BLOCK) & (c + CMP_BLOCK > sb) & (np.arange(ATT_TK)[None, :] < n_cmp)
           & (np.arange(ATT_TK)[:, None] < n_sel))
    ovl = jnp.asarray(ovl, BF16)
    q_t = (q * SCALE).astype(BF16).reshape(bsz, nq, ATT_TQ, KV_HEADS, HPG, HEAD_DIM).transpose(0, 3, 1, 5, 4, 2)
    q_t = q_t.reshape(bsz, KV_HEADS, nq, HEAD_DIM, cols)
    kcmp, ks, kw, vs_t, vw_t = (a.astype(BF16) for a in (kcmp, ks, kw, vs_t, vw_t))
    gate_t = gl.reshape(bsz, nq, ATT_TQ, KV_HEADS, HPG, 3).transpose(0, 3, 1, 5, 4, 2)
    gate_t = gate_t.reshape(bsz, KV_HEADS, nq, 3, cols)
    vc_t = vcmp.transpose(0, 1, 3, 2).astype(BF16)
    rows_spec = lambda n: pl.BlockSpec((1, 1, n, HEAD_DIM), lambda b, g, i: (b, g, 0, 0))
    cols_spec = lambda n: pl.BlockSpec((1, 1, HEAD_DIM, n), lambda b, g, i: (b, g, 0, 0))
    tile_spec = lambda r: pl.BlockSpec((1, 1, 1, r, cols), lambda b, g, i: (b, g, i, 0, 0))
    o_t = pl.pallas_call(
        functools.partial(_nsa_attn_t_body, n_cmp=n_cmp, n_sel=n_sel, topk=min(SEL_TOPK, n_sel)),
        name="nsa_attention",
        grid=(bsz, KV_HEADS, nq),
        in_specs=[
            tile_spec(HEAD_DIM),
            rows_spec(ATT_TK), cols_spec(ATT_TK), rows_spec(L), cols_spec(L), rows_spec(L), cols_spec(L),
            tile_spec(3),
            pl.BlockSpec((1, 1, ATT_TK, cols), lambda b, g, i: (i, g, 0, 0)),
            pl.BlockSpec((1, 3, ATT_TK, cols), lambda b, g, i: (g, 0, 0, 0)),
            pl.BlockSpec((ATT_TK, ATT_TK), lambda b, g, i: (0, 0)),
        ],
        out_specs=tile_spec(HEAD_DIM),
        out_shape=jax.ShapeDtypeStruct((bsz, KV_HEADS, nq, HEAD_DIM, cols), BF16),
        scratch_shapes=[pltpu.VMEM((L, cols), F32)],
        compiler_params=pltpu.CompilerParams(
            dimension_semantics=("parallel", "parallel", "arbitrary"),
            vmem_limit_bytes=VMEM_LIMIT_BYTES),
    )(q_t, kcmp, vc_t, ks, vs_t, kw, vw_t, gate_t, bcmp, btile, ovl)
    o = o_t.reshape(bsz, KV_HEADS, nq, HEAD_DIM, HPG, ATT_TQ).transpose(0, 2, 5, 1, 4, 3)
    return o.reshape(bsz, L, N_HEADS * HEAD_DIM)


def nsa_prompt_mixer(x, g, w_in, q_g, k_g, cmp_pe, cmp_w1, cmp_w2, w_out, rel_table):
    bsz, L, _ = x.shape
    t = bsz * L
    proj = norm_matmul(x.reshape(t, D_MODEL), g, w_in)
    c0 = N_HEADS * HEAD_DIM
    q = _rmsnorm(proj[:, :c0].reshape(bsz, L, N_HEADS, HEAD_DIM), q_g)

    def kvpair(k, normed_g):
        kcols = proj[:, c0 + 2 * k * KV_DIM:c0 + (2 * k + 1) * KV_DIM]
        vcols = proj[:, c0 + (2 * k + 1) * KV_DIM:c0 + (2 * k + 2) * KV_DIM]
        if normed_g is not None:
            kcols = _rmsnorm(kcols.reshape(t, KV_HEADS, HEAD_DIM), normed_g).reshape(t, KV_DIM)
        rows = jnp.stack([kcols, vcols], axis=1).reshape(bsz, L, 2, KV_HEADS, HEAD_DIM)
        k4 = kcols.reshape(bsz, L, KV_HEADS, HEAD_DIM).transpose(0, 2, 1, 3)
        v4 = vcols.reshape(bsz, L, KV_HEADS, HEAD_DIM)
        return rows, k4, v4

    new_cmp, kc_rows, vc_rows = kvpair(0, None)
    new_sel, ks, vs = kvpair(1, k_g[1])
    new_win, kw, vw = kvpair(2, k_g[2])
    gl = proj[:, c0 + 6 * KV_DIM:].reshape(bsz, L, KV_HEADS, HPG, 3)
    vc_rows = vc_rows.transpose(0, 2, 1, 3)
    vs_t, vw_t = vs.transpose(0, 2, 3, 1), vw.transpose(0, 2, 3, 1)

    nb = L // CMP_STRIDE

    def strips(r):
        r = r.reshape(bsz, KV_HEADS, nb, CMP_STRIDE * HEAD_DIM)
        return jnp.pad(r, ((0, 0), (0, 0), (0, ATT_TK + 8 - nb), (0, 0)))

    kcmp, vcmp = compress_rows(strips(kc_rows), strips(vc_rows), cmp_pe, cmp_w1, cmp_w2, k_g[0])
    o = nsa_prompt_attention_t(q, kcmp, vcmp, ks, vs_t, kw, vw_t, gl, rel_table)
    y = norm_matmul(o.reshape(t, c0), None, w_out, normalize=False, res=x.reshape(t, D_MODEL))
    return y.reshape(bsz, L, D_MODEL), new_cmp, new_sel, new_win[:, -min(WINDOW, L):]


DEC_PAGES_PER_STEP = 8
STRIP = CMP_STRIDE
ROW_LANES = 2 * KV_DIM
SEL_LANES = 256


def _group_rmsnorm(x, gain_row):
    lane = lax.broadcasted_iota(jnp.int32, x.shape, 1) // HEAD_DIM
    sq = x * x
    ms = jnp.zeros_like(x)
    for grp in range(KV_HEADS):
        tot = jnp.sum(jnp.where(lane == grp, sq, 0.0), axis=-1, keepdims=True)
        ms = jnp.where(lane == grp, tot, ms)
    return x * lax.rsqrt(ms / HEAD_DIM + EPS) * gain_row


def _decode_cmp_body(pt_ref, *refs, n_strips, n_sel, qblk, topk):
    pages = refs[:DEC_PAGES_PER_STEP]
    (bd_ref, peterm_ref, w2bd_ref, kg_ref, qbd_ref, bias_ref, ovl_ref, upper_ref,
     ocmp_ref, idx_ref, seq, tbuf, hibuf) = refs[DEC_PAGES_PER_STEP:]
    s = pl.program_id(1)
    strips_per_page = pages[0].shape[2] // STRIP
    for r in range(DEC_PAGES_PER_STEP):
        row0 = pl.multiple_of((s * DEC_PAGES_PER_STEP + r) * strips_per_page, strips_per_page)
        for c in range(ROW_LANES // LANES):
            lanes = slice(c * LANES, (c + 1) * LANES)
            tbuf[c] = pages[r][0, lanes, :].T
            for l in range(STRIP):
                seq[l, pl.ds(row0, strips_per_page), lanes] = tbuf[c, pl.ds(l, strips_per_page, stride=STRIP), :]

    @pl.when(s == pl.num_programs(1) - 1)
    def _():
        n_cmp = n_strips - 1
        summaries = []
        for kv in range(2):
            halves = []
            for half in range(2):
                acc = jnp.zeros((n_strips, KV_DIM), F32)
                for l in range(STRIP):
                    w_idx = (kv * 2 + half) * STRIP + l
                    xl = (seq[l, :, kv * KV_DIM:(kv + 1) * KV_DIM] + peterm_ref[w_idx:w_idx + 1, :]).astype(BF16)
                    acc = acc + jnp.dot(xl, bd_ref[w_idx], preferred_element_type=F32)
                halves.append(acc)
            hibuf[0:n_strips, :] = halves[1]
            hibuf[n_strips:n_strips + 8, :] = jnp.zeros((8, KV_DIM), F32)
            h = halves[0] + hibuf[1:n_strips + 1, :]
            h = (h * jax.nn.sigmoid(h)).astype(BF16)
            o = jnp.dot(h, w2bd_ref[kv], preferred_element_type=F32)
            if kv == 0:
                o = _group_rmsnorm(o, kg_ref[...])
            summaries.append(o.astype(BF16))
        kcmp, vcmp = summaries

        qbd = qbd_ref[0].astype(BF16)
        sc = _nt_dot(qbd, kcmp) * SCALE + bias_ref[...]
        col = lax.broadcasted_iota(jnp.int32, sc.shape, 1)
        valid = col < n_cmp
        sc = jnp.where(valid, sc, NEG)
        e = jnp.where(valid, jnp.exp(sc - jnp.max(sc, axis=-1, keepdims=True)), 0.0)
        p = e / jnp.maximum(jnp.sum(e, axis=-1, keepdims=True), 1e-30)
        pb = p.astype(BF16)
        o_cmp = jnp.dot(pb, vcmp, preferred_element_type=F32)
        head_grp = lax.broadcasted_iota(jnp.int32, o_cmp.shape, 0) // HPG
        lane_grp = lax.broadcasted_iota(jnp.int32, o_cmp.shape, 1) // HEAD_DIM
        ocmp_ref[0] = jnp.where(head_grp == lane_grp, o_cmp, 0.0)

        psum = jnp.concatenate(
            [jnp.sum(p[grp * HPG:(grp + 1) * HPG], axis=0, keepdims=True) for grp in range(KV_HEADS)]
            + [jnp.zeros((8 - KV_HEADS, n_strips), F32)], axis=0)
        imp = jnp.dot(psum.astype(BF16), ovl_ref[...], preferred_element_type=F32)
        j = lax.broadcasted_iota(jnp.int32, imp.shape, 1)
        forced = (j == qblk) | (j == 0)
        score = jnp.where(forced, 1e9, jnp.where(j <= qblk, imp, NEG))
        score = jnp.where(j < n_sel, score, -3e38)
        rank = jnp.zeros(imp.shape, F32)
        for jp in range(n_sel):
            other = score[:, jp:jp + 1]
            beats = (other > score) | ((other == score) & (jp < j))
            rank = rank + jnp.where(beats, 1.0, 0.0)
        chosen = (rank < topk) & (j < n_sel)
        cum = jnp.dot(jnp.where(chosen, 1.0, 0.0).astype(BF16), upper_ref[...], preferred_element_type=F32)
        jf = j.astype(F32)
        out_lane = lax.broadcasted_iota(jnp.int32, (8, LANES), 1)
        out = jnp.zeros((8, LANES), F32)
        for k in range(topk):
            pick = jnp.sum(jnp.where(chosen & (cum == k + 1.0), jf, 0.0), axis=-1, keepdims=True)
            out = jnp.where(out_lane == k, pick, out)
        idx_ref[0] = out


def _decode_attn_body(tbl_ref, idx_ref, *refs, n_blk, qblk, per_page):
    pages = refs[:n_blk]
    (win_ref, q_ref, knew_ref, vnew_ref, wknew_ref, wvnew_ref, bsel_ref, bwin_ref, ocmp_ref, gate_ref,
     o_ref) = refs[n_blk:]
    b, g = pl.program_id(0), pl.program_id(1)
    qb = q_ref[0, 0].astype(BF16)
    qf = qb.astype(F32)
    k_rows = pl.ds(pl.multiple_of(g * HEAD_DIM, HEAD_DIM), HEAD_DIM)
    v_rows = pl.ds(pl.multiple_of(KV_DIM + g * HEAD_DIM, HEAD_DIM), HEAD_DIM)

    def attend(parts, k_new, v_new, bias_new):
        s_new = jnp.sum(qf * k_new.astype(BF16).astype(F32), axis=-1, keepdims=True) * SCALE + bias_new
        m = s_new
        for sc, ok, _ in parts:
            m = jnp.maximum(m, jnp.max(jnp.where(ok, sc, NEG), axis=-1, keepdims=True))
        e_new = jnp.exp(s_new - m)
        es = [jnp.where(ok, jnp.exp(sc - m), 0.0) for sc, ok, _ in parts]
        den = e_new
        for e in es:
            den = den + jnp.sum(e, axis=-1, keepdims=True)
        den = jnp.maximum(den, 1e-30)
        acc = (e_new / den).astype(BF16).astype(F32) * v_new.astype(BF16).astype(F32)
        for e, (_, _, v_t) in zip(es, parts):
            acc = acc + _nt_dot((e / den).astype(BF16), v_t)
        return acc

    sel_parts = []
    for k in range(n_blk):
        bidx = idx_ref[b, g, k]
        k_t = pages[k][0, k_rows, :].astype(BF16)
        v_t = pages[k][0, v_rows, :].astype(BF16)
        pg = jnp.minimum(bidx, qblk) // per_page
        sc = jnp.dot(qb, k_t, preferred_element_type=F32) * SCALE + bsel_ref[pg, 0]
        lane_blk = lax.broadcasted_iota(jnp.int32, sc.shape, 1) // SEL_BLOCK
        ok = (lane_blk == bidx % per_page) & (bidx < qblk)
        sel_parts.append((sc, ok, v_t))
    bias0 = bsel_ref[qblk // per_page, 0][:, 0:1]
    o_sel = attend(sel_parts, knew_ref[0, 0], vnew_ref[0, 0], bias0)

    sc = jnp.dot(qb, win_ref[0, k_rows, :].astype(BF16), preferred_element_type=F32) * SCALE + bwin_ref[0]
    o_win = attend([(sc, jnp.full(sc.shape, True), win_ref[0, v_rows, :].astype(BF16))],
                   wknew_ref[0, 0], wvnew_ref[0, 0], bias0)

    gate = jax.nn.sigmoid(gate_ref[0, 0])
    o_ref[0, 0] = gate[:, 0:1] * ocmp_ref[0, 0] + gate[:, 1:2] * o_sel + gate[:, 2:3] * o_win


def nsa_decode_mixer(x, g, past_len, cache_cmp, cache_sel, win_past, page_table,
                     w_in, q_g, k_g, cmp_pe, cmp_w1, cmp_w2, w_out, rel_table):
    bsz = x.shape[0]
    n_pool, page = cache_cmp.shape[0], cache_cmp.shape[1]
    n_pages = page_table.shape[1]
    assert past_len == n_pages * page and page % STRIP == 0 and n_pages % DEC_PAGES_PER_STEP == 0
    assert win_past.shape[1] == WINDOW and past_len >= WINDOW and page % SEL_BLOCK == 0
    x2 = x.reshape(bsz, D_MODEL)
    proj = norm_matmul(x2, g, w_in)
    c0 = N_HEADS * HEAD_DIM
    q = _rmsnorm(proj[:, :c0].reshape(bsz, N_HEADS, HEAD_DIM), q_g)

    def rows(k, gain):
        kcols = proj[:, c0 + 2 * k * KV_DIM:c0 + (2 * k + 1) * KV_DIM]
        vcols = proj[:, c0 + (2 * k + 1) * KV_DIM:c0 + (2 * k + 2) * KV_DIM]
        if gain is not None:
            kcols = _rmsnorm(kcols.reshape(bsz, KV_HEADS, HEAD_DIM), gain).reshape(bsz, KV_DIM)
        return kcols, vcols

    kc_new, vc_new = rows(0, None)
    ks_new, vs_new = rows(1, k_g[1])
    kw_new, vw_new = rows(2, k_g[2])
    gl = proj[:, c0 + 6 * KV_DIM:].reshape(bsz, KV_HEADS, HPG, 3)
    as_row = lambda kk, vv: jnp.stack([kk, vv], axis=1).reshape(bsz, 1, 2, KV_HEADS, HEAD_DIM)
    new_cmp, new_sel, new_win = as_row(kc_new, vc_new), as_row(ks_new, vs_new), as_row(kw_new, vw_new)

    eye = jnp.eye(KV_HEADS, dtype=F32)
    qbd = jnp.einsum('bghd,gk->bghkd', q.reshape(bsz, KV_HEADS, HPG, HEAD_DIM), eye).reshape(bsz, N_HEADS, KV_DIM)
    w1r = cmp_w1.reshape(2, 2, STRIP, HEAD_DIM, HEAD_DIM)
    bd = jnp.einsum('khlio,gj->khlgijo', w1r, eye).reshape(2 * 2 * STRIP, KV_DIM, KV_DIM).astype(BF16)
    w2bd = jnp.einsum('kio,gj->kgijo', cmp_w2, eye).reshape(2, KV_DIM, KV_DIM).astype(BF16)
    peterm = jnp.tile(cmp_pe.reshape(2 * 2 * STRIP, HEAD_DIM), (1, KV_HEADS))
    kg_row = jnp.tile(k_g[0], KV_HEADS).reshape(1, KV_DIM)

    n_strips = past_len // STRIP
    n_cmp = n_strips - 1
    qblk = past_len // SEL_BLOCK
    n_sel = qblk + 1
    topk = min(SEL_TOPK, n_sel)
    assert n_sel <= SEL_LANES and n_strips % 8 == 0
    tab = rel_table[_rel_bucket(jnp.arange(REL_MAX_DIST + 1))].astype(F32).T
    ends = np.arange(n_strips) * CMP_STRIDE + CMP_BLOCK - 1
    bias_cmp = tab[:, np.clip(past_len - ends, 0, REL_MAX_DIST)]
    cs = np.arange(n_strips)[:, None] * CMP_STRIDE
    ss = np.arange(SEL_LANES)[None, :] * SEL_BLOCK
    ovl = (cs < ss + SEL_BLOCK) & (cs + CMP_BLOCK > ss) & (np.arange(n_strips)[:, None] < n_cmp) \
        & (np.arange(SEL_LANES)[None, :] < n_sel)
    ovl = jnp.asarray(ovl, BF16)
    upper = jnp.asarray(np.triu(np.ones((SEL_LANES, SEL_LANES), np.float32)), BF16)

    strips_per_page = page // STRIP
    cmp_view = cache_cmp.reshape(n_pool, page, ROW_LANES).transpose(0, 2, 1)
    steps = n_pages // DEC_PAGES_PER_STEP
    const = lambda a: pl.BlockSpec(a.shape, lambda b, s, pt: (0,) * a.ndim)
    page_spec = lambda r: pl.BlockSpec((1, ROW_LANES, page),
                                       lambda b, s, pt: (pt[b, s * DEC_PAGES_PER_STEP + r], 0, 0))
    o_cmp, idx = pl.pallas_call(
        functools.partial(_decode_cmp_body, n_strips=n_strips, n_sel=n_sel, qblk=qblk, topk=topk),
        name="nsa_decode_cmp",
        grid_spec=pltpu.PrefetchScalarGridSpec(
            num_scalar_prefetch=1,
            grid=(bsz, steps),
            in_specs=[page_spec(r) for r in range(DEC_PAGES_PER_STEP)]
            + [const(bd), const(peterm), const(w2bd), const(kg_row),
               pl.BlockSpec((1, N_HEADS, KV_DIM), lambda b, s, pt: (b, 0, 0)),
               const(bias_cmp), const(ovl), const(upper)],
            out_specs=[pl.BlockSpec((1, N_HEADS, KV_DIM), lambda b, s, pt: (b, 0, 0)),
                       pl.BlockSpec((1, 8, LANES), lambda b, s, pt: (b, 0, 0))],
            scratch_shapes=[pltpu.VMEM((STRIP, n_strips, ROW_LANES), F32),
                            pltpu.VMEM((ROW_LANES // LANES, page, LANES), F32),
                            pltpu.VMEM((n_strips + 8, KV_DIM), F32)],
        ),
        out_shape=[jax.ShapeDtypeStruct((bsz, N_HEADS, KV_DIM), F32),
                   jax.ShapeDtypeStruct((bsz, 8, LANES), F32)],
        compiler_params=pltpu.CompilerParams(
            dimension_semantics=("parallel", "arbitrary"),
            vmem_limit_bytes=DECODE_VMEM_LIMIT_BYTES),
    )(page_table, *([cmp_view] * DEC_PAGES_PER_STEP), bd, peterm, w2bd, kg_row, qbd, bias_cmp, ovl, upper)

    blk_idx = idx[:, :KV_HEADS, :topk].astype(jnp.int32)
    per_page = page // SEL_BLOCK
    safe = jnp.minimum(blk_idx, qblk - 1)
    page_of = jnp.take_along_axis(page_table, (safe // per_page).reshape(bsz, -1), axis=1).reshape(safe.shape)
    sel_view = cache_sel.reshape(n_pool, page, ROW_LANES).transpose(0, 2, 1)
    win_view = win_past.reshape(bsz, WINDOW, ROW_LANES).transpose(0, 2, 1)

    n_pg = n_pages + 1
    dist_sel = past_len - (np.arange(n_pg)[:, None] * page + np.arange(page)[None, :])
    bsel = tab[:, np.clip(dist_sel, 0, REL_MAX_DIST)]
    bsel = bsel.reshape(KV_HEADS, HPG, n_pg, page).transpose(2, 0, 1, 3)
    bwin = tab[:, np.clip(past_len - (past_len - WINDOW + np.arange(WINDOW)), 0, REL_MAX_DIST)]
    bwin = bwin.reshape(KV_HEADS, HPG, WINDOW)

    per_group = lambda a: a.reshape(bsz, KV_HEADS, 1, HEAD_DIM)
    q4 = q.reshape(bsz, KV_HEADS, HPG, HEAD_DIM)
    ocmp4 = o_cmp.reshape(bsz, KV_HEADS, HPG, KV_HEADS, HEAD_DIM).sum(axis=3)
    page_spec2 = lambda k: pl.BlockSpec((1, ROW_LANES, page), lambda b, g, t, i: (t[b, g, k], 0, 0))
    per_bg = lambda *shape: pl.BlockSpec((1, 1) + shape, lambda b, g, t, i: (b, g) + (0,) * len(shape))
    o4 = pl.pallas_call(
        functools.partial(_decode_attn_body, n_blk=topk, qblk=qblk, per_page=per_page),
        name="nsa_decode_attn",
        grid_spec=pltpu.PrefetchScalarGridSpec(
            num_scalar_prefetch=2,
            grid=(bsz, KV_HEADS),
            in_specs=[page_spec2(k) for k in range(topk)]
            + [pl.BlockSpec((1, ROW_LANES, WINDOW), lambda b, g, t, i: (b, 0, 0)),
               per_bg(HPG, HEAD_DIM), per_bg(1, HEAD_DIM), per_bg(1, HEAD_DIM), per_bg(1, HEAD_DIM),
               per_bg(1, HEAD_DIM),
               pl.BlockSpec((n_pg, 1, HPG, page), lambda b, g, t, i: (0, g, 0, 0)),
               pl.BlockSpec((1, HPG, WINDOW), lambda b, g, t, i: (g, 0, 0)),
               per_bg(HPG, HEAD_DIM), per_bg(HPG, 3)],
            out_specs=per_bg(HPG, HEAD_DIM),
        ),
        out_shape=jax.ShapeDtypeStruct((bsz, KV_HEADS, HPG, HEAD_DIM), F32),
        compiler_params=pltpu.CompilerParams(
            dimension_semantics=("parallel", "arbitrary"),
            vmem_limit_bytes=VMEM_LIMIT_BYTES),
    )(page_of, blk_idx, *([sel_view] * topk), win_view, q4, per_group(ks_new), per_group(vs_new),
      per_group(kw_new), per_group(vw_new), bsel, bwin, ocmp4, gl)
    y = norm_matmul(o4.reshape(bsz, c0), None, w_out, normalize=False, res=x2)
    new_win_buf = jnp.concatenate([win_past[:, 1:], new_win], axis=1)
    return y.reshape(x.shape), new_cmp, new_sel, new_win_buf


def _gla_chunked(q, k, v, logf, s0):
    bsz, L = q.shape[0], q.shape[1]
    nc = -(-L // H_CHUNK)
    pad = nc * H_CHUNK - L

    def prep(t):
        t = jnp.pad(t, ((0, 0), (0, pad), (0, 0), (0, 0)))
        return t.reshape(bsz, nc, H_CHUNK, H_HEADS, t.shape[-1])

    q, k, v, logf = prep(q), prep(k), prep(v), prep(logf)
    acum = jnp.cumsum(logf, axis=2)
    alast = acum[:, :, -1:]
    qe = q * jnp.exp(acum)
    ke = k * jnp.exp(-acum)
    kd = k * jnp.exp(alast - acum)
    tri = jnp.tril(jnp.ones((H_CHUNK, H_CHUNK), bool))
    att = jnp.where(tri, jnp.einsum('bcthk,bcshk->bchts', qe, ke), 0.0)
    o_intra = jnp.einsum('bchts,bcshv->bcthv', att, v)
    upd = jnp.einsum('bcshk,bcshv->bchkv', kd, v)

    def step(s, inp):
        dec, up = inp
        return dec[..., None] * s + up, s

    s_last, s_prev = lax.scan(step, s0, (jnp.exp(alast[:, :, 0]).swapaxes(0, 1), upd.swapaxes(0, 1)))
    o_inter = jnp.einsum('bcthk,bchkv->bcthv', qe, s_prev.swapaxes(0, 1))
    o = (o_intra + o_inter).reshape(bsz, nc * H_CHUNK, H_HEADS, H_DV)[:, :L]
    return o, s_last


HGRN_TOKENS = 128


def _hgrn_body(q_ref, f_ref, v_ref, gate_ref, lb_ref, ng_ref, lcum_ref, lsum_ref, o_ref, s_ref):
    @pl.when(pl.program_id(1) == 0)
    def _():
        s_ref[...] = jnp.zeros_like(s_ref)

    n = HGRN_TOKENS
    qr = q_ref[...]
    q = qr * jax.nn.sigmoid(qr)
    lb = lb_ref[...]
    forget = lb + (1.0 - lb) * jax.nn.sigmoid(f_ref[...])
    logf = jnp.log(forget)
    k = 1.0 - forget
    parts = _split3(logf)
    acum = sum(jnp.dot(lcum_ref[...], p, preferred_element_type=F32) for p in parts)
    atot = sum(jnp.dot(lsum_ref[...], p, preferred_element_type=F32) for p in parts)
    qe = (q * jnp.exp(acum)).astype(BF16)
    ke = (k * jnp.exp(-acum)).astype(BF16)
    kd = (k * jnp.exp(atot - acum)).astype(BF16)
    vb = v_ref[...].astype(BF16)
    ti = lax.broadcasted_iota(jnp.int32, (n, n), 0)
    si = lax.broadcasted_iota(jnp.int32, (n, n), 1)
    intra = (ti // H_CHUNK == si // H_CHUNK) & (ti >= si)
    gate = gate_ref[...]
    gate = gate * jax.nn.sigmoid(gate)
    outs = []
    for h in range(H_HEADS):
        cols = slice(h * H_DK, (h + 1) * H_DK)
        att = jnp.where(intra, _nt_dot(qe[:, cols], ke[:, cols]), 0.0)
        o_h = jnp.dot(att.astype(BF16), vb[:, cols], preferred_element_type=F32)
        decay_t = jnp.exp(atot[:, cols]).T
        state = s_ref[0, cols, :]
        inter = []
        for j in range(n // H_CHUNK):
            rows = slice(j * H_CHUNK, (j + 1) * H_CHUNK)
            inter.append(jnp.dot(qe[rows, cols], state.astype(BF16), preferred_element_type=F32))
            upd = lax.dot_general(kd[rows, cols], vb[rows, cols], (((0,), (0,)), ((), ())),
                                  preferred_element_type=F32)
            state = decay_t[:, j * H_CHUNK:j * H_CHUNK + 1] * state + upd
        s_ref[0, cols, :] = state
        o_h = o_h + jnp.concatenate(inter, axis=0)
        o_h = o_h * lax.rsqrt(jnp.mean(o_h * o_h, axis=-1, keepdims=True) + EPS) * ng_ref[...]
        outs.append(o_h * gate[:, cols])
    o_ref[...] = jnp.concatenate(outs, axis=1).astype(BF16)


def hgrn2_prompt_mixer(x, g, lb, w_in, norm_g, w_out):
    bsz, L, _ = x.shape
    n = HGRN_TOKENS
    assert L % n == 0 and n % H_CHUNK == 0 and H_DK == LANES and H_DV == LANES
    t = bsz * L
    x2 = x.reshape(t, D_MODEL)
    proj = norm_matmul(x2, g, w_in)
    nb = L // n
    r = np.arange(n)
    same = (r[:, None] // H_CHUNK) == (r[None, :] // H_CHUNK)
    lcum = jnp.asarray(same & (r[None, :] <= r[:, None]), BF16)
    lsum = jnp.asarray(same, BF16)
    col = lambda kk: pl.BlockSpec((n, D_MODEL), lambda b, c: (b * nb + c, kk))
    const = lambda a: pl.BlockSpec(a.shape, lambda b, c: (0,) * a.ndim)
    lb2 = lb.reshape(1, D_MODEL)
    ng2 = norm_g.reshape(1, H_DV)
    o, s = pl.pallas_call(
        _hgrn_body,
        name="hgrn_blocks",
        grid=(bsz, nb),
        in_specs=[col(0), col(1), col(2), col(3), const(lb2), const(ng2), const(lcum), const(lsum)],
        out_specs=[pl.BlockSpec((n, D_MODEL), lambda b, c: (b * nb + c, 0)),
                   pl.BlockSpec((1, H_HEADS * H_DK, H_DV), lambda b, c: (b, 0, 0))],
        out_shape=[jax.ShapeDtypeStruct((t, D_MODEL), BF16),
                   jax.ShapeDtypeStruct((bsz, H_HEADS * H_DK, H_DV), F32)],
        compiler_params=pltpu.CompilerParams(
            dimension_semantics=("parallel", "arbitrary"),
            vmem_limit_bytes=VMEM_LIMIT_BYTES),
    )(proj, proj, proj, proj, lb2, ng2, lcum, lsum)
    y = norm_matmul(o, None, w_out, normalize=False, res=x2)
    return y.reshape(bsz, L, D_MODEL), s.reshape(bsz, H_HEADS, H_DK, H_DV)


def hgrn2_mixer(x, g, s0, lb, w_in, norm_g, w_out):
    bsz, L, _ = x.shape
    proj = norm_matmul(x.reshape(bsz * L, D_MODEL), g, w_in).reshape(bsz, L, -1)
    q, fr, iv, gate = jnp.split(proj, 4, axis=-1)
    q = jax.nn.silu(q).reshape(bsz, L, H_HEADS, H_DK)
    forget = lb + (1.0 - lb) * jax.nn.sigmoid(fr)
    logf = jnp.log(forget).reshape(bsz, L, H_HEADS, H_DK)
    k = (1.0 - forget).reshape(bsz, L, H_HEADS, H_DK)
    v = iv.reshape(bsz, L, H_HEADS, H_DV)
    o, s = _gla_chunked(q, k, v, logf, s0)
    o = _rmsnorm(o, norm_g) * jax.nn.silu(gate.reshape(bsz, L, H_HEADS, H_DV))
    y = norm_matmul(o.reshape(bsz * L, D_MODEL), None, w_out, normalize=False)
    return y.reshape(bsz, L, D_MODEL), s


def kernel(x_prompt, x_sample, state_ssm, state_conv, cache_kv_cmp, cache_kv_sel, cache_kv_win, state_hgrn, page_table, norm_g, rel_table, m_w_in, m_conv_w, m_conv_b, m_dt_bias, m_a_log, m_d, m_norm_g, m_w_out, n_w_in, n_q_g, n_k_g, n_cmp_pe, n_cmp_w1, n_cmp_w2, n_w_out, h_w_in, h_lb, h_norm_g, h_w_out, moe_w_rg, moe_b_rg, moe_w_re, moe_b_re, moe_w1, moe_w3, moe_w2):
    bp, lp = x_prompt.shape[0], x_prompt.shape[1]
    past_len = page_table.shape[1] * cache_kv_cmp.shape[2]
    dt = x_prompt.dtype
    m_w_in, m_w_out, n_w_in, n_w_out, h_w_in, h_w_out, moe_w1, moe_w3, moe_w2 = (
        w.astype(BF16) for w in (m_w_in, m_w_out, n_w_in, n_w_out, h_w_in, h_w_out, moe_w1, moe_w3, moe_w2))
    lbs = jax.nn.softmax(h_lb.astype(F32), axis=0)
    lbs = jnp.cumsum(lbs, axis=0) - lbs[0]
    xp, xs = x_prompt, x_sample
    ssm_p, conv_p, cmp_p, sel_p, win_p, hg_p = [], [], [], [], [], []
    ssm_s, conv_s, cmp_s, sel_s, win_s, hg_s = [], [], [], [], [], []
    for i in range(DEPTH):
        kind, j = i % N_MIXERS, i // N_MIXERS
        g0 = norm_g[i, 0]
        if kind == 0:
            w = (m_w_in[j], m_conv_w[j], m_conv_b[j], m_dt_bias[j], m_a_log[j], m_d[j], m_norm_g[j], m_w_out[j])
            xp, cbuf, hh = mamba_prompt_mixer(xp, g0, *w)
            ssm_p.append(hh)
            conv_p.append(cbuf)
            ys, cbuf, hh = mamba_mixer(xs, g0, state_conv[j], state_ssm[j], *w)
            xs = xs + ys
            ssm_s.append(hh)
            conv_s.append(cbuf)
        elif kind == 1:
            w = (n_w_in[j], n_q_g[j], n_k_g[j], n_cmp_pe[j], n_cmp_w1[j], n_cmp_w2[j], n_w_out[j], rel_table)
            xp, rc, rs, wb = nsa_prompt_mixer(xp, g0, *w)
            cmp_p.append(rc)
            sel_p.append(rs)
            win_p.append(wb)
            xs, rc, rs, wb = nsa_decode_mixer(xs, g0, past_len, cache_kv_cmp[j], cache_kv_sel[j],
                                              cache_kv_win[j], page_table, *w)
            cmp_s.append(rc)
            sel_s.append(rs)
            win_s.append(wb)
        else:
            w = (lbs[i], h_w_in[j], h_norm_g[j], h_w_out[j])
            xp, st = hgrn2_prompt_mixer(xp, g0, *w)
            hg_p.append(st)
            ys, st = hgrn2_mixer(xs, g0, state_hgrn[j], *w)
            xs = xs + ys
            hg_s.append(st)
        mw = (moe_w_rg[i], moe_b_rg[i], moe_w_re[i], moe_b_re[i], moe_w1[i], moe_w3[i], moe_w2[i])
        xp = hier_moe_residual(xp.reshape(-1, D_MODEL), norm_g[i, 1], *mw).reshape(xp.shape)
        xs = hier_moe_residual(xs.reshape(-1, D_MODEL), norm_g[i, 1], *mw).reshape(xs.shape)
    return (xp, xs,
            jnp.stack(ssm_p), jnp.stack(conv_p), jnp.stack(cmp_p), jnp.stack(sel_p), jnp.stack(win_p), jnp.stack(hg_p),
            jnp.stack(ssm_s), jnp.stack(conv_s), jnp.stack(cmp_s), jnp.stack(sel_s), jnp.stack(win_s), jnp.stack(hg_s))
```
